```python
import jax, jax.numpy as jnp
from jax import lax
import numpy as np

D_MODEL = 2048
BATCH = 16
SEQ = 2048
DEPTH = 4

GLA_HEADS = 4
GLA_DK = D_MODEL // 2
GLA_DV = D_MODEL
GLA_HEAD_K = GLA_DK // GLA_HEADS
GLA_HEAD_V = GLA_DV // GLA_HEADS
GLA_GATE_RANK = 16
GLA_GATE_NORMALIZER = 16.0
GLA_CHUNK = 64
GLA_SUBCHUNK = 16
GLA_SPLITS = (GLA_DK, 2 * GLA_DK, 2 * GLA_DK + GLA_DV, 2 * GLA_DK + 2 * GLA_DV)
GLA_IN = 2 * GLA_DK + 2 * GLA_DV + GLA_GATE_RANK

DIL_PATTERNS = ((128, 1), (512, 4), (2048, 16))
DIL_GROUPS = len(DIL_PATTERNS)
DIL_HEADS = 8
DIL_HEAD_DIM = 128
DIL_WIDTH = DIL_HEADS * DIL_HEAD_DIM
DIL_BLOCK = 128
DIL_IN = 3 * DIL_GROUPS * DIL_WIDTH

D_FF = 5504
CONV_WIDTH = 3

N_GLA_LAYERS = (DEPTH + 1) // 2
N_DIL_LAYERS = DEPTH // 2

DEEPNORM_ALPHA = (2 * DEPTH) ** 0.25
DEEPNORM_BETA = (8 * DEPTH) ** -0.25
LN_EPS = 1e-5
RMS_EPS = 1e-6

kernel_name = "hybrid_gla_dilated_convffn_deepnorm"


def layer_norm(x, g, b):
    xf = x.astype(jnp.float32)
    mu = xf.mean(-1, keepdims=True)
    var = jnp.square(xf - mu).mean(-1, keepdims=True)
    return ((xf - mu) * lax.rsqrt(var + LN_EPS)).astype(x.dtype) * g + b


def gla_chunk_step(state, inp):
    q, k, v, g = inp
    Bb, H, C, dk = q.shape
    c = GLA_SUBCHUNK
    n = C // c
    b = jnp.cumsum(g, axis=2)
    b_last = b[:, :, -1:, :]
    o_inter = jnp.einsum('bhcd,bhde->bhce', q * jnp.exp(b), state)
    qs = q.reshape(Bb, H, n, c, dk)
    ks = k.reshape(Bb, H, n, c, dk)
    vs = v.reshape(Bb, H, n, c, -1)
    bs = b.reshape(Bb, H, n, c, dk)
    b_ref = jnp.concatenate([jnp.zeros_like(bs[:, :, :1, -1]), bs[:, :, :-1, -1]], axis=2)
    q_ref = qs * jnp.exp(bs - b_ref[:, :, :, None, :])
    earlier = jnp.arange(C)[None, :] < (jnp.arange(n) * c)[:, None]
    k_exp = jnp.where(earlier[None, None, :, :, None],
                      b_ref[:, :, :, None, :] - b[:, :, None, :, :], -jnp.inf)
    k_ref = k[:, :, None] * jnp.exp(k_exp)
    a_inter = jnp.einsum('bhsid,bhsjd->bhsij', q_ref, k_ref)
    causal = jnp.tril(jnp.ones((c, c), dtype=bool))
    d_exp = jnp.where(causal[:, :, None],
                      bs[:, :, :, :, None, :] - bs[:, :, :, None, :, :], -jnp.inf)
    a_intra = jnp.einsum('bhsid,bhsjd,bhsijd->bhsij', qs, ks, jnp.exp(d_exp))
    o_intra = (jnp.einsum('bhsij,bhje->bhsie', a_inter, v)
               + jnp.einsum('bhsij,bhsje->bhsie', a_intra, vs)).reshape(Bb, H, C, -1)
    new_state = (jnp.exp(b_last[:, :, 0, :])[..., None] * state
                 + jnp.einsum('bhcd,bhce->bhde', k * jnp.exp(b_last - b), v))
    return new_state, o_inter + o_intra


def gla_mixer(x, w_in, w_gate_up, gate_bias, norm_g, w_out):
    B, S, _ = x.shape
    H, dk, dv, C = GLA_HEADS, GLA_HEAD_K, GLA_HEAD_V, GLA_CHUNK
    proj = x @ w_in
    q, k, v, r, g_low = jnp.split(proj, list(GLA_SPLITS), axis=-1)
    log_gate = jax.nn.log_sigmoid((g_low @ w_gate_up + gate_bias).astype(jnp.float32)) / GLA_GATE_NORMALIZER
    q = q.astype(jnp.float32) * dk ** -0.5

    def to_chunks(t, d):
        return t.astype(jnp.float32).reshape(B, S // C, C, H, d).transpose(1, 0, 3, 2, 4)

    xs = (to_chunks(q, dk), to_chunks(k, dk), to_chunks(v, dv), to_chunks(log_gate, dk))
    state0 = jnp.zeros((B, H, dk, dv), jnp.float32)
    _, o = lax.scan(gla_chunk_step, state0, xs)
    o = o.transpose(1, 0, 3, 2, 4).reshape(B, S, H, dv)
    o = o * lax.rsqrt(jnp.mean(jnp.square(o), -1, keepdims=True) + RMS_EPS) * norm_g
    o = o.reshape(B, S, GLA_DV).astype(x.dtype) * jax.nn.silu(r)
    return o @ w_out


def banded_attention(q, k, v, steps):
    N, L, H, dh = q.shape
    P = DIL_BLOCK
    nb = -(-L // P)
    Lp = nb * P
    pad = ((0, 0), (0, Lp - L), (0, 0), (0, 0))
    q, k, v = (jnp.pad(t, pad).reshape(N, nb, P, H, dh) for t in (q, k, v))

    def with_prev(t):
        prev = jnp.concatenate([jnp.zeros_like(t[:, :1]), t[:, :-1]], axis=1)
        return jnp.concatenate([prev, t], axis=2)

    kw, vw = with_prev(k), with_prev(v)
    s = jnp.einsum('nbqhd,nbkhd->nbhqk', q, kw).astype(jnp.float32) * dh ** -0.5
    qi = jnp.arange(P)[:, None] + P
    kj = jnp.arange(2 * P)[None, :]
    dist = qi - kj
    band = (dist >= 0) & (dist <= steps)
    real_key = (jnp.arange(nb)[:, None, None] > 0) | (kj[None] >= P)
    mask = band[None] & real_key
    s = jnp.where(mask[None, :, None], s, -jnp.inf)
    m = s.max(-1, keepdims=True)
    p = jnp.exp(s - m)
    l = p.sum(-1, keepdims=True)
    o = jnp.einsum('nbhqk,nbkhd->nbqhd', (p / l).astype(v.dtype), vw)
    lse = (m + jnp.log(l))[..., 0]
    o = o.reshape(N, Lp, H, dh)[:, :L]
    lse = lse.transpose(0, 1, 3, 2).reshape(N, Lp, H)[:, :L]
    return o, lse


def dilated_group(q, k, v, window, dilation):
    B, S, H, dh = q.shape
    L = S // dilation

    def to_strided(t):
        return t.reshape(B, L, dilation, H, dh).transpose(0, 2, 1, 3, 4).reshape(B * dilation, L, H, dh)

    o, lse = banded_attention(to_strided(q), to_strided(k), to_strided(v), window // dilation)
    o = o.reshape(B, dilation, L, H, dh).transpose(0, 2, 1, 3, 4).reshape(B, S, H, dh)
    lse = lse.reshape(B, dilation, L, H).transpose(0, 2, 1, 3).reshape(B, S, H)
    return o, lse


def dilated_mixer(x, w_in, w_out):
    B, S, _ = x.shape
    proj = (x @ w_in).reshape(B, S, DIL_GROUPS, 3, DIL_HEADS, DIL_HEAD_DIM)
    outs, lses = [], []
    for gi, (window, dilation) in enumerate(DIL_PATTERNS):
        o, lse = dilated_group(proj[:, :, gi, 0], proj[:, :, gi, 1], proj[:, :, gi, 2], window, dilation)
        outs.append(o)
        lses.append(lse)
    wts = jax.nn.softmax(jnp.stack(lses, 0), axis=0)
    o = jnp.einsum('gbsh,gbshd->bshd', wts.astype(x.dtype), jnp.stack(outs, 0))
    return o.reshape(B, S, DIL_WIDTH) @ w_out


def conv_ffn(x, w_up, conv_w, conv_b, w_down):
    S = x.shape[1]
    h = x @ w_up
    hp = jnp.pad(h, ((0, 0), (CONV_WIDTH - 1, 0), (0, 0)))
    h = sum((conv_w[j] * hp[:, j:j + S] for j in range(CONV_WIDTH)), conv_b)
    gate, up = jnp.split(h, 2, axis=-1)
    return (jax.nn.silu(gate) * up) @ w_down


def _fwd_setup_inputs(seed: int = 0) -> dict:
    key = jax.random.key(seed)
    ks = jax.random.split(key, 16)
    nrm = lambda k, shape, scale: jax.random.normal(k, shape, jnp.float32) * scale
    return {
        "x": nrm(ks[0], (BATCH, SEQ, D_MODEL), 1.0),
        "gla_w_in": nrm(ks[1], (N_GLA_LAYERS, D_MODEL, GLA_IN), D_MODEL ** -0.5),
        "gla_w_gate_up": nrm(ks[2], (N_GLA_LAYERS, GLA_GATE_RANK, GLA_DK), GLA_GATE_RANK ** -0.5),
        "gla_gate_bias": nrm(ks[3], (N_GLA_LAYERS, GLA_DK), 0.1),
        "gla_norm_g": 1.0 + nrm(ks[4], (N_GLA_LAYERS, GLA_HEAD_V), 0.02),
        "gla_w_out": nrm(ks[5], (N_GLA_LAYERS, GLA_DV, D_MODEL), GLA_DV ** -0.5 * DEEPNORM_BETA),
        "dil_w_in": nrm(ks[6], (N_DIL_LAYERS, D_MODEL, DIL_IN), D_MODEL ** -0.5),
        "dil_w_out": nrm(ks[7], (N_DIL_LAYERS, DIL_WIDTH, D_MODEL), DIL_WIDTH ** -0.5 * DEEPNORM_BETA),
        "ffn_w_up": nrm(ks[8], (DEPTH, D_MODEL, 2 * D_FF), D_MODEL ** -0.5),
        "ffn_conv_w": nrm(ks[9], (DEPTH, CONV_WIDTH, 2 * D_FF), CONV_WIDTH ** -0.5),
        "ffn_conv_b": nrm(ks[10], (DEPTH, 2 * D_FF), 0.02),
        "ffn_w_down": nrm(ks[11], (DEPTH, D_FF, D_MODEL), D_FF ** -0.5 * DEEPNORM_BETA),
        "ln_g": 1.0 + nrm(ks[12], (DEPTH, 2, D_MODEL), 0.02),
        "ln_b": nrm(ks[13], (DEPTH, 2, D_MODEL), 0.02),
    }


def _fwd_reference(x, gla_w_in, gla_w_gate_up, gla_gate_bias, gla_norm_g, gla_w_out,
              dil_w_in, dil_w_out, ffn_w_up, ffn_conv_w, ffn_conv_b, ffn_w_down, ln_g, ln_b):
    for i in range(DEPTH):
        j = i // 2
        if i % 2 == 0:
            mix = gla_mixer(x, gla_w_in[j], gla_w_gate_up[j], gla_gate_bias[j], gla_norm_g[j], gla_w_out[j])
        else:
            mix = dilated_mixer(x, dil_w_in[j], dil_w_out[j])
        x = layer_norm(DEEPNORM_ALPHA * x + mix, ln_g[i, 0], ln_b[i, 0])
        ffn = conv_ffn(x, ffn_w_up[i], ffn_conv_w[i], ffn_conv_b[i], ffn_w_down[i])
        x = layer_norm(DEEPNORM_ALPHA * x + ffn, ln_g[i, 1], ln_b[i, 1])
    return x


import jax as _jax
import jax.numpy as _jnp

TWIN_FORMAT = 'train_step'
FWD_PARAMS = ['x', 'gla_w_in', 'gla_w_gate_up', 'gla_gate_bias', 'gla_norm_g', 'gla_w_out', 'dil_w_in', 'dil_w_out', 'ffn_w_up', 'ffn_conv_w', 'ffn_conv_b', 'ffn_w_down', 'ln_g', 'ln_b']
TWIN_WEIGHTS = ['gla_w_in', 'gla_w_gate_up', 'gla_gate_bias', 'gla_norm_g', 'gla_w_out', 'dil_w_in', 'dil_w_out', 'ffn_w_up', 'ffn_conv_w', 'ffn_conv_b', 'ffn_w_down', 'ln_g', 'ln_b']
TWIN_DIFF_INPUT = 'x'
TWIN_INPUTS = ['x', 'gla_w_in', 'gla_w_gate_up', 'gla_gate_bias', 'gla_norm_g', 'gla_w_out', 'dil_w_in', 'dil_w_out', 'ffn_w_up', 'ffn_conv_w', 'ffn_conv_b', 'ffn_w_down', 'ln_g', 'ln_b', 'loss_target', 'm_gla_w_in', 'm_gla_w_gate_up', 'm_gla_gate_bias', 'm_gla_norm_g', 'm_gla_w_out', 'm_dil_w_in', 'm_dil_w_out', 'm_ffn_w_up', 'm_ffn_conv_w', 'm_ffn_conv_b', 'm_ffn_w_down', 'm_ln_g', 'm_ln_b', 'v_gla_w_in', 'v_gla_w_gate_up', 'v_gla_gate_bias', 'v_gla_norm_g', 'v_gla_w_out', 'v_dil_w_in', 'v_dil_w_out', 'v_ffn_w_up', 'v_ffn_conv_w', 'v_ffn_conv_b', 'v_ffn_w_down', 'v_ln_g', 'v_ln_b']
TWIN_OUTPUTS = ['loss', 'grad_x', 'grad_gla_w_in', 'grad_gla_w_gate_up', 'grad_gla_gate_bias', 'grad_gla_norm_g', 'grad_gla_w_out', 'grad_dil_w_in', 'grad_dil_w_out', 'grad_ffn_w_up', 'grad_ffn_conv_w', 'grad_ffn_conv_b', 'grad_ffn_w_down', 'grad_ln_g', 'grad_ln_b', 'delta_gla_w_in', 'delta_gla_w_gate_up', 'delta_gla_gate_bias', 'delta_gla_norm_g', 'delta_gla_w_out', 'delta_dil_w_in', 'delta_dil_w_out', 'delta_ffn_w_up', 'delta_ffn_conv_w', 'delta_ffn_conv_b', 'delta_ffn_w_down', 'delta_ln_g', 'delta_ln_b', 'new_m_gla_w_in', 'new_m_gla_w_gate_up', 'new_m_gla_gate_bias', 'new_m_gla_norm_g', 'new_m_gla_w_out', 'new_m_dil_w_in', 'new_m_dil_w_out', 'new_m_ffn_w_up', 'new_m_ffn_conv_w', 'new_m_ffn_conv_b', 'new_m_ffn_w_down', 'new_m_ln_g', 'new_m_ln_b', 'new_v_gla_w_in', 'new_v_gla_w_gate_up', 'new_v_gla_gate_bias', 'new_v_gla_norm_g', 'new_v_gla_w_out', 'new_v_dil_w_in', 'new_v_dil_w_out', 'new_v_ffn_w_up', 'new_v_ffn_conv_w', 'new_v_ffn_conv_b', 'new_v_ffn_w_down', 'new_v_ln_g', 'new_v_ln_b']
TWIN_LEAF_KINDS = {'loss': 'loss', 'grad_x': 'grad_x', 'grad_gla_w_in': 'grad_w', 'grad_gla_w_gate_up': 'grad_w', 'grad_gla_gate_bias': 'grad_w', 'grad_gla_norm_g': 'grad_w', 'grad_gla_w_out': 'grad_w', 'grad_dil_w_in': 'grad_w', 'grad_dil_w_out': 'grad_w', 'grad_ffn_w_up': 'grad_w', 'grad_ffn_conv_w': 'grad_w', 'grad_ffn_conv_b': 'grad_w', 'grad_ffn_w_down': 'grad_w', 'grad_ln_g': 'grad_w', 'grad_ln_b': 'grad_w', 'delta_gla_w_in': 'delta_w', 'delta_gla_w_gate_up': 'delta_w', 'delta_gla_gate_bias': 'delta_w', 'delta_gla_norm_g': 'delta_w', 'delta_gla_w_out': 'delta_w', 'delta_dil_w_in': 'delta_w', 'delta_dil_w_out': 'delta_w', 'delta_ffn_w_up': 'delta_w', 'delta_ffn_conv_w': 'delta_w', 'delta_ffn_conv_b': 'delta_w', 'delta_ffn_w_down': 'delta_w', 'delta_ln_g': 'delta_w', 'delta_ln_b': 'delta_w', 'new_m_gla_w_in': 'new_m', 'new_m_gla_w_gate_up': 'new_m', 'new_m_gla_gate_bias': 'new_m', 'new_m_gla_norm_g': 'new_m', 'new_m_gla_w_out': 'new_m', 'new_m_dil_w_in': 'new_m', 'new_m_dil_w_out': 'new_m', 'new_m_ffn_w_up': 'new_m', 'new_m_ffn_conv_w': 'new_m', 'new_m_ffn_conv_b': 'new_m', 'new_m_ffn_w_down': 'new_m', 'new_m_ln_g': 'new_m', 'new_m_ln_b': 'new_m', 'new_v_gla_w_in': 'new_v', 'new_v_gla_w_gate_up': 'new_v', 'new_v_gla_gate_bias': 'new_v', 'new_v_gla_norm_g': 'new_v', 'new_v_gla_w_out': 'new_v', 'new_v_dil_w_in': 'new_v', 'new_v_dil_w_out': 'new_v', 'new_v_ffn_w_up': 'new_v', 'new_v_ffn_conv_w': 'new_v', 'new_v_ffn_conv_b': 'new_v', 'new_v_ffn_w_down': 'new_v', 'new_v_ln_g': 'new_v', 'new_v_ln_b': 'new_v'}


def _forward(args):
    return _fwd_reference(*[args[k] for k in FWD_PARAMS])


def _output_shape():
    out = _jax.eval_shape(lambda: _forward(_fwd_setup_inputs(0)))
    return out.shape, out.dtype

N_MICROBATCH = 1
ADAM_LR = 0.001
ADAM_B1 = 0.9
ADAM_B2 = 0.999
ADAM_EPS = 1e-08
ADAM_WD = 0.01
ADAM_STEP = 10
PER_EXAMPLE_BATCH_AXIS = {'x': 0, 'loss_target': 0}
SHARED_INPUTS = []
_WEIGHT_DTYPES = {'gla_w_in': _jnp.float32, 'gla_w_gate_up': _jnp.float32, 'gla_gate_bias': _jnp.float32, 'gla_norm_g': _jnp.float32, 'gla_w_out': _jnp.float32, 'dil_w_in': _jnp.float32, 'dil_w_out': _jnp.float32, 'ffn_w_up': _jnp.float32, 'ffn_conv_w': _jnp.float32, 'ffn_conv_b': _jnp.float32, 'ffn_w_down': _jnp.float32, 'ln_g': _jnp.float32, 'ln_b': _jnp.float32}
MOMENT_SCALE = {'gla_w_in': 1.713240e-02, 'gla_w_gate_up': 2.290419e-03, 'gla_gate_bias': 9.544394e-03, 'gla_norm_g': 3.062331e-02, 'gla_w_out': 3.286119e-02, 'dil_w_in': 3.394798e-03, 'dil_w_out': 1.109020e-02, 'ffn_w_up': 8.554181e-03, 'ffn_conv_w': 8.584623e-03, 'ffn_conv_b': 9.296745e-03, 'ffn_w_down': 3.285038e-02, 'ln_g': 5.686343e+00, 'ln_b': 3.294005e-01}


def _to_microbatches(a, axis):
    t = _jnp.moveaxis(a, axis, 0)
    t = t.reshape((N_MICROBATCH, t.shape[0] // N_MICROBATCH) + t.shape[1:])
    return _jnp.moveaxis(t, 1, axis + 1)


def setup_inputs(seed: int = 0) -> dict:
    inp = _fwd_setup_inputs(seed)
    key = _jax.random.fold_in(_jax.random.key(seed), 7919)
    shape, _ = _output_shape()
    out = dict(inp)
    out["loss_target"] = _jax.random.normal(_jax.random.fold_in(key, 0), shape, _jnp.float32)
    for i, name in enumerate(TWIN_WEIGHTS):
        w = inp[name].astype(_jnp.float32)
        if MOMENT_SCALE is None:
            s = _jnp.sqrt(_jnp.mean(_jnp.square(w)) + 1e-30)
        else:
            s = MOMENT_SCALE[name]
        km, kv = _jax.random.split(_jax.random.fold_in(key, i + 1))
        out[name] = w
        out["m_" + name] = s * _jax.random.normal(km, w.shape, _jnp.float32)
        out["v_" + name] = (s * s) * _jax.random.uniform(kv, w.shape, _jnp.float32, 0.5, 1.5)
    if N_MICROBATCH > 1:
        for name, axis in PER_EXAMPLE_BATCH_AXIS.items():
            out[name] = _to_microbatches(out[name], axis)
    return {'x': out['x'], 'gla_w_in': out['gla_w_in'], 'gla_w_gate_up': out['gla_w_gate_up'], 'gla_gate_bias': out['gla_gate_bias'], 'gla_norm_g': out['gla_norm_g'], 'gla_w_out': out['gla_w_out'], 'dil_w_in': out['dil_w_in'], 'dil_w_out': out['dil_w_out'], 'ffn_w_up': out['ffn_w_up'], 'ffn_conv_w': out['ffn_conv_w'], 'ffn_conv_b': out['ffn_conv_b'], 'ffn_w_down': out['ffn_w_down'], 'ln_g': out['ln_g'], 'ln_b': out['ln_b'], 'loss_target': out['loss_target'], 'm_gla_w_in': out['m_gla_w_in'], 'm_gla_w_gate_up': out['m_gla_w_gate_up'], 'm_gla_gate_bias': out['m_gla_gate_bias'], 'm_gla_norm_g': out['m_gla_norm_g'], 'm_gla_w_out': out['m_gla_w_out'], 'm_dil_w_in': out['m_dil_w_in'], 'm_dil_w_out': out['m_dil_w_out'], 'm_ffn_w_up': out['m_ffn_w_up'], 'm_ffn_conv_w': out['m_ffn_conv_w'], 'm_ffn_conv_b': out['m_ffn_conv_b'], 'm_ffn_w_down': out['m_ffn_w_down'], 'm_ln_g': out['m_ln_g'], 'm_ln_b': out['m_ln_b'], 'v_gla_w_in': out['v_gla_w_in'], 'v_gla_w_gate_up': out['v_gla_w_gate_up'], 'v_gla_gate_bias': out['v_gla_gate_bias'], 'v_gla_norm_g': out['v_gla_norm_g'], 'v_gla_w_out': out['v_gla_w_out'], 'v_dil_w_in': out['v_dil_w_in'], 'v_dil_w_out': out['v_dil_w_out'], 'v_ffn_w_up': out['v_ffn_w_up'], 'v_ffn_conv_w': out['v_ffn_conv_w'], 'v_ffn_conv_b': out['v_ffn_conv_b'], 'v_ffn_w_down': out['v_ffn_w_down'], 'v_ln_g': out['v_ln_g'], 'v_ln_b': out['v_ln_b']}


def _loss(weights, diff, rest, loss_target):
    with _jax.named_scope("forward"):
        args = {**rest, TWIN_DIFF_INPUT: diff, **{k: w.astype(_WEIGHT_DTYPES[k]) for k, w in weights.items()}}
        y = _forward(args)
    with _jax.named_scope("loss_head"):
        err = _jnp.square(y.astype(_jnp.float32) - loss_target)
        return 0.5 * _jnp.sum(_jnp.mean(err, axis=-1)) if err.ndim else 0.5 * err


def _adamw(w, g, m, v):
    m = ADAM_B1 * m + (1.0 - ADAM_B1) * g
    v = ADAM_B2 * v + (1.0 - ADAM_B2) * _jnp.square(g)
    m_hat = m / (1.0 - ADAM_B1 ** ADAM_STEP)
    v_hat = v / (1.0 - ADAM_B2 ** ADAM_STEP)
    delta = -ADAM_LR * (m_hat / (_jnp.sqrt(v_hat) + ADAM_EPS) + ADAM_WD * w)
    return delta, m, v


def reference(x, gla_w_in, gla_w_gate_up, gla_gate_bias, gla_norm_g, gla_w_out, dil_w_in, dil_w_out, ffn_w_up, ffn_conv_w, ffn_conv_b, ffn_w_down, ln_g, ln_b, loss_target, m_gla_w_in, m_gla_w_gate_up, m_gla_gate_bias, m_gla_norm_g, m_gla_w_out, m_dil_w_in, m_dil_w_out, m_ffn_w_up, m_ffn_conv_w, m_ffn_conv_b, m_ffn_w_down, m_ln_g, m_ln_b, v_gla_w_in, v_gla_w_gate_up, v_gla_gate_bias, v_gla_norm_g, v_gla_w_out, v_dil_w_in, v_dil_w_out, v_ffn_w_up, v_ffn_conv_w, v_ffn_conv_b, v_ffn_w_down, v_ln_g, v_ln_b):
    given = dict(x=x, gla_w_in=gla_w_in, gla_w_gate_up=gla_w_gate_up, gla_gate_bias=gla_gate_bias, gla_norm_g=gla_norm_g, gla_w_out=gla_w_out, dil_w_in=dil_w_in, dil_w_out=dil_w_out, ffn_w_up=ffn_w_up, ffn_conv_w=ffn_conv_w, ffn_conv_b=ffn_conv_b, ffn_w_down=ffn_w_down, ln_g=ln_g, ln_b=ln_b, loss_target=loss_target, m_gla_w_in=m_gla_w_in, m_gla_w_gate_up=m_gla_w_gate_up, m_gla_gate_bias=m_gla_gate_bias, m_gla_norm_g=m_gla_norm_g, m_gla_w_out=m_gla_w_out, m_dil_w_in=m_dil_w_in, m_dil_w_out=m_dil_w_out, m_ffn_w_up=m_ffn_w_up, m_ffn_conv_w=m_ffn_conv_w, m_ffn_conv_b=m_ffn_conv_b, m_ffn_w_down=m_ffn_w_down, m_ln_g=m_ln_g, m_ln_b=m_ln_b, v_gla_w_in=v_gla_w_in, v_gla_w_gate_up=v_gla_w_gate_up, v_gla_gate_bias=v_gla_gate_bias, v_gla_norm_g=v_gla_norm_g, v_gla_w_out=v_gla_w_out, v_dil_w_in=v_dil_w_in, v_dil_w_out=v_dil_w_out, v_ffn_w_up=v_ffn_w_up, v_ffn_conv_w=v_ffn_conv_w, v_ffn_conv_b=v_ffn_conv_b, v_ffn_w_down=v_ffn_w_down, v_ln_g=v_ln_g, v_ln_b=v_ln_b)
    weights = {n: given[n] for n in TWIN_WEIGHTS}
    shared = {n: given[n] for n in SHARED_INPUTS}
    per_example = {n: given[n] for n in ['x']}
    grad_fn = _jax.value_and_grad(_loss, argnums=(0, 1))

    def one_microbatch(ex, loss_target):
        ex = dict(ex)
        diff = ex.pop(TWIN_DIFF_INPUT)
        return grad_fn(weights, diff, {**shared, **ex}, loss_target)

    if N_MICROBATCH == 1:
        loss, (grad_w, grad_x) = one_microbatch(per_example, given["loss_target"])
    else:
        def body(carry, xs):
            loss_sum, grad_sum = carry
            l_k, (gw_k, gx_k) = one_microbatch(xs[0], xs[1])
            with _jax.named_scope("update"):
                return (loss_sum + l_k, _jax.tree.map(_jnp.add, grad_sum, gw_k)), gx_k

        init = (_jnp.zeros((), _jnp.float32), _jax.tree.map(_jnp.zeros_like, weights))
        (loss, grad_w), grad_x = _jax.lax.scan(body, init, (per_example, given["loss_target"]))
    with _jax.named_scope("update"):
        delta_w, new_m, new_v = {}, {}, {}
        for n in TWIN_WEIGHTS:
            delta_w[n], new_m[n], new_v[n] = _adamw(weights[n], grad_w[n], given["m_" + n], given["v_" + n])
    return (loss, grad_x, *[grad_w[n] for n in TWIN_WEIGHTS], *[delta_w[n] for n in TWIN_WEIGHTS],
            *[new_m[n] for n in TWIN_WEIGHTS], *[new_v[n] for n in TWIN_WEIGHTS])
```

```python
import functools

import jax
import jax.numpy as jnp
from jax import lax
from jax.experimental import pallas as pl
from jax.experimental.pallas import tpu as pltpu

f32 = jnp.float32
bf16 = jnp.bfloat16
_MXU = jnp.bfloat16

DEPTH = 4
GLA_HEADS = 4
GLA_GATE_RANK = 16
GLA_GATE_NORMALIZER = 16.0
GLA_CHUNK = 64
GLA_SUB = 16
DIL_PATTERNS = ((128, 1), (512, 4), (2048, 16))
DIL_HEADS = 8
DIL_HEAD_DIM = 128
DIL_BLOCK = 128
DEEPNORM_ALPHA = (2 * DEPTH) ** 0.25
LN_EPS = 1e-5
RMS_EPS = 1e-6
ADAM_LR = 0.001
ADAM_B1 = 0.9
ADAM_B2 = 0.999
ADAM_EPS = 1e-08
ADAM_WD = 0.01
ADAM_STEP = 10

LANE = 128
VMEM_LIMIT = 48 * 1024 * 1024
GATE_PAD = LANE
FLAT_W = 1024
FLAT_ROW_ALIGN = 64
MESH = pl.DeviceIdType.MESH


def _cparams(sem=None):
    return pltpu.CompilerParams(dimension_semantics=sem, vmem_limit_bytes=VMEM_LIMIT)


def _div_tile(n, pref, unit):
    if n <= pref:
        return n
    best = None
    for t in range(unit, pref + 1, unit):
        if n % t == 0:
            best = t
    assert best is not None, (n, pref, unit)
    return best


def _dot(a, b, ca, cb):
    return lax.dot_general(a.astype(_MXU), b.astype(_MXU), (((ca,), (cb,)), ((), ())), preferred_element_type=f32)


def _nn(a, b):
    return _dot(a, b, 1, 0)


def _nt(a, b):
    return _dot(a, b, 1, 1)


def _tn(a, b):
    return _dot(a, b, 0, 0)


def _exact_dot(a, b):
    return jnp.dot(a, b, precision=lax.Precision.HIGHEST, preferred_element_type=f32)


def _sigmoid(x):
    return 1.0 / (1.0 + jnp.exp(-x))


def _mm(a, b, mode, name, tm=1024, tn=512, tk=2048, out_dtype=f32):
    if mode == "nn":
        (M, K), N = a.shape, b.shape[1]
    elif mode == "nt":
        (M, K), N = a.shape, b.shape[0]
    else:
        (K, M), N = a.shape, b.shape[1]
    tm, tn, tk = _div_tile(M, tm, LANE), _div_tile(N, tn, LANE), _div_tile(K, tk, LANE)
    nk = K // tk
    if mode == "tn":
        a_spec = pl.BlockSpec((tk, tm), lambda i, j, k: (k, i))
    else:
        a_spec = pl.BlockSpec((tm, tk), lambda i, j, k: (i, k))
    if mode == "nt":
        b_spec = pl.BlockSpec((tn, tk), lambda i, j, k: (j, k))
    else:
        b_spec = pl.BlockSpec((tk, tn), lambda i, j, k: (k, j))
    ca, cb = {"nn": (1, 0), "nt": (1, 1), "tn": (0, 0)}[mode]

    def body(a_ref, b_ref, o_ref, *acc):
        p = _dot(a_ref[...], b_ref[...], ca, cb)
        if nk == 1:
            o_ref[...] = p.astype(o_ref.dtype)
        else:
            k = pl.program_id(2)
            acc_ref = acc[0]

            @pl.when(k == 0)
            def _():
                acc_ref[...] = p

            @pl.when(k > 0)
            def _():
                acc_ref[...] += p

            @pl.when(k == nk - 1)
            def _():
                o_ref[...] = acc_ref[...].astype(o_ref.dtype)

    return pl.pallas_call(
        body, name=name, grid=(M // tm, N // tn, nk), in_specs=[a_spec, b_spec],
        out_specs=pl.BlockSpec((tm, tn), lambda i, j, k: (i, j)),
        out_shape=jax.ShapeDtypeStruct((M, N), out_dtype),
        scratch_shapes=[pltpu.VMEM((tm, tn), f32)] if nk > 1 else [],
        compiler_params=_cparams(("parallel", "parallel", "arbitrary")),
    )(a, b)


def _rowwise(fn, rows, consts, outs, reds, name, tm=256):
    T = rows[0][0].shape[0]
    tm = _div_tile(T, tm, 8)
    n_r, n_c, n_o = len(rows), len(consts), len(outs)

    def body(*refs):
        ins = [r[...] for r in refs[: n_r + n_c]]
        res = fn(*ins)
        res = res if isinstance(res, (tuple, list)) else (res,)
        o_refs = refs[n_r + n_c: n_r + n_c + n_o]
        r_refs = refs[n_r + n_c + n_o:]
        for ref, val in zip(o_refs, res[:n_o]):
            ref[...] = val.astype(ref.dtype)
        i = pl.program_id(0)
        for ref, val in zip(r_refs, res[n_o:]):
            _accumulate(ref, val, i)

    in_specs = [pl.BlockSpec((tm, w), functools.partial(lambda i, cb: (i, cb), cb=cb)) for (_, w, cb) in rows]
    in_specs += [pl.BlockSpec(c.shape, lambda i: (0, 0)) for c in consts]
    out_specs = [pl.BlockSpec((tm, w), lambda i: (i, 0)) for (w, _) in outs]
    out_specs += [pl.BlockSpec((1, w), lambda i: (0, 0)) for w in reds]
    out_shape = [jax.ShapeDtypeStruct((T, w), dt) for (w, dt) in outs]
    out_shape += [jax.ShapeDtypeStruct((1, w), f32) for w in reds]
    return pl.pallas_call(
        body, name=name, grid=(T // tm,), in_specs=in_specs, out_specs=out_specs, out_shape=out_shape,
        compiler_params=_cparams(("arbitrary",)),
    )(*[r[0] for r in rows], *consts)


def _accumulate(ref, val, step):
    @pl.when(step == 0)
    def _():
        ref[...] = val

    @pl.when(step > 0)
    def _():
        ref[...] += val


def _full(a):
    return (a, a.shape[1], 0)


def _colsum(x):
    return jnp.sum(x, axis=0, keepdims=True)


def _ln_stats(u):
    mu = jnp.mean(u, axis=-1, keepdims=True)
    xc = u - mu
    var = jnp.mean(xc * xc, axis=-1, keepdims=True)
    rstd = lax.rsqrt(var + LN_EPS)
    return xc * rstd, rstd


def _ln_fwd(x, f, g, b, name):
    def fn(x, f, g, b):
        xhat, _ = _ln_stats(DEEPNORM_ALPHA * x + f)
        return xhat * g + b

    return _rowwise(fn, [_full(x), _full(f)], [g, b], [(x.shape[1], f32)], [], name)[0]


def _ln_bwd(x, f, g, dys, scales, name):
    def fn(x, f, *rest):
        g = rest[-1]
        dy = None
        for d, s in zip(rest[:-1], scales):
            t = d if s == 1.0 else s * d
            dy = t if dy is None else dy + t
        xhat, rstd = _ln_stats(DEEPNORM_ALPHA * x + f)
        dxh = dy * g
        m1 = jnp.mean(dxh, axis=-1, keepdims=True)
        m2 = jnp.mean(dxh * xhat, axis=-1, keepdims=True)
        du = rstd * (dxh - m1 - xhat * m2)
        return du, _colsum(dy * xhat), _colsum(dy)

    D = x.shape[1]
    return _rowwise(fn, [_full(x), _full(f)] + [_full(d) for d in dys], [g], [(D, f32)], [D, D], name)


def _loss_head(y, t, name):
    D = y.shape[1]

    def fn(y, t):
        e = y - t
        return e * (1.0 / D), _colsum(e * e)

    return _rowwise(fn, [_full(y), _full(t)], [], [(D, f32)], [D], name)


def _axpy(a, b, alpha, name):
    def fn(a, b):
        return alpha * a + b

    return _rowwise(fn, [_full(a), _full(b)], [], [(a.shape[1], f32)], [], name)[0]


def _shift_down(x, k):
    row = lax.broadcasted_iota(jnp.int32, x.shape, 0)
    return jnp.where(row >= k, pltpu.roll(x, k, 0), 0.0)


def _shift_up(x, k):
    S = x.shape[0]
    row = lax.broadcasted_iota(jnp.int32, x.shape, 0)
    return jnp.where(row < S - k, pltpu.roll(x, S - k, 0), 0.0)


def _causal_conv(h, w, b):
    return ((b + w[0:1] * _shift_down(h, 2)) + w[1:2] * _shift_down(h, 1)) + w[2:3] * h


def _conv_gate_fwd(h, cw, cb, nseq, name, tc=256):
    T, F2 = h.shape
    F, S = F2 // 2, T // nseq
    tc = _div_tile(F, tc, LANE)
    nf = F // tc

    def body(hg_ref, hu_ref, wg_ref, wu_ref, bg_ref, bu_ref, a_ref):
        cg = _causal_conv(hg_ref[...], wg_ref[...], bg_ref[...])
        cu = _causal_conv(hu_ref[...], wu_ref[...], bu_ref[...])
        a_ref[...] = cg * _sigmoid(cg) * cu

    return pl.pallas_call(
        body, name=name, grid=(nseq, nf),
        in_specs=[pl.BlockSpec((S, tc), lambda s, j: (s, j)), pl.BlockSpec((S, tc), lambda s, j: (s, nf + j)),
                  pl.BlockSpec((3, tc), lambda s, j: (0, j)), pl.BlockSpec((3, tc), lambda s, j: (0, nf + j)),
                  pl.BlockSpec((1, tc), lambda s, j: (0, j)), pl.BlockSpec((1, tc), lambda s, j: (0, nf + j))],
        out_specs=pl.BlockSpec((S, tc), lambda s, j: (s, j)),
        out_shape=jax.ShapeDtypeStruct((T, F), f32),
        compiler_params=_cparams(("parallel", "parallel")),
    )(h, h, cw, cw, cb, cb)


def _conv_gate_bwd(h, da, cw, cb, nseq, name, tc=128):
    T, F2 = h.shape
    F, S = F2 // 2, T // nseq
    tc = _div_tile(F, tc, LANE)
    nf = F // tc

    def conv_bwd(dc, hx, w):
        dh = (w[2:3] * dc + w[1:2] * _shift_up(dc, 1)) + w[0:1] * _shift_up(dc, 2)
        dw = jnp.concatenate([_colsum(dc * _shift_down(hx, 2)), _colsum(dc * _shift_down(hx, 1)), _colsum(dc * hx)], axis=0)
        return dh, dw, _colsum(dc)

    def body(hg_ref, hu_ref, da_ref, wg_ref, wu_ref, bg_ref, bu_ref, dhg_ref, dhu_ref, dwg_ref, dwu_ref, dbg_ref, dbu_ref):
        hg, hu, da = hg_ref[...], hu_ref[...], da_ref[...]
        wg, wu = wg_ref[...], wu_ref[...]
        cg = _causal_conv(hg, wg, bg_ref[...])
        cu = _causal_conv(hu, wu, bu_ref[...])
        sg = _sigmoid(cg)
        dcu = da * (cg * sg)
        dcg = da * cu * (sg * (1.0 + cg * (1.0 - sg)))
        dhg, dwg, dbg = conv_bwd(dcg, hg, wg)
        dhu, dwu, dbu = conv_bwd(dcu, hu, wu)
        dhg_ref[...] = dhg
        dhu_ref[...] = dhu
        s = pl.program_id(1)
        _accumulate(dwg_ref, dwg, s)
        _accumulate(dwu_ref, dwu, s)
        _accumulate(dbg_ref, dbg, s)
        _accumulate(dbu_ref, dbu, s)

    col = lambda j, s: (s, j)
    par = lambda j, s: (0, j)
    dhg, dhu, dwg, dwu, dbg, dbu = pl.pallas_call(
        body, name=name, grid=(nf, nseq),
        in_specs=[pl.BlockSpec((S, tc), col), pl.BlockSpec((S, tc), lambda j, s: (s, nf + j)), pl.BlockSpec((S, tc), col),
                  pl.BlockSpec((3, tc), par), pl.BlockSpec((3, tc), lambda j, s: (0, nf + j)),
                  pl.BlockSpec((1, tc), par), pl.BlockSpec((1, tc), lambda j, s: (0, nf + j))],
        out_specs=[pl.BlockSpec((S, tc), col), pl.BlockSpec((S, tc), col), pl.BlockSpec((3, tc), par), pl.BlockSpec((3, tc), par),
                   pl.BlockSpec((1, tc), par), pl.BlockSpec((1, tc), par)],
        out_shape=[jax.ShapeDtypeStruct((T, F), f32), jax.ShapeDtypeStruct((T, F), f32), jax.ShapeDtypeStruct((3, F), f32),
                   jax.ShapeDtypeStruct((3, F), f32), jax.ShapeDtypeStruct((1, F), f32), jax.ShapeDtypeStruct((1, F), f32)],
        compiler_params=_cparams(("parallel", "arbitrary")),
    )(h, h, da, cw, cw, cb, cb)
    return jnp.concatenate([dhg, dhu], axis=1), jnp.concatenate([dwg, dwu], axis=1), jnp.concatenate([dbg, dbu], axis=1)


def _group_row(x, jj):
    C, d = x.shape
    n = C // GLA_SUB
    x3 = x.reshape(n, GLA_SUB, d)
    return jnp.broadcast_to(x3[:, jj:jj + 1, :], (n, GLA_SUB, d)).reshape(C, d)


def _group_sum(x):
    C, d = x.shape
    n = C // GLA_SUB
    s = jnp.sum(x.reshape(n, GLA_SUB, d), axis=1, keepdims=True)
    return jnp.broadcast_to(s, (n, GLA_SUB, d)).reshape(C, d)


def _chunk_cumsum(g):
    C = g.shape[0]
    row = lax.broadcasted_iota(jnp.int32, (C, C), 0)
    col = lax.broadcasted_iota(jnp.int32, (C, C), 1)
    return _exact_dot((row >= col).astype(f32), g)


def _chunk_suffix_sum(x):
    C = x.shape[0]
    row = lax.broadcasted_iota(jnp.int32, (C, C), 0)
    col = lax.broadcasted_iota(jnp.int32, (C, C), 1)
    return _exact_dot((col >= row).astype(f32), x)


def _gla_scores(q, k, b):
    C = q.shape[0]
    n = C // GLA_SUB
    row = lax.broadcasted_iota(jnp.int32, (C, C), 0)
    col = lax.broadcasted_iota(jnp.int32, (C, C), 1)
    blocks = [jnp.zeros((GLA_SUB, C), f32)]
    for s in range(1, n):
        lo = s * GLA_SUB
        bref = b[lo - 1:lo, :]
        qr = q[lo:lo + GLA_SUB] * jnp.exp(b[lo:lo + GLA_SUB] - bref)
        kr = k * jnp.exp(jnp.minimum(bref - b, 0.0))
        blocks.append(_nt(qr, kr))
    sub_start = (row // GLA_SUB) * GLA_SUB
    a = jnp.where(col < sub_start, jnp.concatenate(blocks, axis=0), 0.0)
    rin = lax.broadcasted_iota(jnp.int32, (C, 1), 0) % GLA_SUB
    for jj in range(GLA_SUB):
        e = jnp.exp(jnp.minimum(b - _group_row(b, jj), 0.0))
        colv = jnp.sum(q * _group_row(k, jj) * e, axis=1, keepdims=True)
        colv = jnp.where(rin >= jj, colv, 0.0)
        a = jnp.where(col == sub_start + jj, colv, a)
    return a


def _gla_scores_bwd(da, q, k, b):
    C = q.shape[0]
    n = C // GLA_SUB
    row = lax.broadcasted_iota(jnp.int32, (C, C), 0)
    col = lax.broadcasted_iota(jnp.int32, (C, C), 1)
    sub_start = (row // GLA_SUB) * GLA_SUB
    da_inter = jnp.where(col < sub_start, da, 0.0)
    dq_blocks = [jnp.zeros((GLA_SUB, q.shape[1]), f32)]
    dk = jnp.zeros_like(k)
    for s in range(1, n):
        lo = s * GLA_SUB
        bref = b[lo - 1:lo, :]
        eq = jnp.exp(b[lo:lo + GLA_SUB] - bref)
        ek = jnp.exp(jnp.minimum(bref - b, 0.0))
        das = da_inter[lo:lo + GLA_SUB]
        dq_blocks.append(_nn(das, k * ek) * eq)
        dk = dk + _tn(das, q[lo:lo + GLA_SUB] * eq) * ek
    dq = jnp.concatenate(dq_blocks, axis=0)
    rin = lax.broadcasted_iota(jnp.int32, (C, 1), 0) % GLA_SUB
    for jj in range(GLA_SUB):
        e = jnp.exp(jnp.minimum(b - _group_row(b, jj), 0.0))
        dac = jnp.sum(jnp.where(col == sub_start + jj, da, 0.0), axis=1, keepdims=True)
        dac = jnp.where(rin >= jj, dac, 0.0)
        w = dac * e
        dq = dq + w * _group_row(k, jj)
        dk = dk + jnp.where(rin == jj, _group_sum(w * q), 0.0)
    return dq, dk


def _gla_specs(nC, dk, dv):
    H = GLA_HEADS
    voff = (2 * H * dk) // dv
    assert voff * dv == 2 * H * dk
    return H, voff


def _gla_fwd(proj, gate, nseq, name):
    T = proj.shape[0]
    dk = gate.shape[1] // GLA_HEADS
    dv = 2 * dk
    C = GLA_CHUNK
    nC = T // nseq // C
    H, voff = _gla_specs(nC, dk, dv)
    scale = dk ** -0.5

    def body(q_ref, k_ref, v_ref, g_ref, o_ref, st_ref, state):
        c = pl.program_id(2)

        @pl.when(c == 0)
        def _():
            state[...] = jnp.zeros_like(state)

        q, k, v = q_ref[...] * scale, k_ref[...], v_ref[...]
        b = _chunk_cumsum(g_ref[...])
        st = state[...]
        st_ref[0] = st
        a = _gla_scores(q, k, b)
        o_ref[...] = _nt(q * jnp.exp(b), st) + _nn(a, v)
        bl = b[C - 1:C, :]
        state[...] = st * jnp.exp(bl) + _tn(v, k * jnp.exp(bl - b))

    row = lambda s, h, c: s * nC + c
    return pl.pallas_call(
        body, name=name, grid=(nseq, H, nC),
        in_specs=[pl.BlockSpec((C, dk), lambda s, h, c: (row(s, h, c), h)),
                  pl.BlockSpec((C, dk), lambda s, h, c: (row(s, h, c), H + h)),
                  pl.BlockSpec((C, dv), lambda s, h, c: (row(s, h, c), voff + h)),
                  pl.BlockSpec((C, dk), lambda s, h, c: (row(s, h, c), h))],
        out_specs=[pl.BlockSpec((C, dv), lambda s, h, c: (row(s, h, c), h)),
                   pl.BlockSpec((1, dv, dk), lambda s, h, c: ((s * H + h) * nC + c, 0, 0))],
        out_shape=[jax.ShapeDtypeStruct((T, H * dv), f32), jax.ShapeDtypeStruct((nseq * H * nC, dv, dk), f32)],
        scratch_shapes=[pltpu.VMEM((dv, dk), f32)],
        compiler_params=_cparams(("parallel", "parallel", "arbitrary")),
    )(proj, proj, proj, gate)


def _gla_bwd(proj, gate, states, do, nseq, name):
    T = proj.shape[0]
    dk = gate.shape[1] // GLA_HEADS
    dv = 2 * dk
    C = GLA_CHUNK
    nC = T // nseq // C
    H, voff = _gla_specs(nC, dk, dv)
    scale = dk ** -0.5

    def body(q_ref, k_ref, v_ref, g_ref, do_ref, st_ref, dq_ref, dk_ref, dv_ref, dg_ref, dstate, term):
        c = pl.program_id(2)

        @pl.when(c == 0)
        def _():
            dstate[...] = jnp.zeros_like(dstate)
            term[...] = jnp.zeros_like(term)

        q, k, v, do = q_ref[...] * scale, k_ref[...], v_ref[...], do_ref[...]
        b = _chunk_cumsum(g_ref[...])
        st = st_ref[0]
        dst = dstate[...]
        eb = jnp.exp(b)
        bl = b[C - 1:C, :]
        kdec = jnp.exp(bl - b)
        a = _gla_scores(q, k, b)
        rowi = lax.broadcasted_iota(jnp.int32, (C, C), 0)
        coli = lax.broadcasted_iota(jnp.int32, (C, C), 1)
        da = jnp.where(coli <= rowi, _nt(do, v), 0.0)
        dq_s, dk_s = _gla_scores_bwd(da, q, k, b)
        dq = _nn(do, st) * eb + dq_s
        dkk = _nn(v, dst) * kdec + dk_s
        dv_ref[...] = _tn(a, do) + _nt(k * kdec, dst)
        last = lax.broadcasted_iota(jnp.int32, (C, 1), 0) == C - 1
        db = q * dq - k * dkk + jnp.where(last, term[...], 0.0)
        dg_ref[...] = _chunk_suffix_sum(db)
        dq_ref[...] = dq * scale
        dk_ref[...] = dkk
        dprev = dst * jnp.exp(bl) + _tn(do, q * eb)
        dstate[...] = dprev
        term[...] = _colsum(st * dprev)

    row = lambda s, h, c: s * nC + (nC - 1 - c)
    kspec = lambda off: pl.BlockSpec((C, dk), lambda s, h, c: (row(s, h, c), off + h))
    vspec = lambda off: pl.BlockSpec((C, dv), lambda s, h, c: (row(s, h, c), off + h))
    return pl.pallas_call(
        body, name=name, grid=(nseq, H, nC),
        in_specs=[kspec(0), kspec(H), vspec(voff), kspec(0), vspec(0),
                  pl.BlockSpec((1, dv, dk), lambda s, h, c: ((s * H + h) * nC + (nC - 1 - c), 0, 0))],
        out_specs=[kspec(0), kspec(0), vspec(0), kspec(0)],
        out_shape=[jax.ShapeDtypeStruct((T, H * dk), f32), jax.ShapeDtypeStruct((T, H * dk), f32),
                   jax.ShapeDtypeStruct((T, H * dv), f32), jax.ShapeDtypeStruct((T, H * dk), f32)],
        scratch_shapes=[pltpu.VMEM((dv, dk), f32), pltpu.VMEM((1, dk), f32)],
        compiler_params=_cparams(("parallel", "parallel", "arbitrary")),
    )(proj, proj, proj, gate, do, states)


def _head_slices(width, n):
    w = width // n
    return [slice(h * w, (h + 1) * w) for h in range(n)]


def _rms_gate_fwd(o, proj, ng, name):
    W = o.shape[1]

    def fn(o, r, ng):
        parts = []
        for sl in _head_slices(W, GLA_HEADS):
            oh = o[:, sl]
            rstd = lax.rsqrt(jnp.mean(oh * oh, axis=-1, keepdims=True) + RMS_EPS)
            rh = r[:, sl]
            parts.append((oh * rstd * ng) * (rh * _sigmoid(rh)))
        return jnp.concatenate(parts, axis=1)

    return _rowwise(fn, [_full(o), (proj, W, 2)], [ng], [(W, f32)], [], name)[0]


def _rms_gate_bwd(o, proj, ng, dy, name):
    W = o.shape[1]

    def fn(o, r, dy, ng):
        dos, drs = [], []
        dng = jnp.zeros((1, W // GLA_HEADS), f32)
        for sl in _head_slices(W, GLA_HEADS):
            oh, rh, dyh = o[:, sl], r[:, sl], dy[:, sl]
            rstd = lax.rsqrt(jnp.mean(oh * oh, axis=-1, keepdims=True) + RMS_EPS)
            ohat = oh * rstd
            sg = _sigmoid(rh)
            don = dyh * (rh * sg)
            drs.append(dyh * (ohat * ng) * (sg * (1.0 + rh * (1.0 - sg))))
            dng = dng + _colsum(don * ohat)
            dohat = don * ng
            dos.append(rstd * (dohat - ohat * jnp.mean(dohat * ohat, axis=-1, keepdims=True)))
        return jnp.concatenate(dos, axis=1), jnp.concatenate(drs, axis=1), dng

    return _rowwise(fn, [_full(o), (proj, W, 2), _full(dy)], [ng], [(W, f32), (W, f32)], [W // GLA_HEADS], name)


def _log_gate_fwd(z, bias, name):
    def fn(z, bias):
        t = z + bias
        return (jnp.minimum(t, 0.0) - jnp.log1p(jnp.exp(-jnp.abs(t)))) * (1.0 / GLA_GATE_NORMALIZER)

    return _rowwise(fn, [_full(z)], [bias], [(z.shape[1], f32)], [], name)[0]


def _log_gate_bwd(z, bias, dg, name):
    def fn(z, dg, bias):
        dz = dg * (1.0 / GLA_GATE_NORMALIZER) * _sigmoid(-(z + bias))
        return dz, _colsum(dz)

    return _rowwise(fn, [_full(z), _full(dg)], [bias], [(z.shape[1], f32)], [z.shape[1]], name)


def _band_masks(P, steps, has_prev):
    i = lax.broadcasted_iota(jnp.int32, (P, P), 0)
    j = lax.broadcasted_iota(jnp.int32, (P, P), 1)
    cur = (i - j >= 0) & (i - j <= steps)
    prev = (i + P - j <= steps) & has_prev
    return cur, prev


def _dil_dims(T, nseq, dilation):
    L = T // nseq // dilation
    P = min(DIL_BLOCK, L)
    return L, P, L // P


def _dil_fwd(proj, gi, window, dilation, nseq, name):
    T, W3 = proj.shape
    W = DIL_HEADS * DIL_HEAD_DIM
    nblk = W3 // W
    L, P, nb = _dil_dims(T, nseq, dilation)
    steps = window // dilation
    scale = DIL_HEAD_DIM ** -0.5
    pv = proj.reshape(nseq, L, dilation * W3)

    def body(q_ref, kc_ref, kp_ref, vc_ref, vp_ref, o_ref, lse_ref):
        lb = pl.program_id(2)
        mc, mp = _band_masks(P, steps, lb > 0)
        lane = lax.broadcasted_iota(jnp.int32, (P, LANE), 1)
        lse_t = jnp.zeros((P, LANE), f32)
        for h, sl in enumerate(_head_slices(W, DIL_HEADS)):
            q = q_ref[0, :, sl]
            sc = jnp.where(mc, _nt(q, kc_ref[0, :, sl]) * scale, -jnp.inf)
            sp = jnp.where(mp, _nt(q, kp_ref[0, :, sl]) * scale, -jnp.inf)
            m = jnp.maximum(jnp.max(sc, axis=-1, keepdims=True), jnp.max(sp, axis=-1, keepdims=True))
            pc, pp = jnp.exp(sc - m), jnp.exp(sp - m)
            l = jnp.sum(pc, axis=-1, keepdims=True) + jnp.sum(pp, axis=-1, keepdims=True)
            o_ref[0, :, sl] = _nn(pc / l, vc_ref[0, :, sl]) + _nn(pp / l, vp_ref[0, :, sl])
            lse_t = jnp.where(lane == h, m + jnp.log(l), lse_t)
        lse_ref[0] = lse_t

    base = gi * 3
    cur = lambda part: pl.BlockSpec((1, P, W), lambda s, r, lb: (s, lb, r * nblk + base + part))
    prv = lambda part: pl.BlockSpec((1, P, W), lambda s, r, lb: (s, jnp.maximum(lb - 1, 0), r * nblk + base + part))
    o, lse = pl.pallas_call(
        body, name=name, grid=(nseq, dilation, nb),
        in_specs=[cur(0), cur(1), prv(1), cur(2), prv(2)],
        out_specs=[pl.BlockSpec((1, P, W), lambda s, r, lb: (s, lb, r)), pl.BlockSpec((1, P, LANE), lambda s, r, lb: (s, lb, r))],
        out_shape=[jax.ShapeDtypeStruct((nseq, L, dilation * W), f32), jax.ShapeDtypeStruct((nseq, L, dilation * LANE), f32)],
        compiler_params=_cparams(("parallel", "parallel", "parallel")),
    )(pv, pv, pv, pv, pv)
    return o.reshape(T, W), lse.reshape(T, LANE)


def _dil_mix_fwd(os_, lses, name):
    W = os_[0].shape[1]
    G = len(os_)

    def fn(*a):
        o, l = a[:G], a[G:]
        lane = lax.broadcasted_iota(jnp.int32, l[0].shape, 1)
        tot = jnp.zeros(l[0].shape, f32)
        parts = []
        for h, sl in enumerate(_head_slices(W, DIL_HEADS)):
            lh = [x[:, h:h + 1] for x in l]
            m = functools.reduce(jnp.maximum, lh)
            e = [jnp.exp(x - m) for x in lh]
            z = functools.reduce(lambda u, v: u + v, e)
            acc = None
            for g in range(G):
                t = (e[g] / z) * o[g][:, sl]
                acc = t if acc is None else acc + t
            parts.append(acc)
            tot = jnp.where(lane == h, m + jnp.log(z), tot)
        return jnp.concatenate(parts, axis=1), tot

    return _rowwise(fn, [_full(x) for x in os_] + [_full(x) for x in lses], [], [(W, f32), (LANE, f32)], [], name)


def _dil_delta(do, o, name):
    W = o.shape[1]

    def fn(do, o):
        lane = lax.broadcasted_iota(jnp.int32, (do.shape[0], LANE), 1)
        d = jnp.zeros((do.shape[0], LANE), f32)
        for h, sl in enumerate(_head_slices(W, DIL_HEADS)):
            d = jnp.where(lane == h, jnp.sum(do[:, sl] * o[:, sl], axis=-1, keepdims=True), d)
        return d

    return _rowwise(fn, [_full(do), _full(o)], [], [(LANE, f32)], [], name)[0]


def _dil_bwd(proj, do, lse, delta, gi, window, dilation, nseq, name):
    T, W3 = proj.shape
    W = DIL_HEADS * DIL_HEAD_DIM
    nblk = W3 // W
    L, P, nb = _dil_dims(T, nseq, dilation)
    steps = window // dilation
    scale = DIL_HEAD_DIM ** -0.5
    pv = proj.reshape(nseq, L, dilation * W3)
    dov = do.reshape(nseq, L, dilation * W)
    lsev = lse.reshape(nseq, L, dilation * LANE)
    delv = delta.reshape(nseq, L, dilation * LANE)
    base = gi * 3

    def probs(q, k, lse_h, mask):
        return jnp.where(mask, jnp.exp(_nt(q, k) * scale - lse_h), 0.0)

    def dq_body(q_ref, kc_ref, kp_ref, vc_ref, vp_ref, do_ref, lse_ref, del_ref, dq_ref):
        mc, mp = _band_masks(P, steps, pl.program_id(2) > 0)
        for h, sl in enumerate(_head_slices(W, DIL_HEADS)):
            q, doh = q_ref[0, :, sl], do_ref[0, :, sl]
            lse_h, del_h = lse_ref[0, :, h:h + 1], del_ref[0, :, h:h + 1]
            kc, kp = kc_ref[0, :, sl], kp_ref[0, :, sl]
            dsc = probs(q, kc, lse_h, mc) * (_nt(doh, vc_ref[0, :, sl]) - del_h) * scale
            dsp = probs(q, kp, lse_h, mp) * (_nt(doh, vp_ref[0, :, sl]) - del_h) * scale
            dq_ref[0, :, sl] = _nn(dsc, kc) + _nn(dsp, kp)

    cur = lambda part: pl.BlockSpec((1, P, W), lambda s, r, lb: (s, lb, r * nblk + base + part))
    prv = lambda part: pl.BlockSpec((1, P, W), lambda s, r, lb: (s, jnp.maximum(lb - 1, 0), r * nblk + base + part))
    tok = lambda w: pl.BlockSpec((1, P, w), lambda s, r, lb: (s, lb, r))
    dq = pl.pallas_call(
        dq_body, name=name + "_dq", grid=(nseq, dilation, nb),
        in_specs=[cur(0), cur(1), prv(1), cur(2), prv(2), tok(W), tok(LANE), tok(LANE)],
        out_specs=tok(W), out_shape=jax.ShapeDtypeStruct((nseq, L, dilation * W), f32),
        compiler_params=_cparams(("parallel", "parallel", "parallel")),
    )(pv, pv, pv, pv, pv, dov, lsev, delv)

    def dkv_body(k_ref, v_ref, qc_ref, qn_ref, doc_ref, don_ref, lsec_ref, lsen_ref, delc_ref, deln_ref, dk_ref, dv_ref):
        lb = pl.program_id(2)
        mc, mn = _band_masks(P, steps, lb < nb - 1)
        for h, sl in enumerate(_head_slices(W, DIL_HEADS)):
            k, v = k_ref[0, :, sl], v_ref[0, :, sl]
            hs = slice(h, h + 1)
            qc, doc = qc_ref[0, :, sl], doc_ref[0, :, sl]
            pc = probs(qc, k, lsec_ref[0, :, hs], mc)
            dsc = pc * (_nt(doc, v) - delc_ref[0, :, hs]) * scale
            qn, don = qn_ref[0, :, sl], don_ref[0, :, sl]
            pn = probs(qn, k, lsen_ref[0, :, hs], mn)
            dsn = pn * (_nt(don, v) - deln_ref[0, :, hs]) * scale
            dk_ref[0, :, sl] = _tn(dsc, qc) + _tn(dsn, qn)
            dv_ref[0, :, sl] = _tn(pc, doc) + _tn(pn, don)

    nxt_i = lambda lb: jnp.minimum(lb + 1, nb - 1)
    qnx = pl.BlockSpec((1, P, W), lambda s, r, lb: (s, nxt_i(lb), r * nblk + base))
    tokn = lambda w: pl.BlockSpec((1, P, w), lambda s, r, lb: (s, nxt_i(lb), r))
    dkk, dvv = pl.pallas_call(
        dkv_body, name=name + "_dkv", grid=(nseq, dilation, nb),
        in_specs=[cur(1), cur(2), cur(0), qnx, tok(W), tokn(W), tok(LANE), tokn(LANE), tok(LANE), tokn(LANE)],
        out_specs=[tok(W), tok(W)],
        out_shape=[jax.ShapeDtypeStruct((nseq, L, dilation * W), f32), jax.ShapeDtypeStruct((nseq, L, dilation * W), f32)],
        compiler_params=_cparams(("parallel", "parallel", "parallel")),
    )(pv, pv, pv, pv, dov, dov, lsev, lsev, delv, delv)
    return dq.reshape(T, W), dkk.reshape(T, W), dvv.reshape(T, W)


def _adamw(w, g, m, v, name):
    def fn(w, g, m, v):
        m = ADAM_B1 * m + (1.0 - ADAM_B1) * g
        v = ADAM_B2 * v + (1.0 - ADAM_B2) * (g * g)
        m_hat = m / (1.0 - ADAM_B1 ** ADAM_STEP)
        v_hat = v / (1.0 - ADAM_B2 ** ADAM_STEP)
        return -ADAM_LR * (m_hat / (jnp.sqrt(v_hat) + ADAM_EPS) + ADAM_WD * w), m, v

    W = w.shape[1]
    return _rowwise(fn, [_full(w), _full(g), _full(m), _full(v)], [], [(W, f32)] * 3, [], name, tm=512)


def _position():
    x, y, c = lax.axis_index("x"), lax.axis_index("y"), lax.axis_index("c")
    other_chips = [(1 - x, y), (x, 1 - y), (1 - x, 1 - y)]
    return x, y, c, other_chips


def _chip_index(x, y):
    return 2 * x + y


_ANY = pl.BlockSpec(memory_space=pl.ANY)


def _all_reduce_small(p, name):
    R, Wd = p.shape

    def body(p_ref, o_ref, buf, send_sems, recv_sems):
        x, y, c, _ = _position()
        me = 4 * x + 2 * y + c
        buf[me] = p_ref[...]
        copies = []
        for k in range(1, 8):
            fx, fy, fc = (k >> 2) & 1, (k >> 1) & 1, k & 1
            peer = (x + fx - 2 * x * fx, y + fy - 2 * y * fy, c + fc - 2 * c * fc)
            cp = pltpu.make_async_remote_copy(src_ref=p_ref, dst_ref=buf.at[me], send_sem=send_sems.at[k - 1],
                                              recv_sem=recv_sems.at[k - 1], device_id=peer, device_id_type=MESH)
            cp.start()
            copies.append(cp)
        for cp in copies:
            cp.wait()
        acc = buf[0]
        for s in range(1, 8):
            acc = acc + buf[s]
        o_ref[...] = acc

    return pl.pallas_call(
        body, name=name, out_shape=jax.ShapeDtypeStruct((R, Wd), f32),
        in_specs=[pl.BlockSpec(memory_space=pltpu.VMEM)], out_specs=pl.BlockSpec(memory_space=pltpu.VMEM),
        scratch_shapes=[pltpu.VMEM((8, R, Wd), f32), pltpu.SemaphoreType.DMA((7,)), pltpu.SemaphoreType.DMA((7,))],
        compiler_params=pltpu.CompilerParams(vmem_limit_bytes=VMEM_LIMIT),
    )(p)


def _all_gather_chips(w, name):
    R, Cw = w.shape
    hr = R // 2

    def body(w_ref, o_ref, send_sems, recv_sems, local_sem):
        x, y, c, chips = _position()
        p = _chip_index(x, y)
        sibling = (x, y, 1 - c)
        mine = pltpu.make_async_copy(w_ref, o_ref.at[p], local_sem)
        mine.start()

        def half(ref, h):
            return ref.at[pl.ds(pl.multiple_of(h * hr, 16), hr), :]

        def copy(k, src, chip_idx, h, to):
            return pltpu.make_async_remote_copy(src_ref=src, dst_ref=half(o_ref.at[chip_idx], h), send_sem=send_sems.at[k],
                                                recv_sem=recv_sems.at[k], device_id=to, device_id_type=MESH)

        first = [copy(j, half(w_ref, c), p, c, (*chip, c)) for j, chip in enumerate(chips)]
        for cp in first:
            cp.start()
        passed = []
        for j, chip in enumerate(chips):
            q = _chip_index(*chip)
            copy(j, half(w_ref, c), q, c, (*chip, c)).wait_recv()
            fwd = copy(3 + j, half(o_ref.at[q], c), q, c, sibling)
            fwd.start()
            passed.append(fwd)
        for j, chip in enumerate(chips):
            q = _chip_index(*chip)
            copy(3 + j, half(w_ref, c), q, 1 - c, sibling).wait_recv()
        for cp in first + passed:
            cp.wait_send()
        mine.wait()

    return pl.pallas_call(
        body, name=name, out_shape=jax.ShapeDtypeStruct((4, R, Cw), w.dtype), in_specs=[_ANY], out_specs=_ANY,
        scratch_shapes=[pltpu.SemaphoreType.DMA((6,)), pltpu.SemaphoreType.DMA((6,)), pltpu.SemaphoreType.DMA],
    )(w)


def _sibling_halves(g, name):
    _, R, Cw = g.shape
    hr = R // 2

    def body(g_ref, o_ref, send_sem, recv_sem):
        x, y, c, _ = _position()
        cp = pltpu.make_async_remote_copy(src_ref=g_ref.at[:, pl.ds(pl.multiple_of((1 - c) * hr, 16), hr), :], dst_ref=o_ref, send_sem=send_sem,
                                          recv_sem=recv_sem, device_id=(x, y, 1 - c), device_id_type=MESH)
        cp.start()
        cp.wait()

    return pl.pallas_call(
        body, name=name, out_shape=jax.ShapeDtypeStruct((4, hr, Cw), g.dtype), in_specs=[_ANY], out_specs=_ANY,
        scratch_shapes=[pltpu.SemaphoreType.DMA, pltpu.SemaphoreType.DMA],
    )(g)


def _chip_exchange(hsum, name):
    _, R2, Cw = hsum.shape

    def body(h_ref, o_ref, send_sems, recv_sems):
        x, y, c, chips = _position()
        copies = []
        for j, chip in enumerate(chips):
            cp = pltpu.make_async_remote_copy(src_ref=h_ref.at[_chip_index(*chip)], dst_ref=o_ref.at[j], send_sem=send_sems.at[j],
                                              recv_sem=recv_sems.at[j], device_id=(*chip, c), device_id_type=MESH)
            cp.start()
            copies.append(cp)
        for cp in copies:
            cp.wait()

    return pl.pallas_call(
        body, name=name, out_shape=jax.ShapeDtypeStruct((3, R2, Cw), hsum.dtype), in_specs=[_ANY], out_specs=_ANY,
        scratch_shapes=[pltpu.SemaphoreType.DMA((3,)), pltpu.SemaphoreType.DMA((3,))],
    )(hsum)


def _sibling_swap(t, name):
    def body(t_ref, o_ref, send_sem, recv_sem):
        x, y, c, _ = _position()
        cp = pltpu.make_async_remote_copy(src_ref=t_ref, dst_ref=o_ref, send_sem=send_sem, recv_sem=recv_sem,
                                          device_id=(x, y, 1 - c), device_id_type=MESH)
        cp.start()
        cp.wait()

    return pl.pallas_call(
        body, name=name, out_shape=jax.ShapeDtypeStruct(t.shape, t.dtype), in_specs=[_ANY], out_specs=_ANY,
        scratch_shapes=[pltpu.SemaphoreType.DMA, pltpu.SemaphoreType.DMA],
    )(t)


def _reduce_scatter_chips(g, name):
    _, R, Cw = g.shape
    hr = R // 2
    c = lax.axis_index("c")
    p = _chip_index(lax.axis_index("x"), lax.axis_index("y"))
    from_sibling = _sibling_halves(g, name + "_d2d")
    own_half = lax.dynamic_slice_in_dim(g, c * hr, hr, axis=1)

    def add_fn(a, b):
        return a.astype(f32) + b.astype(f32)

    hsum = _rowwise(add_fn, [_full(own_half.reshape(4 * hr, Cw)), _full(from_sibling.reshape(4 * hr, Cw))], [],
                    [(Cw, g.dtype)], [], name + "_add2", tm=512)[0].reshape(4, hr, Cw)
    got = _chip_exchange(hsum, name + "_ici")
    mine = lax.dynamic_index_in_dim(hsum, p, axis=0, keepdims=False)

    def add4(a, b0, b1, b2):
        return ((a.astype(f32) + b0.astype(f32)) + b1.astype(f32)) + b2.astype(f32)

    total_half = _rowwise(add4, [_full(mine), _full(got[0]), _full(got[1]), _full(got[2])], [], [(Cw, f32)], [],
                          name + "_add4", tm=512)[0]
    other_half = _sibling_swap(total_half, name + "_swap")
    lo = jnp.where(c == 0, total_half, other_half)
    hi = jnp.where(c == 0, other_half, total_half)
    return jnp.concatenate([lo, hi], axis=0)


_BIG = (("gla_w_in", 1), ("gla_w_out", 0), ("dil_w_in", 1), ("dil_w_out", 1), ("ffn_w_up", 1), ("ffn_w_down", 0))


def _pad_rows(a, mult):
    r = (-a.shape[0]) % mult
    return a if r == 0 else jnp.concatenate([a, jnp.zeros((r,) + a.shape[1:], a.dtype)], axis=0)


def _unshard(blocks, axis):
    _, nl, K, N = blocks.shape
    if axis == 1:
        return blocks.transpose(1, 2, 0, 3).reshape(nl, K, 4 * N)
    return blocks.transpose(1, 0, 2, 3).reshape(nl, 4 * K, N)


def _to_shards(full, axis):
    nl, K, N = full.shape
    if axis == 1:
        return full.reshape(nl, K, 4, N // 4).transpose(2, 0, 1, 3)
    return full.reshape(nl, 4, K // 4, N).transpose(1, 0, 2, 3)


def _gather_big_weights(shards):
    flat = _pad_rows(jnp.concatenate([shards[n].astype(bf16).reshape(-1, FLAT_W) for n, _ in _BIG], axis=0), FLAT_ROW_ALIGN)
    allw = _all_gather_chips(flat, "gather_weights")
    out, off = {}, 0
    for n, axis in _BIG:
        rows = shards[n].size // FLAT_W
        out[n] = _unshard(allw[:, off:off + rows].reshape((4,) + shards[n].shape), axis)
        off += rows
    return out


def _scatter_big_grads(grads, shards):
    parts = [_to_shards(grads[n], axis).astype(bf16).reshape(4, -1, FLAT_W) for n, axis in _BIG]
    flat = jnp.concatenate(parts, axis=1)
    pad = (-flat.shape[1]) % FLAT_ROW_ALIGN
    if pad:
        flat = jnp.concatenate([flat, jnp.zeros((4, pad, FLAT_W), bf16)], axis=1)
    total = _reduce_scatter_chips(flat, "scatter_grads")
    out, off = {}, 0
    for n, _ in _BIG:
        rows = shards[n].size // FLAT_W
        out[n] = total[off:off + rows].reshape(shards[n].shape)
        off += rows
    return out


_SMALL = (("gla_w_gate_up", 2), ("gla_gate_bias", None), ("gla_norm_g", None), ("ffn_conv_w", 2), ("ffn_conv_b", None),
          ("ln_g", 2), ("ln_b", 2))


def _pack_rows(arrs, width=LANE):
    flat = _pad_rows(jnp.concatenate([a.reshape(-1) for a in arrs]), 8 * width)
    return flat.reshape(-1, width)


def _unpack_rows(packed, shapes):
    flat, out, off = packed.reshape(-1), [], 0
    for s in shapes:
        n = 1
        for d in s:
            n *= d
        out.append(flat[off:off + n].reshape(s))
        off += n
    return out


def _gather_small_params(shards):
    x, y, c = lax.axis_index("x"), lax.axis_index("y"), lax.axis_index("c")
    names = [n for n, axis in _SMALL if axis is not None]
    mine = _pack_rows([shards[n] for n in names])
    mine = jnp.where(c == 0, mine, jnp.zeros_like(mine))
    rows = mine.shape[0]
    placed = lax.dynamic_update_slice(jnp.zeros((4 * rows, LANE), f32), mine, (_chip_index(x, y) * rows, 0))
    allp = _all_reduce_small(placed, "gather_small").reshape(4, rows, LANE)
    out = {n: shards[n] for n, axis in _SMALL if axis is None}
    per_chip = [_unpack_rows(allp[q], [shards[n].shape for n in names]) for q in range(4)]
    for i, n in enumerate(names):
        out[n] = jnp.concatenate([per_chip[q][i] for q in range(4)], axis=2)
    return out


def _reduce_small_grads(grads, shards):
    names = [n for n, _ in _SMALL]
    total = _all_reduce_small(_pack_rows([grads[n] for n in names]), "reduce_small")
    full = dict(zip(names, _unpack_rows(total, [grads[n].shape for n in names])))
    p = _chip_index(lax.axis_index("x"), lax.axis_index("y"))
    out = {}
    for n, axis in _SMALL:
        if axis is None:
            out[n] = full[n]
        else:
            w = shards[n].shape[axis]
            out[n] = lax.dynamic_slice_in_dim(full[n], p * w, w, axis=axis)
    return out


def _pad_cols(a, n):
    return a if a.shape[-1] == n else jnp.concatenate([a, jnp.zeros(a.shape[:-1] + (n - a.shape[-1],), a.dtype)], axis=-1)


def _ffn_width(F):
    return -(-F // 512) * 512


def kernel(x, gla_w_in, gla_w_gate_up, gla_gate_bias, gla_norm_g, gla_w_out, dil_w_in, dil_w_out, ffn_w_up, ffn_conv_w, ffn_conv_b, ffn_w_down, ln_g, ln_b, loss_target, m_gla_w_in, m_gla_w_gate_up, m_gla_gate_bias, m_gla_norm_g, m_gla_w_out, m_dil_w_in, m_dil_w_out, m_ffn_w_up, m_ffn_conv_w, m_ffn_conv_b, m_ffn_w_down, m_ln_g, m_ln_b, v_gla_w_in, v_gla_w_gate_up, v_gla_gate_bias, v_gla_norm_g, v_gla_w_out, v_dil_w_in, v_dil_w_out, v_ffn_w_up, v_ffn_conv_w, v_ffn_conv_b, v_ffn_w_down, v_ln_g, v_ln_b):
    names = ["gla_w_in", "gla_w_gate_up", "gla_gate_bias", "gla_norm_g", "gla_w_out", "dil_w_in", "dil_w_out", "ffn_w_up",
             "ffn_conv_w", "ffn_conv_b", "ffn_w_down", "ln_g", "ln_b"]
    w_sh = dict(zip(names, (gla_w_in, gla_w_gate_up, gla_gate_bias, gla_norm_g, gla_w_out, dil_w_in, dil_w_out, ffn_w_up,
                            ffn_conv_w, ffn_conv_b, ffn_w_down, ln_g, ln_b)))
    m_sh = dict(zip(names, (m_gla_w_in, m_gla_w_gate_up, m_gla_gate_bias, m_gla_norm_g, m_gla_w_out, m_dil_w_in, m_dil_w_out,
                            m_ffn_w_up, m_ffn_conv_w, m_ffn_conv_b, m_ffn_w_down, m_ln_g, m_ln_b)))
    v_sh = dict(zip(names, (v_gla_w_in, v_gla_w_gate_up, v_gla_gate_bias, v_gla_norm_g, v_gla_w_out, v_dil_w_in, v_dil_w_out,
                            v_ffn_w_up, v_ffn_conv_w, v_ffn_conv_b, v_ffn_w_down, v_ln_g, v_ln_b)))
    nseq, S, D = x.shape
    T = nseq * S
    big = _gather_big_weights(w_sh)
    small = _gather_small_params(w_sh)
    F = big["ffn_w_down"].shape[1]
    Fp = _ffn_width(F)
    qkvr = big["gla_w_in"].shape[2] - GLA_GATE_RANK

    def ffn_weights(i):
        wu = big["ffn_w_up"][i]
        w_up = jnp.concatenate([_pad_cols(wu[:, :F], Fp), _pad_cols(wu[:, F:], Fp)], axis=1)
        w_down = _pad_rows(big["ffn_w_down"][i], Fp)
        cw = small["ffn_conv_w"][i]
        cw = jnp.concatenate([_pad_cols(cw[:, :F], Fp), _pad_cols(cw[:, F:], Fp)], axis=1)
        cb = small["ffn_conv_b"][i][None, :]
        cb = jnp.concatenate([_pad_cols(cb[:, :F], Fp), _pad_cols(cb[:, F:], Fp)], axis=1)
        return w_up, w_down, cw, cb

    def gla_weights(j):
        w = big["gla_w_in"][j]
        w_gate = _pad_cols(w[:, qkvr:], GATE_PAD)
        w_gate_up = _pad_rows(small["gla_w_gate_up"][j].astype(bf16), GATE_PAD)
        return w[:, :qkvr], w_gate, w_gate_up, small["gla_gate_bias"][j][None, :], small["gla_norm_g"][j][None, :]

    h0 = x.reshape(T, D)
    saved = []
    cur = h0
    for i in range(DEPTH):
        j = i // 2
        tag = f"l{i}_"
        lg, lb = small["ln_g"][i], small["ln_b"][i]
        if i % 2 == 0:
            w_main, w_gate, w_gate_up, gate_bias, norm_g = gla_weights(j)
            proj = _mm(cur, w_main, "nn", tag + "gla_proj")
            g_low = _mm(cur, w_gate, "nn", tag + "gla_glow", tn=GATE_PAD)
            z = _mm(g_low, w_gate_up, "nn", tag + "gla_z")
            gate = _log_gate_fwd(z, gate_bias, tag + "gla_gate")
            o, states = _gla_fwd(proj, gate, nseq, tag + "gla_core")
            y = _rms_gate_fwd(o, proj, norm_g, tag + "gla_norm")
            mix = _mm(y, big["gla_w_out"][j], "nn", tag + "gla_out")
            mixer_saved = (proj, g_low, z, gate, states, o, y)
        else:
            proj = _mm(cur, big["dil_w_in"][j], "nn", tag + "dil_proj")
            outs, lses = [], []
            for gi, (window, dilation) in enumerate(DIL_PATTERNS):
                og, lg_ = _dil_fwd(proj, gi, window, dilation, nseq, tag + f"dil_attn{gi}")
                outs.append(og)
                lses.append(lg_)
            y, lse_tot = _dil_mix_fwd(outs, lses, tag + "dil_mix")
            mix = _mm(y, big["dil_w_out"][j], "nn", tag + "dil_out")
            mixer_saved = (proj, y, lse_tot)
        x1 = _ln_fwd(cur, mix, lg[0:1], lb[0:1], tag + "ln1")
        w_up, w_down, cw, cb = ffn_weights(i)
        hh = _mm(x1, w_up, "nn", tag + "ffn_up")
        act = _conv_gate_fwd(hh, cw, cb, nseq, tag + "ffn_conv")
        ffn = _mm(act, w_down, "nn", tag + "ffn_down")
        x2 = _ln_fwd(x1, ffn, lg[1:2], lb[1:2], tag + "ln2")
        saved.append((cur, mix, x1, hh, act, ffn, mixer_saved))
        cur = x2

    dy, sq = _loss_head(cur, loss_target.reshape(T, D), "loss_head")
    loss = lax.psum(0.5 * jnp.sum(sq) / D, ("x", "y", "c"))

    gb = {n: [None] * w_sh[n].shape[0] for n in names}
    d_res, d_res_scale = None, None
    d_in = dy
    for i in reversed(range(DEPTH)):
        j = i // 2
        tag = f"l{i}_b_"
        xin, mix, x1, hh, act, ffn, mixer_saved = saved[i]
        lg = small["ln_g"][i]
        w_up, w_down, cw, cb = ffn_weights(i)
        dys, scales = ([d_in], [1.0]) if d_res is None else ([d_res, d_in], [DEEPNORM_ALPHA, 1.0])
        du2, dg2, db2 = _ln_bwd(x1, ffn, lg[1:2], dys, scales, tag + "ln2")
        gb["ffn_w_down"][i] = _mm(act, du2, "tn", tag + "ffn_down_dw", tm=1024, tn=1024, tk=1024)[:F]
        dact = _mm(du2, w_down, "nt", tag + "ffn_down_dx")
        dh, dcw, dcb = _conv_gate_bwd(hh, dact, cw, cb, nseq, tag + "ffn_conv")
        gb["ffn_conv_w"][i] = jnp.concatenate([dcw[:, :F], dcw[:, Fp:Fp + F]], axis=1)
        gb["ffn_conv_b"][i] = jnp.concatenate([dcb[0, :F], dcb[0, Fp:Fp + F]], axis=0)
        dwu = _mm(x1, dh, "tn", tag + "ffn_up_dw", tm=1024, tn=1024, tk=1024)
        gb["ffn_w_up"][i] = jnp.concatenate([dwu[:, :F], dwu[:, Fp:Fp + F]], axis=1)
        dx1 = _mm(dh, w_up, "nt", tag + "ffn_up_dx")
        du1, dg1, db1 = _ln_bwd(xin, mix, lg[0:1], [du2, dx1], [DEEPNORM_ALPHA, 1.0], tag + "ln1")
        gb["ln_g"][i] = jnp.concatenate([dg1, dg2], axis=0)
        gb["ln_b"][i] = jnp.concatenate([db1, db2], axis=0)
        if i % 2 == 0:
            proj, g_low, z, gate, states, o, y = mixer_saved
            w_main, w_gate, w_gate_up, gate_bias, norm_g = gla_weights(j)
            gb["gla_w_out"][j] = _mm(y, du1, "tn", tag + "gla_out_dw", tm=1024, tn=1024, tk=1024)
            dyy = _mm(du1, big["gla_w_out"][j], "nt", tag + "gla_out_dx")
            do, dr, dng = _rms_gate_bwd(o, proj, norm_g, dyy, tag + "gla_norm")
            gb["gla_norm_g"][j] = dng[0]
            dq, dk_, dv_, dgate = _gla_bwd(proj, gate, states, do, nseq, tag + "gla_core")
            dz, dbias = _log_gate_bwd(z, gate_bias, dgate, tag + "gla_gate")
            gb["gla_gate_bias"][j] = dbias[0]
            gb["gla_w_gate_up"][j] = _mm(g_low, dz, "tn", tag + "gla_z_dw", tk=1024)[:GLA_GATE_RANK]
            dg_low = _mm(dz, w_gate_up, "nt", tag + "gla_z_dx", tn=GATE_PAD)
            dproj = jnp.concatenate([dq, dk_, dv_, dr], axis=1)
            dw_main = _mm(xin, dproj, "tn", tag + "gla_proj_dw", tm=1024, tn=1024, tk=1024)
            dw_gate = _mm(xin, dg_low, "tn", tag + "gla_glow_dw", tm=1024, tn=GATE_PAD, tk=1024)[:, :GLA_GATE_RANK]
            gb["gla_w_in"][j] = jnp.concatenate([dw_main, dw_gate], axis=1)
            dxa = _mm(dproj, w_main, "nt", tag + "gla_proj_dx")
            dxb = _mm(dg_low, w_gate, "nt", tag + "gla_glow_dx")
            d_in = _axpy(dxa, dxb, 1.0, tag + "gla_dx_sum")
        else:
            proj, y, lse_tot = mixer_saved
            gb["dil_w_out"][j] = _mm(y, du1, "tn", tag + "dil_out_dw", tm=1024, tn=1024, tk=1024)
            dyy = _mm(du1, big["dil_w_out"][j], "nt", tag + "dil_out_dx")
            delta = _dil_delta(dyy, y, tag + "dil_delta")
            pieces = []
            for gi, (window, dilation) in enumerate(DIL_PATTERNS):
                pieces += list(_dil_bwd(proj, dyy, lse_tot, delta, gi, window, dilation, nseq, tag + f"dil_attn{gi}"))
            dproj = jnp.concatenate(pieces, axis=1)
            gb["dil_w_in"][j] = _mm(xin, dproj, "tn", tag + "dil_proj_dw", tm=1024, tn=1024, tk=1024)
            d_in = _mm(dproj, big["dil_w_in"][j], "nt", tag + "dil_proj_dx")
        d_res = du1
    grad_x = _axpy(d_res, d_in, DEEPNORM_ALPHA, "grad_x").reshape(x.shape)

    g_full = {n: jnp.stack(v, axis=0) for n, v in gb.items()}
    grads = _scatter_big_grads(g_full, w_sh)
    grads.update(_reduce_small_grads(g_full, w_sh))

    delta, new_m, new_v = {}, {}, {}
    for n, _ in _BIG:
        shp = w_sh[n].shape
        r = lambda a: a.reshape(-1, FLAT_W)
        d_, m_, v_ = _adamw(r(w_sh[n]), r(grads[n]), r(m_sh[n]), r(v_sh[n]), "adamw_" + n)
        delta[n], new_m[n], new_v[n] = d_.reshape(shp), m_.reshape(shp), v_.reshape(shp)
    small_names = [n for n, _ in _SMALL]
    packed = [_pack_rows([src[n] for n in small_names]) for src in (w_sh, grads, m_sh, v_sh)]
    res = _adamw(*packed, "adamw_small")
    shapes = [w_sh[n].shape for n in small_names]
    for dst, arr in zip((delta, new_m, new_v), res):
        dst.update(dict(zip(small_names, _unpack_rows(arr, shapes))))

    return (loss, grad_x, *[grads[n] for n in names], *[delta[n] for n in names], *[new_m[n] for n in names],
            *[new_v[n] for n in names])
```

```python
import functools

import jax
import jax.numpy as jnp
from jax import lax
from jax.experimental import pallas as pl
from jax.experimental.pallas import tpu as pltpu

f32 = jnp.float32
bf16 = jnp.bfloat16
_MXU = jnp.bfloat16

DEPTH = 4
GLA_HEADS = 4
GLA_GATE_RANK = 16
GLA_GATE_NORMALIZER = 16.0
GLA_CHUNK = 64
GLA_SUB = 16
DIL_PATTERNS = ((128, 1), (512, 4), (2048, 16))
DIL_HEADS = 8
DIL_HEAD_DIM = 128
DIL_BLOCK = 128
DEEPNORM_ALPHA = (2 * DEPTH) ** 0.25
LN_EPS = 1e-5
RMS_EPS = 1e-6
ADAM_LR = 0.001
ADAM_B1 = 0.9
ADAM_B2 = 0.999
ADAM_EPS = 1e-08
ADAM_WD = 0.01
ADAM_STEP = 10

LANE = 128
VMEM_LIMIT = 48 * 1024 * 1024
GATE_PAD = LANE
MESH = pl.DeviceIdType.MESH


def _cparams(sem=None):
    return pltpu.CompilerParams(dimension_semantics=sem, vmem_limit_bytes=VMEM_LIMIT)


def _div_tile(n, pref, unit):
    if n <= pref:
        return n
    best = None
    for t in range(unit, pref + 1, unit):
        if n % t == 0:
            best = t
    assert best is not None, (n, pref, unit)
    return best


def _dot(a, b, ca, cb):
    return lax.dot_general(a.astype(_MXU), b.astype(_MXU), (((ca,), (cb,)), ((), ())), preferred_element_type=f32)


def _nn(a, b):
    return _dot(a, b, 1, 0)


def _nt(a, b):
    return _dot(a, b, 1, 1)


def _tn(a, b):
    return _dot(a, b, 0, 0)


def _exact_dot(a, b):
    return jnp.dot(a, b, precision=lax.Precision.HIGHEST, preferred_element_type=f32)


def _sigmoid(x):
    return 1.0 / (1.0 + jnp.exp(-x))


def _mm(a, b, mode, name, tm=1024, tn=512, tk=2048, out_dtype=f32, layer=None, n_out=None):
    if mode == "nn":
        (M, K), N = a.shape, (n_out or b.shape[-1])
    elif mode == "nt":
        (M, K), N = a.shape, b.shape[-2]
    else:
        (K, M), N = a.shape, b.shape[-1]
    tm, tn, tk = _div_tile(M, tm, LANE), _div_tile(N, tn, LANE), _div_tile(K, tk, LANE)
    nk = K // tk
    if mode == "tn":
        a_spec = pl.BlockSpec((tk, tm), lambda i, j, k: (k, i))
    else:
        a_spec = pl.BlockSpec((tm, tk), lambda i, j, k: (i, k))
    lead = () if layer is None else (None,)
    pre = () if layer is None else (layer,)
    if mode == "nt":
        b_spec = pl.BlockSpec(lead + (tn, tk), lambda i, j, k: pre + (j, k))
    else:
        b_spec = pl.BlockSpec(lead + (tk, tn), lambda i, j, k: pre + (k, j))
    ca, cb = {"nn": (1, 0), "nt": (1, 1), "tn": (0, 0)}[mode]

    def body(a_ref, b_ref, o_ref, *acc):
        p = _dot(a_ref[...], b_ref[...], ca, cb)
        if nk == 1:
            o_ref[...] = p.astype(o_ref.dtype)
        else:
            k = pl.program_id(2)
            acc_ref = acc[0]

            @pl.when(k == 0)
            def _():
                acc_ref[...] = p

            @pl.when(k > 0)
            def _():
                acc_ref[...] += p

            @pl.when(k == nk - 1)
            def _():
                o_ref[...] = acc_ref[...].astype(o_ref.dtype)

    return pl.pallas_call(
        body, name=name, grid=(M // tm, N // tn, nk), in_specs=[a_spec, b_spec],
        out_specs=pl.BlockSpec((tm, tn), lambda i, j, k: (i, j)),
        out_shape=jax.ShapeDtypeStruct((M, N), out_dtype),
        scratch_shapes=[pltpu.VMEM((tm, tn), f32)] if nk > 1 else [],
        compiler_params=_cparams(("parallel", "parallel", "arbitrary")),
    )(a, b)


def _rowwise(fn, rows, consts, outs, reds, name, tm=256):
    T = rows[0][0].shape[0]
    tm = _div_tile(T, tm, 8)
    n_r, n_c, n_o = len(rows), len(consts), len(outs)

    def body(*refs):
        ins = [r[...] for r in refs[: n_r + n_c]]
        res = fn(*ins)
        res = res if isinstance(res, (tuple, list)) else (res,)
        o_refs = refs[n_r + n_c: n_r + n_c + n_o]
        r_refs = refs[n_r + n_c + n_o:]
        for ref, val in zip(o_refs, res[:n_o]):
            ref[...] = val.astype(ref.dtype)
        i = pl.program_id(0)
        for ref, val in zip(r_refs, res[n_o:]):
            _accumulate(ref, val, i)

    in_specs = [pl.BlockSpec((tm, w), functools.partial(lambda i, cb: (i, cb), cb=cb)) for (_, w, cb) in rows]
    in_specs += [pl.BlockSpec(c.shape, lambda i: (0, 0)) for c in consts]
    out_specs = [pl.BlockSpec((tm, w), lambda i: (i, 0)) for (w, _) in outs]
    out_specs += [pl.BlockSpec((1, w), lambda i: (0, 0)) for w in reds]
    out_shape = [jax.ShapeDtypeStruct((T, w), dt) for (w, dt) in outs]
    out_shape += [jax.ShapeDtypeStruct((1, w), f32) for w in reds]
    return pl.pallas_call(
        body, name=name, grid=(T // tm,), in_specs=in_specs, out_specs=out_specs, out_shape=out_shape,
        compiler_params=_cparams(("arbitrary",)),
    )(*[r[0] for r in rows], *consts)


def _accumulate(ref, val, step):
    @pl.when(step == 0)
    def _():
        ref[...] = val

    @pl.when(step > 0)
    def _():
        ref[...] += val


def _full(a):
    return (a, a.shape[1], 0)


def _colsum(x):
    return jnp.sum(x, axis=0, keepdims=True)


def _ln_stats(u):
    mu = jnp.mean(u, axis=-1, keepdims=True)
    xc = u - mu
    var = jnp.mean(xc * xc, axis=-1, keepdims=True)
    rstd = lax.rsqrt(var + LN_EPS)
    return xc * rstd, rstd


def _ln_fwd(x, f, g, b, name):
    def fn(x, f, g, b):
        xhat, _ = _ln_stats(DEEPNORM_ALPHA * x + f)
        return xhat * g + b

    return _rowwise(fn, [_full(x), _full(f)], [g, b], [(x.shape[1], f32)], [], name)[0]


def _ln_bwd(x, f, g, dys, scales, name):
    def fn(x, f, *rest):
        g = rest[-1]
        dy = None
        for d, s in zip(rest[:-1], scales):
            t = d if s == 1.0 else s * d
            dy = t if dy is None else dy + t
        xhat, rstd = _ln_stats(DEEPNORM_ALPHA * x + f)
        dxh = dy * g
        m1 = jnp.mean(dxh, axis=-1, keepdims=True)
        m2 = jnp.mean(dxh * xhat, axis=-1, keepdims=True)
        du = rstd * (dxh - m1 - xhat * m2)
        return du, _colsum(dy * xhat), _colsum(dy)

    D = x.shape[1]
    return _rowwise(fn, [_full(x), _full(f)] + [_full(d) for d in dys], [g], [(D, f32)], [D, D], name)


def _loss_head(y, t, name):
    D = y.shape[1]

    def fn(y, t):
        e = y - t
        return e * (1.0 / D), _colsum(e * e)

    return _rowwise(fn, [_full(y), _full(t)], [], [(D, f32)], [D], name)


def _axpy(a, b, alpha, name):
    def fn(a, b):
        return alpha * a + b

    return _rowwise(fn, [_full(a), _full(b)], [], [(a.shape[1], f32)], [], name)[0]


def _shift_down(x, k):
    row = lax.broadcasted_iota(jnp.int32, x.shape, 0)
    return jnp.where(row >= k, pltpu.roll(x, k, 0), 0.0)


def _shift_up(x, k):
    S = x.shape[0]
    row = lax.broadcasted_iota(jnp.int32, x.shape, 0)
    return jnp.where(row < S - k, pltpu.roll(x, S - k, 0), 0.0)


def _causal_conv(h, w, b):
    return ((b + w[0:1] * _shift_down(h, 2)) + w[1:2] * _shift_down(h, 1)) + w[2:3] * h


def _conv_gate_fwd(h, cw, cb, nseq, name, tc=256):
    T, F2 = h.shape
    F, S = F2 // 2, T // nseq
    tc = _div_tile(F, tc, LANE)
    nf = F // tc

    def body(hg_ref, hu_ref, wg_ref, wu_ref, bg_ref, bu_ref, a_ref):
        cg = _causal_conv(hg_ref[...], wg_ref[...], bg_ref[...])
        cu = _causal_conv(hu_ref[...], wu_ref[...], bu_ref[...])
        a_ref[...] = cg * _sigmoid(cg) * cu

    return pl.pallas_call(
        body, name=name, grid=(nseq, nf),
        in_specs=[pl.BlockSpec((S, tc), lambda s, j: (s, j)), pl.BlockSpec((S, tc), lambda s, j: (s, nf + j)),
                  pl.BlockSpec((3, tc), lambda s, j: (0, j)), pl.BlockSpec((3, tc), lambda s, j: (0, nf + j)),
                  pl.BlockSpec((1, tc), lambda s, j: (0, j)), pl.BlockSpec((1, tc), lambda s, j: (0, nf + j))],
        out_specs=pl.BlockSpec((S, tc), lambda s, j: (s, j)),
        out_shape=jax.ShapeDtypeStruct((T, F), f32),
        compiler_params=_cparams(("parallel", "parallel")),
    )(h, h, cw, cw, cb, cb)


def _conv_gate_bwd(h, da, cw, cb, nseq, name, tc=128):
    T, F2 = h.shape
    F, S = F2 // 2, T // nseq
    tc = _div_tile(F, tc, LANE)
    nf = F // tc

    def conv_bwd(dc, hx, w):
        dh = (w[2:3] * dc + w[1:2] * _shift_up(dc, 1)) + w[0:1] * _shift_up(dc, 2)
        dw = jnp.concatenate([_colsum(dc * _shift_down(hx, 2)), _colsum(dc * _shift_down(hx, 1)), _colsum(dc * hx)], axis=0)
        return dh, dw, _colsum(dc)

    def body(hg_ref, hu_ref, da_ref, wg_ref, wu_ref, bg_ref, bu_ref, dhg_ref, dhu_ref, dwg_ref, dwu_ref, dbg_ref, dbu_ref):
        hg, hu, da = hg_ref[...], hu_ref[...], da_ref[...]
        wg, wu = wg_ref[...], wu_ref[...]
        cg = _causal_conv(hg, wg, bg_ref[...])
        cu = _causal_conv(hu, wu, bu_ref[...])
        sg = _sigmoid(cg)
        dcu = da * (cg * sg)
        dcg = da * cu * (sg * (1.0 + cg * (1.0 - sg)))
        dhg, dwg, dbg = conv_bwd(dcg, hg, wg)
        dhu, dwu, dbu = conv_bwd(dcu, hu, wu)
        dhg_ref[...] = dhg
        dhu_ref[...] = dhu
        s = pl.program_id(1)
        _accumulate(dwg_ref, dwg, s)
        _accumulate(dwu_ref, dwu, s)
        _accumulate(dbg_ref, dbg, s)
        _accumulate(dbu_ref, dbu, s)

    col = lambda j, s: (s, j)
    par = lambda j, s: (0, j)
    dhg, dhu, dwg, dwu, dbg, dbu = pl.pallas_call(
        body, name=name, grid=(nf, nseq),
        in_specs=[pl.BlockSpec((S, tc), col), pl.BlockSpec((S, tc), lambda j, s: (s, nf + j)), pl.BlockSpec((S, tc), col),
                  pl.BlockSpec((3, tc), par), pl.BlockSpec((3, tc), lambda j, s: (0, nf + j)),
                  pl.BlockSpec((1, tc), par), pl.BlockSpec((1, tc), lambda j, s: (0, nf + j))],
        out_specs=[pl.BlockSpec((S, tc), col), pl.BlockSpec((S, tc), col), pl.BlockSpec((3, tc), par), pl.BlockSpec((3, tc), par),
                   pl.BlockSpec((1, tc), par), pl.BlockSpec((1, tc), par)],
        out_shape=[jax.ShapeDtypeStruct((T, F), f32), jax.ShapeDtypeStruct((T, F), f32), jax.ShapeDtypeStruct((3, F), f32),
                   jax.ShapeDtypeStruct((3, F), f32), jax.ShapeDtypeStruct((1, F), f32), jax.ShapeDtypeStruct((1, F), f32)],
        compiler_params=_cparams(("parallel", "arbitrary")),
    )(h, h, da, cw, cw, cb, cb)
    return jnp.concatenate([dhg, dhu], axis=1), jnp.concatenate([dwg, dwu], axis=1), jnp.concatenate([dbg, dbu], axis=1)


def _group_row(x, jj):
    C, d = x.shape
    n = C // GLA_SUB
    x3 = x.reshape(n, GLA_SUB, d)
    return jnp.broadcast_to(x3[:, jj:jj + 1, :], (n, GLA_SUB, d)).reshape(C, d)


def _group_sum(x):
    C, d = x.shape
    n = C // GLA_SUB
    s = jnp.sum(x.reshape(n, GLA_SUB, d), axis=1, keepdims=True)
    return jnp.broadcast_to(s, (n, GLA_SUB, d)).reshape(C, d)


def _chunk_cumsum(g):
    C = g.shape[0]
    row = lax.broadcasted_iota(jnp.int32, (C, C), 0)
    col = lax.broadcasted_iota(jnp.int32, (C, C), 1)
    return _exact_dot((row >= col).astype(f32), g)


def _chunk_suffix_sum(x):
    C = x.shape[0]
    row = lax.broadcasted_iota(jnp.int32, (C, C), 0)
    col = lax.broadcasted_iota(jnp.int32, (C, C), 1)
    return _exact_dot((col >= row).astype(f32), x)


def _gla_scores(q, k, b):
    C = q.shape[0]
    n = C // GLA_SUB
    row = lax.broadcasted_iota(jnp.int32, (C, C), 0)
    col = lax.broadcasted_iota(jnp.int32, (C, C), 1)
    blocks = [jnp.zeros((GLA_SUB, C), f32)]
    for s in range(1, n):
        lo = s * GLA_SUB
        bref = b[lo - 1:lo, :]
        qr = q[lo:lo + GLA_SUB] * jnp.exp(b[lo:lo + GLA_SUB] - bref)
        kr = k * jnp.exp(jnp.minimum(bref - b, 0.0))
        blocks.append(_nt(qr, kr))
    sub_start = (row // GLA_SUB) * GLA_SUB
    a = jnp.where(col < sub_start, jnp.concatenate(blocks, axis=0), 0.0)
    rin = lax.broadcasted_iota(jnp.int32, (C, 1), 0) % GLA_SUB
    for jj in range(GLA_SUB):
        e = jnp.exp(jnp.minimum(b - _group_row(b, jj), 0.0))
        colv = jnp.sum(q * _group_row(k, jj) * e, axis=1, keepdims=True)
        colv = jnp.where(rin >= jj, colv, 0.0)
        a = jnp.where(col == sub_start + jj, colv, a)
    return a


def _gla_scores_bwd(da, q, k, b):
    C = q.shape[0]
    n = C // GLA_SUB
    row = lax.broadcasted_iota(jnp.int32, (C, C), 0)
    col = lax.broadcasted_iota(jnp.int32, (C, C), 1)
    sub_start = (row // GLA_SUB) * GLA_SUB
    da_inter = jnp.where(col < sub_start, da, 0.0)
    dq_blocks = [jnp.zeros((GLA_SUB, q.shape[1]), f32)]
    dk = jnp.zeros_like(k)
    for s in range(1, n):
        lo = s * GLA_SUB
        bref = b[lo - 1:lo, :]
        eq = jnp.exp(b[lo:lo + GLA_SUB] - bref)
        ek = jnp.exp(jnp.minimum(bref - b, 0.0))
        das = da_inter[lo:lo + GLA_SUB]
        dq_blocks.append(_nn(das, k * ek) * eq)
        dk = dk + _tn(das, q[lo:lo + GLA_SUB] * eq) * ek
    dq = jnp.concatenate(dq_blocks, axis=0)
    rin = lax.broadcasted_iota(jnp.int32, (C, 1), 0) % GLA_SUB
    for jj in range(GLA_SUB):
        e = jnp.exp(jnp.minimum(b - _group_row(b, jj), 0.0))
        dac = jnp.sum(jnp.where(col == sub_start + jj, da, 0.0), axis=1, keepdims=True)
        dac = jnp.where(rin >= jj, dac, 0.0)
        w = dac * e
        dq = dq + w * _group_row(k, jj)
        dk = dk + jnp.where(rin == jj, _group_sum(w * q), 0.0)
    return dq, dk


def _gla_specs(nC, dk, dv):
    H = GLA_HEADS
    voff = (2 * H * dk) // dv
    assert voff * dv == 2 * H * dk
    return H, voff


def _gla_fwd(proj, gate, nseq, name):
    T = proj.shape[0]
    dk = gate.shape[1] // GLA_HEADS
    dv = 2 * dk
    C = GLA_CHUNK
    nC = T // nseq // C
    H, voff = _gla_specs(nC, dk, dv)
    scale = dk ** -0.5

    def body(q_ref, k_ref, v_ref, g_ref, o_ref, st_ref, state):
        c = pl.program_id(2)

        @pl.when(c == 0)
        def _():
            state[...] = jnp.zeros_like(state)

        q, k, v = q_ref[...] * scale, k_ref[...], v_ref[...]
        b = _chunk_cumsum(g_ref[...])
        st = state[...]
        st_ref[0] = st
        a = _gla_scores(q, k, b)
        o_ref[...] = _nt(q * jnp.exp(b), st) + _nn(a, v)
        bl = b[C - 1:C, :]
        state[...] = st * jnp.exp(bl) + _tn(v, k * jnp.exp(bl - b))

    row = lambda s, h, c: s * nC + c
    return pl.pallas_call(
        body, name=name, grid=(nseq, H, nC),
        in_specs=[pl.BlockSpec((C, dk), lambda s, h, c: (row(s, h, c), h)),
                  pl.BlockSpec((C, dk), lambda s, h, c: (row(s, h, c), H + h)),
                  pl.BlockSpec((C, dv), lambda s, h, c: (row(s, h, c), voff + h)),
                  pl.BlockSpec((C, dk), lambda s, h, c: (row(s, h, c), h))],
        out_specs=[pl.BlockSpec((C, dv), lambda s, h, c: (row(s, h, c), h)),
                   pl.BlockSpec((1, dv, dk), lambda s, h, c: ((s * H + h) * nC + c, 0, 0))],
        out_shape=[jax.ShapeDtypeStruct((T, H * dv), f32), jax.ShapeDtypeStruct((nseq * H * nC, dv, dk), f32)],
        scratch_shapes=[pltpu.VMEM((dv, dk), f32)],
        compiler_params=_cparams(("parallel", "parallel", "arbitrary")),
    )(proj, proj, proj, gate)


def _gla_bwd(proj, gate, states, do, nseq, name):
    T = proj.shape[0]
    dk = gate.shape[1] // GLA_HEADS
    dv = 2 * dk
    C = GLA_CHUNK
    nC = T // nseq // C
    H, voff = _gla_specs(nC, dk, dv)
    scale = dk ** -0.5

    def body(q_ref, k_ref, v_ref, g_ref, do_ref, st_ref, dq_ref, dk_ref, dv_ref, dg_ref, dstate, term):
        c = pl.program_id(2)

        @pl.when(c == 0)
        def _():
            dstate[...] = jnp.zeros_like(dstate)
            term[...] = jnp.zeros_like(term)

        q, k, v, do = q_ref[...] * scale, k_ref[...], v_ref[...], do_ref[...]
        b = _chunk_cumsum(g_ref[...])
        st = st_ref[0]
        dst = dstate[...]
        eb = jnp.exp(b)
        bl = b[C - 1:C, :]
        kdec = jnp.exp(bl - b)
        a = _gla_scores(q, k, b)
        rowi = lax.broadcasted_iota(jnp.int32, (C, C), 0)
        coli = lax.broadcasted_iota(jnp.int32, (C, C), 1)
        da = jnp.where(coli <= rowi, _nt(do, v), 0.0)
        dq_s, dk_s = _gla_scores_bwd(da, q, k, b)
        dq = _nn(do, st) * eb + dq_s
        dkk = _nn(v, dst) * kdec + dk_s
        dv_ref[...] = _tn(a, do) + _nt(k * kdec, dst)
        last = lax.broadcasted_iota(jnp.int32, (C, 1), 0) == C - 1
        db = q * dq - k * dkk + jnp.where(last, term[...], 0.0)
        dg_ref[...] = _chunk_suffix_sum(db)
        dq_ref[...] = dq * scale
        dk_ref[...] = dkk
        dprev = dst * jnp.exp(bl) + _tn(do, q * eb)
        dstate[...] = dprev
        term[...] = _colsum(st * dprev)

    row = lambda s, h, c: s * nC + (nC - 1 - c)
    kspec = lambda off: pl.BlockSpec((C, dk), lambda s, h, c: (row(s, h, c), off + h))
    vspec = lambda off: pl.BlockSpec((C, dv), lambda s, h, c: (row(s, h, c), off + h))
    return pl.pallas_call(
        body, name=name, grid=(nseq, H, nC),
        in_specs=[kspec(0), kspec(H), vspec(voff), kspec(0), vspec(0),
                  pl.BlockSpec((1, dv, dk), lambda s, h, c: ((s * H + h) * nC + (nC - 1 - c), 0, 0))],
        out_specs=[kspec(0), kspec(0), vspec(0), kspec(0)],
        out_shape=[jax.ShapeDtypeStruct((T, H * dk), f32), jax.ShapeDtypeStruct((T, H * dk), f32),
                   jax.ShapeDtypeStruct((T, H * dv), f32), jax.ShapeDtypeStruct((T, H * dk), f32)],
        scratch_shapes=[pltpu.VMEM((dv, dk), f32), pltpu.VMEM((1, dk), f32)],
        compiler_params=_cparams(("parallel", "parallel", "arbitrary")),
    )(proj, proj, proj, gate, do, states)


def _head_slices(width, n):
    w = width // n
    return [slice(h * w, (h + 1) * w) for h in range(n)]


def _rms_gate_fwd(o, proj, ng, name):
    W = o.shape[1]

    def fn(o, r, ng):
        parts = []
        for sl in _head_slices(W, GLA_HEADS):
            oh = o[:, sl]
            rstd = lax.rsqrt(jnp.mean(oh * oh, axis=-1, keepdims=True) + RMS_EPS)
            rh = r[:, sl]
            parts.append((oh * rstd * ng) * (rh * _sigmoid(rh)))
        return jnp.concatenate(parts, axis=1)

    return _rowwise(fn, [_full(o), (proj, W, 2)], [ng], [(W, f32)], [], name)[0]


def _rms_gate_bwd(o, proj, ng, dy, name):
    W = o.shape[1]

    def fn(o, r, dy, ng):
        dos, drs = [], []
        dng = jnp.zeros((1, W // GLA_HEADS), f32)
        for sl in _head_slices(W, GLA_HEADS):
            oh, rh, dyh = o[:, sl], r[:, sl], dy[:, sl]
            rstd = lax.rsqrt(jnp.mean(oh * oh, axis=-1, keepdims=True) + RMS_EPS)
            ohat = oh * rstd
            sg = _sigmoid(rh)
            don = dyh * (rh * sg)
            drs.append(dyh * (ohat * ng) * (sg * (1.0 + rh * (1.0 - sg))))
            dng = dng + _colsum(don * ohat)
            dohat = don * ng
            dos.append(rstd * (dohat - ohat * jnp.mean(dohat * ohat, axis=-1, keepdims=True)))
        return jnp.concatenate(dos, axis=1), jnp.concatenate(drs, axis=1), dng

    return _rowwise(fn, [_full(o), (proj, W, 2), _full(dy)], [ng], [(W, f32), (W, f32)], [W // GLA_HEADS], name)


def _log_gate_fwd(z, bias, name):
    def fn(z, bias):
        t = z + bias
        return (jnp.minimum(t, 0.0) - jnp.log1p(jnp.exp(-jnp.abs(t)))) * (1.0 / GLA_GATE_NORMALIZER)

    return _rowwise(fn, [_full(z)], [bias], [(z.shape[1], f32)], [], name)[0]


def _log_gate_bwd(z, bias, dg, name):
    def fn(z, dg, bias):
        dz = dg * (1.0 / GLA_GATE_NORMALIZER) * _sigmoid(-(z + bias))
        return dz, _colsum(dz)

    return _rowwise(fn, [_full(z), _full(dg)], [bias], [(z.shape[1], f32)], [z.shape[1]], name)


def _band_masks(P, steps, has_prev):
    i = lax.broadcasted_iota(jnp.int32, (P, P), 0)
    j = lax.broadcasted_iota(jnp.int32, (P, P), 1)
    cur = (i - j >= 0) & (i - j <= steps)
    prev = (i + P - j <= steps) & has_prev
    return cur, prev


def _dil_dims(T, nseq, dilation):
    L = T // nseq // dilation
    P = min(DIL_BLOCK, L)
    return L, P, L // P


def _dil_fwd(proj, gi, window, dilation, nseq, name):
    T = proj.shape[0]
    H, dh = DIL_HEADS, DIL_HEAD_DIM
    L, P, nb = _dil_dims(T, nseq, dilation)
    SB = P * dilation
    steps = window // dilation
    scale = dh ** -0.5

    def body(q_ref, kc_ref, kp_ref, vc_ref, vp_ref, o_ref, lse_ref):
        sb, h = pl.program_id(1), pl.program_id(2)
        mc, mp = _band_masks(P, steps, sb > 0)
        lane = lax.broadcasted_iota(jnp.int32, (P, LANE), 1)

        @pl.when(h == 0)
        def _():
            lse_ref[...] = jnp.zeros_like(lse_ref)

        for r in range(dilation):
            rows = pl.ds(r, P, stride=dilation) if dilation > 1 else pl.ds(0, P)
            q = q_ref[rows, :]
            sc = jnp.where(mc, _nt(q, kc_ref[rows, :]) * scale, -jnp.inf)
            sp = jnp.where(mp, _nt(q, kp_ref[rows, :]) * scale, -jnp.inf)
            m = jnp.maximum(jnp.max(sc, axis=-1, keepdims=True), jnp.max(sp, axis=-1, keepdims=True))
            pc, pp = jnp.exp(sc - m), jnp.exp(sp - m)
            l = jnp.sum(pc, axis=-1, keepdims=True) + jnp.sum(pp, axis=-1, keepdims=True)
            o_ref[rows, :] = _nn(pc / l, vc_ref[rows, :]) + _nn(pp / l, vp_ref[rows, :])
            lse_ref[rows, :] = jnp.where(lane == h, m + jnp.log(l), lse_ref[rows, :])

    cur = lambda part: pl.BlockSpec((SB, dh), lambda s, sb, h: (s * nb + sb, (gi * 3 + part) * H + h))
    prv = lambda part: pl.BlockSpec((SB, dh), lambda s, sb, h: (s * nb + jnp.maximum(sb - 1, 0), (gi * 3 + part) * H + h))
    return pl.pallas_call(
        body, name=name, grid=(nseq, nb, H),
        in_specs=[cur(0), cur(1), prv(1), cur(2), prv(2)],
        out_specs=[pl.BlockSpec((SB, dh), lambda s, sb, h: (s * nb + sb, h)), pl.BlockSpec((SB, LANE), lambda s, sb, h: (s * nb + sb, 0))],
        out_shape=[jax.ShapeDtypeStruct((T, H * dh), f32), jax.ShapeDtypeStruct((T, LANE), f32)],
        compiler_params=_cparams(("parallel", "parallel", "arbitrary")),
    )(proj, proj, proj, proj, proj)


def _dil_mix_fwd(os_, lses, name):
    W = os_[0].shape[1]
    G = len(os_)

    def fn(*a):
        o, l = a[:G], a[G:]
        lane = lax.broadcasted_iota(jnp.int32, l[0].shape, 1)
        tot = jnp.zeros(l[0].shape, f32)
        parts = []
        for h, sl in enumerate(_head_slices(W, DIL_HEADS)):
            lh = [x[:, h:h + 1] for x in l]
            m = functools.reduce(jnp.maximum, lh)
            e = [jnp.exp(x - m) for x in lh]
            z = functools.reduce(lambda u, v: u + v, e)
            acc = None
            for g in range(G):
                t = (e[g] / z) * o[g][:, sl]
                acc = t if acc is None else acc + t
            parts.append(acc)
            tot = jnp.where(lane == h, m + jnp.log(z), tot)
        return jnp.concatenate(parts, axis=1), tot

    return _rowwise(fn, [_full(x) for x in os_] + [_full(x) for x in lses], [], [(W, f32), (LANE, f32)], [], name)


def _dil_delta(do, o, name):
    W = o.shape[1]

    def fn(do, o):
        lane = lax.broadcasted_iota(jnp.int32, (do.shape[0], LANE), 1)
        d = jnp.zeros((do.shape[0], LANE), f32)
        for h, sl in enumerate(_head_slices(W, DIL_HEADS)):
            d = jnp.where(lane == h, jnp.sum(do[:, sl] * o[:, sl], axis=-1, keepdims=True), d)
        return d

    return _rowwise(fn, [_full(do), _full(o)], [], [(LANE, f32)], [], name)[0]


def _dil_bwd(proj, do, lse, delta, gi, window, dilation, nseq, name):
    T = proj.shape[0]
    H, dh = DIL_HEADS, DIL_HEAD_DIM
    L, P, nb = _dil_dims(T, nseq, dilation)
    SB = P * dilation
    steps = window // dilation
    scale = dh ** -0.5
    row_sets = [pl.ds(r, P, stride=dilation) if dilation > 1 else pl.ds(0, P) for r in range(dilation)]

    def probs(q, k, lse_h, mask):
        return jnp.where(mask, jnp.exp(_nt(q, k) * scale - lse_h), 0.0)

    def head_lane(ref, rows, h):
        lane = lax.broadcasted_iota(jnp.int32, (P, LANE), 1)
        return jnp.sum(jnp.where(lane == h, ref[rows, :], 0.0), axis=1, keepdims=True)

    def dq_body(q_ref, kc_ref, kp_ref, vc_ref, vp_ref, do_ref, lse_ref, del_ref, dq_ref):
        sb, h = pl.program_id(1), pl.program_id(2)
        mc, mp = _band_masks(P, steps, sb > 0)
        for rows in row_sets:
            q, doh = q_ref[rows, :], do_ref[rows, :]
            lse_h, del_h = head_lane(lse_ref, rows, h), head_lane(del_ref, rows, h)
            kc, kp = kc_ref[rows, :], kp_ref[rows, :]
            dsc = probs(q, kc, lse_h, mc) * (_nt(doh, vc_ref[rows, :]) - del_h) * scale
            dsp = probs(q, kp, lse_h, mp) * (_nt(doh, vp_ref[rows, :]) - del_h) * scale
            dq_ref[rows, :] = _nn(dsc, kc) + _nn(dsp, kp)

    cur = lambda part: pl.BlockSpec((SB, dh), lambda s, sb, h: (s * nb + sb, (gi * 3 + part) * H + h))
    prv = lambda part: pl.BlockSpec((SB, dh), lambda s, sb, h: (s * nb + jnp.maximum(sb - 1, 0), (gi * 3 + part) * H + h))
    tok = pl.BlockSpec((SB, dh), lambda s, sb, h: (s * nb + sb, h))
    aux = pl.BlockSpec((SB, LANE), lambda s, sb, h: (s * nb + sb, 0))
    dq = pl.pallas_call(
        dq_body, name=name + "_dq", grid=(nseq, nb, H),
        in_specs=[cur(0), cur(1), prv(1), cur(2), prv(2), tok, aux, aux],
        out_specs=tok, out_shape=jax.ShapeDtypeStruct((T, H * dh), f32),
        compiler_params=_cparams(("parallel", "parallel", "parallel")),
    )(proj, proj, proj, proj, proj, do, lse, delta)

    def dkv_body(k_ref, v_ref, qc_ref, qn_ref, doc_ref, don_ref, lsec_ref, lsen_ref, delc_ref, deln_ref, dk_ref, dv_ref):
        sb, h = pl.program_id(1), pl.program_id(2)
        mc, mn = _band_masks(P, steps, sb < nb - 1)
        for rows in row_sets:
            k, v = k_ref[rows, :], v_ref[rows, :]
            qc, doc = qc_ref[rows, :], doc_ref[rows, :]
            pc = probs(qc, k, head_lane(lsec_ref, rows, h), mc)
            dsc = pc * (_nt(doc, v) - head_lane(delc_ref, rows, h)) * scale
            qn, don = qn_ref[rows, :], don_ref[rows, :]
            pn = probs(qn, k, head_lane(lsen_ref, rows, h), mn)
            dsn = pn * (_nt(don, v) - head_lane(deln_ref, rows, h)) * scale
            dk_ref[rows, :] = _tn(dsc, qc) + _tn(dsn, qn)
            dv_ref[rows, :] = _tn(pc, doc) + _tn(pn, don)

    nxt = lambda s, sb: s * nb + jnp.minimum(sb + 1, nb - 1)
    qnx = pl.BlockSpec((SB, dh), lambda s, sb, h: (nxt(s, sb), gi * 3 * H + h))
    tokn = pl.BlockSpec((SB, dh), lambda s, sb, h: (nxt(s, sb), h))
    auxn = pl.BlockSpec((SB, LANE), lambda s, sb, h: (nxt(s, sb), 0))
    dkk, dvv = pl.pallas_call(
        dkv_body, name=name + "_dkv", grid=(nseq, nb, H),
        in_specs=[cur(1), cur(2), cur(0), qnx, tok, tokn, aux, auxn, aux, auxn],
        out_specs=[tok, tok],
        out_shape=[jax.ShapeDtypeStruct((T, H * dh), f32), jax.ShapeDtypeStruct((T, H * dh), f32)],
        compiler_params=_cparams(("parallel", "parallel", "parallel")),
    )(proj, proj, proj, proj, do, do, lse, lse, delta, delta)
    return dq, dkk, dvv


def _adamw_math(w, g, m, v):
    m = ADAM_B1 * m + (1.0 - ADAM_B1) * g
    v = ADAM_B2 * v + (1.0 - ADAM_B2) * (g * g)
    m_hat = m / (1.0 - ADAM_B1 ** ADAM_STEP)
    v_hat = v / (1.0 - ADAM_B2 ** ADAM_STEP)
    return -ADAM_LR * (m_hat / (jnp.sqrt(v_hat) + ADAM_EPS) + ADAM_WD * w), m, v


def _adamw(w, g, m, v, name):
    W = w.shape[1]
    return _rowwise(_adamw_math, [_full(w), _full(g), _full(m), _full(v)], [], [(W, f32)] * 3, [], name, tm=512)


def _position():
    x, y, c = lax.axis_index("x"), lax.axis_index("y"), lax.axis_index("c")
    other_chips = [(1 - x, y), (x, 1 - y), (1 - x, 1 - y)]
    return x, y, c, other_chips


def _chip_index(x, y):
    return 2 * x + y


_ANY = pl.BlockSpec(memory_space=pl.ANY)


def _all_reduce_small(p, name):
    R, Wd = p.shape

    def body(p_ref, o_ref, buf, send_sems, recv_sems):
        x, y, c, _ = _position()
        me = 4 * x + 2 * y + c
        buf[me] = p_ref[...]
        copies = []
        for k in range(1, 8):
            fx, fy, fc = (k >> 2) & 1, (k >> 1) & 1, k & 1
            peer = (x + fx - 2 * x * fx, y + fy - 2 * y * fy, c + fc - 2 * c * fc)
            cp = pltpu.make_async_remote_copy(src_ref=p_ref, dst_ref=buf.at[me], send_sem=send_sems.at[k - 1],
                                              recv_sem=recv_sems.at[k - 1], device_id=peer, device_id_type=MESH)
            cp.start()
            copies.append(cp)
        for cp in copies:
            cp.wait()
        acc = buf[0]
        for s in range(1, 8):
            acc = acc + buf[s]
        o_ref[...] = acc

    return pl.pallas_call(
        body, name=name, out_shape=jax.ShapeDtypeStruct((R, Wd), f32),
        in_specs=[pl.BlockSpec(memory_space=pltpu.VMEM)], out_specs=pl.BlockSpec(memory_space=pltpu.VMEM),
        scratch_shapes=[pltpu.VMEM((8, R, Wd), f32), pltpu.SemaphoreType.DMA((7,)), pltpu.SemaphoreType.DMA((7,))],
        compiler_params=pltpu.CompilerParams(vmem_limit_bytes=VMEM_LIMIT),
    )(p)


def _layer_half(ref, h, axis):
    n = ref.shape[axis] // 2
    idx = (slice(None),) * axis + (pl.ds(h * n, n),)
    return ref.at[idx]


def _comm_call(body, name, ins, out_shapes, n_sems, n_local=0):
    scratch = [pltpu.SemaphoreType.DMA((n_sems,)), pltpu.SemaphoreType.DMA((n_sems,))]
    if n_local:
        scratch.append(pltpu.SemaphoreType.DMA((n_local,)))
    return pl.pallas_call(body, name=name, out_shape=out_shapes, in_specs=[_ANY] * len(ins), out_specs=[_ANY] * len(out_shapes),
                          scratch_shapes=scratch)(*ins)


def _all_gather_chips(ws, name):
    n = len(ws)

    def body(*refs):
        w_refs, o_refs = refs[:n], refs[n:2 * n]
        send_sems, recv_sems, local_sems = refs[2 * n:]
        x, y, c, chips = _position()
        p = _chip_index(x, y)
        sibling = (x, y, 1 - c)
        local = []
        for i in range(n):
            cp = pltpu.make_async_copy(w_refs[i], o_refs[i].at[p], local_sems.at[i])
            cp.start()
            local.append(cp)

        def copy(k, src, i, chip_idx, h, to):
            return pltpu.make_async_remote_copy(src_ref=src, dst_ref=_layer_half(o_refs[i].at[chip_idx], h, 0), send_sem=send_sems.at[k],
                                                recv_sem=recv_sems.at[k], device_id=to, device_id_type=MESH)

        first = [copy(3 * i + j, _layer_half(w_refs[i], c, 0), i, p, c, (*chip, c)) for i in range(n) for j, chip in enumerate(chips)]
        for cp in first:
            cp.start()
        passed = []
        for i in range(n):
            for j, chip in enumerate(chips):
                q = _chip_index(*chip)
                copy(3 * i + j, _layer_half(w_refs[i], c, 0), i, q, c, (*chip, c)).wait_recv()
                fwd = copy(3 * n + 3 * i + j, _layer_half(o_refs[i].at[q], c, 0), i, q, c, sibling)
                fwd.start()
                passed.append(fwd)
        for i in range(n):
            for j, chip in enumerate(chips):
                copy(3 * n + 3 * i + j, _layer_half(w_refs[i], c, 0), i, _chip_index(*chip), 1 - c, sibling).wait_recv()
        for cp in first + passed:
            cp.wait_send()
        for cp in local:
            cp.wait()

    return _comm_call(body, name, ws, [jax.ShapeDtypeStruct((4,) + w.shape, w.dtype) for w in ws], 6 * n, n)


def _sibling_halves(gs, name):
    n = len(gs)

    def body(*refs):
        g_refs, o_refs, send_sems, recv_sems = refs[:n], refs[n:2 * n], refs[2 * n], refs[2 * n + 1]
        x, y, c, _ = _position()
        copies = []
        for i in range(n):
            cp = pltpu.make_async_remote_copy(src_ref=_layer_half(g_refs[i], 1 - c, 1), dst_ref=o_refs[i], send_sem=send_sems.at[i],
                                              recv_sem=recv_sems.at[i], device_id=(x, y, 1 - c), device_id_type=MESH)
            cp.start()
            copies.append(cp)
        for cp in copies:
            cp.wait()

    outs = [jax.ShapeDtypeStruct((4, g.shape[1] // 2) + g.shape[2:], g.dtype) for g in gs]
    return _comm_call(body, name, gs, outs, n)


def _chip_exchange(hs, name):
    n = len(hs)

    def body(*refs):
        h_refs, o_refs, send_sems, recv_sems = refs[:n], refs[n:2 * n], refs[2 * n], refs[2 * n + 1]
        x, y, c, chips = _position()
        copies = []
        for i in range(n):
            for j, chip in enumerate(chips):
                cp = pltpu.make_async_remote_copy(src_ref=h_refs[i].at[_chip_index(*chip)], dst_ref=o_refs[i].at[j],
                                                  send_sem=send_sems.at[3 * i + j], recv_sem=recv_sems.at[3 * i + j],
                                                  device_id=(*chip, c), device_id_type=MESH)
                cp.start()
                copies.append(cp)
        for cp in copies:
            cp.wait()

    return _comm_call(body, name, hs, [jax.ShapeDtypeStruct((3,) + h.shape[1:], h.dtype) for h in hs], 3 * n)


def _sibling_swap(ts, name):
    n = len(ts)

    def body(*refs):
        t_refs, o_refs, send_sems, recv_sems = refs[:n], refs[n:2 * n], refs[2 * n], refs[2 * n + 1]
        x, y, c, _ = _position()
        copies = []
        for i in range(n):
            cp = pltpu.make_async_remote_copy(src_ref=t_refs[i], dst_ref=o_refs[i], send_sem=send_sems.at[i], recv_sem=recv_sems.at[i],
                                              device_id=(x, y, 1 - c), device_id_type=MESH)
            cp.start()
            copies.append(cp)
        for cp in copies:
            cp.wait()

    return _comm_call(body, name, ts, [jax.ShapeDtypeStruct(t.shape, t.dtype) for t in ts], n)


def _row_tile(K, N):
    return _div_tile(K, 256 if N <= 1024 else 128, 16)


def _prefetch_call(body, name, scalars, grid, in_specs, out_specs, out_shape, args, sem):
    gs = pltpu.PrefetchScalarGridSpec(num_scalar_prefetch=1, grid=grid, in_specs=in_specs, out_specs=out_specs)
    return pl.pallas_call(body, name=name, grid_spec=gs, out_shape=out_shape, compiler_params=_cparams(sem))(scalars, *args)


def _add_own_half(g, got, c, name):
    _, nl, K, N = g.shape
    hl = nl // 2
    tm = _row_tile(K, N)

    def body(c_ref, g_ref, r_ref, o_ref):
        o_ref[...] = (g_ref[...].astype(f32) + r_ref[...].astype(f32)).astype(o_ref.dtype)

    blk = (1, 1, tm, N)
    return _prefetch_call(
        body, name, jnp.reshape(c, (1,)).astype(jnp.int32), (4, hl, K // tm),
        [pl.BlockSpec(blk, lambda s, l, i, c_ref: (s, c_ref[0] * hl + l, i, 0)), pl.BlockSpec(blk, lambda s, l, i, c_ref: (s, l, i, 0))],
        pl.BlockSpec(blk, lambda s, l, i, c_ref: (s, l, i, 0)), jax.ShapeDtypeStruct((4, hl, K, N), g.dtype), (g, got),
        ("parallel", "parallel", "parallel"))


def _add_chips(h, got, p, name):
    _, nl, K, N = h.shape
    tm = _row_tile(K, N)

    def body(p_ref, h_ref, r0_ref, r1_ref, r2_ref, o_ref):
        o_ref[...] = ((h_ref[0].astype(f32) + r0_ref[0].astype(f32)) + r1_ref[0].astype(f32)) + r2_ref[0].astype(f32)

    blk = (1, 1, tm, N)
    got_spec = lambda j: pl.BlockSpec(blk, lambda l, i, p_ref: (j, l, i, 0))
    return _prefetch_call(
        body, name, jnp.reshape(p, (1,)).astype(jnp.int32), (nl, K // tm),
        [pl.BlockSpec(blk, lambda l, i, p_ref: (p_ref[0], l, i, 0)), got_spec(0), got_spec(1), got_spec(2)],
        pl.BlockSpec((1, tm, N), lambda l, i, p_ref: (l, i, 0)), jax.ShapeDtypeStruct((nl, K, N), f32), (h, got, got, got),
        ("parallel", "parallel"))


def _adamw_halves(w, m, v, own, other, c, name):
    nl, K, N = w.shape
    hl = nl // 2
    tm = _row_tile(K, N)

    def body(c_ref, w_ref, m_ref, v_ref, own_ref, other_ref, g_out, d_out, m_out, v_out):
        mine = (pl.program_id(0) // hl) == c_ref[0]
        g = jnp.where(mine, own_ref[...], other_ref[...])
        d, m_new, v_new = _adamw_math(w_ref[...], g, m_ref[...], v_ref[...])
        g_out[...] = g
        d_out[...] = d
        m_out[...] = m_new
        v_out[...] = v_new

    blk = (1, tm, N)
    full = pl.BlockSpec(blk, lambda l, i, c_ref: (l, i, 0))
    half = pl.BlockSpec(blk, lambda l, i, c_ref: (l % hl, i, 0))
    return _prefetch_call(
        body, name, jnp.reshape(c, (1,)).astype(jnp.int32), (nl, K // tm), [full, full, full, half, half], [full] * 4,
        [jax.ShapeDtypeStruct((nl, K, N), f32)] * 4, (w, m, v, own, other), ("parallel", "parallel"))


def _reduce_scatter_chips(gs, name):
    c = lax.axis_index("c")
    p = _chip_index(lax.axis_index("x"), lax.axis_index("y"))
    from_sibling = _sibling_halves(gs, name + "_d2d")
    hs = [_add_own_half(g, r, c, f"{name}_add2_{i}") for i, (g, r) in enumerate(zip(gs, from_sibling))]
    got = _chip_exchange(hs, name + "_ici")
    own = [_add_chips(h, r, p, f"{name}_add4_{i}") for i, (h, r) in enumerate(zip(hs, got))]
    other = _sibling_swap(own, name + "_swap")
    return own, other


_BIG = (("gla_w_in", 1), ("gla_w_out", 0), ("dil_w_in", 1), ("dil_w_out", 1), ("ffn_w_up", 1), ("ffn_w_down", 0))


def _pad_rows(a, mult):
    r = (-a.shape[0]) % mult
    return a if r == 0 else jnp.concatenate([a, jnp.zeros((r,) + a.shape[1:], a.dtype)], axis=0)


def _unshard(blocks, axis):
    _, nl, K, N = blocks.shape
    if axis == 1:
        return blocks.transpose(1, 2, 0, 3).reshape(nl, K, 4 * N)
    return blocks.transpose(1, 0, 2, 3).reshape(nl, 4 * K, N)


def _to_shards(mat, axis):
    K, N = mat.shape
    if axis == 1:
        return mat.reshape(K, 4, N // 4).transpose(1, 0, 2)
    return mat.reshape(4, K // 4, N)


def _gather_big_weights(shards):
    allw = _all_gather_chips([shards[n].astype(bf16) for n, _ in _BIG], "gather_weights")
    return {n: _unshard(blocks, axis) for (n, axis), blocks in zip(_BIG, allw)}


def _scatter_big_grads(grads):
    gs = [jnp.stack([_to_shards(g, axis) for g in grads[n]], axis=1) for n, axis in _BIG]
    own, other = _reduce_scatter_chips(gs, "scatter_grads")
    return {n: (a, b) for (n, _), a, b in zip(_BIG, own, other)}


_SMALL = (("gla_w_gate_up", 2), ("gla_gate_bias", None), ("gla_norm_g", None), ("ffn_conv_w", 2), ("ffn_conv_b", None),
          ("ln_g", 2), ("ln_b", 2))


def _pack_rows(arrs, width=LANE):
    flat = _pad_rows(jnp.concatenate([a.reshape(-1) for a in arrs]), 8 * width)
    return flat.reshape(-1, width)


def _unpack_rows(packed, shapes):
    flat, out, off = packed.reshape(-1), [], 0
    for s in shapes:
        n = 1
        for d in s:
            n *= d
        out.append(flat[off:off + n].reshape(s))
        off += n
    return out


def _gather_small_params(shards):
    x, y, c = lax.axis_index("x"), lax.axis_index("y"), lax.axis_index("c")
    names = [n for n, axis in _SMALL if axis is not None]
    mine = _pack_rows([shards[n] for n in names])
    mine = jnp.where(c == 0, mine, jnp.zeros_like(mine))
    rows = mine.shape[0]
    placed = lax.dynamic_update_slice(jnp.zeros((4 * rows, LANE), f32), mine, (_chip_index(x, y) * rows, 0))
    allp = _all_reduce_small(placed, "gather_small").reshape(4, rows, LANE)
    out = {n: shards[n] for n, axis in _SMALL if axis is None}
    per_chip = [_unpack_rows(allp[q], [shards[n].shape for n in names]) for q in range(4)]
    for i, n in enumerate(names):
        out[n] = jnp.concatenate([per_chip[q][i] for q in range(4)], axis=2)
    return out


def _reduce_small_grads(grads, shards):
    names = [n for n, _ in _SMALL]
    total = _all_reduce_small(_pack_rows([grads[n] for n in names]), "reduce_small")
    full = dict(zip(names, _unpack_rows(total, [grads[n].shape for n in names])))
    p = _chip_index(lax.axis_index("x"), lax.axis_index("y"))
    out = {}
    for n, axis in _SMALL:
        if axis is None:
            out[n] = full[n]
        else:
            w = shards[n].shape[axis]
            out[n] = lax.dynamic_slice_in_dim(full[n], p * w, w, axis=axis)
    return out


def _pad_cols(a, n):
    return a if a.shape[-1] == n else jnp.concatenate([a, jnp.zeros(a.shape[:-1] + (n - a.shape[-1],), a.dtype)], axis=-1)


def _ffn_width(F):
    return -(-F // 512) * 512


def kernel(x, gla_w_in, gla_w_gate_up, gla_gate_bias, gla_norm_g, gla_w_out, dil_w_in, dil_w_out, ffn_w_up, ffn_conv_w, ffn_conv_b, ffn_w_down, ln_g, ln_b, loss_target, m_gla_w_in, m_gla_w_gate_up, m_gla_gate_bias, m_gla_norm_g, m_gla_w_out, m_dil_w_in, m_dil_w_out, m_ffn_w_up, m_ffn_conv_w, m_ffn_conv_b, m_ffn_w_down, m_ln_g, m_ln_b, v_gla_w_in, v_gla_w_gate_up, v_gla_gate_bias, v_gla_norm_g, v_gla_w_out, v_dil_w_in, v_dil_w_out, v_ffn_w_up, v_ffn_conv_w, v_ffn_conv_b, v_ffn_w_down, v_ln_g, v_ln_b):
    names = ["gla_w_in", "gla_w_gate_up", "gla_gate_bias", "gla_norm_g", "gla_w_out", "dil_w_in", "dil_w_out", "ffn_w_up",
             "ffn_conv_w", "ffn_conv_b", "ffn_w_down", "ln_g", "ln_b"]
    w_sh = dict(zip(names, (gla_w_in, gla_w_gate_up, gla_gate_bias, gla_norm_g, gla_w_out, dil_w_in, dil_w_out, ffn_w_up,
                            ffn_conv_w, ffn_conv_b, ffn_w_down, ln_g, ln_b)))
    m_sh = dict(zip(names, (m_gla_w_in, m_gla_w_gate_up, m_gla_gate_bias, m_gla_norm_g, m_gla_w_out, m_dil_w_in, m_dil_w_out,
                            m_ffn_w_up, m_ffn_conv_w, m_ffn_conv_b, m_ffn_w_down, m_ln_g, m_ln_b)))
    v_sh = dict(zip(names, (v_gla_w_in, v_gla_w_gate_up, v_gla_gate_bias, v_gla_norm_g, v_gla_w_out, v_dil_w_in, v_dil_w_out,
                            v_ffn_w_up, v_ffn_conv_w, v_ffn_conv_b, v_ffn_w_down, v_ln_g, v_ln_b)))
    nseq, S, D = x.shape
    T = nseq * S
    big = _gather_big_weights(w_sh)
    small = _gather_small_params(w_sh)
    F = big["ffn_w_down"].shape[1]
    Fp = _ffn_width(F)
    qkvr = big["gla_w_in"].shape[2] - GLA_GATE_RANK
    c_idx = lax.axis_index("c")

    def pad_halves(a):
        return jnp.concatenate([_pad_cols(a[..., :F], Fp), _pad_cols(a[..., F:], Fp)], axis=-1)

    w_up_all = pad_halves(big["ffn_w_up"])
    w_down_all = jnp.pad(big["ffn_w_down"], ((0, 0), (0, Fp - F), (0, 0)))
    cw_all = pad_halves(small["ffn_conv_w"])
    cb_all = pad_halves(small["ffn_conv_b"][:, None, :])
    w_in_all = big["gla_w_in"]
    w_gate_all = _pad_cols(w_in_all[..., qkvr:], GATE_PAD)
    w_gate_up_all = jnp.pad(small["gla_w_gate_up"].astype(bf16), ((0, 0), (0, GATE_PAD - GLA_GATE_RANK), (0, 0)))

    h0 = x.reshape(T, D)
    saved = []
    cur = h0
    for i in range(DEPTH):
        j = i // 2
        tag = f"l{i}_"
        lg, lb = small["ln_g"][i], small["ln_b"][i]
        if i % 2 == 0:
            gate_bias, norm_g = small["gla_gate_bias"][j][None, :], small["gla_norm_g"][j][None, :]
            proj = _mm(cur, w_in_all, "nn", tag + "gla_proj", layer=j, n_out=qkvr)
            g_low = _mm(cur, w_gate_all, "nn", tag + "gla_glow", tn=GATE_PAD, layer=j)
            z = _mm(g_low, w_gate_up_all, "nn", tag + "gla_z", layer=j)
            gate = _log_gate_fwd(z, gate_bias, tag + "gla_gate")
            o, states = _gla_fwd(proj, gate, nseq, tag + "gla_core")
            y = _rms_gate_fwd(o, proj, norm_g, tag + "gla_norm")
            mix = _mm(y, big["gla_w_out"], "nn", tag + "gla_out", layer=j)
            mixer_saved = (proj, g_low, z, gate, states, o, y)
        else:
            proj = _mm(cur, big["dil_w_in"], "nn", tag + "dil_proj", layer=j)
            outs, lses = [], []
            for gi, (window, dilation) in enumerate(DIL_PATTERNS):
                og, lg_ = _dil_fwd(proj, gi, window, dilation, nseq, tag + f"dil_attn{gi}")
                outs.append(og)
                lses.append(lg_)
            y, lse_tot = _dil_mix_fwd(outs, lses, tag + "dil_mix")
            mix = _mm(y, big["dil_w_out"], "nn", tag + "dil_out", layer=j)
            mixer_saved = (proj, y, lse_tot)
        x1 = _ln_fwd(cur, mix, lg[0:1], lb[0:1], tag + "ln1")
        cw, cb = cw_all[i], cb_all[i]
        hh = _mm(x1, w_up_all, "nn", tag + "ffn_up", layer=i)
        act = _conv_gate_fwd(hh, cw, cb, nseq, tag + "ffn_conv")
        ffn = _mm(act, w_down_all, "nn", tag + "ffn_down", layer=i)
        x2 = _ln_fwd(x1, ffn, lg[1:2], lb[1:2], tag + "ln2")
        saved.append((cur, mix, x1, hh, act, ffn, mixer_saved))
        cur = x2

    dy, sq = _loss_head(cur, loss_target.reshape(T, D), "loss_head")
    loss = lax.psum(0.5 * jnp.sum(sq) / D, ("x", "y", "c"))

    gb = {n: [None] * w_sh[n].shape[0] for n in names}
    d_res = None
    d_in = dy
    for i in reversed(range(DEPTH)):
        j = i // 2
        tag = f"l{i}_b_"
        xin, mix, x1, hh, act, ffn, mixer_saved = saved[i]
        lg = small["ln_g"][i]
        cw, cb = cw_all[i], cb_all[i]
        dw_tiles = dict(tm=1024, tn=1024, tk=1024, out_dtype=bf16)
        dys, scales = ([d_in], [1.0]) if d_res is None else ([d_res, d_in], [DEEPNORM_ALPHA, 1.0])
        du2, dg2, db2 = _ln_bwd(x1, ffn, lg[1:2], dys, scales, tag + "ln2")
        gb["ffn_w_down"][i] = _mm(act, du2, "tn", tag + "ffn_down_dw", **dw_tiles)[:F]
        dact = _mm(du2, w_down_all, "nt", tag + "ffn_down_dx", layer=i)
        dh, dcw, dcb = _conv_gate_bwd(hh, dact, cw, cb, nseq, tag + "ffn_conv")
        gb["ffn_conv_w"][i] = jnp.concatenate([dcw[:, :F], dcw[:, Fp:Fp + F]], axis=1)
        gb["ffn_conv_b"][i] = jnp.concatenate([dcb[0, :F], dcb[0, Fp:Fp + F]], axis=0)
        dwu = _mm(x1, dh, "tn", tag + "ffn_up_dw", **dw_tiles)
        gb["ffn_w_up"][i] = jnp.concatenate([dwu[:, :F], dwu[:, Fp:Fp + F]], axis=1)
        dx1 = _mm(dh, w_up_all, "nt", tag + "ffn_up_dx", layer=i)
        du1, dg1, db1 = _ln_bwd(xin, mix, lg[0:1], [du2, dx1], [DEEPNORM_ALPHA, 1.0], tag + "ln1")
        gb["ln_g"][i] = jnp.concatenate([dg1, dg2], axis=0)
        gb["ln_b"][i] = jnp.concatenate([db1, db2], axis=0)
        if i % 2 == 0:
            proj, g_low, z, gate, states, o, y = mixer_saved
            gate_bias, norm_g = small["gla_gate_bias"][j][None, :], small["gla_norm_g"][j][None, :]
            gb["gla_w_out"][j] = _mm(y, du1, "tn", tag + "gla_out_dw", **dw_tiles)
            dyy = _mm(du1, big["gla_w_out"], "nt", tag + "gla_out_dx", layer=j)
            do, dr, dng = _rms_gate_bwd(o, proj, norm_g, dyy, tag + "gla_norm")
            gb["gla_norm_g"][j] = dng[0]
            dq, dk_, dv_, dgate = _gla_bwd(proj, gate, states, do, nseq, tag + "gla_core")
            dz, dbias = _log_gate_bwd(z, gate_bias, dgate, tag + "gla_gate")
            gb["gla_gate_bias"][j] = dbias[0]
            gb["gla_w_gate_up"][j] = _mm(g_low, dz, "tn", tag + "gla_z_dw", tk=1024)[:GLA_GATE_RANK]
            dg_low = _mm(dz, w_gate_up_all, "nt", tag + "gla_z_dx", tn=GATE_PAD, layer=j)
            dproj = jnp.concatenate([dq, dk_, dv_, dr], axis=1)
            dw_main = _mm(xin, dproj, "tn", tag + "gla_proj_dw", **dw_tiles)
            dw_gate = _mm(xin, dg_low, "tn", tag + "gla_glow_dw", tm=1024, tn=GATE_PAD, tk=1024, out_dtype=bf16)[:, :GLA_GATE_RANK]
            gb["gla_w_in"][j] = jnp.concatenate([dw_main, dw_gate], axis=1)
            dxa = _mm(dproj, w_in_all, "nt", tag + "gla_proj_dx", layer=j)
            dxb = _mm(dg_low, w_gate_all, "nt", tag + "gla_glow_dx", layer=j)
            d_in = _axpy(dxa, dxb, 1.0, tag + "gla_dx_sum")
        else:
            proj, y, lse_tot = mixer_saved
            gb["dil_w_out"][j] = _mm(y, du1, "tn", tag + "dil_out_dw", **dw_tiles)
            dyy = _mm(du1, big["dil_w_out"], "nt", tag + "dil_out_dx", layer=j)
            delta = _dil_delta(dyy, y, tag + "dil_delta")
            pieces = []
            for gi, (window, dilation) in enumerate(DIL_PATTERNS):
                pieces += list(_dil_bwd(proj, dyy, lse_tot, delta, gi, window, dilation, nseq, tag + f"dil_attn{gi}"))
            dproj = jnp.concatenate(pieces, axis=1)
            gb["dil_w_in"][j] = _mm(xin, dproj, "tn", tag + "dil_proj_dw", **dw_tiles)
            d_in = _mm(dproj, big["dil_w_in"], "nt", tag + "dil_proj_dx", layer=j)
        d_res = du1
    grad_x = _axpy(d_res, d_in, DEEPNORM_ALPHA, "grad_x").reshape(x.shape)

    halves = _scatter_big_grads(gb)
    grads, delta, new_m, new_v = {}, {}, {}, {}
    for n, _ in _BIG:
        own, other = halves[n]
        grads[n], delta[n], new_m[n], new_v[n] = _adamw_halves(w_sh[n], m_sh[n], v_sh[n], own, other, c_idx, "adamw_" + n)
    grads.update(_reduce_small_grads({n: jnp.stack(gb[n], axis=0) for n, _ in _SMALL}, w_sh))
    small_names = [n for n, _ in _SMALL]
    packed = [_pack_rows([src[n] for n in small_names]) for src in (w_sh, grads, m_sh, v_sh)]
    res = _adamw(*packed, "adamw_small")
    shapes = [w_sh[n].shape for n in small_names]
    for dst, arr in zip((delta, new_m, new_v), res):
        dst.update(dict(zip(small_names, _unpack_rows(arr, shapes))))

    return (loss, grad_x, *[grads[n] for n in names], *[delta[n] for n in names], *[new_m[n] for n in names],
            *[new_v[n] for n in names])
```

```python
import functools

import jax
import jax.numpy as jnp
from jax import lax
from jax.experimental import pallas as pl
from jax.experimental.pallas import tpu as pltpu

f32 = jnp.float32
bf16 = jnp.bfloat16
_MXU = jnp.bfloat16

DEPTH = 4
GLA_HEADS = 4
GLA_GATE_RANK = 16
GLA_GATE_NORMALIZER = 16.0
GLA_CHUNK = 64
GLA_SUB = 16
DIL_PATTERNS = ((128, 1), (512, 4), (2048, 16))
DIL_HEADS = 8
DIL_HEAD_DIM = 128
DIL_BLOCK = 128
DEEPNORM_ALPHA = (2 * DEPTH) ** 0.25
LN_EPS = 1e-5
RMS_EPS = 1e-6
ADAM_LR = 0.001
ADAM_B1 = 0.9
ADAM_B2 = 0.999
ADAM_EPS = 1e-08
ADAM_WD = 0.01
ADAM_STEP = 10

LANE = 128
VMEM_LIMIT = 48 * 1024 * 1024
GATE_PAD = LANE
MESH = pl.DeviceIdType.MESH


def _cparams(sem=None):
    return pltpu.CompilerParams(dimension_semantics=sem, vmem_limit_bytes=VMEM_LIMIT)


def _div_tile(n, pref, unit):
    if n <= pref:
        return n
    best = None
    for t in range(unit, pref + 1, unit):
        if n % t == 0:
            best = t
    assert best is not None, (n, pref, unit)
    return best


def _dot(a, b, ca, cb):
    return lax.dot_general(a.astype(_MXU), b.astype(_MXU), (((ca,), (cb,)), ((), ())), preferred_element_type=f32)


def _nn(a, b):
    return _dot(a, b, 1, 0)


def _nt(a, b):
    return _dot(a, b, 1, 1)


def _tn(a, b):
    return _dot(a, b, 0, 0)


def _exact_dot(a, b):
    return jnp.dot(a, b, precision=lax.Precision.HIGHEST, preferred_element_type=f32)


def _sigmoid(x):
    return 1.0 / (1.0 + jnp.exp(-x))


def _mm(a, b, mode, name, tm=1024, tn=512, tk=2048, out_dtype=f32, layer=None, n_out=None, b_col0=0):
    if mode == "nn":
        (M, K), N = a.shape, (n_out or b.shape[-1])
    elif mode == "nt":
        (M, K), N = a.shape, b.shape[-2]
    else:
        (K, M), N = a.shape, b.shape[-1]
    tm, tn, tk = _div_tile(M, tm, LANE), _div_tile(N, tn, LANE), _div_tile(K, tk, LANE)
    nk = K // tk
    if mode == "tn":
        a_spec = pl.BlockSpec((tk, tm), lambda i, j, k: (k, i))
    else:
        a_spec = pl.BlockSpec((tm, tk), lambda i, j, k: (i, k))
    lead = () if layer is None else (None,)
    pre = () if layer is None else (layer,)
    if mode == "nt":
        k0 = b_col0 // tk
        assert k0 * tk == b_col0
        b_spec = pl.BlockSpec(lead + (tn, tk), lambda i, j, k: pre + (j, k + k0))
    else:
        b_spec = pl.BlockSpec(lead + (tk, tn), lambda i, j, k: pre + (k, j))
    ca, cb = {"nn": (1, 0), "nt": (1, 1), "tn": (0, 0)}[mode]

    def body(a_ref, b_ref, o_ref, *acc):
        p = _dot(a_ref[...], b_ref[...], ca, cb)
        if nk == 1:
            o_ref[...] = p.astype(o_ref.dtype)
        else:
            k = pl.program_id(2)
            acc_ref = acc[0]

            @pl.when(k == 0)
            def _():
                acc_ref[...] = p

            @pl.when(k > 0)
            def _():
                acc_ref[...] += p

            @pl.when(k == nk - 1)
            def _():
                o_ref[...] = acc_ref[...].astype(o_ref.dtype)

    return pl.pallas_call(
        body, name=name, grid=(M // tm, N // tn, nk), in_specs=[a_spec, b_spec],
        out_specs=pl.BlockSpec((tm, tn), lambda i, j, k: (i, j)),
        out_shape=jax.ShapeDtypeStruct((M, N), out_dtype),
        scratch_shapes=[pltpu.VMEM((tm, tn), f32)] if nk > 1 else [],
        compiler_params=_cparams(("parallel", "parallel", "arbitrary")),
    )(a, b)


def _rowwise(fn, rows, consts, outs, reds, name, tm=256):
    T = rows[0][0].shape[0]
    tm = _div_tile(T, tm, 8)
    n_r, n_c, n_o = len(rows), len(consts), len(outs)

    def body(*refs):
        ins = [r[...] for r in refs[: n_r + n_c]]
        res = fn(*ins)
        res = res if isinstance(res, (tuple, list)) else (res,)
        o_refs = refs[n_r + n_c: n_r + n_c + n_o]
        r_refs = refs[n_r + n_c + n_o:]
        for ref, val in zip(o_refs, res[:n_o]):
            ref[...] = val.astype(ref.dtype)
        i = pl.program_id(0)
        for ref, val in zip(r_refs, res[n_o:]):
            _accumulate(ref, val, i)

    in_specs = [pl.BlockSpec((tm, w), functools.partial(lambda i, cb: (i, cb), cb=cb)) for (_, w, cb) in rows]
    in_specs += [pl.BlockSpec(c.shape, lambda i: (0, 0)) for c in consts]
    out_specs = [pl.BlockSpec((tm, w), lambda i: (i, 0)) for (w, _) in outs]
    out_specs += [pl.BlockSpec((1, w), lambda i: (0, 0)) for w in reds]
    out_shape = [jax.ShapeDtypeStruct((T, w), dt) for (w, dt) in outs]
    out_shape += [jax.ShapeDtypeStruct((1, w), f32) for w in reds]
    return pl.pallas_call(
        body, name=name, grid=(T // tm,), in_specs=in_specs, out_specs=out_specs, out_shape=out_shape,
        compiler_params=_cparams(("arbitrary",)),
    )(*[r[0] for r in rows], *consts)


def _accumulate(ref, val, step):
    @pl.when(step == 0)
    def _():
        ref[...] = val

    @pl.when(step > 0)
    def _():
        ref[...] += val


def _full(a):
    return (a, a.shape[1], 0)


def _colsum(x):
    return jnp.sum(x, axis=0, keepdims=True)


def _ln_stats(u):
    mu = jnp.mean(u, axis=-1, keepdims=True)
    xc = u - mu
    var = jnp.mean(xc * xc, axis=-1, keepdims=True)
    rstd = lax.rsqrt(var + LN_EPS)
    return xc * rstd, rstd


def _ln_fwd(x, f, g, b, name):
    def fn(x, f, g, b):
        xhat, _ = _ln_stats(DEEPNORM_ALPHA * x + f)
        y = xhat * g + b
        return y, y

    return _rowwise(fn, [_full(x), _full(f)], [g, b], [(x.shape[1], f32), (x.shape[1], _MXU)], [], name)


def _ln_bwd(x, f, g, dys, scales, name):
    def fn(x, f, *rest):
        g = rest[-1]
        dy = None
        for d, s in zip(rest[:-1], scales):
            t = d if s == 1.0 else s * d
            dy = t if dy is None else dy + t
        xhat, rstd = _ln_stats(DEEPNORM_ALPHA * x + f)
        dxh = dy * g
        m1 = jnp.mean(dxh, axis=-1, keepdims=True)
        m2 = jnp.mean(dxh * xhat, axis=-1, keepdims=True)
        du = rstd * (dxh - m1 - xhat * m2)
        return du, du, _colsum(dy * xhat), _colsum(dy)

    D = x.shape[1]
    return _rowwise(fn, [_full(x), _full(f)] + [_full(d) for d in dys], [g], [(D, f32), (D, _MXU)], [D, D], name)


def _loss_head(y, t, name):
    D = y.shape[1]

    def fn(y, t):
        e = y - t
        return e * (1.0 / D), _colsum(e * e)

    return _rowwise(fn, [_full(y), _full(t)], [], [(D, f32)], [D], name)


def _axpy(a, b, alpha, name):
    def fn(a, b):
        return alpha * a + b

    return _rowwise(fn, [_full(a), _full(b)], [], [(a.shape[1], f32)], [], name)[0]


def _shift_down(x, k):
    row = lax.broadcasted_iota(jnp.int32, x.shape, 0)
    return jnp.where(row >= k, pltpu.roll(x, k, 0), 0.0)


def _shift_up(x, k):
    S = x.shape[0]
    row = lax.broadcasted_iota(jnp.int32, x.shape, 0)
    return jnp.where(row < S - k, pltpu.roll(x, S - k, 0), 0.0)


def _causal_conv(h, w, b):
    return ((b + w[0:1] * _shift_down(h, 2)) + w[1:2] * _shift_down(h, 1)) + w[2:3] * h


def _conv_gate_fwd(h, cw, cb, nseq, name, tc=256):
    T, F2 = h.shape
    F, S = F2 // 2, T // nseq
    tc = _div_tile(F, tc, LANE)
    nf = F // tc

    def body(hg_ref, hu_ref, wg_ref, wu_ref, bg_ref, bu_ref, a_ref):
        cg = _causal_conv(hg_ref[...], wg_ref[...], bg_ref[...])
        cu = _causal_conv(hu_ref[...], wu_ref[...], bu_ref[...])
        a_ref[...] = (cg * _sigmoid(cg) * cu).astype(a_ref.dtype)

    return pl.pallas_call(
        body, name=name, grid=(nseq, nf),
        in_specs=[pl.BlockSpec((S, tc), lambda s, j: (s, j)), pl.BlockSpec((S, tc), lambda s, j: (s, nf + j)),
                  pl.BlockSpec((3, tc), lambda s, j: (0, j)), pl.BlockSpec((3, tc), lambda s, j: (0, nf + j)),
                  pl.BlockSpec((1, tc), lambda s, j: (0, j)), pl.BlockSpec((1, tc), lambda s, j: (0, nf + j))],
        out_specs=pl.BlockSpec((S, tc), lambda s, j: (s, j)),
        out_shape=jax.ShapeDtypeStruct((T, F), _MXU),
        compiler_params=_cparams(("parallel", "parallel")),
    )(h, h, cw, cw, cb, cb)


def _conv_gate_bwd(h, da, cw, cb, nseq, name, tc=128):
    T, F2 = h.shape
    F, S = F2 // 2, T // nseq
    tc = _div_tile(F, tc, LANE)
    nf = F // tc

    def conv_bwd(dc, hx, w):
        dh = (w[2:3] * dc + w[1:2] * _shift_up(dc, 1)) + w[0:1] * _shift_up(dc, 2)
        dw = jnp.concatenate([_colsum(dc * _shift_down(hx, 2)), _colsum(dc * _shift_down(hx, 1)), _colsum(dc * hx)], axis=0)
        return dh, dw, _colsum(dc)

    def body(hg_ref, hu_ref, da_ref, wg_ref, wu_ref, bg_ref, bu_ref, dhg_ref, dhu_ref, dwg_ref, dwu_ref, dbg_ref, dbu_ref):
        hg, hu, da = hg_ref[...], hu_ref[...], da_ref[...]
        wg, wu = wg_ref[...], wu_ref[...]
        cg = _causal_conv(hg, wg, bg_ref[...])
        cu = _causal_conv(hu, wu, bu_ref[...])
        sg = _sigmoid(cg)
        dcu = da * (cg * sg)
        dcg = da * cu * (sg * (1.0 + cg * (1.0 - sg)))
        dhg, dwg, dbg = conv_bwd(dcg, hg, wg)
        dhu, dwu, dbu = conv_bwd(dcu, hu, wu)
        dhg_ref[...] = dhg.astype(dhg_ref.dtype)
        dhu_ref[...] = dhu.astype(dhu_ref.dtype)
        s = pl.program_id(1)
        _accumulate(dwg_ref, dwg, s)
        _accumulate(dwu_ref, dwu, s)
        _accumulate(dbg_ref, dbg, s)
        _accumulate(dbu_ref, dbu, s)

    col = lambda j, s: (s, j)
    par = lambda j, s: (0, j)
    dhg, dhu, dwg, dwu, dbg, dbu = pl.pallas_call(
        body, name=name, grid=(nf, nseq),
        in_specs=[pl.BlockSpec((S, tc), col), pl.BlockSpec((S, tc), lambda j, s: (s, nf + j)), pl.BlockSpec((S, tc), col),
                  pl.BlockSpec((3, tc), par), pl.BlockSpec((3, tc), lambda j, s: (0, nf + j)),
                  pl.BlockSpec((1, tc), par), pl.BlockSpec((1, tc), lambda j, s: (0, nf + j))],
        out_specs=[pl.BlockSpec((S, tc), col), pl.BlockSpec((S, tc), col), pl.BlockSpec((3, tc), par), pl.BlockSpec((3, tc), par),
                   pl.BlockSpec((1, tc), par), pl.BlockSpec((1, tc), par)],
        out_shape=[jax.ShapeDtypeStruct((T, F), _MXU), jax.ShapeDtypeStruct((T, F), _MXU), jax.ShapeDtypeStruct((3, F), f32),
                   jax.ShapeDtypeStruct((3, F), f32), jax.ShapeDtypeStruct((1, F), f32), jax.ShapeDtypeStruct((1, F), f32)],
        compiler_params=_cparams(("parallel", "arbitrary")),
    )(h, h, da, cw, cw, cb, cb)
    return dhg, dhu, jnp.concatenate([dwg, dwu], axis=1), jnp.concatenate([dbg, dbu], axis=1)


def _group_row(x, jj):
    C, d = x.shape
    n = C // GLA_SUB
    x3 = x.reshape(n, GLA_SUB, d)
    return jnp.broadcast_to(x3[:, jj:jj + 1, :], (n, GLA_SUB, d)).reshape(C, d)


def _group_sum(x):
    C, d = x.shape
    n = C // GLA_SUB
    s = jnp.sum(x.reshape(n, GLA_SUB, d), axis=1, keepdims=True)
    return jnp.broadcast_to(s, (n, GLA_SUB, d)).reshape(C, d)


def _chunk_cumsum(g):
    C = g.shape[0]
    row = lax.broadcasted_iota(jnp.int32, (C, C), 0)
    col = lax.broadcasted_iota(jnp.int32, (C, C), 1)
    return _exact_dot((row >= col).astype(f32), g)


def _chunk_suffix_sum(x):
    C = x.shape[0]
    row = lax.broadcasted_iota(jnp.int32, (C, C), 0)
    col = lax.broadcasted_iota(jnp.int32, (C, C), 1)
    return _exact_dot((col >= row).astype(f32), x)


def _gla_scores(q, k, b):
    C = q.shape[0]
    n = C // GLA_SUB
    row = lax.broadcasted_iota(jnp.int32, (C, C), 0)
    col = lax.broadcasted_iota(jnp.int32, (C, C), 1)
    blocks = [jnp.zeros((GLA_SUB, C), f32)]
    for s in range(1, n):
        lo = s * GLA_SUB
        bref = b[lo - 1:lo, :]
        qr = q[lo:lo + GLA_SUB] * jnp.exp(b[lo:lo + GLA_SUB] - bref)
        kr = k * jnp.exp(jnp.minimum(bref - b, 0.0))
        blocks.append(_nt(qr, kr))
    sub_start = (row // GLA_SUB) * GLA_SUB
    a = jnp.where(col < sub_start, jnp.concatenate(blocks, axis=0), 0.0)
    rin = lax.broadcasted_iota(jnp.int32, (C, 1), 0) % GLA_SUB
    for jj in range(GLA_SUB):
        e = jnp.exp(jnp.minimum(b - _group_row(b, jj), 0.0))
        colv = jnp.sum(q * _group_row(k, jj) * e, axis=1, keepdims=True)
        colv = jnp.where(rin >= jj, colv, 0.0)
        a = jnp.where(col == sub_start + jj, colv, a)
    return a


def _gla_scores_bwd(da, q, k, b):
    C = q.shape[0]
    n = C // GLA_SUB
    row = lax.broadcasted_iota(jnp.int32, (C, C), 0)
    col = lax.broadcasted_iota(jnp.int32, (C, C), 1)
    sub_start = (row // GLA_SUB) * GLA_SUB
    da_inter = jnp.where(col < sub_start, da, 0.0)
    dq_blocks = [jnp.zeros((GLA_SUB, q.shape[1]), f32)]
    dk = jnp.zeros_like(k)
    for s in range(1, n):
        lo = s * GLA_SUB
        bref = b[lo - 1:lo, :]
        eq = jnp.exp(b[lo:lo + GLA_SUB] - bref)
        ek = jnp.exp(jnp.minimum(bref - b, 0.0))
        das = da_inter[lo:lo + GLA_SUB]
        dq_blocks.append(_nn(das, k * ek) * eq)
        dk = dk + _tn(das, q[lo:lo + GLA_SUB] * eq) * ek
    dq = jnp.concatenate(dq_blocks, axis=0)
    rin = lax.broadcasted_iota(jnp.int32, (C, 1), 0) % GLA_SUB
    for jj in range(GLA_SUB):
        e = jnp.exp(jnp.minimum(b - _group_row(b, jj), 0.0))
        dac = jnp.sum(jnp.where(col == sub_start + jj, da, 0.0), axis=1, keepdims=True)
        dac = jnp.where(rin >= jj, dac, 0.0)
        w = dac * e
        dq = dq + w * _group_row(k, jj)
        dk = dk + jnp.where(rin == jj, _group_sum(w * q), 0.0)
    return dq, dk


def _gla_specs(nC, dk, dv):
    H = GLA_HEADS
    voff = (2 * H * dk) // dv
    assert voff * dv == 2 * H * dk
    return H, voff


def _gla_fwd(proj, gate, nseq, name):
    T = proj.shape[0]
    dk = gate.shape[1] // GLA_HEADS
    dv = 2 * dk
    C = GLA_CHUNK
    nC = T // nseq // C
    H, voff = _gla_specs(nC, dk, dv)
    scale = dk ** -0.5

    def body(q_ref, k_ref, v_ref, g_ref, o_ref, st_ref, state):
        c = pl.program_id(2)

        @pl.when(c == 0)
        def _():
            state[...] = jnp.zeros_like(state)

        q, k, v = q_ref[...] * scale, k_ref[...], v_ref[...]
        b = _chunk_cumsum(g_ref[...])
        st = state[...]
        st_ref[0] = st
        a = _gla_scores(q, k, b)
        o_ref[...] = _nt(q * jnp.exp(b), st) + _nn(a, v)
        bl = b[C - 1:C, :]
        state[...] = st * jnp.exp(bl) + _tn(v, k * jnp.exp(bl - b))

    row = lambda s, h, c: s * nC + c
    return pl.pallas_call(
        body, name=name, grid=(nseq, H, nC),
        in_specs=[pl.BlockSpec((C, dk), lambda s, h, c: (row(s, h, c), h)),
                  pl.BlockSpec((C, dk), lambda s, h, c: (row(s, h, c), H + h)),
                  pl.BlockSpec((C, dv), lambda s, h, c: (row(s, h, c), voff + h)),
                  pl.BlockSpec((C, dk), lambda s, h, c: (row(s, h, c), h))],
        out_specs=[pl.BlockSpec((C, dv), lambda s, h, c: (row(s, h, c), h)),
                   pl.BlockSpec((1, dv, dk), lambda s, h, c: ((s * H + h) * nC + c, 0, 0))],
        out_shape=[jax.ShapeDtypeStruct((T, H * dv), f32), jax.ShapeDtypeStruct((nseq * H * nC, dv, dk), f32)],
        scratch_shapes=[pltpu.VMEM((dv, dk), f32)],
        compiler_params=_cparams(("parallel", "parallel", "arbitrary")),
    )(proj, proj, proj, gate)


def _gla_bwd(proj, gate, states, do, nseq, name):
    T = proj.shape[0]
    dk = gate.shape[1] // GLA_HEADS
    dv = 2 * dk
    C = GLA_CHUNK
    nC = T // nseq // C
    H, voff = _gla_specs(nC, dk, dv)
    scale = dk ** -0.5

    def body(q_ref, k_ref, v_ref, g_ref, do_ref, st_ref, dq_ref, dk_ref, dv_ref, dg_ref, dstate, term):
        c = pl.program_id(2)

        @pl.when(c == 0)
        def _():
            dstate[...] = jnp.zeros_like(dstate)
            term[...] = jnp.zeros_like(term)

        q, k, v, do = q_ref[...] * scale, k_ref[...], v_ref[...], do_ref[...]
        b = _chunk_cumsum(g_ref[...])
        st = st_ref[0]
        dst = dstate[...]
        eb = jnp.exp(b)
        bl = b[C - 1:C, :]
        kdec = jnp.exp(bl - b)
        a = _gla_scores(q, k, b)
        rowi = lax.broadcasted_iota(jnp.int32, (C, C), 0)
        coli = lax.broadcasted_iota(jnp.int32, (C, C), 1)
        da = jnp.where(coli <= rowi, _nt(do, v), 0.0)
        dq_s, dk_s = _gla_scores_bwd(da, q, k, b)
        dq = _nn(do, st) * eb + dq_s
        dkk = _nn(v, dst) * kdec + dk_s
        dv_ref[...] = (_tn(a, do) + _nt(k * kdec, dst)).astype(dv_ref.dtype)
        last = lax.broadcasted_iota(jnp.int32, (C, 1), 0) == C - 1
        db = q * dq - k * dkk + jnp.where(last, term[...], 0.0)
        dg_ref[...] = _chunk_suffix_sum(db)
        dq_ref[...] = (dq * scale).astype(dq_ref.dtype)
        dk_ref[...] = dkk.astype(dk_ref.dtype)
        dprev = dst * jnp.exp(bl) + _tn(do, q * eb)
        dstate[...] = dprev
        term[...] = _colsum(st * dprev)

    row = lambda s, h, c: s * nC + (nC - 1 - c)
    kspec = lambda off: pl.BlockSpec((C, dk), lambda s, h, c: (row(s, h, c), off + h))
    vspec = lambda off: pl.BlockSpec((C, dv), lambda s, h, c: (row(s, h, c), off + h))
    return pl.pallas_call(
        body, name=name, grid=(nseq, H, nC),
        in_specs=[kspec(0), kspec(H), vspec(voff), kspec(0), vspec(0),
                  pl.BlockSpec((1, dv, dk), lambda s, h, c: ((s * H + h) * nC + (nC - 1 - c), 0, 0))],
        out_specs=[kspec(0), kspec(0), vspec(0), kspec(0)],
        out_shape=[jax.ShapeDtypeStruct((T, H * dk), _MXU), jax.ShapeDtypeStruct((T, H * dk), _MXU),
                   jax.ShapeDtypeStruct((T, H * dv), _MXU), jax.ShapeDtypeStruct((T, H * dk), f32)],
        scratch_shapes=[pltpu.VMEM((dv, dk), f32), pltpu.VMEM((1, dk), f32)],
        compiler_params=_cparams(("parallel", "parallel", "arbitrary")),
    )(proj, proj, proj, gate, do, states)


def _head_slices(width, n):
    w = width // n
    return [slice(h * w, (h + 1) * w) for h in range(n)]


def _rms_gate_fwd(o, proj, ng, name):
    W = o.shape[1]

    def fn(o, r, ng):
        parts = []
        for sl in _head_slices(W, GLA_HEADS):
            oh = o[:, sl]
            rstd = lax.rsqrt(jnp.mean(oh * oh, axis=-1, keepdims=True) + RMS_EPS)
            rh = r[:, sl]
            parts.append((oh * rstd * ng) * (rh * _sigmoid(rh)))
        return jnp.concatenate(parts, axis=1)

    return _rowwise(fn, [_full(o), (proj, W, 2)], [ng], [(W, _MXU)], [], name)[0]


def _rms_gate_bwd(o, proj, ng, dy, name):
    W = o.shape[1]

    def fn(o, r, dy, ng):
        dos, drs = [], []
        dng = jnp.zeros((1, W // GLA_HEADS), f32)
        for sl in _head_slices(W, GLA_HEADS):
            oh, rh, dyh = o[:, sl], r[:, sl], dy[:, sl]
            rstd = lax.rsqrt(jnp.mean(oh * oh, axis=-1, keepdims=True) + RMS_EPS)
            ohat = oh * rstd
            sg = _sigmoid(rh)
            don = dyh * (rh * sg)
            drs.append(dyh * (ohat * ng) * (sg * (1.0 + rh * (1.0 - sg))))
            dng = dng + _colsum(don * ohat)
            dohat = don * ng
            dos.append(rstd * (dohat - ohat * jnp.mean(dohat * ohat, axis=-1, keepdims=True)))
        return jnp.concatenate(dos, axis=1), jnp.concatenate(drs, axis=1), dng

    return _rowwise(fn, [_full(o), (proj, W, 2), _full(dy)], [ng], [(W, _MXU), (W, _MXU)], [W // GLA_HEADS], name)


def _log_gate_fwd(z, bias, name):
    def fn(z, bias):
        t = z + bias
        return (jnp.minimum(t, 0.0) - jnp.log1p(jnp.exp(-jnp.abs(t)))) * (1.0 / GLA_GATE_NORMALIZER)

    return _rowwise(fn, [_full(z)], [bias], [(z.shape[1], f32)], [], name)[0]


def _log_gate_bwd(z, bias, dg, name):
    def fn(z, dg, bias):
        dz = dg * (1.0 / GLA_GATE_NORMALIZER) * _sigmoid(-(z + bias))
        return dz, _colsum(dz)

    return _rowwise(fn, [_full(z), _full(dg)], [bias], [(z.shape[1], f32)], [z.shape[1]], name)


def _band_masks(P, steps, has_prev):
    i = lax.broadcasted_iota(jnp.int32, (P, P), 0)
    j = lax.broadcasted_iota(jnp.int32, (P, P), 1)
    cur = (i - j >= 0) & (i - j <= steps)
    prev = (i + P - j <= steps) & has_prev
    return cur, prev


def _dil_dims(T, nseq, dilation):
    L = T // nseq // dilation
    P = min(DIL_BLOCK, L)
    return L, P, L // P


def _dil_tiling(T, nseq, dilation):
    L, P, nb = _dil_dims(T, nseq, dilation)
    hb = DIL_HEADS if dilation == 1 else 1
    row_sets = [pl.ds(r, P, stride=dilation) if dilation > 1 else pl.ds(0, P) for r in range(dilation)]
    head_cols = _head_slices(hb * DIL_HEAD_DIM, hb)
    return L, P, nb, P * dilation, hb, DIL_HEADS // hb, row_sets, head_cols


def _dil_fwd(proj, gi, window, dilation, nseq, name):
    T = proj.shape[0]
    H, dh = DIL_HEADS, DIL_HEAD_DIM
    L, P, nb, SB, hb, ng, row_sets, head_cols = _dil_tiling(T, nseq, dilation)
    steps = window // dilation
    scale = dh ** -0.5

    def body(q_ref, kc_ref, kp_ref, vc_ref, vp_ref, o_ref, lse_ref):
        sb, hg = pl.program_id(1), pl.program_id(2)
        mc, mp = _band_masks(P, steps, sb > 0)
        lane = lax.broadcasted_iota(jnp.int32, (P, LANE), 1)

        @pl.when(hg == 0)
        def _():
            lse_ref[...] = jnp.zeros_like(lse_ref)

        for hh, cols in enumerate(head_cols):
            for rows in row_sets:
                q = q_ref[rows, cols]
                sc = jnp.where(mc, _nt(q, kc_ref[rows, cols]) * scale, -jnp.inf)
                sp = jnp.where(mp, _nt(q, kp_ref[rows, cols]) * scale, -jnp.inf)
                m = jnp.maximum(jnp.max(sc, axis=-1, keepdims=True), jnp.max(sp, axis=-1, keepdims=True))
                pc, pp = jnp.exp(sc - m), jnp.exp(sp - m)
                l = jnp.sum(pc, axis=-1, keepdims=True) + jnp.sum(pp, axis=-1, keepdims=True)
                o_ref[rows, cols] = _nn(pc / l, vc_ref[rows, cols]) + _nn(pp / l, vp_ref[rows, cols])
                lse_ref[rows, :] = jnp.where(lane == hg * hb + hh, m + jnp.log(l), lse_ref[rows, :])

    cur = lambda part: pl.BlockSpec((SB, hb * dh), lambda s, sb, hg: (s * nb + sb, (gi * 3 + part) * ng + hg))
    prv = lambda part: pl.BlockSpec((SB, hb * dh), lambda s, sb, hg: (s * nb + jnp.maximum(sb - 1, 0), (gi * 3 + part) * ng + hg))
    return pl.pallas_call(
        body, name=name, grid=(nseq, nb, ng),
        in_specs=[cur(0), cur(1), prv(1), cur(2), prv(2)],
        out_specs=[pl.BlockSpec((SB, hb * dh), lambda s, sb, hg: (s * nb + sb, hg)), pl.BlockSpec((SB, LANE), lambda s, sb, hg: (s * nb + sb, 0))],
        out_shape=[jax.ShapeDtypeStruct((T, H * dh), f32), jax.ShapeDtypeStruct((T, LANE), f32)],
        compiler_params=_cparams(("parallel", "parallel", "arbitrary")),
    )(proj, proj, proj, proj, proj)


def _dil_mix_fwd(os_, lses, name):
    W = os_[0].shape[1]
    G = len(os_)

    def fn(*a):
        o, l = a[:G], a[G:]
        lane = lax.broadcasted_iota(jnp.int32, l[0].shape, 1)
        tot = jnp.zeros(l[0].shape, f32)
        parts = []
        for h, sl in enumerate(_head_slices(W, DIL_HEADS)):
            lh = [x[:, h:h + 1] for x in l]
            m = functools.reduce(jnp.maximum, lh)
            e = [jnp.exp(x - m) for x in lh]
            z = functools.reduce(lambda u, v: u + v, e)
            acc = None
            for g in range(G):
                t = (e[g] / z) * o[g][:, sl]
                acc = t if acc is None else acc + t
            parts.append(acc)
            tot = jnp.where(lane == h, m + jnp.log(z), tot)
        y = jnp.concatenate(parts, axis=1)
        return y, y, tot

    return _rowwise(fn, [_full(x) for x in os_] + [_full(x) for x in lses], [], [(W, f32), (W, _MXU), (LANE, f32)], [], name)


def _dil_delta(do, o, name):
    W = o.shape[1]

    def fn(do, o):
        lane = lax.broadcasted_iota(jnp.int32, (do.shape[0], LANE), 1)
        d = jnp.zeros((do.shape[0], LANE), f32)
        for h, sl in enumerate(_head_slices(W, DIL_HEADS)):
            d = jnp.where(lane == h, jnp.sum(do[:, sl] * o[:, sl], axis=-1, keepdims=True), d)
        return d

    return _rowwise(fn, [_full(do), _full(o)], [], [(LANE, f32)], [], name)[0]


def _dil_bwd(proj, do, lse, delta, gi, window, dilation, nseq, name):
    T = proj.shape[0]
    H, dh = DIL_HEADS, DIL_HEAD_DIM
    L, P, nb, SB, hb, ng, row_sets, head_cols = _dil_tiling(T, nseq, dilation)
    steps = window // dilation
    scale = dh ** -0.5

    def probs(q, k, lse_h, mask):
        return jnp.where(mask, jnp.exp(_nt(q, k) * scale - lse_h), 0.0)

    def head_lane(ref, rows, h):
        lane = lax.broadcasted_iota(jnp.int32, (P, LANE), 1)
        return jnp.sum(jnp.where(lane == h, ref[rows, :], 0.0), axis=1, keepdims=True)

    def dq_body(q_ref, kc_ref, kp_ref, vc_ref, vp_ref, do_ref, lse_ref, del_ref, dq_ref):
        sb, hg = pl.program_id(1), pl.program_id(2)
        mc, mp = _band_masks(P, steps, sb > 0)
        for hh, cols in enumerate(head_cols):
            h = hg * hb + hh
            for rows in row_sets:
                q, doh = q_ref[rows, cols], do_ref[rows, cols]
                lse_h, del_h = head_lane(lse_ref, rows, h), head_lane(del_ref, rows, h)
                kc, kp = kc_ref[rows, cols], kp_ref[rows, cols]
                dsc = probs(q, kc, lse_h, mc) * (_nt(doh, vc_ref[rows, cols]) - del_h) * scale
                dsp = probs(q, kp, lse_h, mp) * (_nt(doh, vp_ref[rows, cols]) - del_h) * scale
                dq_ref[rows, cols] = _nn(dsc, kc) + _nn(dsp, kp)

    cur = lambda part: pl.BlockSpec((SB, hb * dh), lambda s, sb, hg: (s * nb + sb, (gi * 3 + part) * ng + hg))
    prv = lambda part: pl.BlockSpec((SB, hb * dh), lambda s, sb, hg: (s * nb + jnp.maximum(sb - 1, 0), (gi * 3 + part) * ng + hg))
    tok = pl.BlockSpec((SB, hb * dh), lambda s, sb, hg: (s * nb + sb, hg))
    aux = pl.BlockSpec((SB, LANE), lambda s, sb, hg: (s * nb + sb, 0))
    dq = pl.pallas_call(
        dq_body, name=name + "_dq", grid=(nseq, nb, ng),
        in_specs=[cur(0), cur(1), prv(1), cur(2), prv(2), tok, aux, aux],
        out_specs=tok, out_shape=jax.ShapeDtypeStruct((T, H * dh), f32),
        compiler_params=_cparams(("parallel", "parallel", "parallel")),
    )(proj, proj, proj, proj, proj, do, lse, delta)

    def dkv_body(k_ref, v_ref, qc_ref, qn_ref, doc_ref, don_ref, lsec_ref, lsen_ref, delc_ref, deln_ref, dk_ref, dv_ref):
        sb, hg = pl.program_id(1), pl.program_id(2)
        mc, mn = _band_masks(P, steps, sb < nb - 1)
        for hh, cols in enumerate(head_cols):
            h = hg * hb + hh
            for rows in row_sets:
                k, v = k_ref[rows, cols], v_ref[rows, cols]
                qc, doc = qc_ref[rows, cols], doc_ref[rows, cols]
                pc = probs(qc, k, head_lane(lsec_ref, rows, h), mc)
                dsc = pc * (_nt(doc, v) - head_lane(delc_ref, rows, h)) * scale
                qn, don = qn_ref[rows, cols], don_ref[rows, cols]
                pn = probs(qn, k, head_lane(lsen_ref, rows, h), mn)
                dsn = pn * (_nt(don, v) - head_lane(deln_ref, rows, h)) * scale
                dk_ref[rows, cols] = _tn(dsc, qc) + _tn(dsn, qn)
                dv_ref[rows, cols] = _tn(pc, doc) + _tn(pn, don)

    nxt = lambda s, sb: s * nb + jnp.minimum(sb + 1, nb - 1)
    qnx = pl.BlockSpec((SB, hb * dh), lambda s, sb, hg: (nxt(s, sb), gi * 3 * ng + hg))
    tokn = pl.BlockSpec((SB, hb * dh), lambda s, sb, hg: (nxt(s, sb), hg))
    auxn = pl.BlockSpec((SB, LANE), lambda s, sb, hg: (nxt(s, sb), 0))
    dkk, dvv = pl.pallas_call(
        dkv_body, name=name + "_dkv", grid=(nseq, nb, ng),
        in_specs=[cur(1), cur(2), cur(0), qnx, tok, tokn, aux, auxn, aux, auxn],
        out_specs=[tok, tok],
        out_shape=[jax.ShapeDtypeStruct((T, H * dh), f32), jax.ShapeDtypeStruct((T, H * dh), f32)],
        compiler_params=_cparams(("parallel", "parallel", "parallel")),
    )(proj, proj, proj, proj, do, do, lse, lse, delta, delta)
    return dq, dkk, dvv


def _adamw_math(w, g, m, v):
    m = ADAM_B1 * m + (1.0 - ADAM_B1) * g
    v = ADAM_B2 * v + (1.0 - ADAM_B2) * (g * g)
    m_hat = m / (1.0 - ADAM_B1 ** ADAM_STEP)
    v_hat = v / (1.0 - ADAM_B2 ** ADAM_STEP)
    return -ADAM_LR * (m_hat / (jnp.sqrt(v_hat) + ADAM_EPS) + ADAM_WD * w), m, v


def _adamw(w, g, m, v, name):
    W = w.shape[1]
    return _rowwise(_adamw_math, [_full(w), _full(g), _full(m), _full(v)], [], [(W, f32)] * 3, [], name, tm=512)


def _position():
    x, y, c = lax.axis_index("x"), lax.axis_index("y"), lax.axis_index("c")
    other_chips = [(1 - x, y), (x, 1 - y), (1 - x, 1 - y)]
    return x, y, c, other_chips


def _chip_index(x, y):
    return 2 * x + y


_ANY = pl.BlockSpec(memory_space=pl.ANY)


def _all_reduce_small(p, name):
    R, Wd = p.shape

    def body(p_ref, o_ref, buf, send_sems, recv_sems):
        x, y, c, _ = _position()
        me = 4 * x + 2 * y + c
        buf[me] = p_ref[...]
        copies = []
        for k in range(1, 8):
            fx, fy, fc = (k >> 2) & 1, (k >> 1) & 1, k & 1
            peer = (x + fx - 2 * x * fx, y + fy - 2 * y * fy, c + fc - 2 * c * fc)
            cp = pltpu.make_async_remote_copy(src_ref=p_ref, dst_ref=buf.at[me], send_sem=send_sems.at[k - 1],
                                              recv_sem=recv_sems.at[k - 1], device_id=peer, device_id_type=MESH)
            cp.start()
            copies.append(cp)
        for cp in copies:
            cp.wait()
        acc = buf[0]
        for s in range(1, 8):
            acc = acc + buf[s]
        o_ref[...] = acc

    return pl.pallas_call(
        body, name=name, out_shape=jax.ShapeDtypeStruct((R, Wd), f32),
        in_specs=[pl.BlockSpec(memory_space=pltpu.VMEM)], out_specs=pl.BlockSpec(memory_space=pltpu.VMEM),
        scratch_shapes=[pltpu.VMEM((8, R, Wd), f32), pltpu.SemaphoreType.DMA((7,)), pltpu.SemaphoreType.DMA((7,))],
        compiler_params=pltpu.CompilerParams(vmem_limit_bytes=VMEM_LIMIT),
    )(p)


def _layer_half(ref, h, axis):
    n = ref.shape[axis] // 2
    idx = (slice(None),) * axis + (pl.ds(h * n, n),)
    return ref.at[idx]


def _comm_call(body, name, ins, out_shapes, n_sems, n_local=0):
    scratch = [pltpu.SemaphoreType.DMA((n_sems,)), pltpu.SemaphoreType.DMA((n_sems,))]
    if n_local:
        scratch.append(pltpu.SemaphoreType.DMA((n_local,)))
    return pl.pallas_call(body, name=name, out_shape=out_shapes, in_specs=[_ANY] * len(ins), out_specs=[_ANY] * len(out_shapes),
                          scratch_shapes=scratch)(*ins)


def _all_gather_chips(ws, name):
    n = len(ws)

    def body(*refs):
        w_refs, o_refs = refs[:n], refs[n:2 * n]
        send_sems, recv_sems, local_sems = refs[2 * n:]
        x, y, c, chips = _position()
        p = _chip_index(x, y)
        sibling = (x, y, 1 - c)
        local = []
        for i in range(n):
            cp = pltpu.make_async_copy(w_refs[i], o_refs[i].at[p], local_sems.at[i])
            cp.start()
            local.append(cp)

        def copy(k, src, i, chip_idx, h, to):
            return pltpu.make_async_remote_copy(src_ref=src, dst_ref=_layer_half(o_refs[i].at[chip_idx], h, 0), send_sem=send_sems.at[k],
                                                recv_sem=recv_sems.at[k], device_id=to, device_id_type=MESH)

        first = [copy(3 * i + j, _layer_half(w_refs[i], c, 0), i, p, c, (*chip, c)) for i in range(n) for j, chip in enumerate(chips)]
        for cp in first:
            cp.start()
        passed = []
        for i in range(n):
            for j, chip in enumerate(chips):
                q = _chip_index(*chip)
                copy(3 * i + j, _layer_half(w_refs[i], c, 0), i, q, c, (*chip, c)).wait_recv()
                fwd = copy(3 * n + 3 * i + j, _layer_half(o_refs[i].at[q], c, 0), i, q, c, sibling)
                fwd.start()
                passed.append(fwd)
        for i in range(n):
            for j, chip in enumerate(chips):
                copy(3 * n + 3 * i + j, _layer_half(w_refs[i], c, 0), i, _chip_index(*chip), 1 - c, sibling).wait_recv()
        for cp in first + passed:
            cp.wait_send()
        for cp in local:
            cp.wait()

    return _comm_call(body, name, ws, [jax.ShapeDtypeStruct((4,) + w.shape, w.dtype) for w in ws], 6 * n, n)


def _sibling_halves(gs, name):
    n = len(gs)

    def body(*refs):
        g_refs, o_refs, send_sems, recv_sems = refs[:n], refs[n:2 * n], refs[2 * n], refs[2 * n + 1]
        x, y, c, _ = _position()
        copies = []
        for i in range(n):
            cp = pltpu.make_async_remote_copy(src_ref=_layer_half(g_refs[i], 1 - c, 1), dst_ref=o_refs[i], send_sem=send_sems.at[i],
                                              recv_sem=recv_sems.at[i], device_id=(x, y, 1 - c), device_id_type=MESH)
            cp.start()
            copies.append(cp)
        for cp in copies:
            cp.wait()

    outs = [jax.ShapeDtypeStruct((4, g.shape[1] // 2) + g.shape[2:], g.dtype) for g in gs]
    return _comm_call(body, name, gs, outs, n)


def _chip_exchange(hs, name):
    n = len(hs)

    def body(*refs):
        h_refs, o_refs, send_sems, recv_sems = refs[:n], refs[n:2 * n], refs[2 * n], refs[2 * n + 1]
        x, y, c, chips = _position()
        copies = []
        for i in range(n):
            for j, chip in enumerate(chips):
                cp = pltpu.make_async_remote_copy(src_ref=h_refs[i].at[_chip_index(*chip)], dst_ref=o_refs[i].at[j],
                                                  send_sem=send_sems.at[3 * i + j], recv_sem=recv_sems.at[3 * i + j],
                                                  device_id=(*chip, c), device_id_type=MESH)
                cp.start()
                copies.append(cp)
        for cp in copies:
            cp.wait()

    return _comm_call(body, name, hs, [jax.ShapeDtypeStruct((3,) + h.shape[1:], h.dtype) for h in hs], 3 * n)


def _sibling_swap(ts, name):
    n = len(ts)

    def body(*refs):
        t_refs, o_refs, send_sems, recv_sems = refs[:n], refs[n:2 * n], refs[2 * n], refs[2 * n + 1]
        x, y, c, _ = _position()
        copies = []
        for i in range(n):
            cp = pltpu.make_async_remote_copy(src_ref=t_refs[i], dst_ref=o_refs[i], send_sem=send_sems.at[i], recv_sem=recv_sems.at[i],
                                              device_id=(x, y, 1 - c), device_id_type=MESH)
            cp.start()
            copies.append(cp)
        for cp in copies:
            cp.wait()

    return _comm_call(body, name, ts, [jax.ShapeDtypeStruct(t.shape, t.dtype) for t in ts], n)


def _row_tile(K, N):
    return _div_tile(K, 256 if N <= 1024 else 128, 16)


def _prefetch_call(body, name, scalars, grid, in_specs, out_specs, out_shape, args, sem):
    gs = pltpu.PrefetchScalarGridSpec(num_scalar_prefetch=1, grid=grid, in_specs=in_specs, out_specs=out_specs)
    return pl.pallas_call(body, name=name, grid_spec=gs, out_shape=out_shape, compiler_params=_cparams(sem))(scalars, *args)


def _add_own_half(g, got, c, name):
    _, nl, K, N = g.shape
    hl = nl // 2
    tm = _row_tile(K, N)

    def body(c_ref, g_ref, r_ref, o_ref):
        o_ref[...] = (g_ref[...].astype(f32) + r_ref[...].astype(f32)).astype(o_ref.dtype)

    blk = (1, 1, tm, N)
    return _prefetch_call(
        body, name, jnp.reshape(c, (1,)).astype(jnp.int32), (4, hl, K // tm),
        [pl.BlockSpec(blk, lambda s, l, i, c_ref: (s, c_ref[0] * hl + l, i, 0)), pl.BlockSpec(blk, lambda s, l, i, c_ref: (s, l, i, 0))],
        pl.BlockSpec(blk, lambda s, l, i, c_ref: (s, l, i, 0)), jax.ShapeDtypeStruct((4, hl, K, N), g.dtype), (g, got),
        ("parallel", "parallel", "parallel"))


def _add_chips(h, got, p, name):
    _, nl, K, N = h.shape
    tm = _row_tile(K, N)

    def body(p_ref, h_ref, r0_ref, r1_ref, r2_ref, o_ref):
        o_ref[...] = ((h_ref[0].astype(f32) + r0_ref[0].astype(f32)) + r1_ref[0].astype(f32)) + r2_ref[0].astype(f32)

    blk = (1, 1, tm, N)
    got_spec = lambda j: pl.BlockSpec(blk, lambda l, i, p_ref: (j, l, i, 0))
    return _prefetch_call(
        body, name, jnp.reshape(p, (1,)).astype(jnp.int32), (nl, K // tm),
        [pl.BlockSpec(blk, lambda l, i, p_ref: (p_ref[0], l, i, 0)), got_spec(0), got_spec(1), got_spec(2)],
        pl.BlockSpec((1, tm, N), lambda l, i, p_ref: (l, i, 0)), jax.ShapeDtypeStruct((nl, K, N), f32), (h, got, got, got),
        ("parallel", "parallel"))


def _adamw_halves(w, m, v, own, other, c, name):
    nl, K, N = w.shape
    hl = nl // 2
    tm = _row_tile(K, N)

    def body(c_ref, w_ref, m_ref, v_ref, own_ref, other_ref, g_out, d_out, m_out, v_out):
        mine = (pl.program_id(0) // hl) == c_ref[0]
        g = jnp.where(mine, own_ref[...], other_ref[...])
        d, m_new, v_new = _adamw_math(w_ref[...], g, m_ref[...], v_ref[...])
        g_out[...] = g
        d_out[...] = d
        m_out[...] = m_new
        v_out[...] = v_new

    blk = (1, tm, N)
    full = pl.BlockSpec(blk, lambda l, i, c_ref: (l, i, 0))
    half = pl.BlockSpec(blk, lambda l, i, c_ref: (l % hl, i, 0))
    return _prefetch_call(
        body, name, jnp.reshape(c, (1,)).astype(jnp.int32), (nl, K // tm), [full, full, full, half, half], [full] * 4,
        [jax.ShapeDtypeStruct((nl, K, N), f32)] * 4, (w, m, v, own, other), ("parallel", "parallel"))


def _reduce_scatter_chips(gs, name):
    c = lax.axis_index("c")
    p = _chip_index(lax.axis_index("x"), lax.axis_index("y"))
    from_sibling = _sibling_halves(gs, name + "_d2d")
    hs = [_add_own_half(g, r, c, f"{name}_add2_{i}") for i, (g, r) in enumerate(zip(gs, from_sibling))]
    got = _chip_exchange(hs, name + "_ici")
    own = [_add_chips(h, r, p, f"{name}_add4_{i}") for i, (h, r) in enumerate(zip(hs, got))]
    other = _sibling_swap(own, name + "_swap")
    return own, other


_BIG = (("gla_w_in", 1), ("gla_w_out", 0), ("dil_w_in", 1), ("dil_w_out", 1), ("ffn_w_up", 1), ("ffn_w_down", 0))


def _pad_rows(a, mult):
    r = (-a.shape[0]) % mult
    return a if r == 0 else jnp.concatenate([a, jnp.zeros((r,) + a.shape[1:], a.dtype)], axis=0)


def _unshard(blocks, axis):
    _, nl, K, N = blocks.shape
    if axis == 1:
        return blocks.transpose(1, 2, 0, 3).reshape(nl, K, 4 * N)
    return blocks.transpose(1, 0, 2, 3).reshape(nl, 4 * K, N)


def _to_shards(mat, axis):
    K, N = mat.shape
    if axis == 1:
        return mat.reshape(K, 4, N // 4).transpose(1, 0, 2)
    return mat.reshape(4, K // 4, N)


def _gather_big_weights(shards):
    allw = _all_gather_chips([shards[n].astype(bf16) for n, _ in _BIG], "gather_weights")
    return {n: _unshard(blocks, axis) for (n, axis), blocks in zip(_BIG, allw)}


def _scatter_big_grads(grads):
    gs = [jnp.stack([_to_shards(g, axis) for g in grads[n]], axis=1) for n, axis in _BIG]
    own, other = _reduce_scatter_chips(gs, "scatter_grads")
    return {n: (a, b) for (n, _), a, b in zip(_BIG, own, other)}


_SMALL = (("gla_w_gate_up", 2), ("gla_gate_bias", None), ("gla_norm_g", None), ("ffn_conv_w", 2), ("ffn_conv_b", None),
          ("ln_g", 2), ("ln_b", 2))


def _pack_rows(arrs, width=LANE):
    flat = _pad_rows(jnp.concatenate([a.reshape(-1) for a in arrs]), 8 * width)
    return flat.reshape(-1, width)


def _unpack_rows(packed, shapes):
    flat, out, off = packed.reshape(-1), [], 0
    for s in shapes:
        n = 1
        for d in s:
            n *= d
        out.append(flat[off:off + n].reshape(s))
        off += n
    return out


def _gather_small_params(shards):
    x, y, c = lax.axis_index("x"), lax.axis_index("y"), lax.axis_index("c")
    names = [n for n, axis in _SMALL if axis is not None]
    mine = _pack_rows([shards[n] for n in names])
    mine = jnp.where(c == 0, mine, jnp.zeros_like(mine))
    rows = mine.shape[0]
    placed = lax.dynamic_update_slice(jnp.zeros((4 * rows, LANE), f32), mine, (_chip_index(x, y) * rows, 0))
    allp = _all_reduce_small(placed, "gather_small").reshape(4, rows, LANE)
    out = {n: shards[n] for n, axis in _SMALL if axis is None}
    per_chip = [_unpack_rows(allp[q], [shards[n].shape for n in names]) for q in range(4)]
    for i, n in enumerate(names):
        out[n] = jnp.concatenate([per_chip[q][i] for q in range(4)], axis=2)
    return out


def _reduce_small_grads(grads, shards):
    names = [n for n, _ in _SMALL]
    total = _all_reduce_small(_pack_rows([grads[n] for n in names]), "reduce_small")
    full = dict(zip(names, _unpack_rows(total, [grads[n].shape for n in names])))
    p = _chip_index(lax.axis_index("x"), lax.axis_index("y"))
    out = {}
    for n, axis in _SMALL:
        if axis is None:
            out[n] = full[n]
        else:
            w = shards[n].shape[axis]
            out[n] = lax.dynamic_slice_in_dim(full[n], p * w, w, axis=axis)
    return out


def _pad_cols(a, n):
    return a if a.shape[-1] == n else jnp.concatenate([a, jnp.zeros(a.shape[:-1] + (n - a.shape[-1],), a.dtype)], axis=-1)


def _ffn_width(F):
    return -(-F // 512) * 512


def kernel(x, gla_w_in, gla_w_gate_up, gla_gate_bias, gla_norm_g, gla_w_out, dil_w_in, dil_w_out, ffn_w_up, ffn_conv_w, ffn_conv_b, ffn_w_down, ln_g, ln_b, loss_target, m_gla_w_in, m_gla_w_gate_up, m_gla_gate_bias, m_gla_norm_g, m_gla_w_out, m_dil_w_in, m_dil_w_out, m_ffn_w_up, m_ffn_conv_w, m_ffn_conv_b, m_ffn_w_down, m_ln_g, m_ln_b, v_gla_w_in, v_gla_w_gate_up, v_gla_gate_bias, v_gla_norm_g, v_gla_w_out, v_dil_w_in, v_dil_w_out, v_ffn_w_up, v_ffn_conv_w, v_ffn_conv_b, v_ffn_w_down, v_ln_g, v_ln_b):
    names = ["gla_w_in", "gla_w_gate_up", "gla_gate_bias", "gla_norm_g", "gla_w_out", "dil_w_in", "dil_w_out", "ffn_w_up",
             "ffn_conv_w", "ffn_conv_b", "ffn_w_down", "ln_g", "ln_b"]
    w_sh = dict(zip(names, (gla_w_in, gla_w_gate_up, gla_gate_bias, gla_norm_g, gla_w_out, dil_w_in, dil_w_out, ffn_w_up,
                            ffn_conv_w, ffn_conv_b, ffn_w_down, ln_g, ln_b)))
    m_sh = dict(zip(names, (m_gla_w_in, m_gla_w_gate_up, m_gla_gate_bias, m_gla_norm_g, m_gla_w_out, m_dil_w_in, m_dil_w_out,
                            m_ffn_w_up, m_ffn_conv_w, m_ffn_conv_b, m_ffn_w_down, m_ln_g, m_ln_b)))
    v_sh = dict(zip(names, (v_gla_w_in, v_gla_w_gate_up, v_gla_gate_bias, v_gla_norm_g, v_gla_w_out, v_dil_w_in, v_dil_w_out,
                            v_ffn_w_up, v_ffn_conv_w, v_ffn_conv_b, v_ffn_w_down, v_ln_g, v_ln_b)))
    nseq, S, D = x.shape
    T = nseq * S
    big = _gather_big_weights(w_sh)
    small = _gather_small_params(w_sh)
    F = big["ffn_w_down"].shape[1]
    Fp = _ffn_width(F)
    qkvr = big["gla_w_in"].shape[2] - GLA_GATE_RANK
    c_idx = lax.axis_index("c")

    def pad_halves(a):
        return jnp.concatenate([_pad_cols(a[..., :F], Fp), _pad_cols(a[..., F:], Fp)], axis=-1)

    w_up_all = pad_halves(big["ffn_w_up"])
    w_down_all = jnp.pad(big["ffn_w_down"], ((0, 0), (0, Fp - F), (0, 0)))
    cw_all = pad_halves(small["ffn_conv_w"])
    cb_all = pad_halves(small["ffn_conv_b"][:, None, :])
    w_in_all = big["gla_w_in"]
    w_gate_all = _pad_cols(w_in_all[..., qkvr:], GATE_PAD)
    w_gate_up_all = jnp.pad(small["gla_w_gate_up"].astype(bf16), ((0, 0), (0, GATE_PAD - GLA_GATE_RANK), (0, 0)))

    h0 = x.reshape(T, D)
    saved = []
    cur, cur_b = h0, h0.astype(_MXU)
    fwd = dict(tm=1024, tn=1024, tk=2048)
    for i in range(DEPTH):
        j = i // 2
        tag = f"l{i}_"
        lg, lb = small["ln_g"][i], small["ln_b"][i]
        if i % 2 == 0:
            gate_bias, norm_g = small["gla_gate_bias"][j][None, :], small["gla_norm_g"][j][None, :]
            proj = _mm(cur_b, w_in_all, "nn", tag + "gla_proj", layer=j, n_out=qkvr, **fwd)
            g_low = _mm(cur_b, w_gate_all, "nn", tag + "gla_glow", tn=GATE_PAD, layer=j)
            z = _mm(g_low, w_gate_up_all, "nn", tag + "gla_z", layer=j)
            gate = _log_gate_fwd(z, gate_bias, tag + "gla_gate")
            o, states = _gla_fwd(proj, gate, nseq, tag + "gla_core")
            y_b = _rms_gate_fwd(o, proj, norm_g, tag + "gla_norm")
            mix = _mm(y_b, big["gla_w_out"], "nn", tag + "gla_out", layer=j, **fwd)
            mixer_saved = (proj, g_low, z, gate, states, o, y_b)
        else:
            proj = _mm(cur_b, big["dil_w_in"], "nn", tag + "dil_proj", layer=j, **fwd)
            outs, lses = [], []
            for gi, (window, dilation) in enumerate(DIL_PATTERNS):
                og, lg_ = _dil_fwd(proj, gi, window, dilation, nseq, tag + f"dil_attn{gi}")
                outs.append(og)
                lses.append(lg_)
            y, y_b, lse_tot = _dil_mix_fwd(outs, lses, tag + "dil_mix")
            mix = _mm(y_b, big["dil_w_out"], "nn", tag + "dil_out", layer=j, **fwd)
            mixer_saved = (proj, y, y_b, lse_tot)
        x1, x1_b = _ln_fwd(cur, mix, lg[0:1], lb[0:1], tag + "ln1")
        cw, cb = cw_all[i], cb_all[i]
        hh = _mm(x1_b, w_up_all, "nn", tag + "ffn_up", layer=i, **fwd)
        act_b = _conv_gate_fwd(hh, cw, cb, nseq, tag + "ffn_conv")
        ffn = _mm(act_b, w_down_all, "nn", tag + "ffn_down", layer=i, tm=1024, tn=512, tk=Fp)
        x2, x2_b = _ln_fwd(x1, ffn, lg[1:2], lb[1:2], tag + "ln2")
        saved.append((cur, cur_b, mix, x1, x1_b, hh, act_b, ffn, mixer_saved))
        cur, cur_b = x2, x2_b

    dy, sq = _loss_head(cur, loss_target.reshape(T, D), "loss_head")
    loss = lax.psum(0.5 * jnp.sum(sq) / D, ("x", "y", "c"))

    gb = {n: [None] * w_sh[n].shape[0] for n in names}
    d_res = None
    d_in = dy
    for i in reversed(range(DEPTH)):
        j = i // 2
        tag = f"l{i}_b_"
        xin, xin_b, mix, x1, x1_b, hh, act_b, ffn, mixer_saved = saved[i]
        lg = small["ln_g"][i]
        cw, cb = cw_all[i], cb_all[i]
        dw_tiles = dict(tm=1024, tn=1024, tk=2048, out_dtype=bf16)
        dx_tiles = dict(tm=1024, tn=512, tk=6144)
        dys, scales = ([d_in], [1.0]) if d_res is None else ([d_res, d_in], [DEEPNORM_ALPHA, 1.0])
        du2, du2_b, dg2, db2 = _ln_bwd(x1, ffn, lg[1:2], dys, scales, tag + "ln2")
        gb["ffn_w_down"][i] = _mm(act_b, du2_b, "tn", tag + "ffn_down_dw", **dw_tiles)[:F]
        dact = _mm(du2_b, w_down_all, "nt", tag + "ffn_down_dx", layer=i, **dx_tiles)
        dhg, dhu, dcw, dcb = _conv_gate_bwd(hh, dact, cw, cb, nseq, tag + "ffn_conv")
        gb["ffn_conv_w"][i] = jnp.concatenate([dcw[:, :F], dcw[:, Fp:Fp + F]], axis=1)
        gb["ffn_conv_b"][i] = jnp.concatenate([dcb[0, :F], dcb[0, Fp:Fp + F]], axis=0)
        dwg = _mm(x1_b, dhg, "tn", tag + "ffn_gate_dw", **dw_tiles)
        dwu = _mm(x1_b, dhu, "tn", tag + "ffn_up_dw", **dw_tiles)
        gb["ffn_w_up"][i] = jnp.concatenate([dwg[:, :F], dwu[:, :F]], axis=1)
        dx1g = _mm(dhg, w_up_all, "nt", tag + "ffn_gate_dx", layer=i, **dx_tiles)
        dx1u = _mm(dhu, w_up_all, "nt", tag + "ffn_up_dx", layer=i, b_col0=Fp, **dx_tiles)
        du1, du1_b, dg1, db1 = _ln_bwd(xin, mix, lg[0:1], [du2, dx1g, dx1u], [DEEPNORM_ALPHA, 1.0, 1.0], tag + "ln1")
        gb["ln_g"][i] = jnp.concatenate([dg1, dg2], axis=0)
        gb["ln_b"][i] = jnp.concatenate([db1, db2], axis=0)
        if i % 2 == 0:
            proj, g_low, z, gate, states, o, y_b = mixer_saved
            gate_bias, norm_g = small["gla_gate_bias"][j][None, :], small["gla_norm_g"][j][None, :]
            gb["gla_w_out"][j] = _mm(y_b, du1_b, "tn", tag + "gla_out_dw", **dw_tiles)
            dyy = _mm(du1_b, big["gla_w_out"], "nt", tag + "gla_out_dx", layer=j, **dx_tiles)
            do, dr, dng = _rms_gate_bwd(o, proj, norm_g, dyy, tag + "gla_norm")
            gb["gla_norm_g"][j] = dng[0]
            dq, dk_, dv_, dgate = _gla_bwd(proj, gate, states, do, nseq, tag + "gla_core")
            dz, dbias = _log_gate_bwd(z, gate_bias, dgate, tag + "gla_gate")
            gb["gla_gate_bias"][j] = dbias[0]
            gb["gla_w_gate_up"][j] = _mm(g_low, dz, "tn", tag + "gla_z_dw", tk=1024)[:GLA_GATE_RANK]
            dg_low = _mm(dz, w_gate_up_all, "nt", tag + "gla_z_dx", tn=GATE_PAD, layer=j)
            dproj = jnp.concatenate([dq, dk_, dv_, dr], axis=1)
            dw_main = _mm(xin_b, dproj, "tn", tag + "gla_proj_dw", **dw_tiles)
            dw_gate = _mm(xin_b, dg_low, "tn", tag + "gla_glow_dw", tm=1024, tn=GATE_PAD, tk=2048, out_dtype=bf16)[:, :GLA_GATE_RANK]
            gb["gla_w_in"][j] = jnp.concatenate([dw_main, dw_gate], axis=1)
            dxa = _mm(dproj, w_in_all, "nt", tag + "gla_proj_dx", layer=j, **dx_tiles)
            dxb = _mm(dg_low, w_gate_all, "nt", tag + "gla_glow_dx", layer=j)
            d_in = _axpy(dxa, dxb, 1.0, tag + "gla_dx_sum")
        else:
            proj, y, y_b, lse_tot = mixer_saved
            gb["dil_w_out"][j] = _mm(y_b, du1_b, "tn", tag + "dil_out_dw", **dw_tiles)
            dyy = _mm(du1_b, big["dil_w_out"], "nt", tag + "dil_out_dx", layer=j, **dx_tiles)
            delta = _dil_delta(dyy, y, tag + "dil_delta")
            pieces = []
            for gi, (window, dilation) in enumerate(DIL_PATTERNS):
                pieces += list(_dil_bwd(proj, dyy, lse_tot, delta, gi, window, dilation, nseq, tag + f"dil_attn{gi}"))
            dproj = jnp.concatenate(pieces, axis=1).astype(_MXU)
            gb["dil_w_in"][j] = _mm(xin_b, dproj, "tn", tag + "dil_proj_dw", **dw_tiles)
            d_in = _mm(dproj, big["dil_w_in"], "nt", tag + "dil_proj_dx", layer=j, **dx_tiles)
        d_res = du1
    grad_x = _axpy(d_res, d_in, DEEPNORM_ALPHA, "grad_x").reshape(x.shape)

    halves = _scatter_big_grads(gb)
    grads, delta, new_m, new_v = {}, {}, {}, {}
    for n, _ in _BIG:
        own, other = halves[n]
        grads[n], delta[n], new_m[n], new_v[n] = _adamw_halves(w_sh[n], m_sh[n], v_sh[n], own, other, c_idx, "adamw_" + n)
    grads.update(_reduce_small_grads({n: jnp.stack(gb[n], axis=0) for n, _ in _SMALL}, w_sh))
    small_names = [n for n, _ in _SMALL]
    packed = [_pack_rows([src[n] for n in small_names]) for src in (w_sh, grads, m_sh, v_sh)]
    res = _adamw(*packed, "adamw_small")
    shapes = [w_sh[n].shape for n in small_names]
    for dst, arr in zip((delta, new_m, new_v), res):
        dst.update(dict(zip(small_names, _unpack_rows(arr, shapes))))

    return (loss, grad_x, *[grads[n] for n in names], *[delta[n] for n in names], *[new_m[n] for n in names],
            *[new_v[n] for n in names])
```

```python
import functools

import jax
import jax.numpy as jnp
from jax import lax
from jax.experimental import pallas as pl
from jax.experimental.pallas import tpu as pltpu

f32 = jnp.float32
bf16 = jnp.bfloat16
_MXU = jnp.bfloat16

DEPTH = 4
GLA_HEADS = 4
GLA_GATE_RANK = 16
GLA_GATE_NORMALIZER = 16.0
GLA_CHUNK = 64
GLA_SUB = 16
DIL_PATTERNS = ((128, 1), (512, 4), (2048, 16))
DIL_HEADS = 8
DIL_HEAD_DIM = 128
DIL_BLOCK = 128
DEEPNORM_ALPHA = (2 * DEPTH) ** 0.25
LN_EPS = 1e-5
RMS_EPS = 1e-6
ADAM_LR = 0.001
ADAM_B1 = 0.9
ADAM_B2 = 0.999
ADAM_EPS = 1e-08
ADAM_WD = 0.01
ADAM_STEP = 10

LANE = 128
VMEM_LIMIT = 48 * 1024 * 1024
GATE_PAD = LANE
MESH = pl.DeviceIdType.MESH


def _cparams(sem=None):
    return pltpu.CompilerParams(dimension_semantics=sem, vmem_limit_bytes=VMEM_LIMIT)


def _div_tile(n, pref, unit):
    if n <= pref:
        return n
    best = None
    for t in range(unit, pref + 1, unit):
        if n % t == 0:
            best = t
    assert best is not None, (n, pref, unit)
    return best


def _dot(a, b, ca, cb):
    return lax.dot_general(a.astype(_MXU), b.astype(_MXU), (((ca,), (cb,)), ((), ())), preferred_element_type=f32)


def _nn(a, b):
    return _dot(a, b, 1, 0)


def _nt(a, b):
    return _dot(a, b, 1, 1)


def _tn(a, b):
    return _dot(a, b, 0, 0)


def _exact_dot(a, b):
    return jnp.dot(a, b, precision=lax.Precision.HIGHEST, preferred_element_type=f32)


def _sigmoid(x):
    return 1.0 / (1.0 + jnp.exp(-x))


def _mm(a, b, mode, name, tm=1024, tn=512, tk=2048, out_dtype=f32, layer=None, n_out=None, b_col0=0, side=None):
    if mode == "nn":
        (M, K), N = a.shape, (n_out or b.shape[-1])
    elif mode == "nt":
        (M, K), N = a.shape, b.shape[-2]
    else:
        (K, M), N = a.shape, b.shape[-1]
    tm, tn, tk = _div_tile(M, tm, LANE), _div_tile(N, tn, LANE), _div_tile(K, tk, LANE)
    nk = K // tk
    if mode == "tn":
        a_spec = pl.BlockSpec((tk, tm), lambda i, j, k: (k, i))
    else:
        a_spec = pl.BlockSpec((tm, tk), lambda i, j, k: (i, k))
    lead = () if layer is None else (None,)
    pre = () if layer is None else (layer,)
    if mode == "nt":
        k0 = b_col0 // tk
        assert k0 * tk == b_col0
        b_spec = pl.BlockSpec(lead + (tn, tk), lambda i, j, k: pre + (j, k + k0))
    else:
        b_spec = pl.BlockSpec(lead + (tk, tn), lambda i, j, k: pre + (k, j))
    ca, cb = {"nn": (1, 0), "nt": (1, 1), "tn": (0, 0)}[mode]
    grid = (M // tm, N // tn, nk)
    n_si, n_so = (len(side.ins), len(side.outs)) if side else (0, 0)

    def body(*refs):
        a_ref, b_ref, s_in = refs[0], refs[1], refs[2:2 + n_si]
        o_ref, s_out = refs[2 + n_si], refs[3 + n_si:3 + n_si + n_so]
        scratch = refs[3 + n_si + n_so:]
        acc_ref, sems = (scratch[0], scratch[1:]) if nk > 1 else (None, scratch)
        i, j, k = pl.program_id(0), pl.program_id(1), pl.program_id(2)
        if side:
            @pl.when((i == 0) & (j == 0) & (k == 0))
            def _():
                side.start(s_in, s_out, *sems)

        p = _dot(a_ref[...], b_ref[...], ca, cb)
        if nk == 1:
            o_ref[...] = p.astype(o_ref.dtype)
        else:
            @pl.when(k == 0)
            def _():
                acc_ref[...] = p

            @pl.when(k > 0)
            def _():
                acc_ref[...] += p

            @pl.when(k == nk - 1)
            def _():
                o_ref[...] = acc_ref[...].astype(o_ref.dtype)

        if side:
            @pl.when((i == grid[0] - 1) & (j == grid[1] - 1) & (k == nk - 1))
            def _():
                side.finish(s_in, s_out, *sems)

    scratch = ([pltpu.VMEM((tm, tn), f32)] if nk > 1 else []) + (side.scratch if side else [])
    res = pl.pallas_call(
        body, name=name, grid=grid, in_specs=[a_spec, b_spec] + [_ANY] * n_si,
        out_specs=[pl.BlockSpec((tm, tn), lambda i, j, k: (i, j))] + [_ANY] * n_so,
        out_shape=[jax.ShapeDtypeStruct((M, N), out_dtype)] + (side.outs if side else []),
        scratch_shapes=scratch,
        compiler_params=_cparams(("arbitrary",) * 3 if side else ("parallel", "parallel", "arbitrary")),
    )(a, b, *(side.ins if side else []))
    return res if side else res[0]


def _rowwise(fn, rows, consts, outs, reds, name, tm=256):
    T = rows[0][0].shape[0]
    tm = _div_tile(T, tm, 8)
    n_r, n_c, n_o = len(rows), len(consts), len(outs)

    def body(*refs):
        ins = [r[...] for r in refs[: n_r + n_c]]
        res = fn(*ins)
        res = res if isinstance(res, (tuple, list)) else (res,)
        o_refs = refs[n_r + n_c: n_r + n_c + n_o]
        r_refs = refs[n_r + n_c + n_o:]
        for ref, val in zip(o_refs, res[:n_o]):
            ref[...] = val.astype(ref.dtype)
        i = pl.program_id(0)
        for ref, val in zip(r_refs, res[n_o:]):
            _accumulate(ref, val, i)

    in_specs = [pl.BlockSpec((tm, w), functools.partial(lambda i, cb: (i, cb), cb=cb)) for (_, w, cb) in rows]
    in_specs += [pl.BlockSpec(c.shape, lambda i: (0, 0)) for c in consts]
    out_specs = [pl.BlockSpec((tm, w), lambda i: (i, 0)) for (w, _) in outs]
    out_specs += [pl.BlockSpec((1, w), lambda i: (0, 0)) for w in reds]
    out_shape = [jax.ShapeDtypeStruct((T, w), dt) for (w, dt) in outs]
    out_shape += [jax.ShapeDtypeStruct((1, w), f32) for w in reds]
    return pl.pallas_call(
        body, name=name, grid=(T // tm,), in_specs=in_specs, out_specs=out_specs, out_shape=out_shape,
        compiler_params=_cparams(("arbitrary",)),
    )(*[r[0] for r in rows], *consts)


def _accumulate(ref, val, step):
    @pl.when(step == 0)
    def _():
        ref[...] = val

    @pl.when(step > 0)
    def _():
        ref[...] += val


def _full(a):
    return (a, a.shape[1], 0)


def _colsum(x):
    return jnp.sum(x, axis=0, keepdims=True)


def _ln_stats(u):
    mu = jnp.mean(u, axis=-1, keepdims=True)
    xc = u - mu
    var = jnp.mean(xc * xc, axis=-1, keepdims=True)
    rstd = lax.rsqrt(var + LN_EPS)
    return xc * rstd, rstd


def _ln_fwd(x, f, g, b, name):
    def fn(x, f, g, b):
        xhat, _ = _ln_stats(DEEPNORM_ALPHA * x + f)
        y = xhat * g + b
        return y, y

    return _rowwise(fn, [_full(x), _full(f)], [g, b], [(x.shape[1], f32), (x.shape[1], _MXU)], [], name)


def _ln_bwd(x, f, g, dys, scales, name):
    def fn(x, f, *rest):
        g = rest[-1]
        dy = None
        for d, s in zip(rest[:-1], scales):
            t = d if s == 1.0 else s * d
            dy = t if dy is None else dy + t
        xhat, rstd = _ln_stats(DEEPNORM_ALPHA * x + f)
        dxh = dy * g
        m1 = jnp.mean(dxh, axis=-1, keepdims=True)
        m2 = jnp.mean(dxh * xhat, axis=-1, keepdims=True)
        du = rstd * (dxh - m1 - xhat * m2)
        return du, du, _colsum(dy * xhat), _colsum(dy)

    D = x.shape[1]
    return _rowwise(fn, [_full(x), _full(f)] + [_full(d) for d in dys], [g], [(D, f32), (D, _MXU)], [D, D], name)


def _loss_head(y, t, name):
    D = y.shape[1]

    def fn(y, t):
        e = y - t
        return e * (1.0 / D), _colsum(e * e)

    return _rowwise(fn, [_full(y), _full(t)], [], [(D, f32)], [D], name)


def _axpy(a, b, alpha, name):
    def fn(a, b):
        return alpha * a + b

    return _rowwise(fn, [_full(a), _full(b)], [], [(a.shape[1], f32)], [], name)[0]


def _shift_down(x, k):
    row = lax.broadcasted_iota(jnp.int32, x.shape, 0)
    return jnp.where(row >= k, pltpu.roll(x, k, 0), 0.0)


def _shift_up(x, k):
    S = x.shape[0]
    row = lax.broadcasted_iota(jnp.int32, x.shape, 0)
    return jnp.where(row < S - k, pltpu.roll(x, S - k, 0), 0.0)


def _causal_conv(h, w, b):
    return ((b + w[0:1] * _shift_down(h, 2)) + w[1:2] * _shift_down(h, 1)) + w[2:3] * h


def _conv_gate_fwd(h, cw, cb, nseq, name, tc=256):
    T, F2 = h.shape
    F, S = F2 // 2, T // nseq
    tc = _div_tile(F, tc, LANE)
    nf = F // tc

    def body(hg_ref, hu_ref, wg_ref, wu_ref, bg_ref, bu_ref, a_ref):
        cg = _causal_conv(hg_ref[...], wg_ref[...], bg_ref[...])
        cu = _causal_conv(hu_ref[...], wu_ref[...], bu_ref[...])
        a_ref[...] = (cg * _sigmoid(cg) * cu).astype(a_ref.dtype)

    return pl.pallas_call(
        body, name=name, grid=(nseq, nf),
        in_specs=[pl.BlockSpec((S, tc), lambda s, j: (s, j)), pl.BlockSpec((S, tc), lambda s, j: (s, nf + j)),
                  pl.BlockSpec((3, tc), lambda s, j: (0, j)), pl.BlockSpec((3, tc), lambda s, j: (0, nf + j)),
                  pl.BlockSpec((1, tc), lambda s, j: (0, j)), pl.BlockSpec((1, tc), lambda s, j: (0, nf + j))],
        out_specs=pl.BlockSpec((S, tc), lambda s, j: (s, j)),
        out_shape=jax.ShapeDtypeStruct((T, F), _MXU),
        compiler_params=_cparams(("parallel", "parallel")),
    )(h, h, cw, cw, cb, cb)


def _conv_gate_bwd(h, da, cw, cb, nseq, name, tc=128):
    T, F2 = h.shape
    F, S = F2 // 2, T // nseq
    tc = _div_tile(F, tc, LANE)
    nf = F // tc

    def conv_bwd(dc, hx, w):
        dh = (w[2:3] * dc + w[1:2] * _shift_up(dc, 1)) + w[0:1] * _shift_up(dc, 2)
        dw = jnp.concatenate([_colsum(dc * _shift_down(hx, 2)), _colsum(dc * _shift_down(hx, 1)), _colsum(dc * hx)], axis=0)
        return dh, dw, _colsum(dc)

    def body(hg_ref, hu_ref, da_ref, wg_ref, wu_ref, bg_ref, bu_ref, dhg_ref, dhu_ref, dwg_ref, dwu_ref, dbg_ref, dbu_ref):
        hg, hu, da = hg_ref[...], hu_ref[...], da_ref[...]
        wg, wu = wg_ref[...], wu_ref[...]
        cg = _causal_conv(hg, wg, bg_ref[...])
        cu = _causal_conv(hu, wu, bu_ref[...])
        sg = _sigmoid(cg)
        dcu = da * (cg * sg)
        dcg = da * cu * (sg * (1.0 + cg * (1.0 - sg)))
        dhg, dwg, dbg = conv_bwd(dcg, hg, wg)
        dhu, dwu, dbu = conv_bwd(dcu, hu, wu)
        dhg_ref[...] = dhg.astype(dhg_ref.dtype)
        dhu_ref[...] = dhu.astype(dhu_ref.dtype)
        s = pl.program_id(1)
        _accumulate(dwg_ref, dwg, s)
        _accumulate(dwu_ref, dwu, s)
        _accumulate(dbg_ref, dbg, s)
        _accumulate(dbu_ref, dbu, s)

    col = lambda j, s: (s, j)
    par = lambda j, s: (0, j)
    dhg, dhu, dwg, dwu, dbg, dbu = pl.pallas_call(
        body, name=name, grid=(nf, nseq),
        in_specs=[pl.BlockSpec((S, tc), col), pl.BlockSpec((S, tc), lambda j, s: (s, nf + j)), pl.BlockSpec((S, tc), col),
                  pl.BlockSpec((3, tc), par), pl.BlockSpec((3, tc), lambda j, s: (0, nf + j)),
                  pl.BlockSpec((1, tc), par), pl.BlockSpec((1, tc), lambda j, s: (0, nf + j))],
        out_specs=[pl.BlockSpec((S, tc), col), pl.BlockSpec((S, tc), col), pl.BlockSpec((3, tc), par), pl.BlockSpec((3, tc), par),
                   pl.BlockSpec((1, tc), par), pl.BlockSpec((1, tc), par)],
        out_shape=[jax.ShapeDtypeStruct((T, F), _MXU), jax.ShapeDtypeStruct((T, F), _MXU), jax.ShapeDtypeStruct((3, F), f32),
                   jax.ShapeDtypeStruct((3, F), f32), jax.ShapeDtypeStruct((1, F), f32), jax.ShapeDtypeStruct((1, F), f32)],
        compiler_params=_cparams(("parallel", "arbitrary")),
    )(h, h, da, cw, cw, cb, cb)
    return dhg, dhu, jnp.concatenate([dwg, dwu], axis=1), jnp.concatenate([dbg, dbu], axis=1)


def _group_row(x, jj):
    C, d = x.shape
    n = C // GLA_SUB
    x3 = x.reshape(n, GLA_SUB, d)
    return jnp.broadcast_to(x3[:, jj:jj + 1, :], (n, GLA_SUB, d)).reshape(C, d)


def _group_sum(x):
    C, d = x.shape
    n = C // GLA_SUB
    s = jnp.sum(x.reshape(n, GLA_SUB, d), axis=1, keepdims=True)
    return jnp.broadcast_to(s, (n, GLA_SUB, d)).reshape(C, d)


def _chunk_cumsum(g):
    C = g.shape[0]
    row = lax.broadcasted_iota(jnp.int32, (C, C), 0)
    col = lax.broadcasted_iota(jnp.int32, (C, C), 1)
    return _exact_dot((row >= col).astype(f32), g)


def _chunk_suffix_sum(x):
    C = x.shape[0]
    row = lax.broadcasted_iota(jnp.int32, (C, C), 0)
    col = lax.broadcasted_iota(jnp.int32, (C, C), 1)
    return _exact_dot((col >= row).astype(f32), x)


def _gla_scores(q, k, b):
    C = q.shape[0]
    n = C // GLA_SUB
    row = lax.broadcasted_iota(jnp.int32, (C, C), 0)
    col = lax.broadcasted_iota(jnp.int32, (C, C), 1)
    blocks = [jnp.zeros((GLA_SUB, C), f32)]
    for s in range(1, n):
        lo = s * GLA_SUB
        bref = b[lo - 1:lo, :]
        qr = q[lo:lo + GLA_SUB] * jnp.exp(b[lo:lo + GLA_SUB] - bref)
        kr = k * jnp.exp(jnp.minimum(bref - b, 0.0))
        blocks.append(_nt(qr, kr))
    sub_start = (row // GLA_SUB) * GLA_SUB
    a = jnp.where(col < sub_start, jnp.concatenate(blocks, axis=0), 0.0)
    rin = lax.broadcasted_iota(jnp.int32, (C, 1), 0) % GLA_SUB
    for jj in range(GLA_SUB):
        e = jnp.exp(jnp.minimum(b - _group_row(b, jj), 0.0))
        colv = jnp.sum(q * _group_row(k, jj) * e, axis=1, keepdims=True)
        colv = jnp.where(rin >= jj, colv, 0.0)
        a = jnp.where(col == sub_start + jj, colv, a)
    return a


def _gla_scores_bwd(da, q, k, b):
    C = q.shape[0]
    n = C // GLA_SUB
    row = lax.broadcasted_iota(jnp.int32, (C, C), 0)
    col = lax.broadcasted_iota(jnp.int32, (C, C), 1)
    sub_start = (row // GLA_SUB) * GLA_SUB
    da_inter = jnp.where(col < sub_start, da, 0.0)
    dq_blocks = [jnp.zeros((GLA_SUB, q.shape[1]), f32)]
    dk = jnp.zeros_like(k)
    for s in range(1, n):
        lo = s * GLA_SUB
        bref = b[lo - 1:lo, :]
        eq = jnp.exp(b[lo:lo + GLA_SUB] - bref)
        ek = jnp.exp(jnp.minimum(bref - b, 0.0))
        das = da_inter[lo:lo + GLA_SUB]
        dq_blocks.append(_nn(das, k * ek) * eq)
        dk = dk + _tn(das, q[lo:lo + GLA_SUB] * eq) * ek
    dq = jnp.concatenate(dq_blocks, axis=0)
    rin = lax.broadcasted_iota(jnp.int32, (C, 1), 0) % GLA_SUB
    for jj in range(GLA_SUB):
        e = jnp.exp(jnp.minimum(b - _group_row(b, jj), 0.0))
        dac = jnp.sum(jnp.where(col == sub_start + jj, da, 0.0), axis=1, keepdims=True)
        dac = jnp.where(rin >= jj, dac, 0.0)
        w = dac * e
        dq = dq + w * _group_row(k, jj)
        dk = dk + jnp.where(rin == jj, _group_sum(w * q), 0.0)
    return dq, dk


def _gla_specs(nC, dk, dv):
    H = GLA_HEADS
    voff = (2 * H * dk) // dv
    assert voff * dv == 2 * H * dk
    return H, voff


def _gla_fwd(proj, gate, nseq, name):
    T = proj.shape[0]
    dk = gate.shape[1] // GLA_HEADS
    dv = 2 * dk
    C = GLA_CHUNK
    nC = T // nseq // C
    H, voff = _gla_specs(nC, dk, dv)
    scale = dk ** -0.5

    def body(q_ref, k_ref, v_ref, g_ref, o_ref, st_ref, state):
        c = pl.program_id(2)

        @pl.when(c == 0)
        def _():
            state[...] = jnp.zeros_like(state)

        q, k, v = q_ref[...] * scale, k_ref[...], v_ref[...]
        b = _chunk_cumsum(g_ref[...])
        st = state[...]
        st_ref[0] = st
        a = _gla_scores(q, k, b)
        o_ref[...] = _nt(q * jnp.exp(b), st) + _nn(a, v)
        bl = b[C - 1:C, :]
        state[...] = st * jnp.exp(bl) + _tn(v, k * jnp.exp(bl - b))

    row = lambda s, h, c: s * nC + c
    return pl.pallas_call(
        body, name=name, grid=(nseq, H, nC),
        in_specs=[pl.BlockSpec((C, dk), lambda s, h, c: (row(s, h, c), h)),
                  pl.BlockSpec((C, dk), lambda s, h, c: (row(s, h, c), H + h)),
                  pl.BlockSpec((C, dv), lambda s, h, c: (row(s, h, c), voff + h)),
                  pl.BlockSpec((C, dk), lambda s, h, c: (row(s, h, c), h))],
        out_specs=[pl.BlockSpec((C, dv), lambda s, h, c: (row(s, h, c), h)),
                   pl.BlockSpec((1, dv, dk), lambda s, h, c: ((s * H + h) * nC + c, 0, 0))],
        out_shape=[jax.ShapeDtypeStruct((T, H * dv), f32), jax.ShapeDtypeStruct((nseq * H * nC, dv, dk), f32)],
        scratch_shapes=[pltpu.VMEM((dv, dk), f32)],
        compiler_params=_cparams(("parallel", "parallel", "arbitrary")),
    )(proj, proj, proj, gate)


def _gla_bwd(proj, gate, states, do, nseq, name):
    T = proj.shape[0]
    dk = gate.shape[1] // GLA_HEADS
    dv = 2 * dk
    C = GLA_CHUNK
    nC = T // nseq // C
    H, voff = _gla_specs(nC, dk, dv)
    scale = dk ** -0.5

    def body(q_ref, k_ref, v_ref, g_ref, do_ref, st_ref, dq_ref, dk_ref, dv_ref, dg_ref, dstate, term):
        c = pl.program_id(2)

        @pl.when(c == 0)
        def _():
            dstate[...] = jnp.zeros_like(dstate)
            term[...] = jnp.zeros_like(term)

        q, k, v, do = q_ref[...] * scale, k_ref[...], v_ref[...], do_ref[...]
        b = _chunk_cumsum(g_ref[...])
        st = st_ref[0]
        dst = dstate[...]
        eb = jnp.exp(b)
        bl = b[C - 1:C, :]
        kdec = jnp.exp(bl - b)
        a = _gla_scores(q, k, b)
        rowi = lax.broadcasted_iota(jnp.int32, (C, C), 0)
        coli = lax.broadcasted_iota(jnp.int32, (C, C), 1)
        da = jnp.where(coli <= rowi, _nt(do, v), 0.0)
        dq_s, dk_s = _gla_scores_bwd(da, q, k, b)
        dq = _nn(do, st) * eb + dq_s
        dkk = _nn(v, dst) * kdec + dk_s
        dv_ref[...] = (_tn(a, do) + _nt(k * kdec, dst)).astype(dv_ref.dtype)
        last = lax.broadcasted_iota(jnp.int32, (C, 1), 0) == C - 1
        db = q * dq - k * dkk + jnp.where(last, term[...], 0.0)
        dg_ref[...] = _chunk_suffix_sum(db)
        dq_ref[...] = (dq * scale).astype(dq_ref.dtype)
        dk_ref[...] = dkk.astype(dk_ref.dtype)
        dprev = dst * jnp.exp(bl) + _tn(do, q * eb)
        dstate[...] = dprev
        term[...] = _colsum(st * dprev)

    row = lambda s, h, c: s * nC + (nC - 1 - c)
    kspec = lambda off: pl.BlockSpec((C, dk), lambda s, h, c: (row(s, h, c), off + h))
    vspec = lambda off: pl.BlockSpec((C, dv), lambda s, h, c: (row(s, h, c), off + h))
    return pl.pallas_call(
        body, name=name, grid=(nseq, H, nC),
        in_specs=[kspec(0), kspec(H), vspec(voff), kspec(0), vspec(0),
                  pl.BlockSpec((1, dv, dk), lambda s, h, c: ((s * H + h) * nC + (nC - 1 - c), 0, 0))],
        out_specs=[kspec(0), kspec(0), vspec(0), kspec(0)],
        out_shape=[jax.ShapeDtypeStruct((T, H * dk), _MXU), jax.ShapeDtypeStruct((T, H * dk), _MXU),
                   jax.ShapeDtypeStruct((T, H * dv), _MXU), jax.ShapeDtypeStruct((T, H * dk), f32)],
        scratch_shapes=[pltpu.VMEM((dv, dk), f32), pltpu.VMEM((1, dk), f32)],
        compiler_params=_cparams(("parallel", "parallel", "arbitrary")),
    )(proj, proj, proj, gate, do, states)


def _head_slices(width, n):
    w = width // n
    return [slice(h * w, (h + 1) * w) for h in range(n)]


def _rms_gate_fwd(o, proj, ng, name):
    W = o.shape[1]

    def fn(o, r, ng):
        parts = []
        for sl in _head_slices(W, GLA_HEADS):
            oh = o[:, sl]
            rstd = lax.rsqrt(jnp.mean(oh * oh, axis=-1, keepdims=True) + RMS_EPS)
            rh = r[:, sl]
            parts.append((oh * rstd * ng) * (rh * _sigmoid(rh)))
        return jnp.concatenate(parts, axis=1)

    return _rowwise(fn, [_full(o), (proj, W, 2)], [ng], [(W, _MXU)], [], name)[0]


def _rms_gate_bwd(o, proj, ng, dy, name):
    W = o.shape[1]

    def fn(o, r, dy, ng):
        dos, drs = [], []
        dng = jnp.zeros((1, W // GLA_HEADS), f32)
        for sl in _head_slices(W, GLA_HEADS):
            oh, rh, dyh = o[:, sl], r[:, sl], dy[:, sl]
            rstd = lax.rsqrt(jnp.mean(oh * oh, axis=-1, keepdims=True) + RMS_EPS)
            ohat = oh * rstd
            sg = _sigmoid(rh)
            don = dyh * (rh * sg)
            drs.append(dyh * (ohat * ng) * (sg * (1.0 + rh * (1.0 - sg))))
            dng = dng + _colsum(don * ohat)
            dohat = don * ng
            dos.append(rstd * (dohat - ohat * jnp.mean(dohat * ohat, axis=-1, keepdims=True)))
        return jnp.concatenate(dos, axis=1), jnp.concatenate(drs, axis=1), dng

    return _rowwise(fn, [_full(o), (proj, W, 2), _full(dy)], [ng], [(W, _MXU), (W, _MXU)], [W // GLA_HEADS], name)


def _log_gate_fwd(z, bias, name):
    def fn(z, bias):
        t = z + bias
        return (jnp.minimum(t, 0.0) - jnp.log1p(jnp.exp(-jnp.abs(t)))) * (1.0 / GLA_GATE_NORMALIZER)

    return _rowwise(fn, [_full(z)], [bias], [(z.shape[1], f32)], [], name)[0]


def _log_gate_bwd(z, bias, dg, name):
    def fn(z, dg, bias):
        dz = dg * (1.0 / GLA_GATE_NORMALIZER) * _sigmoid(-(z + bias))
        return dz, _colsum(dz)

    return _rowwise(fn, [_full(z), _full(dg)], [bias], [(z.shape[1], f32)], [z.shape[1]], name)


def _band_masks(P, steps, has_prev):
    i = lax.broadcasted_iota(jnp.int32, (P, P), 0)
    j = lax.broadcasted_iota(jnp.int32, (P, P), 1)
    cur = (i - j >= 0) & (i - j <= steps)
    prev = (i + P - j <= steps) & has_prev
    return cur, prev


def _dil_dims(T, nseq, dilation):
    L = T // nseq // dilation
    P = min(DIL_BLOCK, L)
    return L, P, L // P


def _dil_tiling(T, nseq, dilation):
    L, P, nb = _dil_dims(T, nseq, dilation)
    hb = DIL_HEADS if dilation == 1 else 1
    row_sets = [pl.ds(r, P, stride=dilation) if dilation > 1 else pl.ds(0, P) for r in range(dilation)]
    head_cols = _head_slices(hb * DIL_HEAD_DIM, hb)
    return L, P, nb, P * dilation, hb, DIL_HEADS // hb, row_sets, head_cols


def _dil_fwd(proj, gi, window, dilation, nseq, name):
    T = proj.shape[0]
    H, dh = DIL_HEADS, DIL_HEAD_DIM
    L, P, nb, SB, hb, ng, row_sets, head_cols = _dil_tiling(T, nseq, dilation)
    steps = window // dilation
    scale = dh ** -0.5

    def body(q_ref, kc_ref, kp_ref, vc_ref, vp_ref, o_ref, lse_ref):
        sb, hg = pl.program_id(1), pl.program_id(2)
        mc, mp = _band_masks(P, steps, sb > 0)
        lane = lax.broadcasted_iota(jnp.int32, (P, LANE), 1)

        @pl.when(hg == 0)
        def _():
            lse_ref[...] = jnp.zeros_like(lse_ref)

        for hh, cols in enumerate(head_cols):
            for rows in row_sets:
                q = q_ref[rows, cols]
                sc = jnp.where(mc, _nt(q, kc_ref[rows, cols]) * scale, -jnp.inf)
                sp = jnp.where(mp, _nt(q, kp_ref[rows, cols]) * scale, -jnp.inf)
                m = jnp.maximum(jnp.max(sc, axis=-1, keepdims=True), jnp.max(sp, axis=-1, keepdims=True))
                pc, pp = jnp.exp(sc - m), jnp.exp(sp - m)
                l = jnp.sum(pc, axis=-1, keepdims=True) + jnp.sum(pp, axis=-1, keepdims=True)
                o_ref[rows, cols] = _nn(pc / l, vc_ref[rows, cols]) + _nn(pp / l, vp_ref[rows, cols])
                lse_ref[rows, :] = jnp.where(lane == hg * hb + hh, m + jnp.log(l), lse_ref[rows, :])

    cur = lambda part: pl.BlockSpec((SB, hb * dh), lambda s, sb, hg: (s * nb + sb, (gi * 3 + part) * ng + hg))
    prv = lambda part: pl.BlockSpec((SB, hb * dh), lambda s, sb, hg: (s * nb + jnp.maximum(sb - 1, 0), (gi * 3 + part) * ng + hg))
    return pl.pallas_call(
        body, name=name, grid=(nseq, nb, ng),
        in_specs=[cur(0), cur(1), prv(1), cur(2), prv(2)],
        out_specs=[pl.BlockSpec((SB, hb * dh), lambda s, sb, hg: (s * nb + sb, hg)), pl.BlockSpec((SB, LANE), lambda s, sb, hg: (s * nb + sb, 0))],
        out_shape=[jax.ShapeDtypeStruct((T, H * dh), f32), jax.ShapeDtypeStruct((T, LANE), f32)],
        compiler_params=_cparams(("parallel", "parallel", "arbitrary")),
    )(proj, proj, proj, proj, proj)


def _dil_mix_fwd(os_, lses, name):
    W = os_[0].shape[1]
    G = len(os_)

    def fn(*a):
        o, l = a[:G], a[G:]
        lane = lax.broadcasted_iota(jnp.int32, l[0].shape, 1)
        tot = jnp.zeros(l[0].shape, f32)
        parts = []
        for h, sl in enumerate(_head_slices(W, DIL_HEADS)):
            lh = [x[:, h:h + 1] for x in l]
            m = functools.reduce(jnp.maximum, lh)
            e = [jnp.exp(x - m) for x in lh]
            z = functools.reduce(lambda u, v: u + v, e)
            acc = None
            for g in range(G):
                t = (e[g] / z) * o[g][:, sl]
                acc = t if acc is None else acc + t
            parts.append(acc)
            tot = jnp.where(lane == h, m + jnp.log(z), tot)
        y = jnp.concatenate(parts, axis=1)
        return y, y, tot

    return _rowwise(fn, [_full(x) for x in os_] + [_full(x) for x in lses], [], [(W, f32), (W, _MXU), (LANE, f32)], [], name)


def _dil_delta(do, o, name):
    W = o.shape[1]

    def fn(do, o):
        lane = lax.broadcasted_iota(jnp.int32, (do.shape[0], LANE), 1)
        d = jnp.zeros((do.shape[0], LANE), f32)
        for h, sl in enumerate(_head_slices(W, DIL_HEADS)):
            d = jnp.where(lane == h, jnp.sum(do[:, sl] * o[:, sl], axis=-1, keepdims=True), d)
        return d

    return _rowwise(fn, [_full(do), _full(o)], [], [(LANE, f32)], [], name)[0]


def _dil_bwd(proj, do, lse, delta, gi, window, dilation, nseq, name):
    T = proj.shape[0]
    H, dh = DIL_HEADS, DIL_HEAD_DIM
    L, P, nb, SB, hb, ng, row_sets, head_cols = _dil_tiling(T, nseq, dilation)
    steps = window // dilation
    scale = dh ** -0.5

    def probs(q, k, lse_h, mask):
        return jnp.where(mask, jnp.exp(_nt(q, k) * scale - lse_h), 0.0)

    def head_lane(ref, rows, h):
        lane = lax.broadcasted_iota(jnp.int32, (P, LANE), 1)
        return jnp.sum(jnp.where(lane == h, ref[rows, :], 0.0), axis=1, keepdims=True)

    def dq_body(q_ref, kc_ref, kp_ref, vc_ref, vp_ref, do_ref, lse_ref, del_ref, dq_ref):
        sb, hg = pl.program_id(1), pl.program_id(2)
        mc, mp = _band_masks(P, steps, sb > 0)
        for hh, cols in enumerate(head_cols):
            h = hg * hb + hh
            for rows in row_sets:
                q, doh = q_ref[rows, cols], do_ref[rows, cols]
                lse_h, del_h = head_lane(lse_ref, rows, h), head_lane(del_ref, rows, h)
                kc, kp = kc_ref[rows, cols], kp_ref[rows, cols]
                dsc = probs(q, kc, lse_h, mc) * (_nt(doh, vc_ref[rows, cols]) - del_h) * scale
                dsp = probs(q, kp, lse_h, mp) * (_nt(doh, vp_ref[rows, cols]) - del_h) * scale
                dq_ref[rows, cols] = _nn(dsc, kc) + _nn(dsp, kp)

    cur = lambda part: pl.BlockSpec((SB, hb * dh), lambda s, sb, hg: (s * nb + sb, (gi * 3 + part) * ng + hg))
    prv = lambda part: pl.BlockSpec((SB, hb * dh), lambda s, sb, hg: (s * nb + jnp.maximum(sb - 1, 0), (gi * 3 + part) * ng + hg))
    tok = pl.BlockSpec((SB, hb * dh), lambda s, sb, hg: (s * nb + sb, hg))
    aux = pl.BlockSpec((SB, LANE), lambda s, sb, hg: (s * nb + sb, 0))
    dq = pl.pallas_call(
        dq_body, name=name + "_dq", grid=(nseq, nb, ng),
        in_specs=[cur(0), cur(1), prv(1), cur(2), prv(2), tok, aux, aux],
        out_specs=tok, out_shape=jax.ShapeDtypeStruct((T, H * dh), f32),
        compiler_params=_cparams(("parallel", "parallel", "parallel")),
    )(proj, proj, proj, proj, proj, do, lse, delta)

    def dkv_body(k_ref, v_ref, qc_ref, qn_ref, doc_ref, don_ref, lsec_ref, lsen_ref, delc_ref, deln_ref, dk_ref, dv_ref):
        sb, hg = pl.program_id(1), pl.program_id(2)
        mc, mn = _band_masks(P, steps, sb < nb - 1)
        for hh, cols in enumerate(head_cols):
            h = hg * hb + hh
            for rows in row_sets:
                k, v = k_ref[rows, cols], v_ref[rows, cols]
                qc, doc = qc_ref[rows, cols], doc_ref[rows, cols]
                pc = probs(qc, k, head_lane(lsec_ref, rows, h), mc)
                dsc = pc * (_nt(doc, v) - head_lane(delc_ref, rows, h)) * scale
                qn, don = qn_ref[rows, cols], don_ref[rows, cols]
                pn = probs(qn, k, head_lane(lsen_ref, rows, h), mn)
                dsn = pn * (_nt(don, v) - head_lane(deln_ref, rows, h)) * scale
                dk_ref[rows, cols] = _tn(dsc, qc) + _tn(dsn, qn)
                dv_ref[rows, cols] = _tn(pc, doc) + _tn(pn, don)

    nxt = lambda s, sb: s * nb + jnp.minimum(sb + 1, nb - 1)
    qnx = pl.BlockSpec((SB, hb * dh), lambda s, sb, hg: (nxt(s, sb), gi * 3 * ng + hg))
    tokn = pl.BlockSpec((SB, hb * dh), lambda s, sb, hg: (nxt(s, sb), hg))
    auxn = pl.BlockSpec((SB, LANE), lambda s, sb, hg: (nxt(s, sb), 0))
    dkk, dvv = pl.pallas_call(
        dkv_body, name=name + "_dkv", grid=(nseq, nb, ng),
        in_specs=[cur(1), cur(2), cur(0), qnx, tok, tokn, aux, auxn, aux, auxn],
        out_specs=[tok, tok],
        out_shape=[jax.ShapeDtypeStruct((T, H * dh), f32), jax.ShapeDtypeStruct((T, H * dh), f32)],
        compiler_params=_cparams(("parallel", "parallel", "parallel")),
    )(proj, proj, proj, proj, do, do, lse, lse, delta, delta)
    return dq, dkk, dvv


def _adamw_math(w, g, m, v):
    m = ADAM_B1 * m + (1.0 - ADAM_B1) * g
    v = ADAM_B2 * v + (1.0 - ADAM_B2) * (g * g)
    m_hat = m / (1.0 - ADAM_B1 ** ADAM_STEP)
    v_hat = v / (1.0 - ADAM_B2 ** ADAM_STEP)
    return -ADAM_LR * (m_hat / (jnp.sqrt(v_hat) + ADAM_EPS) + ADAM_WD * w), m, v


def _adamw(w, g, m, v, name):
    W = w.shape[1]
    return _rowwise(_adamw_math, [_full(w), _full(g), _full(m), _full(v)], [], [(W, f32)] * 3, [], name, tm=512)


def _position():
    x, y, c = lax.axis_index("x"), lax.axis_index("y"), lax.axis_index("c")
    other_chips = [(1 - x, y), (x, 1 - y), (1 - x, 1 - y)]
    return x, y, c, other_chips


def _chip_index(x, y):
    return 2 * x + y


_ANY = pl.BlockSpec(memory_space=pl.ANY)


def _all_reduce_small(p, name):
    R, Wd = p.shape

    def body(p_ref, o_ref, buf, send_sems, recv_sems):
        x, y, c, _ = _position()
        me = 4 * x + 2 * y + c
        buf[me] = p_ref[...]
        copies = []
        for k in range(1, 8):
            fx, fy, fc = (k >> 2) & 1, (k >> 1) & 1, k & 1
            peer = (x + fx - 2 * x * fx, y + fy - 2 * y * fy, c + fc - 2 * c * fc)
            cp = pltpu.make_async_remote_copy(src_ref=p_ref, dst_ref=buf.at[me], send_sem=send_sems.at[k - 1],
                                              recv_sem=recv_sems.at[k - 1], device_id=peer, device_id_type=MESH)
            cp.start()
            copies.append(cp)
        for cp in copies:
            cp.wait()
        acc = buf[0]
        for s in range(1, 8):
            acc = acc + buf[s]
        o_ref[...] = acc

    return pl.pallas_call(
        body, name=name, out_shape=jax.ShapeDtypeStruct((R, Wd), f32),
        in_specs=[pl.BlockSpec(memory_space=pltpu.VMEM)], out_specs=pl.BlockSpec(memory_space=pltpu.VMEM),
        scratch_shapes=[pltpu.VMEM((8, R, Wd), f32), pltpu.SemaphoreType.DMA((7,)), pltpu.SemaphoreType.DMA((7,))],
        compiler_params=pltpu.CompilerParams(vmem_limit_bytes=VMEM_LIMIT),
    )(p)


def _layer_half(ref, h, axis):
    n = ref.shape[axis] // 2
    idx = (slice(None),) * axis + (pl.ds(h * n, n),)
    return ref.at[idx]


def _comm_call(body, name, ins, out_shapes, n_sems, n_local=0):
    scratch = [pltpu.SemaphoreType.DMA((n_sems,)), pltpu.SemaphoreType.DMA((n_sems,))]
    if n_local:
        scratch.append(pltpu.SemaphoreType.DMA((n_local,)))
    return pl.pallas_call(body, name=name, out_shape=out_shapes, in_specs=[_ANY] * len(ins), out_specs=[_ANY] * len(out_shapes),
                          scratch_shapes=scratch)(*ins)


class _SideCopies:
    def __init__(self, ins, outs, scratch, start, finish):
        self.ins, self.outs, self.scratch, self.start, self.finish = ins, outs, scratch, start, finish


def _row_half(ref, h):
    r = ref.shape[0] // 2
    start = h * r
    if r % 16 == 0:
        start = pl.multiple_of(start, 16)
    return ref.at[pl.ds(start, r)]


def _gather_plan(ws):
    n = len(ws)

    def copy(o_refs, sems, k, src, i, chip_idx, h, to):
        return pltpu.make_async_remote_copy(src_ref=src, dst_ref=_row_half(o_refs[i].at[chip_idx], h), send_sem=sems[0].at[k],
                                            recv_sem=sems[1].at[k], device_id=to, device_id_type=MESH)

    def own_copy(w_refs, o_refs, sems, i, p):
        return pltpu.make_async_copy(w_refs[i], o_refs[i].at[p], sems[2].at[i])

    def over_ici(w_refs, o_refs, sems, i, j, chip, dst_chip_idx, c):
        return copy(o_refs, sems, 3 * i + j, _row_half(w_refs[i], c), i, dst_chip_idx, c, (*chip, c))

    def start(w_refs, o_refs, *sems):
        x, y, c, chips = _position()
        p = _chip_index(x, y)
        for i in range(n):
            own_copy(w_refs, o_refs, sems, i, p).start()
            for j, chip in enumerate(chips):
                over_ici(w_refs, o_refs, sems, i, j, chip, p, c).start()

    def finish(w_refs, o_refs, *sems):
        x, y, c, chips = _position()
        p = _chip_index(x, y)
        sibling = (x, y, 1 - c)
        passed = []
        for i in range(n):
            for j, chip in enumerate(chips):
                q = _chip_index(*chip)
                over_ici(w_refs, o_refs, sems, i, j, chip, q, c).wait_recv()
                fwd = copy(o_refs, sems, 3 * n + 3 * i + j, _row_half(o_refs[i].at[q], c), i, q, c, sibling)
                fwd.start()
                passed.append(fwd)
        for i in range(n):
            for j, chip in enumerate(chips):
                copy(o_refs, sems, 3 * n + 3 * i + j, _row_half(w_refs[i], c), i, _chip_index(*chip), 1 - c, sibling).wait_recv()
        for i in range(n):
            for j, chip in enumerate(chips):
                over_ici(w_refs, o_refs, sems, i, j, chip, p, c).wait_send()
            own_copy(w_refs, o_refs, sems, i, p).wait()
        for fwd in passed:
            fwd.wait_send()

    scratch = [pltpu.SemaphoreType.DMA((6 * n,)), pltpu.SemaphoreType.DMA((6 * n,)), pltpu.SemaphoreType.DMA((n,))]
    return _SideCopies(list(ws), [jax.ShapeDtypeStruct((4,) + w.shape, w.dtype) for w in ws], scratch, start, finish)


def _run_copies(plan, name):
    n_i, n_o = len(plan.ins), len(plan.outs)

    def body(*refs):
        plan.start(refs[:n_i], refs[n_i:n_i + n_o], *refs[n_i + n_o:])
        plan.finish(refs[:n_i], refs[n_i:n_i + n_o], *refs[n_i + n_o:])

    return pl.pallas_call(body, name=name, out_shape=plan.outs, in_specs=[_ANY] * n_i, out_specs=[_ANY] * n_o,
                          scratch_shapes=plan.scratch)(*plan.ins)


def _sibling_halves(gs, name):
    n = len(gs)

    def body(*refs):
        g_refs, o_refs, send_sems, recv_sems = refs[:n], refs[n:2 * n], refs[2 * n], refs[2 * n + 1]
        x, y, c, _ = _position()
        copies = []
        for i in range(n):
            cp = pltpu.make_async_remote_copy(src_ref=_layer_half(g_refs[i], 1 - c, 1), dst_ref=o_refs[i], send_sem=send_sems.at[i],
                                              recv_sem=recv_sems.at[i], device_id=(x, y, 1 - c), device_id_type=MESH)
            cp.start()
            copies.append(cp)
        for cp in copies:
            cp.wait()

    outs = [jax.ShapeDtypeStruct((4, g.shape[1] // 2) + g.shape[2:], g.dtype) for g in gs]
    return _comm_call(body, name, gs, outs, n)


def _chip_exchange(hs, name):
    n = len(hs)

    def body(*refs):
        h_refs, o_refs, send_sems, recv_sems = refs[:n], refs[n:2 * n], refs[2 * n], refs[2 * n + 1]
        x, y, c, chips = _position()
        copies = []
        for i in range(n):
            for j, chip in enumerate(chips):
                cp = pltpu.make_async_remote_copy(src_ref=h_refs[i].at[_chip_index(*chip)], dst_ref=o_refs[i].at[j],
                                                  send_sem=send_sems.at[3 * i + j], recv_sem=recv_sems.at[3 * i + j],
                                                  device_id=(*chip, c), device_id_type=MESH)
                cp.start()
                copies.append(cp)
        for cp in copies:
            cp.wait()

    return _comm_call(body, name, hs, [jax.ShapeDtypeStruct((3,) + h.shape[1:], h.dtype) for h in hs], 3 * n)


def _sibling_swap(ts, name):
    n = len(ts)

    def body(*refs):
        t_refs, o_refs, send_sems, recv_sems = refs[:n], refs[n:2 * n], refs[2 * n], refs[2 * n + 1]
        x, y, c, _ = _position()
        copies = []
        for i in range(n):
            cp = pltpu.make_async_remote_copy(src_ref=t_refs[i], dst_ref=o_refs[i], send_sem=send_sems.at[i], recv_sem=recv_sems.at[i],
                                              device_id=(x, y, 1 - c), device_id_type=MESH)
            cp.start()
            copies.append(cp)
        for cp in copies:
            cp.wait()

    return _comm_call(body, name, ts, [jax.ShapeDtypeStruct(t.shape, t.dtype) for t in ts], n)


def _row_tile(K, N):
    return _div_tile(K, 256 if N <= 1024 else 128, 16)


def _prefetch_call(body, name, scalars, grid, in_specs, out_specs, out_shape, args, sem):
    gs = pltpu.PrefetchScalarGridSpec(num_scalar_prefetch=1, grid=grid, in_specs=in_specs, out_specs=out_specs)
    return pl.pallas_call(body, name=name, grid_spec=gs, out_shape=out_shape, compiler_params=_cparams(sem))(scalars, *args)


def _add_own_half(g, got, c, name):
    _, nl, K, N = g.shape
    hl = nl // 2
    tm = _row_tile(K, N)

    def body(c_ref, g_ref, r_ref, o_ref):
        o_ref[...] = (g_ref[...].astype(f32) + r_ref[...].astype(f32)).astype(o_ref.dtype)

    blk = (1, 1, tm, N)
    return _prefetch_call(
        body, name, jnp.reshape(c, (1,)).astype(jnp.int32), (4, hl, K // tm),
        [pl.BlockSpec(blk, lambda s, l, i, c_ref: (s, c_ref[0] * hl + l, i, 0)), pl.BlockSpec(blk, lambda s, l, i, c_ref: (s, l, i, 0))],
        pl.BlockSpec(blk, lambda s, l, i, c_ref: (s, l, i, 0)), jax.ShapeDtypeStruct((4, hl, K, N), g.dtype), (g, got),
        ("parallel", "parallel", "parallel"))


def _add_chips(h, got, p, name):
    _, nl, K, N = h.shape
    tm = _row_tile(K, N)

    def body(p_ref, h_ref, r0_ref, r1_ref, r2_ref, o_ref):
        o_ref[...] = ((h_ref[0].astype(f32) + r0_ref[0].astype(f32)) + r1_ref[0].astype(f32)) + r2_ref[0].astype(f32)

    blk = (1, 1, tm, N)
    got_spec = lambda j: pl.BlockSpec(blk, lambda l, i, p_ref: (j, l, i, 0))
    return _prefetch_call(
        body, name, jnp.reshape(p, (1,)).astype(jnp.int32), (nl, K // tm),
        [pl.BlockSpec(blk, lambda l, i, p_ref: (p_ref[0], l, i, 0)), got_spec(0), got_spec(1), got_spec(2)],
        pl.BlockSpec((1, tm, N), lambda l, i, p_ref: (l, i, 0)), jax.ShapeDtypeStruct((nl, K, N), f32), (h, got, got, got),
        ("parallel", "parallel"))


def _adamw_halves(w, m, v, own, other, c, name):
    nl, K, N = w.shape
    hl = nl // 2
    tm = _row_tile(K, N)

    def body(c_ref, w_ref, m_ref, v_ref, own_ref, other_ref, g_out, d_out, m_out, v_out):
        mine = (pl.program_id(0) // hl) == c_ref[0]
        g = jnp.where(mine, own_ref[...], other_ref[...])
        d, m_new, v_new = _adamw_math(w_ref[...], g, m_ref[...], v_ref[...])
        g_out[...] = g
        d_out[...] = d
        m_out[...] = m_new
        v_out[...] = v_new

    blk = (1, tm, N)
    full = pl.BlockSpec(blk, lambda l, i, c_ref: (l, i, 0))
    half = pl.BlockSpec(blk, lambda l, i, c_ref: (l % hl, i, 0))
    return _prefetch_call(
        body, name, jnp.reshape(c, (1,)).astype(jnp.int32), (nl, K // tm), [full, full, full, half, half], [full] * 4,
        [jax.ShapeDtypeStruct((nl, K, N), f32)] * 4, (w, m, v, own, other), ("parallel", "parallel"))


def _reduce_scatter_chips(gs, name):
    c = lax.axis_index("c")
    p = _chip_index(lax.axis_index("x"), lax.axis_index("y"))
    from_sibling = _sibling_halves(gs, name + "_d2d")
    hs = [_add_own_half(g, r, c, f"{name}_add2_{i}") for i, (g, r) in enumerate(zip(gs, from_sibling))]
    got = _chip_exchange(hs, name + "_ici")
    own = [_add_chips(h, r, p, f"{name}_add4_{i}") for i, (h, r) in enumerate(zip(hs, got))]
    other = _sibling_swap(own, name + "_swap")
    return own, other


_BIG = (("gla_w_in", 1), ("gla_w_out", 0), ("dil_w_in", 1), ("dil_w_out", 1), ("ffn_w_up", 1), ("ffn_w_down", 0))


def _pad_rows(a, mult):
    r = (-a.shape[0]) % mult
    return a if r == 0 else jnp.concatenate([a, jnp.zeros((r,) + a.shape[1:], a.dtype)], axis=0)


def _unshard(blocks, axis):
    _, K, N = blocks.shape
    if axis == 1:
        return blocks.transpose(1, 0, 2).reshape(K, 4 * N)
    return blocks.reshape(4 * K, N)


def _to_shards(mat, axis):
    K, N = mat.shape
    if axis == 1:
        return mat.reshape(K, 4, N // 4).transpose(1, 0, 2)
    return mat.reshape(4, K // 4, N)


def _scatter_big_grads(grads):
    gs = [jnp.stack([_to_shards(g, axis) for g in grads[n]], axis=1) for n, axis in _BIG]
    own, other = _reduce_scatter_chips(gs, "scatter_grads")
    return {n: (a, b) for (n, _), a, b in zip(_BIG, own, other)}


_SMALL = (("gla_w_gate_up", 2), ("gla_gate_bias", None), ("gla_norm_g", None), ("ffn_conv_w", 2), ("ffn_conv_b", None),
          ("ln_g", 2), ("ln_b", 2))


def _pack_rows(arrs, width=LANE):
    flat = _pad_rows(jnp.concatenate([a.reshape(-1) for a in arrs]), 8 * width)
    return flat.reshape(-1, width)


def _unpack_rows(packed, shapes):
    flat, out, off = packed.reshape(-1), [], 0
    for s in shapes:
        n = 1
        for d in s:
            n *= d
        out.append(flat[off:off + n].reshape(s))
        off += n
    return out


def _gather_small_params(shards):
    x, y, c = lax.axis_index("x"), lax.axis_index("y"), lax.axis_index("c")
    names = [n for n, axis in _SMALL if axis is not None]
    mine = _pack_rows([shards[n] for n in names])
    mine = jnp.where(c == 0, mine, jnp.zeros_like(mine))
    rows = mine.shape[0]
    placed = lax.dynamic_update_slice(jnp.zeros((4 * rows, LANE), f32), mine, (_chip_index(x, y) * rows, 0))
    allp = _all_reduce_small(placed, "gather_small").reshape(4, rows, LANE)
    out = {n: shards[n] for n, axis in _SMALL if axis is None}
    per_chip = [_unpack_rows(allp[q], [shards[n].shape for n in names]) for q in range(4)]
    for i, n in enumerate(names):
        out[n] = jnp.concatenate([per_chip[q][i] for q in range(4)], axis=2)
    return out


def _reduce_small_grads(grads, shards):
    names = [n for n, _ in _SMALL]
    total = _all_reduce_small(_pack_rows([grads[n] for n in names]), "reduce_small")
    full = dict(zip(names, _unpack_rows(total, [grads[n].shape for n in names])))
    p = _chip_index(lax.axis_index("x"), lax.axis_index("y"))
    out = {}
    for n, axis in _SMALL:
        if axis is None:
            out[n] = full[n]
        else:
            w = shards[n].shape[axis]
            out[n] = lax.dynamic_slice_in_dim(full[n], p * w, w, axis=axis)
    return out


def _pad_cols(a, n):
    return a if a.shape[-1] == n else jnp.concatenate([a, jnp.zeros(a.shape[:-1] + (n - a.shape[-1],), a.dtype)], axis=-1)


def _ffn_width(F):
    return -(-F // 512) * 512


def kernel(x, gla_w_in, gla_w_gate_up, gla_gate_bias, gla_norm_g, gla_w_out, dil_w_in, dil_w_out, ffn_w_up, ffn_conv_w, ffn_conv_b, ffn_w_down, ln_g, ln_b, loss_target, m_gla_w_in, m_gla_w_gate_up, m_gla_gate_bias, m_gla_norm_g, m_gla_w_out, m_dil_w_in, m_dil_w_out, m_ffn_w_up, m_ffn_conv_w, m_ffn_conv_b, m_ffn_w_down, m_ln_g, m_ln_b, v_gla_w_in, v_gla_w_gate_up, v_gla_gate_bias, v_gla_norm_g, v_gla_w_out, v_dil_w_in, v_dil_w_out, v_ffn_w_up, v_ffn_conv_w, v_ffn_conv_b, v_ffn_w_down, v_ln_g, v_ln_b):
    names = ["gla_w_in", "gla_w_gate_up", "gla_gate_bias", "gla_norm_g", "gla_w_out", "dil_w_in", "dil_w_out", "ffn_w_up",
             "ffn_conv_w", "ffn_conv_b", "ffn_w_down", "ln_g", "ln_b"]
    w_sh = dict(zip(names, (gla_w_in, gla_w_gate_up, gla_gate_bias, gla_norm_g, gla_w_out, dil_w_in, dil_w_out, ffn_w_up,
                            ffn_conv_w, ffn_conv_b, ffn_w_down, ln_g, ln_b)))
    m_sh = dict(zip(names, (m_gla_w_in, m_gla_w_gate_up, m_gla_gate_bias, m_gla_norm_g, m_gla_w_out, m_dil_w_in, m_dil_w_out,
                            m_ffn_w_up, m_ffn_conv_w, m_ffn_conv_b, m_ffn_w_down, m_ln_g, m_ln_b)))
    v_sh = dict(zip(names, (v_gla_w_in, v_gla_w_gate_up, v_gla_gate_bias, v_gla_norm_g, v_gla_w_out, v_dil_w_in, v_dil_w_out,
                            v_ffn_w_up, v_ffn_conv_w, v_ffn_conv_b, v_ffn_w_down, v_ln_g, v_ln_b)))
    nseq, S, D = x.shape
    T = nseq * S
    small = _gather_small_params(w_sh)
    F = 4 * w_sh["ffn_w_down"].shape[1]
    Fp = _ffn_width(F)
    qkvr = 4 * w_sh["gla_w_in"].shape[2] - GLA_GATE_RANK
    c_idx = lax.axis_index("c")
    shard_axis = dict(_BIG)

    def pad_halves(a):
        return jnp.concatenate([_pad_cols(a[..., :F], Fp), _pad_cols(a[..., F:], Fp)], axis=-1)

    cw_all = pad_halves(small["ffn_conv_w"])
    cb_all = pad_halves(small["ffn_conv_b"][:, None, :])
    w_gate_up_all = jnp.pad(small["gla_w_gate_up"].astype(bf16), ((0, 0), (0, GATE_PAD - GLA_GATE_RANK), (0, 0)))

    def mixer_names(l):
        return ("gla_w_in", "gla_w_out") if l % 2 == 0 else ("dil_w_in", "dil_w_out")

    def shards_of(l, which):
        n_in, n_out = mixer_names(l)
        return {"mixer": [w_sh[n_in][l // 2].astype(bf16), w_sh[n_out][l // 2].astype(bf16)],
                "up": [w_sh["ffn_w_up"][l].astype(bf16)], "down": [w_sh["ffn_w_down"][l].astype(bf16)]}[which]

    def side_for(l, which):
        return _gather_plan(shards_of(l + 1, which)) if l + 1 < DEPTH else None

    weights = [dict() for _ in range(DEPTH)]

    def install(l, which, gathered):
        n_in, n_out = mixer_names(l)
        if which == "mixer":
            w_in = _unshard(gathered[0], shard_axis[n_in])
            weights[l]["w_in"], weights[l]["w_out"] = w_in, _unshard(gathered[1], shard_axis[n_out])
            if l % 2 == 0:
                weights[l]["w_gate"] = _pad_cols(w_in[:, qkvr:], GATE_PAD)
        elif which == "up":
            weights[l]["w_up"] = pad_halves(_unshard(gathered[0], shard_axis["ffn_w_up"]))
        else:
            weights[l]["w_down"] = _pad_rows(_unshard(gathered[0], shard_axis["ffn_w_down"]), Fp)

    first = _run_copies(_gather_plan(shards_of(0, "mixer") + shards_of(0, "up") + shards_of(0, "down")), "gather_layer0")
    install(0, "mixer", first[0:2])
    install(0, "up", first[2:3])
    install(0, "down", first[3:4])

    def carried(l, which, res):
        if l + 1 < DEPTH:
            install(l + 1, which, res[1:])
            return res[0]
        return res

    h0 = x.reshape(T, D)
    saved = []
    cur, cur_b = h0, h0.astype(_MXU)
    fwd = dict(tm=1024, tn=1024, tk=2048)
    for i in range(DEPTH):
        j = i // 2
        tag = f"l{i}_"
        lg, lb = small["ln_g"][i], small["ln_b"][i]
        W = weights[i]
        if i % 2 == 0:
            gate_bias, norm_g = small["gla_gate_bias"][j][None, :], small["gla_norm_g"][j][None, :]
            proj = carried(i, "mixer", _mm(cur_b, W["w_in"], "nn", tag + "gla_proj", n_out=qkvr, side=side_for(i, "mixer"), **fwd))
            g_low = _mm(cur_b, W["w_gate"], "nn", tag + "gla_glow", tn=GATE_PAD)
            z = _mm(g_low, w_gate_up_all, "nn", tag + "gla_z", layer=j)
            gate = _log_gate_fwd(z, gate_bias, tag + "gla_gate")
            o, states = _gla_fwd(proj, gate, nseq, tag + "gla_core")
            y_b = _rms_gate_fwd(o, proj, norm_g, tag + "gla_norm")
            mix = _mm(y_b, W["w_out"], "nn", tag + "gla_out", **fwd)
            mixer_saved = (proj, g_low, z, gate, states, o, y_b)
        else:
            proj = carried(i, "mixer", _mm(cur_b, W["w_in"], "nn", tag + "dil_proj", side=side_for(i, "mixer"), **fwd))
            outs, lses = [], []
            for gi, (window, dilation) in enumerate(DIL_PATTERNS):
                og, lg_ = _dil_fwd(proj, gi, window, dilation, nseq, tag + f"dil_attn{gi}")
                outs.append(og)
                lses.append(lg_)
            y, y_b, lse_tot = _dil_mix_fwd(outs, lses, tag + "dil_mix")
            mix = _mm(y_b, W["w_out"], "nn", tag + "dil_out", **fwd)
            mixer_saved = (proj, y, y_b, lse_tot)
        x1, x1_b = _ln_fwd(cur, mix, lg[0:1], lb[0:1], tag + "ln1")
        cw, cb = cw_all[i], cb_all[i]
        hh = carried(i, "up", _mm(x1_b, W["w_up"], "nn", tag + "ffn_up", side=side_for(i, "up"), **fwd))
        act_b = _conv_gate_fwd(hh, cw, cb, nseq, tag + "ffn_conv")
        ffn = carried(i, "down", _mm(act_b, W["w_down"], "nn", tag + "ffn_down", tm=1024, tn=512, tk=Fp, side=side_for(i, "down")))
        x2, x2_b = _ln_fwd(x1, ffn, lg[1:2], lb[1:2], tag + "ln2")
        saved.append((cur, cur_b, mix, x1, x1_b, hh, act_b, ffn, mixer_saved))
        cur, cur_b = x2, x2_b

    dy, sq = _loss_head(cur, loss_target.reshape(T, D), "loss_head")
    loss = lax.psum(0.5 * jnp.sum(sq) / D, ("x", "y", "c"))

    gb = {n: [None] * w_sh[n].shape[0] for n in names}
    d_res = None
    d_in = dy
    for i in reversed(range(DEPTH)):
        j = i // 2
        tag = f"l{i}_b_"
        xin, xin_b, mix, x1, x1_b, hh, act_b, ffn, mixer_saved = saved[i]
        lg = small["ln_g"][i]
        cw, cb = cw_all[i], cb_all[i]
        dw_tiles = dict(tm=1024, tn=1024, tk=2048, out_dtype=bf16)
        dx_tiles = dict(tm=1024, tn=512, tk=6144)
        dys, scales = ([d_in], [1.0]) if d_res is None else ([d_res, d_in], [DEEPNORM_ALPHA, 1.0])
        du2, du2_b, dg2, db2 = _ln_bwd(x1, ffn, lg[1:2], dys, scales, tag + "ln2")
        gb["ffn_w_down"][i] = _mm(act_b, du2_b, "tn", tag + "ffn_down_dw", **dw_tiles)[:F]
        W = weights[i]
        dact = _mm(du2_b, W["w_down"], "nt", tag + "ffn_down_dx", **dx_tiles)
        dhg, dhu, dcw, dcb = _conv_gate_bwd(hh, dact, cw, cb, nseq, tag + "ffn_conv")
        gb["ffn_conv_w"][i] = jnp.concatenate([dcw[:, :F], dcw[:, Fp:Fp + F]], axis=1)
        gb["ffn_conv_b"][i] = jnp.concatenate([dcb[0, :F], dcb[0, Fp:Fp + F]], axis=0)
        dwg = _mm(x1_b, dhg, "tn", tag + "ffn_gate_dw", **dw_tiles)
        dwu = _mm(x1_b, dhu, "tn", tag + "ffn_up_dw", **dw_tiles)
        gb["ffn_w_up"][i] = jnp.concatenate([dwg[:, :F], dwu[:, :F]], axis=1)
        dx1g = _mm(dhg, W["w_up"], "nt", tag + "ffn_gate_dx", **dx_tiles)
        dx1u = _mm(dhu, W["w_up"], "nt", tag + "ffn_up_dx", b_col0=Fp, **dx_tiles)
        du1, du1_b, dg1, db1 = _ln_bwd(xin, mix, lg[0:1], [du2, dx1g, dx1u], [DEEPNORM_ALPHA, 1.0, 1.0], tag + "ln1")
        gb["ln_g"][i] = jnp.concatenate([dg1, dg2], axis=0)
        gb["ln_b"][i] = jnp.concatenate([db1, db2], axis=0)
        if i % 2 == 0:
            proj, g_low, z, gate, states, o, y_b = mixer_saved
            gate_bias, norm_g = small["gla_gate_bias"][j][None, :], small["gla_norm_g"][j][None, :]
            gb["gla_w_out"][j] = _mm(y_b, du1_b, "tn", tag + "gla_out_dw", **dw_tiles)
            dyy = _mm(du1_b, W["w_out"], "nt", tag + "gla_out_dx", **dx_tiles)
            do, dr, dng = _rms_gate_bwd(o, proj, norm_g, dyy, tag + "gla_norm")
            gb["gla_norm_g"][j] = dng[0]
            dq, dk_, dv_, dgate = _gla_bwd(proj, gate, states, do, nseq, tag + "gla_core")
            dz, dbias = _log_gate_bwd(z, gate_bias, dgate, tag + "gla_gate")
            gb["gla_gate_bias"][j] = dbias[0]
            gb["gla_w_gate_up"][j] = _mm(g_low, dz, "tn", tag + "gla_z_dw", tk=1024)[:GLA_GATE_RANK]
            dg_low = _mm(dz, w_gate_up_all, "nt", tag + "gla_z_dx", tn=GATE_PAD, layer=j)
            dproj = jnp.concatenate([dq, dk_, dv_, dr], axis=1)
            dw_main = _mm(xin_b, dproj, "tn", tag + "gla_proj_dw", **dw_tiles)
            dw_gate = _mm(xin_b, dg_low, "tn", tag + "gla_glow_dw", tm=1024, tn=GATE_PAD, tk=2048, out_dtype=bf16)[:, :GLA_GATE_RANK]
            gb["gla_w_in"][j] = jnp.concatenate([dw_main, dw_gate], axis=1)
            dxa = _mm(dproj, W["w_in"], "nt", tag + "gla_proj_dx", **dx_tiles)
            dxb = _mm(dg_low, W["w_gate"], "nt", tag + "gla_glow_dx")
            d_in = _axpy(dxa, dxb, 1.0, tag + "gla_dx_sum")
        else:
            proj, y, y_b, lse_tot = mixer_saved
            gb["dil_w_out"][j] = _mm(y_b, du1_b, "tn", tag + "dil_out_dw", **dw_tiles)
            dyy = _mm(du1_b, W["w_out"], "nt", tag + "dil_out_dx", **dx_tiles)
            delta = _dil_delta(dyy, y, tag + "dil_delta")
            pieces = []
            for gi, (window, dilation) in enumerate(DIL_PATTERNS):
                pieces += list(_dil_bwd(proj, dyy, lse_tot, delta, gi, window, dilation, nseq, tag + f"dil_attn{gi}"))
            dproj = jnp.concatenate(pieces, axis=1).astype(_MXU)
            gb["dil_w_in"][j] = _mm(xin_b, dproj, "tn", tag + "dil_proj_dw", **dw_tiles)
            d_in = _mm(dproj, W["w_in"], "nt", tag + "dil_proj_dx", **dx_tiles)
        d_res = du1
    grad_x = _axpy(d_res, d_in, DEEPNORM_ALPHA, "grad_x").reshape(x.shape)

    halves = _scatter_big_grads(gb)
    grads, delta, new_m, new_v = {}, {}, {}, {}
    for n, _ in _BIG:
        own, other = halves[n]
        grads[n], delta[n], new_m[n], new_v[n] = _adamw_halves(w_sh[n], m_sh[n], v_sh[n], own, other, c_idx, "adamw_" + n)
    grads.update(_reduce_small_grads({n: jnp.stack(gb[n], axis=0) for n, _ in _SMALL}, w_sh))
    small_names = [n for n, _ in _SMALL]
    packed = [_pack_rows([src[n] for n in small_names]) for src in (w_sh, grads, m_sh, v_sh)]
    res = _adamw(*packed, "adamw_small")
    shapes = [w_sh[n].shape for n in small_names]
    for dst, arr in zip((delta, new_m, new_v), res):
        dst.update(dict(zip(small_names, _unpack_rows(arr, shapes))))

    return (loss, grad_x, *[grads[n] for n in names], *[delta[n] for n in names], *[new_m[n] for n in names],
            *[new_v[n] for n in names])
```

```python
import functools

import jax
import jax.numpy as jnp
from jax import lax
from jax.experimental import pallas as pl
from jax.experimental.pallas import tpu as pltpu

f32 = jnp.float32
bf16 = jnp.bfloat16
_MXU = jnp.bfloat16

DEPTH = 4
GLA_HEADS = 4
GLA_GATE_RANK = 16
GLA_GATE_NORMALIZER = 16.0
GLA_CHUNK = 64
GLA_SUB = 16
DIL_PATTERNS = ((128, 1), (512, 4), (2048, 16))
DIL_HEADS = 8
DIL_HEAD_DIM = 128
DIL_BLOCK = 128
DEEPNORM_ALPHA = (2 * DEPTH) ** 0.25
LN_EPS = 1e-5
RMS_EPS = 1e-6
ADAM_LR = 0.001
ADAM_B1 = 0.9
ADAM_B2 = 0.999
ADAM_EPS = 1e-08
ADAM_WD = 0.01
ADAM_STEP = 10

LANE = 128
VMEM_LIMIT = 48 * 1024 * 1024
GATE_PAD = LANE
MESH = pl.DeviceIdType.MESH


def _cparams(sem=None):
    return pltpu.CompilerParams(dimension_semantics=sem, vmem_limit_bytes=VMEM_LIMIT)


def _div_tile(n, pref, unit):
    if n <= pref:
        return n
    best = None
    for t in range(unit, pref + 1, unit):
        if n % t == 0:
            best = t
    assert best is not None, (n, pref, unit)
    return best


def _dot(a, b, ca, cb):
    return lax.dot_general(a.astype(_MXU), b.astype(_MXU), (((ca,), (cb,)), ((), ())), preferred_element_type=f32)


def _nn(a, b):
    return _dot(a, b, 1, 0)


def _nt(a, b):
    return _dot(a, b, 1, 1)


def _tn(a, b):
    return _dot(a, b, 0, 0)


def _exact_dot(a, b):
    return jnp.dot(a, b, precision=lax.Precision.HIGHEST, preferred_element_type=f32)


def _sigmoid(x):
    return 1.0 / (1.0 + jnp.exp(-x))


def _call_with_side(body, name, grid, in_specs, out_specs, out_shape, scratch, sem, args, side):
    if side is None:
        return pl.pallas_call(body, name=name, grid=grid, in_specs=in_specs, out_specs=out_specs, out_shape=out_shape,
                              scratch_shapes=scratch, compiler_params=_cparams(sem))(*args)
    n_in, n_out, n_scr = len(in_specs), len(out_specs), len(scratch)
    n_si, n_so = len(side.ins), len(side.outs)

    def wrapped(*refs):
        ins, s_in = refs[:n_in], refs[n_in:n_in + n_si]
        outs, s_out = refs[n_in + n_si:n_in + n_si + n_out], refs[n_in + n_si + n_out:n_in + n_si + n_out + n_so]
        rest = refs[n_in + n_si + n_out + n_so:]
        sems = rest[n_scr:]
        ids = [pl.program_id(ax) for ax in range(len(grid))]
        first = functools.reduce(lambda u, v: u & v, [i == 0 for i in ids])
        last = functools.reduce(lambda u, v: u & v, [i == g - 1 for i, g in zip(ids, grid)])

        @pl.when(first)
        def _():
            side.start(s_in, s_out, *sems)

        body(*ins, *outs, *rest[:n_scr])

        @pl.when(last)
        def _():
            side.finish(s_in, s_out, *sems)

    return pl.pallas_call(
        wrapped, name=name, grid=grid, in_specs=list(in_specs) + [_ANY] * n_si, out_specs=list(out_specs) + [_ANY] * n_so,
        out_shape=list(out_shape) + list(side.outs), scratch_shapes=list(scratch) + list(side.scratch),
        compiler_params=_cparams(("arbitrary",) * len(grid)))(*args, *side.ins)


def _mm(a, b, mode, name, tm=1024, tn=512, tk=2048, out_dtype=f32, layer=None, n_out=None, b_col0=0, side=None):
    if mode == "nn":
        (M, K), N = a.shape, (n_out or b.shape[-1])
    elif mode == "nt":
        (M, K), N = a.shape, b.shape[-2]
    else:
        (K, M), N = a.shape, b.shape[-1]
    tm, tn, tk = _div_tile(M, tm, LANE), _div_tile(N, tn, LANE), _div_tile(K, tk, LANE)
    nk = K // tk
    if mode == "tn":
        a_spec = pl.BlockSpec((tk, tm), lambda i, j, k: (k, i))
    else:
        a_spec = pl.BlockSpec((tm, tk), lambda i, j, k: (i, k))
    lead = () if layer is None else (None,)
    pre = () if layer is None else (layer,)
    if mode == "nt":
        k0 = b_col0 // tk
        assert k0 * tk == b_col0
        b_spec = pl.BlockSpec(lead + (tn, tk), lambda i, j, k: pre + (j, k + k0))
    else:
        b_spec = pl.BlockSpec(lead + (tk, tn), lambda i, j, k: pre + (k, j))
    ca, cb = {"nn": (1, 0), "nt": (1, 1), "tn": (0, 0)}[mode]

    def body(a_ref, b_ref, o_ref, *acc):
        p = _dot(a_ref[...], b_ref[...], ca, cb)
        if nk == 1:
            o_ref[...] = p.astype(o_ref.dtype)
        else:
            k = pl.program_id(2)
            acc_ref = acc[0]

            @pl.when(k == 0)
            def _():
                acc_ref[...] = p

            @pl.when(k > 0)
            def _():
                acc_ref[...] += p

            @pl.when(k == nk - 1)
            def _():
                o_ref[...] = acc_ref[...].astype(o_ref.dtype)

    res = _call_with_side(
        body, name, (M // tm, N // tn, nk), [a_spec, b_spec], [pl.BlockSpec((tm, tn), lambda i, j, k: (i, j))],
        [jax.ShapeDtypeStruct((M, N), out_dtype)], [pltpu.VMEM((tm, tn), f32)] if nk > 1 else [],
        ("parallel", "parallel", "arbitrary"), (a, b), side)
    return res if side else res[0]


def _rowwise(fn, rows, consts, outs, reds, name, tm=256):
    T = rows[0][0].shape[0]
    tm = _div_tile(T, tm, 8)
    n_r, n_c, n_o = len(rows), len(consts), len(outs)

    def body(*refs):
        ins = [r[...] for r in refs[: n_r + n_c]]
        res = fn(*ins)
        res = res if isinstance(res, (tuple, list)) else (res,)
        o_refs = refs[n_r + n_c: n_r + n_c + n_o]
        r_refs = refs[n_r + n_c + n_o:]
        for ref, val in zip(o_refs, res[:n_o]):
            ref[...] = val.astype(ref.dtype)
        i = pl.program_id(0)
        for ref, val in zip(r_refs, res[n_o:]):
            _accumulate(ref, val, i)

    in_specs = [pl.BlockSpec((tm, w), functools.partial(lambda i, cb: (i, cb), cb=cb)) for (_, w, cb) in rows]
    in_specs += [pl.BlockSpec(c.shape, lambda i: (0, 0)) for c in consts]
    out_specs = [pl.BlockSpec((tm, w), lambda i: (i, 0)) for (w, _) in outs]
    out_specs += [pl.BlockSpec((1, w), lambda i: (0, 0)) for w in reds]
    out_shape = [jax.ShapeDtypeStruct((T, w), dt) for (w, dt) in outs]
    out_shape += [jax.ShapeDtypeStruct((1, w), f32) for w in reds]
    return pl.pallas_call(
        body, name=name, grid=(T // tm,), in_specs=in_specs, out_specs=out_specs, out_shape=out_shape,
        compiler_params=_cparams(("arbitrary",)),
    )(*[r[0] for r in rows], *consts)


def _accumulate(ref, val, step):
    @pl.when(step == 0)
    def _():
        ref[...] = val

    @pl.when(step > 0)
    def _():
        ref[...] += val


def _full(a):
    return (a, a.shape[1], 0)


def _colsum(x):
    return jnp.sum(x, axis=0, keepdims=True)


def _ln_stats(u):
    mu = jnp.mean(u, axis=-1, keepdims=True)
    xc = u - mu
    var = jnp.mean(xc * xc, axis=-1, keepdims=True)
    rstd = lax.rsqrt(var + LN_EPS)
    return xc * rstd, rstd


def _ln_fwd(x, f, g, b, name):
    def fn(x, f, g, b):
        xhat, _ = _ln_stats(DEEPNORM_ALPHA * x + f)
        y = xhat * g + b
        return y, y

    return _rowwise(fn, [_full(x), _full(f)], [g, b], [(x.shape[1], f32), (x.shape[1], _MXU)], [], name)


def _ln_bwd(x, f, g, dys, scales, name):
    def fn(x, f, *rest):
        g = rest[-1]
        dy = None
        for d, s in zip(rest[:-1], scales):
            t = d if s == 1.0 else s * d
            dy = t if dy is None else dy + t
        xhat, rstd = _ln_stats(DEEPNORM_ALPHA * x + f)
        dxh = dy * g
        m1 = jnp.mean(dxh, axis=-1, keepdims=True)
        m2 = jnp.mean(dxh * xhat, axis=-1, keepdims=True)
        du = rstd * (dxh - m1 - xhat * m2)
        return du, du, _colsum(dy * xhat), _colsum(dy)

    D = x.shape[1]
    return _rowwise(fn, [_full(x), _full(f)] + [_full(d) for d in dys], [g], [(D, f32), (D, _MXU)], [D, D], name)


def _loss_head(y, t, name):
    D = y.shape[1]

    def fn(y, t):
        e = y - t
        return e * (1.0 / D), _colsum(e * e)

    return _rowwise(fn, [_full(y), _full(t)], [], [(D, f32)], [D], name)


def _axpy(a, b, alpha, name):
    def fn(a, b):
        return alpha * a + b

    return _rowwise(fn, [_full(a), _full(b)], [], [(a.shape[1], f32)], [], name)[0]


def _shift_down(x, k):
    row = lax.broadcasted_iota(jnp.int32, x.shape, 0)
    return jnp.where(row >= k, pltpu.roll(x, k, 0), 0.0)


def _shift_up(x, k):
    S = x.shape[0]
    row = lax.broadcasted_iota(jnp.int32, x.shape, 0)
    return jnp.where(row < S - k, pltpu.roll(x, S - k, 0), 0.0)


def _causal_conv(h, w, b):
    return ((b + w[0:1] * _shift_down(h, 2)) + w[1:2] * _shift_down(h, 1)) + w[2:3] * h


def _conv_gate_fwd(h, cw, cb, nseq, name, tc=256, side=None):
    T, F2 = h.shape
    F, S = F2 // 2, T // nseq
    tc = _div_tile(F, tc, LANE)
    nf = F // tc

    def body(hg_ref, hu_ref, wg_ref, wu_ref, bg_ref, bu_ref, a_ref):
        cg = _causal_conv(hg_ref[...], wg_ref[...], bg_ref[...])
        cu = _causal_conv(hu_ref[...], wu_ref[...], bu_ref[...])
        a_ref[...] = (cg * _sigmoid(cg) * cu).astype(a_ref.dtype)

    res = _call_with_side(
        body, name, (nseq, nf),
        [pl.BlockSpec((S, tc), lambda s, j: (s, j)), pl.BlockSpec((S, tc), lambda s, j: (s, nf + j)),
         pl.BlockSpec((3, tc), lambda s, j: (0, j)), pl.BlockSpec((3, tc), lambda s, j: (0, nf + j)),
         pl.BlockSpec((1, tc), lambda s, j: (0, j)), pl.BlockSpec((1, tc), lambda s, j: (0, nf + j))],
        [pl.BlockSpec((S, tc), lambda s, j: (s, j))], [jax.ShapeDtypeStruct((T, F), _MXU)], [],
        ("parallel", "parallel"), (h, h, cw, cw, cb, cb), side)
    return res if side else res[0]


def _conv_gate_bwd(h, da, cw, cb, nseq, name, tc=128, side=None):
    T, F2 = h.shape
    F, S = F2 // 2, T // nseq
    tc = _div_tile(F, tc, LANE)
    nf = F // tc

    def conv_bwd(dc, hx, w):
        dh = (w[2:3] * dc + w[1:2] * _shift_up(dc, 1)) + w[0:1] * _shift_up(dc, 2)
        dw = jnp.concatenate([_colsum(dc * _shift_down(hx, 2)), _colsum(dc * _shift_down(hx, 1)), _colsum(dc * hx)], axis=0)
        return dh, dw, _colsum(dc)

    def body(hg_ref, hu_ref, da_ref, wg_ref, wu_ref, bg_ref, bu_ref, dhg_ref, dhu_ref, dwg_ref, dwu_ref, dbg_ref, dbu_ref):
        hg, hu, da = hg_ref[...], hu_ref[...], da_ref[...]
        wg, wu = wg_ref[...], wu_ref[...]
        cg = _causal_conv(hg, wg, bg_ref[...])
        cu = _causal_conv(hu, wu, bu_ref[...])
        sg = _sigmoid(cg)
        dcu = da * (cg * sg)
        dcg = da * cu * (sg * (1.0 + cg * (1.0 - sg)))
        dhg, dwg, dbg = conv_bwd(dcg, hg, wg)
        dhu, dwu, dbu = conv_bwd(dcu, hu, wu)
        dhg_ref[...] = dhg.astype(dhg_ref.dtype)
        dhu_ref[...] = dhu.astype(dhu_ref.dtype)
        s = pl.program_id(1)
        _accumulate(dwg_ref, dwg, s)
        _accumulate(dwu_ref, dwu, s)
        _accumulate(dbg_ref, dbg, s)
        _accumulate(dbu_ref, dbu, s)

    col = lambda j, s: (s, j)
    par = lambda j, s: (0, j)
    dhg, dhu, dwg, dwu, dbg, dbu, *side_res = _call_with_side(
        body, name, (nf, nseq),
        [pl.BlockSpec((S, tc), col), pl.BlockSpec((S, tc), lambda j, s: (s, nf + j)), pl.BlockSpec((S, tc), col),
         pl.BlockSpec((3, tc), par), pl.BlockSpec((3, tc), lambda j, s: (0, nf + j)),
         pl.BlockSpec((1, tc), par), pl.BlockSpec((1, tc), lambda j, s: (0, nf + j))],
        [pl.BlockSpec((S, tc), col), pl.BlockSpec((S, tc), col), pl.BlockSpec((3, tc), par), pl.BlockSpec((3, tc), par),
         pl.BlockSpec((1, tc), par), pl.BlockSpec((1, tc), par)],
        [jax.ShapeDtypeStruct((T, F), _MXU), jax.ShapeDtypeStruct((T, F), _MXU), jax.ShapeDtypeStruct((3, F), f32),
         jax.ShapeDtypeStruct((3, F), f32), jax.ShapeDtypeStruct((1, F), f32), jax.ShapeDtypeStruct((1, F), f32)],
        [], ("parallel", "arbitrary"), (h, h, da, cw, cw, cb, cb), side)
    return dhg, dhu, jnp.concatenate([dwg, dwu], axis=1), jnp.concatenate([dbg, dbu], axis=1), side_res


def _group_row(x, jj):
    C, d = x.shape
    n = C // GLA_SUB
    x3 = x.reshape(n, GLA_SUB, d)
    return jnp.broadcast_to(x3[:, jj:jj + 1, :], (n, GLA_SUB, d)).reshape(C, d)


def _group_sum(x):
    C, d = x.shape
    n = C // GLA_SUB
    s = jnp.sum(x.reshape(n, GLA_SUB, d), axis=1, keepdims=True)
    return jnp.broadcast_to(s, (n, GLA_SUB, d)).reshape(C, d)


def _chunk_cumsum(g):
    C = g.shape[0]
    row = lax.broadcasted_iota(jnp.int32, (C, C), 0)
    col = lax.broadcasted_iota(jnp.int32, (C, C), 1)
    return _exact_dot((row >= col).astype(f32), g)


def _chunk_suffix_sum(x):
    C = x.shape[0]
    row = lax.broadcasted_iota(jnp.int32, (C, C), 0)
    col = lax.broadcasted_iota(jnp.int32, (C, C), 1)
    return _exact_dot((col >= row).astype(f32), x)


def _gla_scores(q, k, b):
    C = q.shape[0]
    n = C // GLA_SUB
    row = lax.broadcasted_iota(jnp.int32, (C, C), 0)
    col = lax.broadcasted_iota(jnp.int32, (C, C), 1)
    blocks = [jnp.zeros((GLA_SUB, C), f32)]
    for s in range(1, n):
        lo = s * GLA_SUB
        bref = b[lo - 1:lo, :]
        qr = q[lo:lo + GLA_SUB] * jnp.exp(b[lo:lo + GLA_SUB] - bref)
        kr = k * jnp.exp(jnp.minimum(bref - b, 0.0))
        blocks.append(_nt(qr, kr))
    sub_start = (row // GLA_SUB) * GLA_SUB
    a = jnp.where(col < sub_start, jnp.concatenate(blocks, axis=0), 0.0)
    rin = lax.broadcasted_iota(jnp.int32, (C, 1), 0) % GLA_SUB
    for jj in range(GLA_SUB):
        e = jnp.exp(jnp.minimum(b - _group_row(b, jj), 0.0))
        colv = jnp.sum(q * _group_row(k, jj) * e, axis=1, keepdims=True)
        colv = jnp.where(rin >= jj, colv, 0.0)
        a = jnp.where(col == sub_start + jj, colv, a)
    return a


def _gla_scores_bwd(da, q, k, b):
    C = q.shape[0]
    n = C // GLA_SUB
    row = lax.broadcasted_iota(jnp.int32, (C, C), 0)
    col = lax.broadcasted_iota(jnp.int32, (C, C), 1)
    sub_start = (row // GLA_SUB) * GLA_SUB
    da_inter = jnp.where(col < sub_start, da, 0.0)
    dq_blocks = [jnp.zeros((GLA_SUB, q.shape[1]), f32)]
    dk = jnp.zeros_like(k)
    for s in range(1, n):
        lo = s * GLA_SUB
        bref = b[lo - 1:lo, :]
        eq = jnp.exp(b[lo:lo + GLA_SUB] - bref)
        ek = jnp.exp(jnp.minimum(bref - b, 0.0))
        das = da_inter[lo:lo + GLA_SUB]
        dq_blocks.append(_nn(das, k * ek) * eq)
        dk = dk + _tn(das, q[lo:lo + GLA_SUB] * eq) * ek
    dq = jnp.concatenate(dq_blocks, axis=0)
    rin = lax.broadcasted_iota(jnp.int32, (C, 1), 0) % GLA_SUB
    for jj in range(GLA_SUB):
        e = jnp.exp(jnp.minimum(b - _group_row(b, jj), 0.0))
        dac = jnp.sum(jnp.where(col == sub_start + jj, da, 0.0), axis=1, keepdims=True)
        dac = jnp.where(rin >= jj, dac, 0.0)
        w = dac * e
        dq = dq + w * _group_row(k, jj)
        dk = dk + jnp.where(rin == jj, _group_sum(w * q), 0.0)
    return dq, dk


def _gla_specs(nC, dk, dv):
    H = GLA_HEADS
    voff = (2 * H * dk) // dv
    assert voff * dv == 2 * H * dk
    return H, voff


def _gla_fwd(proj, gate, nseq, name, side=None):
    T = proj.shape[0]
    dk = gate.shape[1] // GLA_HEADS
    dv = 2 * dk
    C = GLA_CHUNK
    nC = T // nseq // C
    H, voff = _gla_specs(nC, dk, dv)
    scale = dk ** -0.5

    def body(q_ref, k_ref, v_ref, g_ref, o_ref, st_ref, state):
        c = pl.program_id(2)

        @pl.when(c == 0)
        def _():
            state[...] = jnp.zeros_like(state)

        q, k, v = q_ref[...] * scale, k_ref[...], v_ref[...]
        b = _chunk_cumsum(g_ref[...])
        st = state[...]
        st_ref[0] = st
        a = _gla_scores(q, k, b)
        o_ref[...] = _nt(q * jnp.exp(b), st) + _nn(a, v)
        bl = b[C - 1:C, :]
        state[...] = st * jnp.exp(bl) + _tn(v, k * jnp.exp(bl - b))

    row = lambda s, h, c: s * nC + c
    return _call_with_side(
        body, name, (nseq, H, nC),
        [pl.BlockSpec((C, dk), lambda s, h, c: (row(s, h, c), h)),
         pl.BlockSpec((C, dk), lambda s, h, c: (row(s, h, c), H + h)),
         pl.BlockSpec((C, dv), lambda s, h, c: (row(s, h, c), voff + h)),
         pl.BlockSpec((C, dk), lambda s, h, c: (row(s, h, c), h))],
        [pl.BlockSpec((C, dv), lambda s, h, c: (row(s, h, c), h)),
         pl.BlockSpec((1, dv, dk), lambda s, h, c: ((s * H + h) * nC + c, 0, 0))],
        [jax.ShapeDtypeStruct((T, H * dv), f32), jax.ShapeDtypeStruct((nseq * H * nC, dv, dk), f32)],
        [pltpu.VMEM((dv, dk), f32)], ("parallel", "parallel", "arbitrary"), (proj, proj, proj, gate), side)


def _gla_bwd(proj, gate, states, do, nseq, name):
    T = proj.shape[0]
    dk = gate.shape[1] // GLA_HEADS
    dv = 2 * dk
    C = GLA_CHUNK
    nC = T // nseq // C
    H, voff = _gla_specs(nC, dk, dv)
    scale = dk ** -0.5

    def body(q_ref, k_ref, v_ref, g_ref, do_ref, st_ref, dq_ref, dk_ref, dv_ref, dg_ref, dstate, term):
        c = pl.program_id(2)

        @pl.when(c == 0)
        def _():
            dstate[...] = jnp.zeros_like(dstate)
            term[...] = jnp.zeros_like(term)

        q, k, v, do = q_ref[...] * scale, k_ref[...], v_ref[...], do_ref[...]
        b = _chunk_cumsum(g_ref[...])
        st = st_ref[0]
        dst = dstate[...]
        eb = jnp.exp(b)
        bl = b[C - 1:C, :]
        kdec = jnp.exp(bl - b)
        a = _gla_scores(q, k, b)
        rowi = lax.broadcasted_iota(jnp.int32, (C, C), 0)
        coli = lax.broadcasted_iota(jnp.int32, (C, C), 1)
        da = jnp.where(coli <= rowi, _nt(do, v), 0.0)
        dq_s, dk_s = _gla_scores_bwd(da, q, k, b)
        dq = _nn(do, st) * eb + dq_s
        dkk = _nn(v, dst) * kdec + dk_s
        dv_ref[...] = (_tn(a, do) + _nt(k * kdec, dst)).astype(dv_ref.dtype)
        last = lax.broadcasted_iota(jnp.int32, (C, 1), 0) == C - 1
        db = q * dq - k * dkk + jnp.where(last, term[...], 0.0)
        dg_ref[...] = _chunk_suffix_sum(db)
        dq_ref[...] = (dq * scale).astype(dq_ref.dtype)
        dk_ref[...] = dkk.astype(dk_ref.dtype)
        dprev = dst * jnp.exp(bl) + _tn(do, q * eb)
        dstate[...] = dprev
        term[...] = _colsum(st * dprev)

    row = lambda s, h, c: s * nC + (nC - 1 - c)
    kspec = lambda off: pl.BlockSpec((C, dk), lambda s, h, c: (row(s, h, c), off + h))
    vspec = lambda off: pl.BlockSpec((C, dv), lambda s, h, c: (row(s, h, c), off + h))
    return pl.pallas_call(
        body, name=name, grid=(nseq, H, nC),
        in_specs=[kspec(0), kspec(H), vspec(voff), kspec(0), vspec(0),
                  pl.BlockSpec((1, dv, dk), lambda s, h, c: ((s * H + h) * nC + (nC - 1 - c), 0, 0))],
        out_specs=[kspec(0), kspec(0), vspec(0), kspec(0)],
        out_shape=[jax.ShapeDtypeStruct((T, H * dk), _MXU), jax.ShapeDtypeStruct((T, H * dk), _MXU),
                   jax.ShapeDtypeStruct((T, H * dv), _MXU), jax.ShapeDtypeStruct((T, H * dk), f32)],
        scratch_shapes=[pltpu.VMEM((dv, dk), f32), pltpu.VMEM((1, dk), f32)],
        compiler_params=_cparams(("parallel", "parallel", "arbitrary")),
    )(proj, proj, proj, gate, do, states)


def _head_slices(width, n):
    w = width // n
    return [slice(h * w, (h + 1) * w) for h in range(n)]


def _rms_gate_fwd(o, proj, ng, name):
    W = o.shape[1]

    def fn(o, r, ng):
        parts = []
        for sl in _head_slices(W, GLA_HEADS):
            oh = o[:, sl]
            rstd = lax.rsqrt(jnp.mean(oh * oh, axis=-1, keepdims=True) + RMS_EPS)
            rh = r[:, sl]
            parts.append((oh * rstd * ng) * (rh * _sigmoid(rh)))
        return jnp.concatenate(parts, axis=1)

    return _rowwise(fn, [_full(o), (proj, W, 2)], [ng], [(W, _MXU)], [], name)[0]


def _rms_gate_bwd(o, proj, ng, dy, name):
    W = o.shape[1]

    def fn(o, r, dy, ng):
        dos, drs = [], []
        dng = jnp.zeros((1, W // GLA_HEADS), f32)
        for sl in _head_slices(W, GLA_HEADS):
            oh, rh, dyh = o[:, sl], r[:, sl], dy[:, sl]
            rstd = lax.rsqrt(jnp.mean(oh * oh, axis=-1, keepdims=True) + RMS_EPS)
            ohat = oh * rstd
            sg = _sigmoid(rh)
            don = dyh * (rh * sg)
            drs.append(dyh * (ohat * ng) * (sg * (1.0 + rh * (1.0 - sg))))
            dng = dng + _colsum(don * ohat)
            dohat = don * ng
            dos.append(rstd * (dohat - ohat * jnp.mean(dohat * ohat, axis=-1, keepdims=True)))
        return jnp.concatenate(dos, axis=1), jnp.concatenate(drs, axis=1), dng

    return _rowwise(fn, [_full(o), (proj, W, 2), _full(dy)], [ng], [(W, _MXU), (W, _MXU)], [W // GLA_HEADS], name)


def _log_gate_fwd(z, bias, name):
    def fn(z, bias):
        t = z + bias
        return (jnp.minimum(t, 0.0) - jnp.log1p(jnp.exp(-jnp.abs(t)))) * (1.0 / GLA_GATE_NORMALIZER)

    return _rowwise(fn, [_full(z)], [bias], [(z.shape[1], f32)], [], name)[0]


def _log_gate_bwd(z, bias, dg, name):
    def fn(z, dg, bias):
        dz = dg * (1.0 / GLA_GATE_NORMALIZER) * _sigmoid(-(z + bias))
        return dz, _colsum(dz)

    return _rowwise(fn, [_full(z), _full(dg)], [bias], [(z.shape[1], f32)], [z.shape[1]], name)


def _band_masks(P, steps, has_prev):
    i = lax.broadcasted_iota(jnp.int32, (P, P), 0)
    j = lax.broadcasted_iota(jnp.int32, (P, P), 1)
    cur = (i - j >= 0) & (i - j <= steps)
    prev = (i + P - j <= steps) & has_prev
    return cur, prev


def _dil_dims(T, nseq, dilation):
    L = T // nseq // dilation
    P = min(DIL_BLOCK, L)
    return L, P, L // P


def _dil_tiling(T, nseq, dilation):
    L, P, nb = _dil_dims(T, nseq, dilation)
    hb = DIL_HEADS if dilation == 1 else 1
    row_sets = [pl.ds(r, P, stride=dilation) if dilation > 1 else pl.ds(0, P) for r in range(dilation)]
    head_cols = _head_slices(hb * DIL_HEAD_DIM, hb)
    return L, P, nb, P * dilation, hb, DIL_HEADS // hb, row_sets, head_cols


def _dil_fwd(proj, gi, window, dilation, nseq, name):
    T = proj.shape[0]
    H, dh = DIL_HEADS, DIL_HEAD_DIM
    L, P, nb, SB, hb, ng, row_sets, head_cols = _dil_tiling(T, nseq, dilation)
    steps = window // dilation
    scale = dh ** -0.5

    def body(q_ref, kc_ref, kp_ref, vc_ref, vp_ref, o_ref, lse_ref):
        sb, hg = pl.program_id(1), pl.program_id(2)
        mc, mp = _band_masks(P, steps, sb > 0)
        lane = lax.broadcasted_iota(jnp.int32, (P, LANE), 1)

        @pl.when(hg == 0)
        def _():
            lse_ref[...] = jnp.zeros_like(lse_ref)

        for hh, cols in enumerate(head_cols):
            for rows in row_sets:
                q = q_ref[rows, cols]
                sc = jnp.where(mc, _nt(q, kc_ref[rows, cols]) * scale, -jnp.inf)
                sp = jnp.where(mp, _nt(q, kp_ref[rows, cols]) * scale, -jnp.inf)
                m = jnp.maximum(jnp.max(sc, axis=-1, keepdims=True), jnp.max(sp, axis=-1, keepdims=True))
                pc, pp = jnp.exp(sc - m), jnp.exp(sp - m)
                l = jnp.sum(pc, axis=-1, keepdims=True) + jnp.sum(pp, axis=-1, keepdims=True)
                o_ref[rows, cols] = _nn(pc / l, vc_ref[rows, cols]) + _nn(pp / l, vp_ref[rows, cols])
                lse_ref[rows, :] = jnp.where(lane == hg * hb + hh, m + jnp.log(l), lse_ref[rows, :])

    cur = lambda part: pl.BlockSpec((SB, hb * dh), lambda s, sb, hg: (s * nb + sb, (gi * 3 + part) * ng + hg))
    prv = lambda part: pl.BlockSpec((SB, hb * dh), lambda s, sb, hg: (s * nb + jnp.maximum(sb - 1, 0), (gi * 3 + part) * ng + hg))
    return pl.pallas_call(
        body, name=name, grid=(nseq, nb, ng),
        in_specs=[cur(0), cur(1), prv(1), cur(2), prv(2)],
        out_specs=[pl.BlockSpec((SB, hb * dh), lambda s, sb, hg: (s * nb + sb, hg)), pl.BlockSpec((SB, LANE), lambda s, sb, hg: (s * nb + sb, 0))],
        out_shape=[jax.ShapeDtypeStruct((T, H * dh), f32), jax.ShapeDtypeStruct((T, LANE), f32)],
        compiler_params=_cparams(("parallel", "parallel", "arbitrary")),
    )(proj, proj, proj, proj, proj)


def _dil_mix_fwd(os_, lses, name):
    W = os_[0].shape[1]
    G = len(os_)

    def fn(*a):
        o, l = a[:G], a[G:]
        lane = lax.broadcasted_iota(jnp.int32, l[0].shape, 1)
        tot = jnp.zeros(l[0].shape, f32)
        parts = []
        for h, sl in enumerate(_head_slices(W, DIL_HEADS)):
            lh = [x[:, h:h + 1] for x in l]
            m = functools.reduce(jnp.maximum, lh)
            e = [jnp.exp(x - m) for x in lh]
            z = functools.reduce(lambda u, v: u + v, e)
            acc = None
            for g in range(G):
                t = (e[g] / z) * o[g][:, sl]
                acc = t if acc is None else acc + t
            parts.append(acc)
            tot = jnp.where(lane == h, m + jnp.log(z), tot)
        y = jnp.concatenate(parts, axis=1)
        return y, y, tot

    return _rowwise(fn, [_full(x) for x in os_] + [_full(x) for x in lses], [], [(W, f32), (W, _MXU), (LANE, f32)], [], name)


def _dil_delta(do, o, name):
    W = o.shape[1]

    def fn(do, o):
        lane = lax.broadcasted_iota(jnp.int32, (do.shape[0], LANE), 1)
        d = jnp.zeros((do.shape[0], LANE), f32)
        for h, sl in enumerate(_head_slices(W, DIL_HEADS)):
            d = jnp.where(lane == h, jnp.sum(do[:, sl] * o[:, sl], axis=-1, keepdims=True), d)
        return d

    return _rowwise(fn, [_full(do), _full(o)], [], [(LANE, f32)], [], name)[0]


def _dil_bwd(proj, do, lse, delta, gi, window, dilation, nseq, name):
    T = proj.shape[0]
    H, dh = DIL_HEADS, DIL_HEAD_DIM
    L, P, nb, SB, hb, ng, row_sets, head_cols = _dil_tiling(T, nseq, dilation)
    steps = window // dilation
    scale = dh ** -0.5

    def probs(q, k, lse_h, mask):
        return jnp.where(mask, jnp.exp(_nt(q, k) * scale - lse_h), 0.0)

    def head_lane(ref, rows, h):
        lane = lax.broadcasted_iota(jnp.int32, (P, LANE), 1)
        return jnp.sum(jnp.where(lane == h, ref[rows, :], 0.0), axis=1, keepdims=True)

    def dq_body(q_ref, kc_ref, kp_ref, vc_ref, vp_ref, do_ref, lse_ref, del_ref, dq_ref):
        sb, hg = pl.program_id(1), pl.program_id(2)
        mc, mp = _band_masks(P, steps, sb > 0)
        for hh, cols in enumerate(head_cols):
            h = hg * hb + hh
            for rows in row_sets:
                q, doh = q_ref[rows, cols], do_ref[rows, cols]
                lse_h, del_h = head_lane(lse_ref, rows, h), head_lane(del_ref, rows, h)
                kc, kp = kc_ref[rows, cols], kp_ref[rows, cols]
                dsc = probs(q, kc, lse_h, mc) * (_nt(doh, vc_ref[rows, cols]) - del_h) * scale
                dsp = probs(q, kp, lse_h, mp) * (_nt(doh, vp_ref[rows, cols]) - del_h) * scale
                dq_ref[rows, cols] = _nn(dsc, kc) + _nn(dsp, kp)

    cur = lambda part: pl.BlockSpec((SB, hb * dh), lambda s, sb, hg: (s * nb + sb, (gi * 3 + part) * ng + hg))
    prv = lambda part: pl.BlockSpec((SB, hb * dh), lambda s, sb, hg: (s * nb + jnp.maximum(sb - 1, 0), (gi * 3 + part) * ng + hg))
    tok = pl.BlockSpec((SB, hb * dh), lambda s, sb, hg: (s * nb + sb, hg))
    aux = pl.BlockSpec((SB, LANE), lambda s, sb, hg: (s * nb + sb, 0))
    dq = pl.pallas_call(
        dq_body, name=name + "_dq", grid=(nseq, nb, ng),
        in_specs=[cur(0), cur(1), prv(1), cur(2), prv(2), tok, aux, aux],
        out_specs=tok, out_shape=jax.ShapeDtypeStruct((T, H * dh), f32),
        compiler_params=_cparams(("parallel", "parallel", "parallel")),
    )(proj, proj, proj, proj, proj, do, lse, delta)

    def dkv_body(k_ref, v_ref, qc_ref, qn_ref, doc_ref, don_ref, lsec_ref, lsen_ref, delc_ref, deln_ref, dk_ref, dv_ref):
        sb, hg = pl.program_id(1), pl.program_id(2)
        mc, mn = _band_masks(P, steps, sb < nb - 1)
        for hh, cols in enumerate(head_cols):
            h = hg * hb + hh
            for rows in row_sets:
                k, v = k_ref[rows, cols], v_ref[rows, cols]
                qc, doc = qc_ref[rows, cols], doc_ref[rows, cols]
                pc = probs(qc, k, head_lane(lsec_ref, rows, h), mc)
                dsc = pc * (_nt(doc, v) - head_lane(delc_ref, rows, h)) * scale
                qn, don = qn_ref[rows, cols], don_ref[rows, cols]
                pn = probs(qn, k, head_lane(lsen_ref, rows, h), mn)
                dsn = pn * (_nt(don, v) - head_lane(deln_ref, rows, h)) * scale
                dk_ref[rows, cols] = _tn(dsc, qc) + _tn(dsn, qn)
                dv_ref[rows, cols] = _tn(pc, doc) + _tn(pn, don)

    nxt = lambda s, sb: s * nb + jnp.minimum(sb + 1, nb - 1)
    qnx = pl.BlockSpec((SB, hb * dh), lambda s, sb, hg: (nxt(s, sb), gi * 3 * ng + hg))
    tokn = pl.BlockSpec((SB, hb * dh), lambda s, sb, hg: (nxt(s, sb), hg))
    auxn = pl.BlockSpec((SB, LANE), lambda s, sb, hg: (nxt(s, sb), 0))
    dkk, dvv = pl.pallas_call(
        dkv_body, name=name + "_dkv", grid=(nseq, nb, ng),
        in_specs=[cur(1), cur(2), cur(0), qnx, tok, tokn, aux, auxn, aux, auxn],
        out_specs=[tok, tok],
        out_shape=[jax.ShapeDtypeStruct((T, H * dh), f32), jax.ShapeDtypeStruct((T, H * dh), f32)],
        compiler_params=_cparams(("parallel", "parallel", "parallel")),
    )(proj, proj, proj, proj, do, do, lse, lse, delta, delta)
    return dq, dkk, dvv


def _adamw_math(w, g, m, v):
    m = ADAM_B1 * m + (1.0 - ADAM_B1) * g
    v = ADAM_B2 * v + (1.0 - ADAM_B2) * (g * g)
    m_hat = m / (1.0 - ADAM_B1 ** ADAM_STEP)
    v_hat = v / (1.0 - ADAM_B2 ** ADAM_STEP)
    return -ADAM_LR * (m_hat / (jnp.sqrt(v_hat) + ADAM_EPS) + ADAM_WD * w), m, v


def _adamw(w, g, m, v, name):
    W = w.shape[1]
    return _rowwise(_adamw_math, [_full(w), _full(g), _full(m), _full(v)], [], [(W, f32)] * 3, [], name, tm=512)


def _position():
    x, y, c = lax.axis_index("x"), lax.axis_index("y"), lax.axis_index("c")
    other_chips = [(1 - x, y), (x, 1 - y), (1 - x, 1 - y)]
    return x, y, c, other_chips


def _chip_index(x, y):
    return 2 * x + y


_ANY = pl.BlockSpec(memory_space=pl.ANY)


def _all_reduce_small(p, name):
    R, Wd = p.shape

    def body(p_ref, o_ref, buf, send_sems, recv_sems):
        x, y, c, _ = _position()
        me = 4 * x + 2 * y + c
        buf[me] = p_ref[...]
        copies = []
        for k in range(1, 8):
            fx, fy, fc = (k >> 2) & 1, (k >> 1) & 1, k & 1
            peer = (x + fx - 2 * x * fx, y + fy - 2 * y * fy, c + fc - 2 * c * fc)
            cp = pltpu.make_async_remote_copy(src_ref=p_ref, dst_ref=buf.at[me], send_sem=send_sems.at[k - 1],
                                              recv_sem=recv_sems.at[k - 1], device_id=peer, device_id_type=MESH)
            cp.start()
            copies.append(cp)
        for cp in copies:
            cp.wait()
        acc = buf[0]
        for s in range(1, 8):
            acc = acc + buf[s]
        o_ref[...] = acc

    return pl.pallas_call(
        body, name=name, out_shape=jax.ShapeDtypeStruct((R, Wd), f32),
        in_specs=[pl.BlockSpec(memory_space=pltpu.VMEM)], out_specs=pl.BlockSpec(memory_space=pltpu.VMEM),
        scratch_shapes=[pltpu.VMEM((8, R, Wd), f32), pltpu.SemaphoreType.DMA((7,)), pltpu.SemaphoreType.DMA((7,))],
        compiler_params=pltpu.CompilerParams(vmem_limit_bytes=VMEM_LIMIT),
    )(p)


def _layer_half(ref, h, axis):
    n = ref.shape[axis] // 2
    idx = (slice(None),) * axis + (pl.ds(h * n, n),)
    return ref.at[idx]


def _comm_call(body, name, ins, out_shapes, n_sems, n_local=0):
    scratch = [pltpu.SemaphoreType.DMA((n_sems,)), pltpu.SemaphoreType.DMA((n_sems,))]
    if n_local:
        scratch.append(pltpu.SemaphoreType.DMA((n_local,)))
    return pl.pallas_call(body, name=name, out_shape=out_shapes, in_specs=[_ANY] * len(ins), out_specs=[_ANY] * len(out_shapes),
                          scratch_shapes=scratch)(*ins)


class _SideCopies:
    def __init__(self, ins, outs, scratch, start, finish):
        self.ins, self.outs, self.scratch, self.start, self.finish = ins, outs, scratch, start, finish


def _row_half(ref, h):
    r = ref.shape[0] // 2
    start = h * r
    if r % 16 == 0:
        start = pl.multiple_of(start, 16)
    return ref.at[pl.ds(start, r)]


def _gather_plan(ws):
    n = len(ws)

    def copy(o_refs, sems, k, src, i, chip_idx, h, to):
        return pltpu.make_async_remote_copy(src_ref=src, dst_ref=_row_half(o_refs[i].at[chip_idx], h), send_sem=sems[0].at[k],
                                            recv_sem=sems[1].at[k], device_id=to, device_id_type=MESH)

    def own_copy(w_refs, o_refs, sems, i, p):
        return pltpu.make_async_copy(w_refs[i], o_refs[i].at[p], sems[2].at[i])

    def over_ici(w_refs, o_refs, sems, i, j, chip, dst_chip_idx, c):
        return copy(o_refs, sems, 3 * i + j, _row_half(w_refs[i], c), i, dst_chip_idx, c, (*chip, c))

    def start(w_refs, o_refs, *sems):
        x, y, c, chips = _position()
        p = _chip_index(x, y)
        for i in range(n):
            own_copy(w_refs, o_refs, sems, i, p).start()
            for j, chip in enumerate(chips):
                over_ici(w_refs, o_refs, sems, i, j, chip, p, c).start()

    def finish(w_refs, o_refs, *sems):
        x, y, c, chips = _position()
        p = _chip_index(x, y)
        sibling = (x, y, 1 - c)
        passed = []
        for i in range(n):
            for j, chip in enumerate(chips):
                q = _chip_index(*chip)
                over_ici(w_refs, o_refs, sems, i, j, chip, q, c).wait_recv()
                fwd = copy(o_refs, sems, 3 * n + 3 * i + j, _row_half(o_refs[i].at[q], c), i, q, c, sibling)
                fwd.start()
                passed.append(fwd)
        for i in range(n):
            for j, chip in enumerate(chips):
                copy(o_refs, sems, 3 * n + 3 * i + j, _row_half(w_refs[i], c), i, _chip_index(*chip), 1 - c, sibling).wait_recv()
        for i in range(n):
            for j, chip in enumerate(chips):
                over_ici(w_refs, o_refs, sems, i, j, chip, p, c).wait_send()
            own_copy(w_refs, o_refs, sems, i, p).wait()
        for fwd in passed:
            fwd.wait_send()

    scratch = [pltpu.SemaphoreType.DMA((6 * n,)), pltpu.SemaphoreType.DMA((6 * n,)), pltpu.SemaphoreType.DMA((n,))]
    return _SideCopies(list(ws), [jax.ShapeDtypeStruct((4,) + w.shape, w.dtype) for w in ws], scratch, start, finish)


def _run_copies(plan, name):
    n_i, n_o = len(plan.ins), len(plan.outs)

    def body(*refs):
        plan.start(refs[:n_i], refs[n_i:n_i + n_o], *refs[n_i + n_o:])
        plan.finish(refs[:n_i], refs[n_i:n_i + n_o], *refs[n_i + n_o:])

    return pl.pallas_call(body, name=name, out_shape=plan.outs, in_specs=[_ANY] * n_i, out_specs=[_ANY] * n_o,
                          scratch_shapes=plan.scratch)(*plan.ins)


def _sibling_halves(gs, name):
    n = len(gs)

    def body(*refs):
        g_refs, o_refs, send_sems, recv_sems = refs[:n], refs[n:2 * n], refs[2 * n], refs[2 * n + 1]
        x, y, c, _ = _position()
        copies = []
        for i in range(n):
            cp = pltpu.make_async_remote_copy(src_ref=_layer_half(g_refs[i], 1 - c, 1), dst_ref=o_refs[i], send_sem=send_sems.at[i],
                                              recv_sem=recv_sems.at[i], device_id=(x, y, 1 - c), device_id_type=MESH)
            cp.start()
            copies.append(cp)
        for cp in copies:
            cp.wait()

    outs = [jax.ShapeDtypeStruct((4, g.shape[1] // 2) + g.shape[2:], g.dtype) for g in gs]
    return _comm_call(body, name, gs, outs, n)


def _exchange_plan(hs):
    n = len(hs)

    def copies(h_refs, o_refs, send_sems, recv_sems):
        x, y, c, chips = _position()
        return [pltpu.make_async_remote_copy(src_ref=h_refs[i].at[_chip_index(*chip)], dst_ref=o_refs[i].at[j],
                                             send_sem=send_sems.at[3 * i + j], recv_sem=recv_sems.at[3 * i + j],
                                             device_id=(*chip, c), device_id_type=MESH)
                for i in range(n) for j, chip in enumerate(chips)]

    def start(h_refs, o_refs, *sems):
        for cp in copies(h_refs, o_refs, *sems):
            cp.start()

    def finish(h_refs, o_refs, *sems):
        for cp in copies(h_refs, o_refs, *sems):
            cp.wait()

    scratch = [pltpu.SemaphoreType.DMA((3 * n,)), pltpu.SemaphoreType.DMA((3 * n,))]
    return _SideCopies(list(hs), [jax.ShapeDtypeStruct((3,) + h.shape[1:], h.dtype) for h in hs], scratch, start, finish)


def _sibling_swap(ts, name):
    n = len(ts)

    def body(*refs):
        t_refs, o_refs, send_sems, recv_sems = refs[:n], refs[n:2 * n], refs[2 * n], refs[2 * n + 1]
        x, y, c, _ = _position()
        copies = []
        for i in range(n):
            cp = pltpu.make_async_remote_copy(src_ref=t_refs[i], dst_ref=o_refs[i], send_sem=send_sems.at[i], recv_sem=recv_sems.at[i],
                                              device_id=(x, y, 1 - c), device_id_type=MESH)
            cp.start()
            copies.append(cp)
        for cp in copies:
            cp.wait()

    return _comm_call(body, name, ts, [jax.ShapeDtypeStruct(t.shape, t.dtype) for t in ts], n)


def _row_tile(K, N):
    return _div_tile(K, 256 if N <= 1024 else 128, 16)


def _prefetch_call(body, name, scalars, grid, in_specs, out_specs, out_shape, args, sem):
    gs = pltpu.PrefetchScalarGridSpec(num_scalar_prefetch=1, grid=grid, in_specs=in_specs, out_specs=out_specs)
    return pl.pallas_call(body, name=name, grid_spec=gs, out_shape=out_shape, compiler_params=_cparams(sem))(scalars, *args)


def _add_own_half(g, got, c, name):
    _, nl, K, N = g.shape
    hl = nl // 2
    tm = _row_tile(K, N)

    def body(c_ref, g_ref, r_ref, o_ref):
        o_ref[...] = (g_ref[...].astype(f32) + r_ref[...].astype(f32)).astype(o_ref.dtype)

    blk = (1, 1, tm, N)
    return _prefetch_call(
        body, name, jnp.reshape(c, (1,)).astype(jnp.int32), (4, hl, K // tm),
        [pl.BlockSpec(blk, lambda s, l, i, c_ref: (s, c_ref[0] * hl + l, i, 0)), pl.BlockSpec(blk, lambda s, l, i, c_ref: (s, l, i, 0))],
        pl.BlockSpec(blk, lambda s, l, i, c_ref: (s, l, i, 0)), jax.ShapeDtypeStruct((4, hl, K, N), g.dtype), (g, got),
        ("parallel", "parallel", "parallel"))


def _add_chips(h, got, p, name):
    _, nl, K, N = h.shape
    tm = _row_tile(K, N)

    def body(p_ref, h_ref, r0_ref, r1_ref, r2_ref, o_ref):
        o_ref[...] = ((h_ref[0].astype(f32) + r0_ref[0].astype(f32)) + r1_ref[0].astype(f32)) + r2_ref[0].astype(f32)

    blk = (1, 1, tm, N)
    got_spec = lambda j: pl.BlockSpec(blk, lambda l, i, p_ref: (j, l, i, 0))
    return _prefetch_call(
        body, name, jnp.reshape(p, (1,)).astype(jnp.int32), (nl, K // tm),
        [pl.BlockSpec(blk, lambda l, i, p_ref: (p_ref[0], l, i, 0)), got_spec(0), got_spec(1), got_spec(2)],
        pl.BlockSpec((1, tm, N), lambda l, i, p_ref: (l, i, 0)), jax.ShapeDtypeStruct((nl, K, N), f32), (h, got, got, got),
        ("parallel", "parallel"))


def _adamw_halves(w, m, v, own, other, c, name):
    nl, K, N = w.shape
    tm = _row_tile(K // 2, N)
    nb = K // 2 // tm

    def body(c_ref, w_ref, m_ref, v_ref, own_ref, other_ref, g_out, d_out, m_out, v_out):
        g = jnp.where(pl.program_id(1) == c_ref[0], own_ref[...], other_ref[...])
        d, m_new, v_new = _adamw_math(w_ref[...], g, m_ref[...], v_ref[...])
        g_out[...] = g
        d_out[...] = d
        m_out[...] = m_new
        v_out[...] = v_new

    blk = (1, tm, N)
    full = pl.BlockSpec(blk, lambda l, h, i, c_ref: (l, h * nb + i, 0))
    half = pl.BlockSpec(blk, lambda l, h, i, c_ref: (l, i, 0))
    return _prefetch_call(
        body, name, jnp.reshape(c, (1,)).astype(jnp.int32), (nl, 2, nb), [full, full, full, half, half], [full] * 4,
        [jax.ShapeDtypeStruct((nl, K, N), f32)] * 4, (w, m, v, own, other), ("parallel", "parallel", "parallel"))


def _row_halves_view(g):
    return g.reshape(4, 2, g.shape[1] // 2, g.shape[2])


def _reduce_scatter_start(gs, name):
    c = lax.axis_index("c")
    views = [_row_halves_view(g) for g in gs]
    from_sibling = _sibling_halves(views, name + "_d2d")
    return [_add_own_half(g, r, c, f"{name}_add2_{i}") for i, (g, r) in enumerate(zip(views, from_sibling))]


def _reduce_scatter_finish(hs, got, name):
    p = _chip_index(lax.axis_index("x"), lax.axis_index("y"))
    return [_add_chips(h, r, p, f"{name}_add4_{i}")[0] for i, (h, r) in enumerate(zip(hs, got))]


_BIG = (("gla_w_in", 1), ("gla_w_out", 0), ("dil_w_in", 1), ("dil_w_out", 1), ("ffn_w_up", 1), ("ffn_w_down", 0))


def _pad_rows(a, mult):
    r = (-a.shape[0]) % mult
    return a if r == 0 else jnp.concatenate([a, jnp.zeros((r,) + a.shape[1:], a.dtype)], axis=0)


def _unshard(blocks, axis):
    _, K, N = blocks.shape
    if axis == 1:
        return blocks.transpose(1, 0, 2).reshape(K, 4 * N)
    return blocks.reshape(4 * K, N)


def _to_shards(mat, axis):
    K, N = mat.shape
    if axis == 1:
        return mat.reshape(K, 4, N // 4).transpose(1, 0, 2)
    return mat.reshape(4, K // 4, N)


_SMALL = (("gla_w_gate_up", 2), ("gla_gate_bias", None), ("gla_norm_g", None), ("ffn_conv_w", 2), ("ffn_conv_b", None),
          ("ln_g", 2), ("ln_b", 2))


def _pack_rows(arrs, width=LANE):
    flat = _pad_rows(jnp.concatenate([a.reshape(-1) for a in arrs]), 8 * width)
    return flat.reshape(-1, width)


def _unpack_rows(packed, shapes):
    flat, out, off = packed.reshape(-1), [], 0
    for s in shapes:
        n = 1
        for d in s:
            n *= d
        out.append(flat[off:off + n].reshape(s))
        off += n
    return out


def _gather_small_params(shards):
    x, y, c = lax.axis_index("x"), lax.axis_index("y"), lax.axis_index("c")
    names = [n for n, axis in _SMALL if axis is not None]
    mine = _pack_rows([shards[n] for n in names])
    mine = jnp.where(c == 0, mine, jnp.zeros_like(mine))
    rows = mine.shape[0]
    placed = lax.dynamic_update_slice(jnp.zeros((4 * rows, LANE), f32), mine, (_chip_index(x, y) * rows, 0))
    allp = _all_reduce_small(placed, "gather_small").reshape(4, rows, LANE)
    out = {n: shards[n] for n, axis in _SMALL if axis is None}
    per_chip = [_unpack_rows(allp[q], [shards[n].shape for n in names]) for q in range(4)]
    for i, n in enumerate(names):
        out[n] = jnp.concatenate([per_chip[q][i] for q in range(4)], axis=2)
    return out


def _reduce_small_grads(grads, shards):
    names = [n for n, _ in _SMALL]
    total = _all_reduce_small(_pack_rows([grads[n] for n in names]), "reduce_small")
    full = dict(zip(names, _unpack_rows(total, [grads[n].shape for n in names])))
    p = _chip_index(lax.axis_index("x"), lax.axis_index("y"))
    out = {}
    for n, axis in _SMALL:
        if axis is None:
            out[n] = full[n]
        else:
            w = shards[n].shape[axis]
            out[n] = lax.dynamic_slice_in_dim(full[n], p * w, w, axis=axis)
    return out


def _pad_cols(a, n):
    return a if a.shape[-1] == n else jnp.concatenate([a, jnp.zeros(a.shape[:-1] + (n - a.shape[-1],), a.dtype)], axis=-1)


def _ffn_width(F):
    return -(-F // 512) * 512


def kernel(x, gla_w_in, gla_w_gate_up, gla_gate_bias, gla_norm_g, gla_w_out, dil_w_in, dil_w_out, ffn_w_up, ffn_conv_w, ffn_conv_b, ffn_w_down, ln_g, ln_b, loss_target, m_gla_w_in, m_gla_w_gate_up, m_gla_gate_bias, m_gla_norm_g, m_gla_w_out, m_dil_w_in, m_dil_w_out, m_ffn_w_up, m_ffn_conv_w, m_ffn_conv_b, m_ffn_w_down, m_ln_g, m_ln_b, v_gla_w_in, v_gla_w_gate_up, v_gla_gate_bias, v_gla_norm_g, v_gla_w_out, v_dil_w_in, v_dil_w_out, v_ffn_w_up, v_ffn_conv_w, v_ffn_conv_b, v_ffn_w_down, v_ln_g, v_ln_b):
    names = ["gla_w_in", "gla_w_gate_up", "gla_gate_bias", "gla_norm_g", "gla_w_out", "dil_w_in", "dil_w_out", "ffn_w_up",
             "ffn_conv_w", "ffn_conv_b", "ffn_w_down", "ln_g", "ln_b"]
    w_sh = dict(zip(names, (gla_w_in, gla_w_gate_up, gla_gate_bias, gla_norm_g, gla_w_out, dil_w_in, dil_w_out, ffn_w_up,
                            ffn_conv_w, ffn_conv_b, ffn_w_down, ln_g, ln_b)))
    m_sh = dict(zip(names, (m_gla_w_in, m_gla_w_gate_up, m_gla_gate_bias, m_gla_norm_g, m_gla_w_out, m_dil_w_in, m_dil_w_out,
                            m_ffn_w_up, m_ffn_conv_w, m_ffn_conv_b, m_ffn_w_down, m_ln_g, m_ln_b)))
    v_sh = dict(zip(names, (v_gla_w_in, v_gla_w_gate_up, v_gla_gate_bias, v_gla_norm_g, v_gla_w_out, v_dil_w_in, v_dil_w_out,
                            v_ffn_w_up, v_ffn_conv_w, v_ffn_conv_b, v_ffn_w_down, v_ln_g, v_ln_b)))
    nseq, S, D = x.shape
    T = nseq * S
    small = _gather_small_params(w_sh)
    F = 4 * w_sh["ffn_w_down"].shape[1]
    Fp = _ffn_width(F)
    qkvr = 4 * w_sh["gla_w_in"].shape[2] - GLA_GATE_RANK
    c_idx = lax.axis_index("c")
    shard_axis = dict(_BIG)

    def pad_halves(a):
        return jnp.concatenate([_pad_cols(a[..., :F], Fp), _pad_cols(a[..., F:], Fp)], axis=-1)

    cw_all = pad_halves(small["ffn_conv_w"])
    cb_all = pad_halves(small["ffn_conv_b"][:, None, :])
    w_gate_up_all = jnp.pad(small["gla_w_gate_up"].astype(bf16), ((0, 0), (0, GATE_PAD - GLA_GATE_RANK), (0, 0)))

    def mixer_names(l):
        return ("gla_w_in", "gla_w_out") if l % 2 == 0 else ("dil_w_in", "dil_w_out")

    def shard_of(l, which):
        n_in, n_out = mixer_names(l)
        name, idx = {"w_in": (n_in, l // 2), "w_out": (n_out, l // 2), "w_up": ("ffn_w_up", l), "w_down": ("ffn_w_down", l)}[which]
        return w_sh[name][idx].astype(bf16)

    weights = [dict() for _ in range(DEPTH)]

    def install(l, which, gathered):
        n_in, n_out = mixer_names(l)
        if which == "w_in":
            weights[l]["w_in"] = w_in = _unshard(gathered, shard_axis[n_in])
            if l % 2 == 0:
                weights[l]["w_gate"] = _pad_cols(w_in[:, qkvr:], GATE_PAD)
        elif which == "w_out":
            weights[l]["w_out"] = _unshard(gathered, shard_axis[n_out])
        elif which == "w_up":
            weights[l]["w_up"] = pad_halves(_unshard(gathered, shard_axis["ffn_w_up"]))
        else:
            weights[l]["w_down"] = _pad_rows(_unshard(gathered, shard_axis["ffn_w_down"]), Fp)

    kinds = ("w_in", "w_out", "w_up", "w_down")
    for which, gathered in zip(kinds, _run_copies(_gather_plan([shard_of(0, k) for k in kinds]), "gather_layer0")):
        install(0, which, gathered)

    riders = [{"proj": "w_down", "core": "w_up", "ffn_up": "w_in", "ffn_down": "w_out"},
              {"proj": "w_in", "ffn_up": "w_up", "conv": "w_down", "ffn_down": "w_out"}]

    def side_for(l, kernel):
        which = riders[l % 2].get(kernel)
        return _gather_plan([shard_of(l + 1, which)]) if which and l + 1 < DEPTH else None

    def carried(l, kernel, res, n_own=1):
        which = riders[l % 2].get(kernel)
        if which and l + 1 < DEPTH:
            install(l + 1, which, res[n_own])
            return res[0] if n_own == 1 else res[:n_own]
        return res

    h0 = x.reshape(T, D)
    saved = []
    cur, cur_b = h0, h0.astype(_MXU)
    fwd = dict(tm=1024, tn=1024, tk=2048)
    for i in range(DEPTH):
        j = i // 2
        tag = f"l{i}_"
        lg, lb = small["ln_g"][i], small["ln_b"][i]
        W = weights[i]
        if i % 2 == 0:
            gate_bias, norm_g = small["gla_gate_bias"][j][None, :], small["gla_norm_g"][j][None, :]
            proj = carried(i, "proj", _mm(cur_b, W["w_in"], "nn", tag + "gla_proj", n_out=qkvr, side=side_for(i, "proj"), **fwd))
            g_low = _mm(cur_b, W["w_gate"], "nn", tag + "gla_glow", tn=GATE_PAD)
            z = _mm(g_low, w_gate_up_all, "nn", tag + "gla_z", layer=j)
            gate = _log_gate_fwd(z, gate_bias, tag + "gla_gate")
            o, states = carried(i, "core", _gla_fwd(proj, gate, nseq, tag + "gla_core", side=side_for(i, "core")), n_own=2)
            y_b = _rms_gate_fwd(o, proj, norm_g, tag + "gla_norm")
            mix = _mm(y_b, W["w_out"], "nn", tag + "gla_out", **fwd)
            mixer_saved = (proj, g_low, z, gate, states, o, y_b)
        else:
            proj = carried(i, "proj", _mm(cur_b, W["w_in"], "nn", tag + "dil_proj", side=side_for(i, "proj"), **fwd))
            outs, lses = [], []
            for gi, (window, dilation) in enumerate(DIL_PATTERNS):
                og, lg_ = _dil_fwd(proj, gi, window, dilation, nseq, tag + f"dil_attn{gi}")
                outs.append(og)
                lses.append(lg_)
            y, y_b, lse_tot = _dil_mix_fwd(outs, lses, tag + "dil_mix")
            mix = _mm(y_b, W["w_out"], "nn", tag + "dil_out", **fwd)
            mixer_saved = (proj, y, y_b, lse_tot)
        x1, x1_b = _ln_fwd(cur, mix, lg[0:1], lb[0:1], tag + "ln1")
        cw, cb = cw_all[i], cb_all[i]
        hh = carried(i, "ffn_up", _mm(x1_b, W["w_up"], "nn", tag + "ffn_up", side=side_for(i, "ffn_up"), **fwd))
        act_b = carried(i, "conv", _conv_gate_fwd(hh, cw, cb, nseq, tag + "ffn_conv", side=side_for(i, "conv")))
        ffn = carried(i, "ffn_down", _mm(act_b, W["w_down"], "nn", tag + "ffn_down", tm=1024, tn=512, tk=Fp, side=side_for(i, "ffn_down")))
        x2, x2_b = _ln_fwd(x1, ffn, lg[1:2], lb[1:2], tag + "ln2")
        saved.append((cur, cur_b, mix, x1, x1_b, hh, act_b, ffn, mixer_saved))
        cur, cur_b = x2, x2_b

    dy, sq = _loss_head(cur, loss_target.reshape(T, D), "loss_head")
    loss = lax.psum(0.5 * jnp.sum(sq) / D, ("x", "y", "c"))

    gb = {n: [None] * w_sh[n].shape[0] for n in names}
    own_half = {n: [None] * w_sh[n].shape[0] for n, _ in _BIG}
    bwd_riders = {"down_dx": "w_down", "conv": "w_up", "gate_dw": "w_in", "up_dx": "w_out"}
    riding = {"layer": None, "partial": None}

    def grad_slot(l, which):
        n_in, n_out = mixer_names(l)
        return {"w_in": (n_in, l // 2), "w_out": (n_out, l // 2), "w_up": ("ffn_w_up", l), "w_down": ("ffn_w_down", l)}[which]

    def ride(kernel):
        return None if riding["layer"] is None else _exchange_plan([riding["partial"][bwd_riders[kernel]]])

    def landed(kernel, got):
        if riding["layer"] is not None:
            l, which = riding["layer"], bwd_riders[kernel]
            n, idx = grad_slot(l, which)
            own_half[n][idx] = _reduce_scatter_finish([riding["partial"][which]], got, f"scatter_l{l}_{which}")[0]

    def unwrap(kernel, res):
        if riding["layer"] is None:
            return res
        landed(kernel, res[1:])
        return res[0]

    d_res = None
    d_in = dy
    for i in reversed(range(DEPTH)):
        j = i // 2
        tag = f"l{i}_b_"
        xin, xin_b, mix, x1, x1_b, hh, act_b, ffn, mixer_saved = saved[i]
        lg = small["ln_g"][i]
        cw, cb = cw_all[i], cb_all[i]
        dw_tiles = dict(tm=1024, tn=1024, tk=2048, out_dtype=bf16)
        dx_tiles = dict(tm=1024, tn=512, tk=6144)
        dys, scales = ([d_in], [1.0]) if d_res is None else ([d_res, d_in], [DEEPNORM_ALPHA, 1.0])
        du2, du2_b, dg2, db2 = _ln_bwd(x1, ffn, lg[1:2], dys, scales, tag + "ln2")
        gb["ffn_w_down"][i] = _mm(act_b, du2_b, "tn", tag + "ffn_down_dw", **dw_tiles)[:F]
        W = weights[i]
        dact = unwrap("down_dx", _mm(du2_b, W["w_down"], "nt", tag + "ffn_down_dx", side=ride("down_dx"), **dx_tiles))
        dhg, dhu, dcw, dcb, rode = _conv_gate_bwd(hh, dact, cw, cb, nseq, tag + "ffn_conv", side=ride("conv"))
        landed("conv", rode)
        gb["ffn_conv_w"][i] = jnp.concatenate([dcw[:, :F], dcw[:, Fp:Fp + F]], axis=1)
        gb["ffn_conv_b"][i] = jnp.concatenate([dcb[0, :F], dcb[0, Fp:Fp + F]], axis=0)
        dwg = unwrap("gate_dw", _mm(x1_b, dhg, "tn", tag + "ffn_gate_dw", side=ride("gate_dw"), **dw_tiles))
        dwu = _mm(x1_b, dhu, "tn", tag + "ffn_up_dw", **dw_tiles)
        gb["ffn_w_up"][i] = jnp.concatenate([dwg[:, :F], dwu[:, :F]], axis=1)
        dx1g = _mm(dhg, W["w_up"], "nt", tag + "ffn_gate_dx", **dx_tiles)
        dx1u = unwrap("up_dx", _mm(dhu, W["w_up"], "nt", tag + "ffn_up_dx", b_col0=Fp, side=ride("up_dx"), **dx_tiles))
        du1, du1_b, dg1, db1 = _ln_bwd(xin, mix, lg[0:1], [du2, dx1g, dx1u], [DEEPNORM_ALPHA, 1.0, 1.0], tag + "ln1")
        gb["ln_g"][i] = jnp.concatenate([dg1, dg2], axis=0)
        gb["ln_b"][i] = jnp.concatenate([db1, db2], axis=0)
        if i % 2 == 0:
            proj, g_low, z, gate, states, o, y_b = mixer_saved
            gate_bias, norm_g = small["gla_gate_bias"][j][None, :], small["gla_norm_g"][j][None, :]
            gb["gla_w_out"][j] = _mm(y_b, du1_b, "tn", tag + "gla_out_dw", **dw_tiles)
            dyy = _mm(du1_b, W["w_out"], "nt", tag + "gla_out_dx", **dx_tiles)
            do, dr, dng = _rms_gate_bwd(o, proj, norm_g, dyy, tag + "gla_norm")
            gb["gla_norm_g"][j] = dng[0]
            dq, dk_, dv_, dgate = _gla_bwd(proj, gate, states, do, nseq, tag + "gla_core")
            dz, dbias = _log_gate_bwd(z, gate_bias, dgate, tag + "gla_gate")
            gb["gla_gate_bias"][j] = dbias[0]
            gb["gla_w_gate_up"][j] = _mm(g_low, dz, "tn", tag + "gla_z_dw", tk=1024)[:GLA_GATE_RANK]
            dg_low = _mm(dz, w_gate_up_all, "nt", tag + "gla_z_dx", tn=GATE_PAD, layer=j)
            dproj = jnp.concatenate([dq, dk_, dv_, dr], axis=1)
            dw_main = _mm(xin_b, dproj, "tn", tag + "gla_proj_dw", **dw_tiles)
            dw_gate = _mm(xin_b, dg_low, "tn", tag + "gla_glow_dw", tm=1024, tn=GATE_PAD, tk=2048, out_dtype=bf16)[:, :GLA_GATE_RANK]
            gb["gla_w_in"][j] = jnp.concatenate([dw_main, dw_gate], axis=1)
            dxa = _mm(dproj, W["w_in"], "nt", tag + "gla_proj_dx", **dx_tiles)
            dxb = _mm(dg_low, W["w_gate"], "nt", tag + "gla_glow_dx")
            d_in = _axpy(dxa, dxb, 1.0, tag + "gla_dx_sum")
        else:
            proj, y, y_b, lse_tot = mixer_saved
            gb["dil_w_out"][j] = _mm(y_b, du1_b, "tn", tag + "dil_out_dw", **dw_tiles)
            dyy = _mm(du1_b, W["w_out"], "nt", tag + "dil_out_dx", **dx_tiles)
            delta = _dil_delta(dyy, y, tag + "dil_delta")
            pieces = []
            for gi, (window, dilation) in enumerate(DIL_PATTERNS):
                pieces += list(_dil_bwd(proj, dyy, lse_tot, delta, gi, window, dilation, nseq, tag + f"dil_attn{gi}"))
            dproj = jnp.concatenate(pieces, axis=1).astype(_MXU)
            gb["dil_w_in"][j] = _mm(xin_b, dproj, "tn", tag + "dil_proj_dw", **dw_tiles)
            d_in = _mm(dproj, W["w_in"], "nt", tag + "dil_proj_dx", **dx_tiles)
        d_res = du1
        local = [_to_shards(gb[n][idx], shard_axis[n]) for n, idx in (grad_slot(i, k) for k in kinds)]
        riding["layer"], riding["partial"] = i, dict(zip(kinds, _reduce_scatter_start(local, f"scatter_l{i}")))
    grad_x = _axpy(d_res, d_in, DEEPNORM_ALPHA, "grad_x").reshape(x.shape)

    last = [riding["partial"][k] for k in kinds]
    for k, own in zip(kinds, _reduce_scatter_finish(last, _run_copies(_exchange_plan(last), "scatter_l0_ici"), "scatter_l0")):
        n, idx = grad_slot(0, k)
        own_half[n][idx] = own
    own_all = [jnp.stack(own_half[n], axis=0) for n, _ in _BIG]
    other_all = _sibling_swap(own_all, "scatter_swap")
    grads, delta, new_m, new_v = {}, {}, {}, {}
    for (n, _), own, other in zip(_BIG, own_all, other_all):
        grads[n], delta[n], new_m[n], new_v[n] = _adamw_halves(w_sh[n], m_sh[n], v_sh[n], own, other, c_idx, "adamw_" + n)
    grads.update(_reduce_small_grads({n: jnp.stack(gb[n], axis=0) for n, _ in _SMALL}, w_sh))
    small_names = [n for n, _ in _SMALL]
    packed = [_pack_rows([src[n] for n in small_names]) for src in (w_sh, grads, m_sh, v_sh)]
    res = _adamw(*packed, "adamw_small")
    shapes = [w_sh[n].shape for n in small_names]
    for dst, arr in zip((delta, new_m, new_v), res):
        dst.update(dict(zip(small_names, _unpack_rows(arr, shapes))))

    return (loss, grad_x, *[grads[n] for n in names], *[delta[n] for n in names], *[new_m[n] for n in names],
            *[new_v[n] for n in names])
```

```python
import functools

import jax
import jax.numpy as jnp
from jax import lax
from jax.experimental import pallas as pl
from jax.experimental.pallas import tpu as pltpu

f32 = jnp.float32
bf16 = jnp.bfloat16
_MXU = jnp.bfloat16

DEPTH = 4
GLA_HEADS = 4
GLA_GATE_RANK = 16
GLA_GATE_NORMALIZER = 16.0
GLA_CHUNK = 64
GLA_SUB = 16
DIL_PATTERNS = ((128, 1), (512, 4), (2048, 16))
DIL_HEADS = 8
DIL_HEAD_DIM = 128
DIL_BLOCK = 128
DEEPNORM_ALPHA = (2 * DEPTH) ** 0.25
LN_EPS = 1e-5
RMS_EPS = 1e-6
ADAM_LR = 0.001
ADAM_B1 = 0.9
ADAM_B2 = 0.999
ADAM_EPS = 1e-08
ADAM_WD = 0.01
ADAM_STEP = 10

LANE = 128
VMEM_LIMIT = 48 * 1024 * 1024
GATE_PAD = LANE
MESH = pl.DeviceIdType.MESH


def _cparams(sem=None):
    return pltpu.CompilerParams(dimension_semantics=sem, vmem_limit_bytes=VMEM_LIMIT)


def _div_tile(n, pref, unit):
    if n <= pref:
        return n
    best = None
    for t in range(unit, pref + 1, unit):
        if n % t == 0:
            best = t
    assert best is not None, (n, pref, unit)
    return best


def _dot(a, b, ca, cb):
    return lax.dot_general(a.astype(_MXU), b.astype(_MXU), (((ca,), (cb,)), ((), ())), preferred_element_type=f32)


def _nn(a, b):
    return _dot(a, b, 1, 0)


def _nt(a, b):
    return _dot(a, b, 1, 1)


def _tn(a, b):
    return _dot(a, b, 0, 0)


def _exact_dot(a, b):
    return jnp.dot(a, b, precision=lax.Precision.HIGHEST, preferred_element_type=f32)


def _sigmoid(x):
    return 1.0 / (1.0 + jnp.exp(-x))


def _call_with_side(body, name, grid, in_specs, out_specs, out_shape, scratch, sem, args, side):
    if side is None:
        return pl.pallas_call(body, name=name, grid=grid, in_specs=in_specs, out_specs=out_specs, out_shape=out_shape,
                              scratch_shapes=scratch, compiler_params=_cparams(sem))(*args)
    n_in, n_out, n_scr = len(in_specs), len(out_specs), len(scratch)
    n_si, n_so = len(side.ins), len(side.outs)

    def wrapped(*refs):
        ins, s_in = refs[:n_in], refs[n_in:n_in + n_si]
        outs, s_out = refs[n_in + n_si:n_in + n_si + n_out], refs[n_in + n_si + n_out:n_in + n_si + n_out + n_so]
        rest = refs[n_in + n_si + n_out + n_so:]
        sems = rest[n_scr:]
        ids = [pl.program_id(ax) for ax in range(len(grid))]
        first = functools.reduce(lambda u, v: u & v, [i == 0 for i in ids])
        last = functools.reduce(lambda u, v: u & v, [i == g - 1 for i, g in zip(ids, grid)])

        @pl.when(first)
        def _():
            side.start(s_in, s_out, *sems)

        body(*ins, *outs, *rest[:n_scr])

        @pl.when(last)
        def _():
            side.finish(s_in, s_out, *sems)

    return pl.pallas_call(
        wrapped, name=name, grid=grid, in_specs=list(in_specs) + [_ANY] * n_si, out_specs=list(out_specs) + [_ANY] * n_so,
        out_shape=list(out_shape) + list(side.outs), scratch_shapes=list(scratch) + list(side.scratch),
        compiler_params=_cparams(("arbitrary",) * len(grid)))(*args, *side.ins)


def _mm(a, b, mode, name, tm=1024, tn=512, tk=2048, out_dtype=f32, layer=None, n_out=None, b_k0=0, side=None):
    if mode == "nn":
        (M, K), N = a.shape, (n_out or b.shape[-1])
    elif mode == "nt":
        (M, K), N = a.shape, b.shape[-2]
    else:
        (K, M), N = a.shape, b.shape[-1]
    tm, tn, tk = _div_tile(M, tm, LANE), _div_tile(N, tn, LANE), _div_tile(K, tk, LANE)
    nk = K // tk
    if mode == "tn":
        a_spec = pl.BlockSpec((tk, tm), lambda i, j, k: (k, i))
    else:
        a_spec = pl.BlockSpec((tm, tk), lambda i, j, k: (i, k))
    lead = () if layer is None else (None,)
    pre = () if layer is None else (layer,)
    k0 = b_k0 // tk
    assert k0 * tk == b_k0 and (mode != "tn" or b_k0 == 0)
    if mode == "nt":
        b_spec = pl.BlockSpec(lead + (tn, tk), lambda i, j, k: pre + (j, k + k0))
    else:
        b_spec = pl.BlockSpec(lead + (tk, tn), lambda i, j, k: pre + (k + k0, j))
    ca, cb = {"nn": (1, 0), "nt": (1, 1), "tn": (0, 0)}[mode]

    def body(a_ref, b_ref, o_ref, *acc):
        p = _dot(a_ref[...], b_ref[...], ca, cb)
        if nk == 1:
            o_ref[...] = p.astype(o_ref.dtype)
        else:
            k = pl.program_id(2)
            acc_ref = acc[0]

            @pl.when(k == 0)
            def _():
                acc_ref[...] = p

            @pl.when(k > 0)
            def _():
                acc_ref[...] += p

            @pl.when(k == nk - 1)
            def _():
                o_ref[...] = acc_ref[...].astype(o_ref.dtype)

    res = _call_with_side(
        body, name, (M // tm, N // tn, nk), [a_spec, b_spec], [pl.BlockSpec((tm, tn), lambda i, j, k: (i, j))],
        [jax.ShapeDtypeStruct((M, N), out_dtype)], [pltpu.VMEM((tm, tn), f32)] if nk > 1 else [],
        ("parallel", "parallel", "arbitrary"), (a, b), side)
    return res if side else res[0]


def _rowwise(fn, rows, consts, outs, reds, name, tm=256):
    T = rows[0][0].shape[0]
    tm = _div_tile(T, tm, 8)
    n_r, n_c, n_o = len(rows), len(consts), len(outs)

    def body(*refs):
        ins = [r[...] for r in refs[: n_r + n_c]]
        res = fn(*ins)
        res = res if isinstance(res, (tuple, list)) else (res,)
        o_refs = refs[n_r + n_c: n_r + n_c + n_o]
        r_refs = refs[n_r + n_c + n_o:]
        for ref, val in zip(o_refs, res[:n_o]):
            ref[...] = val.astype(ref.dtype)
        i = pl.program_id(0)
        for ref, val in zip(r_refs, res[n_o:]):
            _accumulate(ref, val, i)

    in_specs = [pl.BlockSpec((tm, w), functools.partial(lambda i, cb: (i, cb), cb=cb)) for (_, w, cb) in rows]
    in_specs += [pl.BlockSpec(c.shape, lambda i: (0, 0)) for c in consts]
    out_specs = [pl.BlockSpec((tm, w), lambda i: (i, 0)) for (w, _) in outs]
    out_specs += [pl.BlockSpec((1, w), lambda i: (0, 0)) for w in reds]
    out_shape = [jax.ShapeDtypeStruct((T, w), dt) for (w, dt) in outs]
    out_shape += [jax.ShapeDtypeStruct((1, w), f32) for w in reds]
    return pl.pallas_call(
        body, name=name, grid=(T // tm,), in_specs=in_specs, out_specs=out_specs, out_shape=out_shape,
        compiler_params=_cparams(("arbitrary",)),
    )(*[r[0] for r in rows], *consts)


def _accumulate(ref, val, step):
    @pl.when(step == 0)
    def _():
        ref[...] = val

    @pl.when(step > 0)
    def _():
        ref[...] += val


def _full(a):
    return (a, a.shape[1], 0)


def _colsum(x):
    return jnp.sum(x, axis=0, keepdims=True)


def _ln_stats(u):
    mu = jnp.mean(u, axis=-1, keepdims=True)
    xc = u - mu
    var = jnp.mean(xc * xc, axis=-1, keepdims=True)
    rstd = lax.rsqrt(var + LN_EPS)
    return xc * rstd, rstd


def _ln_fwd(x, f, g, b, name):
    def fn(x, f, g, b):
        xhat, _ = _ln_stats(DEEPNORM_ALPHA * x + f)
        y = xhat * g + b
        return y, y

    return _rowwise(fn, [_full(x), _full(f)], [g, b], [(x.shape[1], f32), (x.shape[1], _MXU)], [], name)


def _ln_bwd(x, f, g, dys, scales, name):
    def fn(x, f, *rest):
        g = rest[-1]
        dy = None
        for d, s in zip(rest[:-1], scales):
            t = d if s == 1.0 else s * d
            dy = t if dy is None else dy + t
        xhat, rstd = _ln_stats(DEEPNORM_ALPHA * x + f)
        dxh = dy * g
        m1 = jnp.mean(dxh, axis=-1, keepdims=True)
        m2 = jnp.mean(dxh * xhat, axis=-1, keepdims=True)
        du = rstd * (dxh - m1 - xhat * m2)
        return du, du, _colsum(dy * xhat), _colsum(dy)

    D = x.shape[1]
    return _rowwise(fn, [_full(x), _full(f)] + [_full(d) for d in dys], [g], [(D, f32), (D, _MXU)], [D, D], name)


def _loss_head(y, t, name):
    D = y.shape[1]

    def fn(y, t):
        e = y - t
        return e * (1.0 / D), _colsum(e * e)

    return _rowwise(fn, [_full(y), _full(t)], [], [(D, f32)], [D], name)


def _axpy(a, b, alpha, name):
    def fn(a, b):
        return alpha * a + b

    return _rowwise(fn, [_full(a), _full(b)], [], [(a.shape[1], f32)], [], name)[0]


def _shift_down(x, k):
    row = lax.broadcasted_iota(jnp.int32, x.shape, 0)
    return jnp.where(row >= k, pltpu.roll(x, k, 0), 0.0)


def _shift_up(x, k):
    S = x.shape[0]
    row = lax.broadcasted_iota(jnp.int32, x.shape, 0)
    return jnp.where(row < S - k, pltpu.roll(x, S - k, 0), 0.0)


def _causal_conv(h, w, b):
    return ((b + w[0:1] * _shift_down(h, 2)) + w[1:2] * _shift_down(h, 1)) + w[2:3] * h


def _conv_gate_fwd(h, cw, cb, nseq, name, tc=256, side=None):
    T, F2 = h.shape
    F, S = F2 // 2, T // nseq
    tc = _div_tile(F, tc, LANE)
    nf = F // tc

    def body(hg_ref, hu_ref, wg_ref, wu_ref, bg_ref, bu_ref, a_ref):
        cg = _causal_conv(hg_ref[...], wg_ref[...], bg_ref[...])
        cu = _causal_conv(hu_ref[...], wu_ref[...], bu_ref[...])
        a_ref[...] = (cg * _sigmoid(cg) * cu).astype(a_ref.dtype)

    res = _call_with_side(
        body, name, (nseq, nf),
        [pl.BlockSpec((S, tc), lambda s, j: (s, j)), pl.BlockSpec((S, tc), lambda s, j: (s, nf + j)),
         pl.BlockSpec((3, tc), lambda s, j: (0, j)), pl.BlockSpec((3, tc), lambda s, j: (0, nf + j)),
         pl.BlockSpec((1, tc), lambda s, j: (0, j)), pl.BlockSpec((1, tc), lambda s, j: (0, nf + j))],
        [pl.BlockSpec((S, tc), lambda s, j: (s, j))], [jax.ShapeDtypeStruct((T, F), _MXU)], [],
        ("parallel", "parallel"), (h, h, cw, cw, cb, cb), side)
    return res if side else res[0]


def _conv_gate_bwd(h, da, cw, cb, nseq, name, tc=128, side=None):
    T, F2 = h.shape
    F, S = F2 // 2, T // nseq
    tc = _div_tile(F, tc, LANE)
    nf = F // tc

    def conv_bwd(dc, hx, w):
        dh = (w[2:3] * dc + w[1:2] * _shift_up(dc, 1)) + w[0:1] * _shift_up(dc, 2)
        dw = jnp.concatenate([_colsum(dc * _shift_down(hx, 2)), _colsum(dc * _shift_down(hx, 1)), _colsum(dc * hx)], axis=0)
        return dh, dw, _colsum(dc)

    def body(hg_ref, hu_ref, da_ref, wg_ref, wu_ref, bg_ref, bu_ref, dhg_ref, dhu_ref, dwg_ref, dwu_ref, dbg_ref, dbu_ref):
        hg, hu, da = hg_ref[...], hu_ref[...], da_ref[...]
        wg, wu = wg_ref[...], wu_ref[...]
        cg = _causal_conv(hg, wg, bg_ref[...])
        cu = _causal_conv(hu, wu, bu_ref[...])
        sg = _sigmoid(cg)
        dcu = da * (cg * sg)
        dcg = da * cu * (sg * (1.0 + cg * (1.0 - sg)))
        dhg, dwg, dbg = conv_bwd(dcg, hg, wg)
        dhu, dwu, dbu = conv_bwd(dcu, hu, wu)
        dhg_ref[...] = dhg.astype(dhg_ref.dtype)
        dhu_ref[...] = dhu.astype(dhu_ref.dtype)
        s = pl.program_id(1)
        _accumulate(dwg_ref, dwg, s)
        _accumulate(dwu_ref, dwu, s)
        _accumulate(dbg_ref, dbg, s)
        _accumulate(dbu_ref, dbu, s)

    col = lambda j, s: (s, j)
    par = lambda j, s: (0, j)
    dhg, dhu, dwg, dwu, dbg, dbu, *side_res = _call_with_side(
        body, name, (nf, nseq),
        [pl.BlockSpec((S, tc), col), pl.BlockSpec((S, tc), lambda j, s: (s, nf + j)), pl.BlockSpec((S, tc), col),
         pl.BlockSpec((3, tc), par), pl.BlockSpec((3, tc), lambda j, s: (0, nf + j)),
         pl.BlockSpec((1, tc), par), pl.BlockSpec((1, tc), lambda j, s: (0, nf + j))],
        [pl.BlockSpec((S, tc), col), pl.BlockSpec((S, tc), col), pl.BlockSpec((3, tc), par), pl.BlockSpec((3, tc), par),
         pl.BlockSpec((1, tc), par), pl.BlockSpec((1, tc), par)],
        [jax.ShapeDtypeStruct((T, F), _MXU), jax.ShapeDtypeStruct((T, F), _MXU), jax.ShapeDtypeStruct((3, F), f32),
         jax.ShapeDtypeStruct((3, F), f32), jax.ShapeDtypeStruct((1, F), f32), jax.ShapeDtypeStruct((1, F), f32)],
        [], ("parallel", "arbitrary"), (h, h, da, cw, cw, cb, cb), side)
    return dhg, dhu, jnp.concatenate([dwg, dwu], axis=1), jnp.concatenate([dbg, dbu], axis=1), side_res


def _group_row(x, jj):
    C, d = x.shape
    n = C // GLA_SUB
    x3 = x.reshape(n, GLA_SUB, d)
    return jnp.broadcast_to(x3[:, jj:jj + 1, :], (n, GLA_SUB, d)).reshape(C, d)


def _group_sum(x):
    C, d = x.shape
    n = C // GLA_SUB
    s = jnp.sum(x.reshape(n, GLA_SUB, d), axis=1, keepdims=True)
    return jnp.broadcast_to(s, (n, GLA_SUB, d)).reshape(C, d)


def _chunk_cumsum(g):
    C = g.shape[0]
    row = lax.broadcasted_iota(jnp.int32, (C, C), 0)
    col = lax.broadcasted_iota(jnp.int32, (C, C), 1)
    return _exact_dot((row >= col).astype(f32), g)


def _chunk_suffix_sum(x):
    C = x.shape[0]
    row = lax.broadcasted_iota(jnp.int32, (C, C), 0)
    col = lax.broadcasted_iota(jnp.int32, (C, C), 1)
    return _exact_dot((col >= row).astype(f32), x)


def _gla_scores(q, k, b):
    C = q.shape[0]
    n = C // GLA_SUB
    row = lax.broadcasted_iota(jnp.int32, (C, C), 0)
    col = lax.broadcasted_iota(jnp.int32, (C, C), 1)
    blocks = [jnp.zeros((GLA_SUB, C), f32)]
    for s in range(1, n):
        lo = s * GLA_SUB
        bref = b[lo - 1:lo, :]
        qr = q[lo:lo + GLA_SUB] * jnp.exp(b[lo:lo + GLA_SUB] - bref)
        kr = k * jnp.exp(jnp.minimum(bref - b, 0.0))
        blocks.append(_nt(qr, kr))
    sub_start = (row // GLA_SUB) * GLA_SUB
    a = jnp.where(col < sub_start, jnp.concatenate(blocks, axis=0), 0.0)
    rin = lax.broadcasted_iota(jnp.int32, (C, 1), 0) % GLA_SUB
    for jj in range(GLA_SUB):
        e = jnp.exp(jnp.minimum(b - _group_row(b, jj), 0.0))
        colv = jnp.sum(q * _group_row(k, jj) * e, axis=1, keepdims=True)
        colv = jnp.where(rin >= jj, colv, 0.0)
        a = jnp.where(col == sub_start + jj, colv, a)
    return a


def _gla_scores_bwd(da, q, k, b):
    C = q.shape[0]
    n = C // GLA_SUB
    row = lax.broadcasted_iota(jnp.int32, (C, C), 0)
    col = lax.broadcasted_iota(jnp.int32, (C, C), 1)
    sub_start = (row // GLA_SUB) * GLA_SUB
    da_inter = jnp.where(col < sub_start, da, 0.0)
    dq_blocks = [jnp.zeros((GLA_SUB, q.shape[1]), f32)]
    dk = jnp.zeros_like(k)
    for s in range(1, n):
        lo = s * GLA_SUB
        bref = b[lo - 1:lo, :]
        eq = jnp.exp(b[lo:lo + GLA_SUB] - bref)
        ek = jnp.exp(jnp.minimum(bref - b, 0.0))
        das = da_inter[lo:lo + GLA_SUB]
        dq_blocks.append(_nn(das, k * ek) * eq)
        dk = dk + _tn(das, q[lo:lo + GLA_SUB] * eq) * ek
    dq = jnp.concatenate(dq_blocks, axis=0)
    rin = lax.broadcasted_iota(jnp.int32, (C, 1), 0) % GLA_SUB
    for jj in range(GLA_SUB):
        e = jnp.exp(jnp.minimum(b - _group_row(b, jj), 0.0))
        dac = jnp.sum(jnp.where(col == sub_start + jj, da, 0.0), axis=1, keepdims=True)
        dac = jnp.where(rin >= jj, dac, 0.0)
        w = dac * e
        dq = dq + w * _group_row(k, jj)
        dk = dk + jnp.where(rin == jj, _group_sum(w * q), 0.0)
    return dq, dk


def _gla_specs(nC, dk, dv):
    H = GLA_HEADS
    voff = (2 * H * dk) // dv
    assert voff * dv == 2 * H * dk
    return H, voff


def _gla_fwd(proj, gate, nseq, name, side=None):
    T = proj.shape[0]
    dk = gate.shape[1] // GLA_HEADS
    dv = 2 * dk
    C = GLA_CHUNK
    nC = T // nseq // C
    H, voff = _gla_specs(nC, dk, dv)
    scale = dk ** -0.5

    def body(q_ref, k_ref, v_ref, g_ref, o_ref, st_ref, state):
        c = pl.program_id(2)

        @pl.when(c == 0)
        def _():
            state[...] = jnp.zeros_like(state)

        q, k, v = q_ref[...] * scale, k_ref[...], v_ref[...]
        b = _chunk_cumsum(g_ref[...])
        st = state[...]
        st_ref[0] = st
        a = _gla_scores(q, k, b)
        o_ref[...] = _nt(q * jnp.exp(b), st) + _nn(a, v)
        bl = b[C - 1:C, :]
        state[...] = st * jnp.exp(bl) + _tn(v, k * jnp.exp(bl - b))

    row = lambda s, h, c: s * nC + c
    return _call_with_side(
        body, name, (nseq, H, nC),
        [pl.BlockSpec((C, dk), lambda s, h, c: (row(s, h, c), h)),
         pl.BlockSpec((C, dk), lambda s, h, c: (row(s, h, c), H + h)),
         pl.BlockSpec((C, dv), lambda s, h, c: (row(s, h, c), voff + h)),
         pl.BlockSpec((C, dk), lambda s, h, c: (row(s, h, c), h))],
        [pl.BlockSpec((C, dv), lambda s, h, c: (row(s, h, c), h)),
         pl.BlockSpec((1, dv, dk), lambda s, h, c: ((s * H + h) * nC + c, 0, 0))],
        [jax.ShapeDtypeStruct((T, H * dv), f32), jax.ShapeDtypeStruct((nseq * H * nC, dv, dk), f32)],
        [pltpu.VMEM((dv, dk), f32)], ("parallel", "parallel", "arbitrary"), (proj, proj, proj, gate), side)


def _gla_bwd(proj, gate, states, do, nseq, name):
    T = proj.shape[0]
    dk = gate.shape[1] // GLA_HEADS
    dv = 2 * dk
    C = GLA_CHUNK
    nC = T // nseq // C
    H, voff = _gla_specs(nC, dk, dv)
    scale = dk ** -0.5

    def body(q_ref, k_ref, v_ref, g_ref, do_ref, st_ref, dq_ref, dk_ref, dv_ref, dg_ref, dstate, term):
        c = pl.program_id(2)

        @pl.when(c == 0)
        def _():
            dstate[...] = jnp.zeros_like(dstate)
            term[...] = jnp.zeros_like(term)

        q, k, v, do = q_ref[...] * scale, k_ref[...], v_ref[...], do_ref[...]
        b = _chunk_cumsum(g_ref[...])
        st = st_ref[0]
        dst = dstate[...]
        eb = jnp.exp(b)
        bl = b[C - 1:C, :]
        kdec = jnp.exp(bl - b)
        a = _gla_scores(q, k, b)
        rowi = lax.broadcasted_iota(jnp.int32, (C, C), 0)
        coli = lax.broadcasted_iota(jnp.int32, (C, C), 1)
        da = jnp.where(coli <= rowi, _nt(do, v), 0.0)
        dq_s, dk_s = _gla_scores_bwd(da, q, k, b)
        dq = _nn(do, st) * eb + dq_s
        dkk = _nn(v, dst) * kdec + dk_s
        dv_ref[...] = (_tn(a, do) + _nt(k * kdec, dst)).astype(dv_ref.dtype)
        last = lax.broadcasted_iota(jnp.int32, (C, 1), 0) == C - 1
        db = q * dq - k * dkk + jnp.where(last, term[...], 0.0)
        dg_ref[...] = _chunk_suffix_sum(db)
        dq_ref[...] = (dq * scale).astype(dq_ref.dtype)
        dk_ref[...] = dkk.astype(dk_ref.dtype)
        dprev = dst * jnp.exp(bl) + _tn(do, q * eb)
        dstate[...] = dprev
        term[...] = _colsum(st * dprev)

    row = lambda s, h, c: s * nC + (nC - 1 - c)
    kspec = lambda off: pl.BlockSpec((C, dk), lambda s, h, c: (row(s, h, c), off + h))
    vspec = lambda off: pl.BlockSpec((C, dv), lambda s, h, c: (row(s, h, c), off + h))
    return pl.pallas_call(
        body, name=name, grid=(nseq, H, nC),
        in_specs=[kspec(0), kspec(H), vspec(voff), kspec(0), vspec(0),
                  pl.BlockSpec((1, dv, dk), lambda s, h, c: ((s * H + h) * nC + (nC - 1 - c), 0, 0))],
        out_specs=[kspec(0), kspec(0), vspec(0), kspec(0)],
        out_shape=[jax.ShapeDtypeStruct((T, H * dk), _MXU), jax.ShapeDtypeStruct((T, H * dk), _MXU),
                   jax.ShapeDtypeStruct((T, H * dv), _MXU), jax.ShapeDtypeStruct((T, H * dk), f32)],
        scratch_shapes=[pltpu.VMEM((dv, dk), f32), pltpu.VMEM((1, dk), f32)],
        compiler_params=_cparams(("parallel", "parallel", "arbitrary")),
    )(proj, proj, proj, gate, do, states)


def _head_slices(width, n):
    w = width // n
    return [slice(h * w, (h + 1) * w) for h in range(n)]


def _rms_gate_fwd(o, proj, ng, name):
    W = o.shape[1]

    def fn(o, r, ng):
        parts = []
        for sl in _head_slices(W, GLA_HEADS):
            oh = o[:, sl]
            rstd = lax.rsqrt(jnp.mean(oh * oh, axis=-1, keepdims=True) + RMS_EPS)
            rh = r[:, sl]
            parts.append((oh * rstd * ng) * (rh * _sigmoid(rh)))
        return jnp.concatenate(parts, axis=1)

    return _rowwise(fn, [_full(o), (proj, W, 2)], [ng], [(W, _MXU)], [], name)[0]


def _rms_gate_bwd(o, proj, ng, dy, name):
    W = o.shape[1]

    def fn(o, r, dy, ng):
        dos, drs = [], []
        dng = jnp.zeros((1, W // GLA_HEADS), f32)
        for sl in _head_slices(W, GLA_HEADS):
            oh, rh, dyh = o[:, sl], r[:, sl], dy[:, sl]
            rstd = lax.rsqrt(jnp.mean(oh * oh, axis=-1, keepdims=True) + RMS_EPS)
            ohat = oh * rstd
            sg = _sigmoid(rh)
            don = dyh * (rh * sg)
            drs.append(dyh * (ohat * ng) * (sg * (1.0 + rh * (1.0 - sg))))
            dng = dng + _colsum(don * ohat)
            dohat = don * ng
            dos.append(rstd * (dohat - ohat * jnp.mean(dohat * ohat, axis=-1, keepdims=True)))
        return jnp.concatenate(dos, axis=1), jnp.concatenate(drs, axis=1), dng

    return _rowwise(fn, [_full(o), (proj, W, 2), _full(dy)], [ng], [(W, _MXU), (W, _MXU)], [W // GLA_HEADS], name)


def _log_gate_fwd(z, bias, name):
    def fn(z, bias):
        t = z + bias
        return (jnp.minimum(t, 0.0) - jnp.log1p(jnp.exp(-jnp.abs(t)))) * (1.0 / GLA_GATE_NORMALIZER)

    return _rowwise(fn, [_full(z)], [bias], [(z.shape[1], f32)], [], name)[0]


def _log_gate_bwd(z, bias, dg, name):
    def fn(z, dg, bias):
        dz = dg * (1.0 / GLA_GATE_NORMALIZER) * _sigmoid(-(z + bias))
        return dz, _colsum(dz)

    return _rowwise(fn, [_full(z), _full(dg)], [bias], [(z.shape[1], f32)], [z.shape[1]], name)


def _band_masks(P, steps, has_prev):
    i = lax.broadcasted_iota(jnp.int32, (P, P), 0)
    j = lax.broadcasted_iota(jnp.int32, (P, P), 1)
    cur = (i - j >= 0) & (i - j <= steps)
    prev = (i + P - j <= steps) & has_prev
    return cur, prev


def _dil_dims(T, nseq, dilation):
    L = T // nseq // dilation
    P = min(DIL_BLOCK, L)
    return L, P, L // P


def _dil_tiling(T, nseq, dilation):
    L, P, nb = _dil_dims(T, nseq, dilation)
    hb = DIL_HEADS if dilation == 1 else 1
    row_sets = [pl.ds(r, P, stride=dilation) if dilation > 1 else pl.ds(0, P) for r in range(dilation)]
    head_cols = _head_slices(hb * DIL_HEAD_DIM, hb)
    return L, P, nb, P * dilation, hb, DIL_HEADS // hb, row_sets, head_cols


def _dil_fwd(proj, gi, window, dilation, nseq, name):
    T = proj.shape[0]
    H, dh = DIL_HEADS, DIL_HEAD_DIM
    L, P, nb, SB, hb, ng, row_sets, head_cols = _dil_tiling(T, nseq, dilation)
    steps = window // dilation
    scale = dh ** -0.5

    def body(q_ref, kc_ref, kp_ref, vc_ref, vp_ref, o_ref, lse_ref):
        sb, hg = pl.program_id(1), pl.program_id(2)
        mc, mp = _band_masks(P, steps, sb > 0)
        lane = lax.broadcasted_iota(jnp.int32, (P, LANE), 1)

        @pl.when(hg == 0)
        def _():
            lse_ref[...] = jnp.zeros_like(lse_ref)

        for hh, cols in enumerate(head_cols):
            for rows in row_sets:
                q = q_ref[rows, cols]
                sc = jnp.where(mc, _nt(q, kc_ref[rows, cols]) * scale, -jnp.inf)
                sp = jnp.where(mp, _nt(q, kp_ref[rows, cols]) * scale, -jnp.inf)
                m = jnp.maximum(jnp.max(sc, axis=-1, keepdims=True), jnp.max(sp, axis=-1, keepdims=True))
                pc, pp = jnp.exp(sc - m), jnp.exp(sp - m)
                l = jnp.sum(pc, axis=-1, keepdims=True) + jnp.sum(pp, axis=-1, keepdims=True)
                o_ref[rows, cols] = _nn(pc / l, vc_ref[rows, cols]) + _nn(pp / l, vp_ref[rows, cols])
                lse_ref[rows, :] = jnp.where(lane == hg * hb + hh, m + jnp.log(l), lse_ref[rows, :])

    cur = lambda part: pl.BlockSpec((SB, hb * dh), lambda s, sb, hg: (s * nb + sb, (gi * 3 + part) * ng + hg))
    prv = lambda part: pl.BlockSpec((SB, hb * dh), lambda s, sb, hg: (s * nb + jnp.maximum(sb - 1, 0), (gi * 3 + part) * ng + hg))
    return pl.pallas_call(
        body, name=name, grid=(nseq, nb, ng),
        in_specs=[cur(0), cur(1), prv(1), cur(2), prv(2)],
        out_specs=[pl.BlockSpec((SB, hb * dh), lambda s, sb, hg: (s * nb + sb, hg)), pl.BlockSpec((SB, LANE), lambda s, sb, hg: (s * nb + sb, 0))],
        out_shape=[jax.ShapeDtypeStruct((T, H * dh), f32), jax.ShapeDtypeStruct((T, LANE), f32)],
        compiler_params=_cparams(("parallel", "parallel", "arbitrary")),
    )(proj, proj, proj, proj, proj)


def _dil_mix_fwd(os_, lses, name):
    W = os_[0].shape[1]
    G = len(os_)

    def fn(*a):
        o, l = a[:G], a[G:]
        lane = lax.broadcasted_iota(jnp.int32, l[0].shape, 1)
        tot = jnp.zeros(l[0].shape, f32)
        parts = []
        for h, sl in enumerate(_head_slices(W, DIL_HEADS)):
            lh = [x[:, h:h + 1] for x in l]
            m = functools.reduce(jnp.maximum, lh)
            e = [jnp.exp(x - m) for x in lh]
            z = functools.reduce(lambda u, v: u + v, e)
            acc = None
            for g in range(G):
                t = (e[g] / z) * o[g][:, sl]
                acc = t if acc is None else acc + t
            parts.append(acc)
            tot = jnp.where(lane == h, m + jnp.log(z), tot)
        y = jnp.concatenate(parts, axis=1)
        return y, y, tot

    return _rowwise(fn, [_full(x) for x in os_] + [_full(x) for x in lses], [], [(W, f32), (W, _MXU), (LANE, f32)], [], name)


def _dil_delta(do, o, name):
    W = o.shape[1]

    def fn(do, o):
        lane = lax.broadcasted_iota(jnp.int32, (do.shape[0], LANE), 1)
        d = jnp.zeros((do.shape[0], LANE), f32)
        for h, sl in enumerate(_head_slices(W, DIL_HEADS)):
            d = jnp.where(lane == h, jnp.sum(do[:, sl] * o[:, sl], axis=-1, keepdims=True), d)
        return d

    return _rowwise(fn, [_full(do), _full(o)], [], [(LANE, f32)], [], name)[0]


def _dil_bwd(proj, do, lse, delta, gi, window, dilation, nseq, name):
    T = proj.shape[0]
    H, dh = DIL_HEADS, DIL_HEAD_DIM
    L, P, nb, SB, hb, ng, row_sets, head_cols = _dil_tiling(T, nseq, dilation)
    steps = window // dilation
    scale = dh ** -0.5

    def probs(q, k, lse_h, mask):
        return jnp.where(mask, jnp.exp(_nt(q, k) * scale - lse_h), 0.0)

    def head_lane(ref, rows, h):
        lane = lax.broadcasted_iota(jnp.int32, (P, LANE), 1)
        return jnp.sum(jnp.where(lane == h, ref[rows, :], 0.0), axis=1, keepdims=True)

    def dq_body(q_ref, kc_ref, kp_ref, vc_ref, vp_ref, do_ref, lse_ref, del_ref, dq_ref):
        sb, hg = pl.program_id(1), pl.program_id(2)
        mc, mp = _band_masks(P, steps, sb > 0)
        for hh, cols in enumerate(head_cols):
            h = hg * hb + hh
            for rows in row_sets:
                q, doh = q_ref[rows, cols], do_ref[rows, cols]
                lse_h, del_h = head_lane(lse_ref, rows, h), head_lane(del_ref, rows, h)
                kc, kp = kc_ref[rows, cols], kp_ref[rows, cols]
                dsc = probs(q, kc, lse_h, mc) * (_nt(doh, vc_ref[rows, cols]) - del_h) * scale
                dsp = probs(q, kp, lse_h, mp) * (_nt(doh, vp_ref[rows, cols]) - del_h) * scale
                dq_ref[rows, cols] = _nn(dsc, kc) + _nn(dsp, kp)

    cur = lambda part: pl.BlockSpec((SB, hb * dh), lambda s, sb, hg: (s * nb + sb, (gi * 3 + part) * ng + hg))
    prv = lambda part: pl.BlockSpec((SB, hb * dh), lambda s, sb, hg: (s * nb + jnp.maximum(sb - 1, 0), (gi * 3 + part) * ng + hg))
    tok = pl.BlockSpec((SB, hb * dh), lambda s, sb, hg: (s * nb + sb, hg))
    aux = pl.BlockSpec((SB, LANE), lambda s, sb, hg: (s * nb + sb, 0))
    dq = pl.pallas_call(
        dq_body, name=name + "_dq", grid=(nseq, nb, ng),
        in_specs=[cur(0), cur(1), prv(1), cur(2), prv(2), tok, aux, aux],
        out_specs=tok, out_shape=jax.ShapeDtypeStruct((T, H * dh), f32),
        compiler_params=_cparams(("parallel", "parallel", "parallel")),
    )(proj, proj, proj, proj, proj, do, lse, delta)

    def dkv_body(k_ref, v_ref, qc_ref, qn_ref, doc_ref, don_ref, lsec_ref, lsen_ref, delc_ref, deln_ref, dk_ref, dv_ref):
        sb, hg = pl.program_id(1), pl.program_id(2)
        mc, mn = _band_masks(P, steps, sb < nb - 1)
        for hh, cols in enumerate(head_cols):
            h = hg * hb + hh
            for rows in row_sets:
                k, v = k_ref[rows, cols], v_ref[rows, cols]
                qc, doc = qc_ref[rows, cols], doc_ref[rows, cols]
                pc = probs(qc, k, head_lane(lsec_ref, rows, h), mc)
                dsc = pc * (_nt(doc, v) - head_lane(delc_ref, rows, h)) * scale
                qn, don = qn_ref[rows, cols], don_ref[rows, cols]
                pn = probs(qn, k, head_lane(lsen_ref, rows, h), mn)
                dsn = pn * (_nt(don, v) - head_lane(deln_ref, rows, h)) * scale
                dk_ref[rows, cols] = _tn(dsc, qc) + _tn(dsn, qn)
                dv_ref[rows, cols] = _tn(pc, doc) + _tn(pn, don)

    nxt = lambda s, sb: s * nb + jnp.minimum(sb + 1, nb - 1)
    qnx = pl.BlockSpec((SB, hb * dh), lambda s, sb, hg: (nxt(s, sb), gi * 3 * ng + hg))
    tokn = pl.BlockSpec((SB, hb * dh), lambda s, sb, hg: (nxt(s, sb), hg))
    auxn = pl.BlockSpec((SB, LANE), lambda s, sb, hg: (nxt(s, sb), 0))
    dkk, dvv = pl.pallas_call(
        dkv_body, name=name + "_dkv", grid=(nseq, nb, ng),
        in_specs=[cur(1), cur(2), cur(0), qnx, tok, tokn, aux, auxn, aux, auxn],
        out_specs=[tok, tok],
        out_shape=[jax.ShapeDtypeStruct((T, H * dh), f32), jax.ShapeDtypeStruct((T, H * dh), f32)],
        compiler_params=_cparams(("parallel", "parallel", "parallel")),
    )(proj, proj, proj, proj, do, do, lse, lse, delta, delta)
    return dq, dkk, dvv


def _adamw_math(w, g, m, v):
    m = ADAM_B1 * m + (1.0 - ADAM_B1) * g
    v = ADAM_B2 * v + (1.0 - ADAM_B2) * (g * g)
    m_hat = m / (1.0 - ADAM_B1 ** ADAM_STEP)
    v_hat = v / (1.0 - ADAM_B2 ** ADAM_STEP)
    return -ADAM_LR * (m_hat / (jnp.sqrt(v_hat) + ADAM_EPS) + ADAM_WD * w), m, v


def _adamw(w, g, m, v, name):
    W = w.shape[1]
    return _rowwise(_adamw_math, [_full(w), _full(g), _full(m), _full(v)], [], [(W, f32)] * 3, [], name, tm=512)


def _position():
    x, y, c = lax.axis_index("x"), lax.axis_index("y"), lax.axis_index("c")
    other_chips = [(1 - x, y), (x, 1 - y), (1 - x, 1 - y)]
    return x, y, c, other_chips


def _chip_index(x, y):
    return 2 * x + y


_ANY = pl.BlockSpec(memory_space=pl.ANY)


def _all_reduce_small(p, name):
    R, Wd = p.shape

    def body(p_ref, o_ref, buf, send_sems, recv_sems):
        x, y, c, _ = _position()
        me = 4 * x + 2 * y + c
        buf[me] = p_ref[...]
        copies = []
        for k in range(1, 8):
            fx, fy, fc = (k >> 2) & 1, (k >> 1) & 1, k & 1
            peer = (x + fx - 2 * x * fx, y + fy - 2 * y * fy, c + fc - 2 * c * fc)
            cp = pltpu.make_async_remote_copy(src_ref=p_ref, dst_ref=buf.at[me], send_sem=send_sems.at[k - 1],
                                              recv_sem=recv_sems.at[k - 1], device_id=peer, device_id_type=MESH)
            cp.start()
            copies.append(cp)
        for cp in copies:
            cp.wait()
        acc = buf[0]
        for s in range(1, 8):
            acc = acc + buf[s]
        o_ref[...] = acc

    return pl.pallas_call(
        body, name=name, out_shape=jax.ShapeDtypeStruct((R, Wd), f32),
        in_specs=[pl.BlockSpec(memory_space=pltpu.VMEM)], out_specs=pl.BlockSpec(memory_space=pltpu.VMEM),
        scratch_shapes=[pltpu.VMEM((8, R, Wd), f32), pltpu.SemaphoreType.DMA((7,)), pltpu.SemaphoreType.DMA((7,))],
        compiler_params=pltpu.CompilerParams(vmem_limit_bytes=VMEM_LIMIT),
    )(p)


def _layer_half(ref, h, axis):
    n = ref.shape[axis] // 2
    idx = (slice(None),) * axis + (pl.ds(h * n, n),)
    return ref.at[idx]


def _comm_call(body, name, ins, out_shapes, n_sems, n_local=0):
    scratch = [pltpu.SemaphoreType.DMA((n_sems,)), pltpu.SemaphoreType.DMA((n_sems,))]
    if n_local:
        scratch.append(pltpu.SemaphoreType.DMA((n_local,)))
    return pl.pallas_call(body, name=name, out_shape=out_shapes, in_specs=[_ANY] * len(ins), out_specs=[_ANY] * len(out_shapes),
                          scratch_shapes=scratch)(*ins)


class _SideCopies:
    def __init__(self, ins, outs, scratch, start, finish):
        self.ins, self.outs, self.scratch, self.start, self.finish = ins, outs, scratch, start, finish


def _row_half(ref, h):
    r = ref.shape[0] // 2
    start = h * r
    if r % 16 == 0:
        start = pl.multiple_of(start, 16)
    return ref.at[pl.ds(start, r)]


def _gather_plan(ws):
    n = len(ws)

    def copy(o_refs, sems, k, src, i, chip_idx, h, to):
        return pltpu.make_async_remote_copy(src_ref=src, dst_ref=_row_half(o_refs[i].at[chip_idx], h), send_sem=sems[0].at[k],
                                            recv_sem=sems[1].at[k], device_id=to, device_id_type=MESH)

    def own_copy(w_refs, o_refs, sems, i, p):
        return pltpu.make_async_copy(w_refs[i], o_refs[i].at[p], sems[2].at[i])

    def over_ici(w_refs, o_refs, sems, i, j, chip, dst_chip_idx, c):
        return copy(o_refs, sems, 3 * i + j, _row_half(w_refs[i], c), i, dst_chip_idx, c, (*chip, c))

    def start(w_refs, o_refs, *sems):
        x, y, c, chips = _position()
        p = _chip_index(x, y)
        for i in range(n):
            own_copy(w_refs, o_refs, sems, i, p).start()
            for j, chip in enumerate(chips):
                over_ici(w_refs, o_refs, sems, i, j, chip, p, c).start()

    def finish(w_refs, o_refs, *sems):
        x, y, c, chips = _position()
        p = _chip_index(x, y)
        sibling = (x, y, 1 - c)
        passed = []
        for i in range(n):
            for j, chip in enumerate(chips):
                q = _chip_index(*chip)
                over_ici(w_refs, o_refs, sems, i, j, chip, q, c).wait_recv()
                fwd = copy(o_refs, sems, 3 * n + 3 * i + j, _row_half(o_refs[i].at[q], c), i, q, c, sibling)
                fwd.start()
                passed.append(fwd)
        for i in range(n):
            for j, chip in enumerate(chips):
                copy(o_refs, sems, 3 * n + 3 * i + j, _row_half(w_refs[i], c), i, _chip_index(*chip), 1 - c, sibling).wait_recv()
        for i in range(n):
            for j, chip in enumerate(chips):
                over_ici(w_refs, o_refs, sems, i, j, chip, p, c).wait_send()
            own_copy(w_refs, o_refs, sems, i, p).wait()
        for fwd in passed:
            fwd.wait_send()

    scratch = [pltpu.SemaphoreType.DMA((6 * n,)), pltpu.SemaphoreType.DMA((6 * n,)), pltpu.SemaphoreType.DMA((n,))]
    return _SideCopies(list(ws), [jax.ShapeDtypeStruct((4,) + w.shape, w.dtype) for w in ws], scratch, start, finish)


def _run_copies(plan, name):
    n_i, n_o = len(plan.ins), len(plan.outs)

    def body(*refs):
        plan.start(refs[:n_i], refs[n_i:n_i + n_o], *refs[n_i + n_o:])
        plan.finish(refs[:n_i], refs[n_i:n_i + n_o], *refs[n_i + n_o:])

    return pl.pallas_call(body, name=name, out_shape=plan.outs, in_specs=[_ANY] * n_i, out_specs=[_ANY] * n_o,
                          scratch_shapes=plan.scratch)(*plan.ins)


def _sibling_halves(gs, name):
    n = len(gs)

    def body(*refs):
        g_refs, o_refs, send_sems, recv_sems = refs[:n], refs[n:2 * n], refs[2 * n], refs[2 * n + 1]
        x, y, c, _ = _position()
        copies = []
        for i in range(n):
            cp = pltpu.make_async_remote_copy(src_ref=_layer_half(g_refs[i], 1 - c, 1), dst_ref=o_refs[i], send_sem=send_sems.at[i],
                                              recv_sem=recv_sems.at[i], device_id=(x, y, 1 - c), device_id_type=MESH)
            cp.start()
            copies.append(cp)
        for cp in copies:
            cp.wait()

    outs = [jax.ShapeDtypeStruct((4, g.shape[1] // 2) + g.shape[2:], g.dtype) for g in gs]
    return _comm_call(body, name, gs, outs, n)


def _exchange_plan(hs):
    n = len(hs)

    def copies(h_refs, o_refs, send_sems, recv_sems):
        x, y, c, chips = _position()
        return [pltpu.make_async_remote_copy(src_ref=h_refs[i].at[_chip_index(*chip)], dst_ref=o_refs[i].at[j],
                                             send_sem=send_sems.at[3 * i + j], recv_sem=recv_sems.at[3 * i + j],
                                             device_id=(*chip, c), device_id_type=MESH)
                for i in range(n) for j, chip in enumerate(chips)]

    def start(h_refs, o_refs, *sems):
        for cp in copies(h_refs, o_refs, *sems):
            cp.start()

    def finish(h_refs, o_refs, *sems):
        for cp in copies(h_refs, o_refs, *sems):
            cp.wait()

    scratch = [pltpu.SemaphoreType.DMA((3 * n,)), pltpu.SemaphoreType.DMA((3 * n,))]
    return _SideCopies(list(hs), [jax.ShapeDtypeStruct((3,) + h.shape[1:], h.dtype) for h in hs], scratch, start, finish)


def _sibling_swap(ts, name):
    n = len(ts)

    def body(*refs):
        t_refs, o_refs, send_sems, recv_sems = refs[:n], refs[n:2 * n], refs[2 * n], refs[2 * n + 1]
        x, y, c, _ = _position()
        copies = []
        for i in range(n):
            cp = pltpu.make_async_remote_copy(src_ref=t_refs[i], dst_ref=o_refs[i], send_sem=send_sems.at[i], recv_sem=recv_sems.at[i],
                                              device_id=(x, y, 1 - c), device_id_type=MESH)
            cp.start()
            copies.append(cp)
        for cp in copies:
            cp.wait()

    return _comm_call(body, name, ts, [jax.ShapeDtypeStruct(t.shape, t.dtype) for t in ts], n)


BLOCK_ELEMS = 384 * 1024


def _block_2d(rows, cols, sub):
    tns = [t for t in range(LANE, cols + 1, LANE) if cols % t == 0] if cols % LANE == 0 else [cols]
    tms = [t for t in range(sub, rows + 1, sub) if rows % t == 0] or [rows]
    fits = [(tm * tn, tn, tm) for tm in tms for tn in tns if tm * tn <= BLOCK_ELEMS]
    assert fits, (rows, cols, sub)
    _, tn, tm = max(fits)
    return tm, tn


def _prefetch_call(body, name, scalars, grid, in_specs, out_specs, out_shape, args, sem):
    gs = pltpu.PrefetchScalarGridSpec(num_scalar_prefetch=1, grid=grid, in_specs=in_specs, out_specs=out_specs)
    return pl.pallas_call(body, name=name, grid_spec=gs, out_shape=out_shape, compiler_params=_cparams(sem))(scalars, *args)


def _add_own_half(g, got, c, name):
    _, nl, K, N = g.shape
    hl = nl // 2
    tm, tn = _block_2d(K, N, 16)

    def body(c_ref, g_ref, r_ref, o_ref):
        o_ref[...] = (g_ref[...].astype(f32) + r_ref[...].astype(f32)).astype(o_ref.dtype)

    blk = (1, 1, tm, tn)
    return _prefetch_call(
        body, name, jnp.reshape(c, (1,)).astype(jnp.int32), (4, hl, K // tm, N // tn),
        [pl.BlockSpec(blk, lambda s, l, i, j, c_ref: (s, c_ref[0] * hl + l, i, j)), pl.BlockSpec(blk, lambda s, l, i, j, c_ref: (s, l, i, j))],
        pl.BlockSpec(blk, lambda s, l, i, j, c_ref: (s, l, i, j)), jax.ShapeDtypeStruct((4, hl, K, N), g.dtype), (g, got),
        ("parallel",) * 4)


def _add_chips(h, got, p, name):
    _, nl, K, N = h.shape
    tm, tn = _block_2d(K, N, 16)

    def body(p_ref, h_ref, r0_ref, r1_ref, r2_ref, o_ref):
        o_ref[...] = ((h_ref[0].astype(f32) + r0_ref[0].astype(f32)) + r1_ref[0].astype(f32)) + r2_ref[0].astype(f32)

    blk = (1, 1, tm, tn)
    got_spec = lambda q: pl.BlockSpec(blk, lambda l, i, j, p_ref: (q, l, i, j))
    return _prefetch_call(
        body, name, jnp.reshape(p, (1,)).astype(jnp.int32), (nl, K // tm, N // tn),
        [pl.BlockSpec(blk, lambda l, i, j, p_ref: (p_ref[0], l, i, j)), got_spec(0), got_spec(1), got_spec(2)],
        pl.BlockSpec((1, tm, tn), lambda l, i, j, p_ref: (l, i, j)), jax.ShapeDtypeStruct((nl, K, N), f32), (h, got, got, got),
        ("parallel",) * 3)


def _adamw_halves(w, m, v, own, other, c, name):
    nl, K, N = w.shape
    tm, tn = _block_2d(K // 2, N, 8)
    nb = K // 2 // tm

    def body(c_ref, w_ref, m_ref, v_ref, own_ref, other_ref, g_out, d_out, m_out, v_out):
        g = jnp.where(pl.program_id(1) == c_ref[0], own_ref[...], other_ref[...])
        d, m_new, v_new = _adamw_math(w_ref[...], g, m_ref[...], v_ref[...])
        g_out[...] = g
        d_out[...] = d
        m_out[...] = m_new
        v_out[...] = v_new

    blk = (1, tm, tn)
    full = pl.BlockSpec(blk, lambda l, h, i, j, c_ref: (l, h * nb + i, j))
    half = pl.BlockSpec(blk, lambda l, h, i, j, c_ref: (l, i, j))
    return _prefetch_call(
        body, name, jnp.reshape(c, (1,)).astype(jnp.int32), (nl, 2, nb, N // tn), [full, full, full, half, half], [full] * 4,
        [jax.ShapeDtypeStruct((nl, K, N), f32)] * 4, (w, m, v, own, other), ("parallel",) * 4)


def _row_halves_view(g):
    return g.reshape(4, 2, g.shape[1] // 2, g.shape[2])


def _reduce_scatter_start(gs, name):
    c = lax.axis_index("c")
    views = [_row_halves_view(g) for g in gs]
    from_sibling = _sibling_halves(views, name + "_d2d")
    return [_add_own_half(g, r, c, f"{name}_add2_{i}") for i, (g, r) in enumerate(zip(views, from_sibling))]


def _reduce_scatter_finish(hs, got, name):
    p = _chip_index(lax.axis_index("x"), lax.axis_index("y"))
    return [_add_chips(h, r, p, f"{name}_add4_{i}")[0] for i, (h, r) in enumerate(zip(hs, got))]


_BIG = (("gla_w_in", 1), ("gla_w_out", 0), ("dil_w_in", 1), ("dil_w_out", 1), ("ffn_w_up", 1), ("ffn_w_down", 0))


def _pad_rows(a, mult):
    r = (-a.shape[0]) % mult
    return a if r == 0 else jnp.concatenate([a, jnp.zeros((r,) + a.shape[1:], a.dtype)], axis=0)


def _unshard(blocks, axis):
    _, K, N = blocks.shape
    if axis == 1:
        return blocks.transpose(1, 0, 2).reshape(K, 4 * N)
    return blocks.reshape(4 * K, N)


def _to_shards(mat, axis):
    K, N = mat.shape
    if axis == 1:
        return mat.reshape(K, 4, N // 4).transpose(1, 0, 2)
    return mat.reshape(4, K // 4, N)


_SMALL = (("gla_w_gate_up", 2), ("gla_gate_bias", None), ("gla_norm_g", None), ("ffn_conv_w", 2), ("ffn_conv_b", None),
          ("ln_g", 2), ("ln_b", 2))


def _pack_rows(arrs, width=LANE):
    flat = _pad_rows(jnp.concatenate([a.reshape(-1) for a in arrs]), 8 * width)
    return flat.reshape(-1, width)


def _unpack_rows(packed, shapes):
    flat, out, off = packed.reshape(-1), [], 0
    for s in shapes:
        n = 1
        for d in s:
            n *= d
        out.append(flat[off:off + n].reshape(s))
        off += n
    return out


def _gather_small_params(shards):
    x, y, c = lax.axis_index("x"), lax.axis_index("y"), lax.axis_index("c")
    names = [n for n, axis in _SMALL if axis is not None]
    mine = _pack_rows([shards[n] for n in names])
    mine = jnp.where(c == 0, mine, jnp.zeros_like(mine))
    rows = mine.shape[0]
    placed = lax.dynamic_update_slice(jnp.zeros((4 * rows, LANE), f32), mine, (_chip_index(x, y) * rows, 0))
    allp = _all_reduce_small(placed, "gather_small").reshape(4, rows, LANE)
    out = {n: shards[n] for n, axis in _SMALL if axis is None}
    per_chip = [_unpack_rows(allp[q], [shards[n].shape for n in names]) for q in range(4)]
    for i, n in enumerate(names):
        out[n] = jnp.concatenate([per_chip[q][i] for q in range(4)], axis=2)
    return out


def _reduce_small_grads(grads, shards):
    names = [n for n, _ in _SMALL]
    total = _all_reduce_small(_pack_rows([grads[n] for n in names]), "reduce_small")
    full = dict(zip(names, _unpack_rows(total, [grads[n].shape for n in names])))
    p = _chip_index(lax.axis_index("x"), lax.axis_index("y"))
    out = {}
    for n, axis in _SMALL:
        if axis is None:
            out[n] = full[n]
        else:
            w = shards[n].shape[axis]
            out[n] = lax.dynamic_slice_in_dim(full[n], p * w, w, axis=axis)
    return out


def _pad_cols(a, n):
    return a if a.shape[-1] == n else jnp.concatenate([a, jnp.zeros(a.shape[:-1] + (n - a.shape[-1],), a.dtype)], axis=-1)


def _ffn_width(F):
    return -(-F // 512) * 512


def kernel(x, gla_w_in, gla_w_gate_up, gla_gate_bias, gla_norm_g, gla_w_out, dil_w_in, dil_w_out, ffn_w_up, ffn_conv_w, ffn_conv_b, ffn_w_down, ln_g, ln_b, loss_target, m_gla_w_in, m_gla_w_gate_up, m_gla_gate_bias, m_gla_norm_g, m_gla_w_out, m_dil_w_in, m_dil_w_out, m_ffn_w_up, m_ffn_conv_w, m_ffn_conv_b, m_ffn_w_down, m_ln_g, m_ln_b, v_gla_w_in, v_gla_w_gate_up, v_gla_gate_bias, v_gla_norm_g, v_gla_w_out, v_dil_w_in, v_dil_w_out, v_ffn_w_up, v_ffn_conv_w, v_ffn_conv_b, v_ffn_w_down, v_ln_g, v_ln_b):
    names = ["gla_w_in", "gla_w_gate_up", "gla_gate_bias", "gla_norm_g", "gla_w_out", "dil_w_in", "dil_w_out", "ffn_w_up",
             "ffn_conv_w", "ffn_conv_b", "ffn_w_down", "ln_g", "ln_b"]
    w_sh = dict(zip(names, (gla_w_in, gla_w_gate_up, gla_gate_bias, gla_norm_g, gla_w_out, dil_w_in, dil_w_out, ffn_w_up,
                            ffn_conv_w, ffn_conv_b, ffn_w_down, ln_g, ln_b)))
    m_sh = dict(zip(names, (m_gla_w_in, m_gla_w_gate_up, m_gla_gate_bias, m_gla_norm_g, m_gla_w_out, m_dil_w_in, m_dil_w_out,
                            m_ffn_w_up, m_ffn_conv_w, m_ffn_conv_b, m_ffn_w_down, m_ln_g, m_ln_b)))
    v_sh = dict(zip(names, (v_gla_w_in, v_gla_w_gate_up, v_gla_gate_bias, v_gla_norm_g, v_gla_w_out, v_dil_w_in, v_dil_w_out,
                            v_ffn_w_up, v_ffn_conv_w, v_ffn_conv_b, v_ffn_w_down, v_ln_g, v_ln_b)))
    nseq, S, D = x.shape
    T = nseq * S
    small = _gather_small_params(w_sh)
    F = 4 * w_sh["ffn_w_down"].shape[1]
    Fp = _ffn_width(F)
    qkvr = 4 * w_sh["gla_w_in"].shape[2] - GLA_GATE_RANK
    c_idx = lax.axis_index("c")
    shard_axis = dict(_BIG)

    def pad_halves(a):
        return jnp.concatenate([_pad_cols(a[..., :F], Fp), _pad_cols(a[..., F:], Fp)], axis=-1)

    cw_all = pad_halves(small["ffn_conv_w"])
    cb_all = pad_halves(small["ffn_conv_b"][:, None, :])
    w_gate_up_all = jnp.pad(small["gla_w_gate_up"].astype(bf16), ((0, 0), (0, GATE_PAD - GLA_GATE_RANK), (0, 0)))

    def mixer_names(l):
        return ("gla_w_in", "gla_w_out") if l % 2 == 0 else ("dil_w_in", "dil_w_out")

    def shard_of(l, which):
        n_in, n_out = mixer_names(l)
        name, idx = {"w_in": (n_in, l // 2), "w_out": (n_out, l // 2), "w_up": ("ffn_w_up", l), "w_down": ("ffn_w_down", l)}[which]
        shard = w_sh[name][idx]
        return (shard.T if which == "w_up" else shard).astype(bf16)

    weights = [dict() for _ in range(DEPTH)]

    def install(l, which, gathered):
        n_in, n_out = mixer_names(l)
        if which == "w_in":
            weights[l]["w_in"] = w_in = _unshard(gathered, shard_axis[n_in])
            if l % 2 == 0:
                weights[l]["w_gate"] = _pad_cols(w_in[:, qkvr:], GATE_PAD)
        elif which == "w_out":
            weights[l]["w_out"] = _unshard(gathered, shard_axis[n_out])
        elif which == "w_up":
            wt = _unshard(gathered, 0)
            weights[l]["w_up_t"] = jnp.concatenate([_pad_rows(wt[:F], Fp), _pad_rows(wt[F:], Fp)], axis=0)
        else:
            weights[l]["w_down"] = _pad_rows(_unshard(gathered, shard_axis["ffn_w_down"]), Fp)

    kinds = ("w_in", "w_out", "w_up", "w_down")
    for which, gathered in zip(kinds, _run_copies(_gather_plan([shard_of(0, k) for k in kinds]), "gather_layer0")):
        install(0, which, gathered)

    riders = [{"proj": "w_down", "core": "w_up", "ffn_up": "w_in", "ffn_down": "w_out"},
              {"proj": "w_in", "ffn_up": "w_up", "conv": "w_down", "ffn_down": "w_out"}]

    def side_for(l, kernel):
        which = riders[l % 2].get(kernel)
        return _gather_plan([shard_of(l + 1, which)]) if which and l + 1 < DEPTH else None

    def carried(l, kernel, res, n_own=1):
        which = riders[l % 2].get(kernel)
        if which and l + 1 < DEPTH:
            install(l + 1, which, res[n_own])
            return res[0] if n_own == 1 else res[:n_own]
        return res

    h0 = x.reshape(T, D)
    saved = []
    cur, cur_b = h0, h0.astype(_MXU)
    fwd = dict(tm=1024, tn=1024, tk=2048)
    for i in range(DEPTH):
        j = i // 2
        tag = f"l{i}_"
        lg, lb = small["ln_g"][i], small["ln_b"][i]
        W = weights[i]
        if i % 2 == 0:
            gate_bias, norm_g = small["gla_gate_bias"][j][None, :], small["gla_norm_g"][j][None, :]
            proj = carried(i, "proj", _mm(cur_b, W["w_in"], "nn", tag + "gla_proj", n_out=qkvr, side=side_for(i, "proj"), **fwd))
            g_low = _mm(cur_b, W["w_gate"], "nn", tag + "gla_glow", tn=GATE_PAD)
            z = _mm(g_low, w_gate_up_all, "nn", tag + "gla_z", layer=j)
            gate = _log_gate_fwd(z, gate_bias, tag + "gla_gate")
            o, states = carried(i, "core", _gla_fwd(proj, gate, nseq, tag + "gla_core", side=side_for(i, "core")), n_own=2)
            y_b = _rms_gate_fwd(o, proj, norm_g, tag + "gla_norm")
            mix = _mm(y_b, W["w_out"], "nn", tag + "gla_out", **fwd)
            mixer_saved = (proj, g_low, z, gate, states, o, y_b)
        else:
            proj = carried(i, "proj", _mm(cur_b, W["w_in"], "nn", tag + "dil_proj", side=side_for(i, "proj"), **fwd))
            outs, lses = [], []
            for gi, (window, dilation) in enumerate(DIL_PATTERNS):
                og, lg_ = _dil_fwd(proj, gi, window, dilation, nseq, tag + f"dil_attn{gi}")
                outs.append(og)
                lses.append(lg_)
            y, y_b, lse_tot = _dil_mix_fwd(outs, lses, tag + "dil_mix")
            mix = _mm(y_b, W["w_out"], "nn", tag + "dil_out", **fwd)
            mixer_saved = (proj, y, y_b, lse_tot)
        x1, x1_b = _ln_fwd(cur, mix, lg[0:1], lb[0:1], tag + "ln1")
        cw, cb = cw_all[i], cb_all[i]
        hh = carried(i, "ffn_up", _mm(x1_b, W["w_up_t"], "nt", tag + "ffn_up", side=side_for(i, "ffn_up"), **fwd))
        act_b = carried(i, "conv", _conv_gate_fwd(hh, cw, cb, nseq, tag + "ffn_conv", side=side_for(i, "conv")))
        ffn = carried(i, "ffn_down", _mm(act_b, W["w_down"], "nn", tag + "ffn_down", tm=1024, tn=512, tk=Fp, side=side_for(i, "ffn_down")))
        x2, x2_b = _ln_fwd(x1, ffn, lg[1:2], lb[1:2], tag + "ln2")
        saved.append((cur, cur_b, mix, x1, x1_b, hh, act_b, ffn, mixer_saved))
        cur, cur_b = x2, x2_b

    dy, sq = _loss_head(cur, loss_target.reshape(T, D), "loss_head")
    loss = lax.psum(0.5 * jnp.sum(sq) / D, ("x", "y", "c"))

    gb = {n: [None] * w_sh[n].shape[0] for n in names}
    own_half = {n: [None] * w_sh[n].shape[0] for n, _ in _BIG}
    bwd_riders = {"down_dx": "w_down", "conv": "w_up", "gate_dw": "w_in", "up_dx": "w_out"}
    riding = {"layer": None, "partial": None}

    def grad_slot(l, which):
        n_in, n_out = mixer_names(l)
        return {"w_in": (n_in, l // 2), "w_out": (n_out, l // 2), "w_up": ("ffn_w_up", l), "w_down": ("ffn_w_down", l)}[which]

    def ride(kernel):
        return None if riding["layer"] is None else _exchange_plan([riding["partial"][bwd_riders[kernel]]])

    def landed(kernel, got):
        if riding["layer"] is not None:
            l, which = riding["layer"], bwd_riders[kernel]
            n, idx = grad_slot(l, which)
            own_half[n][idx] = _reduce_scatter_finish([riding["partial"][which]], got, f"scatter_l{l}_{which}")[0]

    def unwrap(kernel, res):
        if riding["layer"] is None:
            return res
        landed(kernel, res[1:])
        return res[0]

    d_res = None
    d_in = dy
    for i in reversed(range(DEPTH)):
        j = i // 2
        tag = f"l{i}_b_"
        xin, xin_b, mix, x1, x1_b, hh, act_b, ffn, mixer_saved = saved[i]
        lg = small["ln_g"][i]
        cw, cb = cw_all[i], cb_all[i]
        dw_tiles = dict(tm=1024, tn=1024, tk=4096, out_dtype=bf16)
        dx_tiles = dict(tm=1024, tn=512, tk=6144)
        dys, scales = ([d_in], [1.0]) if d_res is None else ([d_res, d_in], [DEEPNORM_ALPHA, 1.0])
        du2, du2_b, dg2, db2 = _ln_bwd(x1, ffn, lg[1:2], dys, scales, tag + "ln2")
        gb["ffn_w_down"][i] = _mm(act_b, du2_b, "tn", tag + "ffn_down_dw", **dw_tiles)[:F]
        W = weights[i]
        dact = unwrap("down_dx", _mm(du2_b, W["w_down"], "nt", tag + "ffn_down_dx", side=ride("down_dx"), **dx_tiles))
        dhg, dhu, dcw, dcb, rode = _conv_gate_bwd(hh, dact, cw, cb, nseq, tag + "ffn_conv", side=ride("conv"))
        landed("conv", rode)
        gb["ffn_conv_w"][i] = jnp.concatenate([dcw[:, :F], dcw[:, Fp:Fp + F]], axis=1)
        gb["ffn_conv_b"][i] = jnp.concatenate([dcb[0, :F], dcb[0, Fp:Fp + F]], axis=0)
        dwg_t = unwrap("gate_dw", _mm(dhg, x1_b, "tn", tag + "ffn_gate_dw", side=ride("gate_dw"), **dw_tiles))
        dwu_t = _mm(dhu, x1_b, "tn", tag + "ffn_up_dw", **dw_tiles)
        gb["ffn_w_up"][i] = jnp.concatenate([dwg_t[:F], dwu_t[:F]], axis=0)
        dx1g = _mm(dhg, W["w_up_t"], "nn", tag + "ffn_gate_dx", **dx_tiles)
        dx1u = unwrap("up_dx", _mm(dhu, W["w_up_t"], "nn", tag + "ffn_up_dx", b_k0=Fp, side=ride("up_dx"), **dx_tiles))
        du1, du1_b, dg1, db1 = _ln_bwd(xin, mix, lg[0:1], [du2, dx1g, dx1u], [DEEPNORM_ALPHA, 1.0, 1.0], tag + "ln1")
        gb["ln_g"][i] = jnp.concatenate([dg1, dg2], axis=0)
        gb["ln_b"][i] = jnp.concatenate([db1, db2], axis=0)
        if i % 2 == 0:
            proj, g_low, z, gate, states, o, y_b = mixer_saved
            gate_bias, norm_g = small["gla_gate_bias"][j][None, :], small["gla_norm_g"][j][None, :]
            gb["gla_w_out"][j] = _mm(y_b, du1_b, "tn", tag + "gla_out_dw", **dw_tiles)
            dyy = _mm(du1_b, W["w_out"], "nt", tag + "gla_out_dx", **dx_tiles)
            do, dr, dng = _rms_gate_bwd(o, proj, norm_g, dyy, tag + "gla_norm")
            gb["gla_norm_g"][j] = dng[0]
            dq, dk_, dv_, dgate = _gla_bwd(proj, gate, states, do, nseq, tag + "gla_core")
            dz, dbias = _log_gate_bwd(z, gate_bias, dgate, tag + "gla_gate")
            gb["gla_gate_bias"][j] = dbias[0]
            gb["gla_w_gate_up"][j] = _mm(g_low, dz, "tn", tag + "gla_z_dw", tk=1024)[:GLA_GATE_RANK]
            dg_low = _mm(dz, w_gate_up_all, "nt", tag + "gla_z_dx", tn=GATE_PAD, layer=j)
            dproj = jnp.concatenate([dq, dk_, dv_, dr], axis=1)
            dw_main = _mm(xin_b, dproj, "tn", tag + "gla_proj_dw", **dw_tiles)
            dw_gate = _mm(xin_b, dg_low, "tn", tag + "gla_glow_dw", tm=1024, tn=GATE_PAD, tk=2048, out_dtype=bf16)[:, :GLA_GATE_RANK]
            gb["gla_w_in"][j] = jnp.concatenate([dw_main, dw_gate], axis=1)
            dxa = _mm(dproj, W["w_in"], "nt", tag + "gla_proj_dx", **dx_tiles)
            dxb = _mm(dg_low, W["w_gate"], "nt", tag + "gla_glow_dx")
            d_in = _axpy(dxa, dxb, 1.0, tag + "gla_dx_sum")
        else:
            proj, y, y_b, lse_tot = mixer_saved
            gb["dil_w_out"][j] = _mm(y_b, du1_b, "tn", tag + "dil_out_dw", **dw_tiles)
            dyy = _mm(du1_b, W["w_out"], "nt", tag + "dil_out_dx", **dx_tiles)
            delta = _dil_delta(dyy, y, tag + "dil_delta")
            pieces = []
            for gi, (window, dilation) in enumerate(DIL_PATTERNS):
                pieces += list(_dil_bwd(proj, dyy, lse_tot, delta, gi, window, dilation, nseq, tag + f"dil_attn{gi}"))
            dproj = jnp.concatenate(pieces, axis=1).astype(_MXU)
            gb["dil_w_in"][j] = _mm(xin_b, dproj, "tn", tag + "dil_proj_dw", **dw_tiles)
            d_in = _mm(dproj, W["w_in"], "nt", tag + "dil_proj_dx", **dx_tiles)
        d_res = du1
        local = [_to_shards(gb[n][idx], 0 if n == "ffn_w_up" else shard_axis[n]) for n, idx in (grad_slot(i, k) for k in kinds)]
        riding["layer"], riding["partial"] = i, dict(zip(kinds, _reduce_scatter_start(local, f"scatter_l{i}")))
    grad_x = _axpy(d_res, d_in, DEEPNORM_ALPHA, "grad_x").reshape(x.shape)

    last = [riding["partial"][k] for k in kinds]
    for k, own in zip(kinds, _reduce_scatter_finish(last, _run_copies(_exchange_plan(last), "scatter_l0_ici"), "scatter_l0")):
        n, idx = grad_slot(0, k)
        own_half[n][idx] = own
    own_all = [jnp.stack(own_half[n], axis=0) for n, _ in _BIG]
    other_all = _sibling_swap(own_all, "scatter_swap")
    grads, delta, new_m, new_v = {}, {}, {}, {}
    for (n, _), own, other in zip(_BIG, own_all, other_all):
        view = (lambda a: jnp.swapaxes(a, 1, 2)) if n == "ffn_w_up" else (lambda a: a)
        res = _adamw_halves(view(w_sh[n]), view(m_sh[n]), view(v_sh[n]), own, other, c_idx, "adamw_" + n)
        grads[n], delta[n], new_m[n], new_v[n] = [view(r) for r in res]
    grads.update(_reduce_small_grads({n: jnp.stack(gb[n], axis=0) for n, _ in _SMALL}, w_sh))
    small_names = [n for n, _ in _SMALL]
    packed = [_pack_rows([src[n] for n in small_names]) for src in (w_sh, grads, m_sh, v_sh)]
    res = _adamw(*packed, "adamw_small")
    shapes = [w_sh[n].shape for n in small_names]
    for dst, arr in zip((delta, new_m, new_v), res):
        dst.update(dict(zip(small_names, _unpack_rows(arr, shapes))))

    return (loss, grad_x, *[grads[n] for n in names], *[delta[n] for n in names], *[new_m[n] for n in names],
            *[new_v[n] for n in names])
```

```python
import functools

import jax
import jax.numpy as jnp
from jax import lax
from jax.experimental import pallas as pl
from jax.experimental.pallas import tpu as pltpu

f32 = jnp.float32
bf16 = jnp.bfloat16
_MXU = jnp.bfloat16

DEPTH = 4
GLA_HEADS = 4
GLA_GATE_RANK = 16
GLA_GATE_NORMALIZER = 16.0
GLA_CHUNK = 64
GLA_SUB = 16
DIL_PATTERNS = ((128, 1), (512, 4), (2048, 16))
DIL_HEADS = 8
DIL_HEAD_DIM = 128
DIL_BLOCK = 128
DEEPNORM_ALPHA = (2 * DEPTH) ** 0.25
LN_EPS = 1e-5
RMS_EPS = 1e-6
ADAM_LR = 0.001
ADAM_B1 = 0.9
ADAM_B2 = 0.999
ADAM_EPS = 1e-08
ADAM_WD = 0.01
ADAM_STEP = 10

LANE = 128
VMEM_LIMIT = 48 * 1024 * 1024
GATE_PAD = LANE
MESH = pl.DeviceIdType.MESH


def _cparams(sem=None):
    return pltpu.CompilerParams(dimension_semantics=sem, vmem_limit_bytes=VMEM_LIMIT)


def _div_tile(n, pref, unit):
    if n <= pref:
        return n
    best = None
    for t in range(unit, pref + 1, unit):
        if n % t == 0:
            best = t
    assert best is not None, (n, pref, unit)
    return best


def _dot(a, b, ca, cb):
    return lax.dot_general(a.astype(_MXU), b.astype(_MXU), (((ca,), (cb,)), ((), ())), preferred_element_type=f32)


def _nn(a, b):
    return _dot(a, b, 1, 0)


def _nt(a, b):
    return _dot(a, b, 1, 1)


def _tn(a, b):
    return _dot(a, b, 0, 0)


def _exact_dot(a, b):
    return jnp.dot(a, b, precision=lax.Precision.HIGHEST, preferred_element_type=f32)


def _sigmoid(x):
    return 1.0 / (1.0 + jnp.exp(-x))


def _call_with_side(body, name, grid, in_specs, out_specs, out_shape, scratch, sem, args, side):
    if side is None:
        return pl.pallas_call(body, name=name, grid=grid, in_specs=in_specs, out_specs=out_specs, out_shape=out_shape,
                              scratch_shapes=scratch, compiler_params=_cparams(sem))(*args)
    n_in, n_out, n_scr = len(in_specs), len(out_specs), len(scratch)
    n_si, n_so = len(side.ins), len(side.outs)

    def wrapped(*refs):
        ins, s_in = refs[:n_in], refs[n_in:n_in + n_si]
        outs, s_out = refs[n_in + n_si:n_in + n_si + n_out], refs[n_in + n_si + n_out:n_in + n_si + n_out + n_so]
        rest = refs[n_in + n_si + n_out + n_so:]
        sems = rest[n_scr:]
        ids = [pl.program_id(ax) for ax in range(len(grid))]
        first = functools.reduce(lambda u, v: u & v, [i == 0 for i in ids])
        last = functools.reduce(lambda u, v: u & v, [i == g - 1 for i, g in zip(ids, grid)])

        @pl.when(first)
        def _():
            side.start(s_in, s_out, *sems)

        body(*ins, *outs, *rest[:n_scr])

        @pl.when(last)
        def _():
            side.finish(s_in, s_out, *sems)

    return pl.pallas_call(
        wrapped, name=name, grid=grid, in_specs=list(in_specs) + [_ANY] * n_si, out_specs=list(out_specs) + [_ANY] * n_so,
        out_shape=list(out_shape) + list(side.outs), scratch_shapes=list(scratch) + list(side.scratch),
        compiler_params=_cparams(("arbitrary",) * len(grid)))(*args, *side.ins)


def _mm(a, b, mode, name, tm=1024, tn=512, tk=2048, out_dtype=f32, layer=None, n_out=None, b_k0=0, side=None):
    if mode == "nn":
        (M, K), N = a.shape, (n_out or b.shape[-1])
    elif mode == "nt":
        (M, K), N = a.shape, b.shape[-2]
    else:
        (K, M), N = a.shape, b.shape[-1]
    tm, tn, tk = _div_tile(M, tm, LANE), _div_tile(N, tn, LANE), _div_tile(K, tk, LANE)
    nk = K // tk
    if mode == "tn":
        a_spec = pl.BlockSpec((tk, tm), lambda i, j, k: (k, i))
    else:
        a_spec = pl.BlockSpec((tm, tk), lambda i, j, k: (i, k))
    lead = () if layer is None else (None,)
    pre = () if layer is None else (layer,)
    k0 = b_k0 // tk
    assert k0 * tk == b_k0 and (mode != "tn" or b_k0 == 0)
    if mode == "nt":
        b_spec = pl.BlockSpec(lead + (tn, tk), lambda i, j, k: pre + (j, k + k0))
    else:
        b_spec = pl.BlockSpec(lead + (tk, tn), lambda i, j, k: pre + (k + k0, j))
    ca, cb = {"nn": (1, 0), "nt": (1, 1), "tn": (0, 0)}[mode]

    def body(a_ref, b_ref, o_ref, *acc):
        p = _dot(a_ref[...], b_ref[...], ca, cb)
        if nk == 1:
            o_ref[...] = p.astype(o_ref.dtype)
        else:
            k = pl.program_id(2)
            acc_ref = acc[0]

            @pl.when(k == 0)
            def _():
                acc_ref[...] = p

            @pl.when(k > 0)
            def _():
                acc_ref[...] += p

            @pl.when(k == nk - 1)
            def _():
                o_ref[...] = acc_ref[...].astype(o_ref.dtype)

    res = _call_with_side(
        body, name, (M // tm, N // tn, nk), [a_spec, b_spec], [pl.BlockSpec((tm, tn), lambda i, j, k: (i, j))],
        [jax.ShapeDtypeStruct((M, N), out_dtype)], [pltpu.VMEM((tm, tn), f32)] if nk > 1 else [],
        ("parallel", "parallel", "arbitrary"), (a, b), side)
    return res if side else res[0]


def _rowwise(fn, rows, consts, outs, reds, name, tm=256):
    T = rows[0][0].shape[0]
    tm = _div_tile(T, tm, 8)
    n_r, n_c, n_o = len(rows), len(consts), len(outs)

    def body(*refs):
        ins = [r[...] for r in refs[: n_r + n_c]]
        res = fn(*ins)
        res = res if isinstance(res, (tuple, list)) else (res,)
        o_refs = refs[n_r + n_c: n_r + n_c + n_o]
        r_refs = refs[n_r + n_c + n_o:]
        for ref, val in zip(o_refs, res[:n_o]):
            ref[...] = val.astype(ref.dtype)
        i = pl.program_id(0)
        for ref, val in zip(r_refs, res[n_o:]):
            _accumulate(ref, val, i)

    in_specs = [pl.BlockSpec((tm, w), functools.partial(lambda i, cb: (i, cb), cb=cb)) for (_, w, cb) in rows]
    in_specs += [pl.BlockSpec(c.shape, lambda i: (0, 0)) for c in consts]
    out_specs = [pl.BlockSpec((tm, w), lambda i: (i, 0)) for (w, _) in outs]
    out_specs += [pl.BlockSpec((1, w), lambda i: (0, 0)) for w in reds]
    out_shape = [jax.ShapeDtypeStruct((T, w), dt) for (w, dt) in outs]
    out_shape += [jax.ShapeDtypeStruct((1, w), f32) for w in reds]
    return pl.pallas_call(
        body, name=name, grid=(T // tm,), in_specs=in_specs, out_specs=out_specs, out_shape=out_shape,
        compiler_params=_cparams(("arbitrary",)),
    )(*[r[0] for r in rows], *consts)


def _accumulate(ref, val, step):
    @pl.when(step == 0)
    def _():
        ref[...] = val

    @pl.when(step > 0)
    def _():
        ref[...] += val


def _full(a):
    return (a, a.shape[1], 0)


def _colsum(x):
    return jnp.sum(x, axis=0, keepdims=True)


def _ln_stats(u):
    mu = jnp.mean(u, axis=-1, keepdims=True)
    xc = u - mu
    var = jnp.mean(xc * xc, axis=-1, keepdims=True)
    rstd = lax.rsqrt(var + LN_EPS)
    return xc * rstd, rstd


def _ln_fwd(x, f, g, b, name):
    def fn(x, f, g, b):
        xhat, _ = _ln_stats(DEEPNORM_ALPHA * x + f)
        y = xhat * g + b
        return y, y

    return _rowwise(fn, [_full(x), _full(f)], [g, b], [(x.shape[1], f32), (x.shape[1], _MXU)], [], name)


def _ln_bwd(x, f, g, dys, scales, name):
    def fn(x, f, *rest):
        g = rest[-1]
        dy = None
        for d, s in zip(rest[:-1], scales):
            t = d if s == 1.0 else s * d
            dy = t if dy is None else dy + t
        xhat, rstd = _ln_stats(DEEPNORM_ALPHA * x + f)
        dxh = dy * g
        m1 = jnp.mean(dxh, axis=-1, keepdims=True)
        m2 = jnp.mean(dxh * xhat, axis=-1, keepdims=True)
        du = rstd * (dxh - m1 - xhat * m2)
        return du, du, _colsum(dy * xhat), _colsum(dy)

    D = x.shape[1]
    return _rowwise(fn, [_full(x), _full(f)] + [_full(d) for d in dys], [g], [(D, f32), (D, _MXU)], [D, D], name)


def _loss_head(y, t, name):
    D = y.shape[1]

    def fn(y, t):
        e = y - t
        return e * (1.0 / D), _colsum(e * e)

    return _rowwise(fn, [_full(y), _full(t)], [], [(D, f32)], [D], name)


def _axpy(a, b, alpha, name):
    def fn(a, b):
        return alpha * a + b

    return _rowwise(fn, [_full(a), _full(b)], [], [(a.shape[1], f32)], [], name)[0]


def _shift_down(x, k):
    row = lax.broadcasted_iota(jnp.int32, x.shape, 0)
    return jnp.where(row >= k, pltpu.roll(x, k, 0), 0.0)


def _shift_up(x, k):
    S = x.shape[0]
    row = lax.broadcasted_iota(jnp.int32, x.shape, 0)
    return jnp.where(row < S - k, pltpu.roll(x, S - k, 0), 0.0)


def _causal_conv(h, w, b):
    return ((b + w[0:1] * _shift_down(h, 2)) + w[1:2] * _shift_down(h, 1)) + w[2:3] * h


def _conv_gate_fwd(h, cw, cb, nseq, name, tc=256, side=None):
    T, F2 = h.shape
    F, S = F2 // 2, T // nseq
    tc = _div_tile(F, tc, LANE)
    nf = F // tc

    def body(hg_ref, hu_ref, wg_ref, wu_ref, bg_ref, bu_ref, a_ref):
        cg = _causal_conv(hg_ref[...], wg_ref[...], bg_ref[...])
        cu = _causal_conv(hu_ref[...], wu_ref[...], bu_ref[...])
        a_ref[...] = (cg * _sigmoid(cg) * cu).astype(a_ref.dtype)

    res = _call_with_side(
        body, name, (nseq, nf),
        [pl.BlockSpec((S, tc), lambda s, j: (s, j)), pl.BlockSpec((S, tc), lambda s, j: (s, nf + j)),
         pl.BlockSpec((3, tc), lambda s, j: (0, j)), pl.BlockSpec((3, tc), lambda s, j: (0, nf + j)),
         pl.BlockSpec((1, tc), lambda s, j: (0, j)), pl.BlockSpec((1, tc), lambda s, j: (0, nf + j))],
        [pl.BlockSpec((S, tc), lambda s, j: (s, j))], [jax.ShapeDtypeStruct((T, F), _MXU)], [],
        ("parallel", "parallel"), (h, h, cw, cw, cb, cb), side)
    return res if side else res[0]


def _conv_gate_bwd(h, da, cw, cb, nseq, name, tc=128, side=None):
    T, F2 = h.shape
    F, S = F2 // 2, T // nseq
    tc = _div_tile(F, tc, LANE)
    nf = F // tc

    def conv_bwd(dc, hx, w):
        dh = (w[2:3] * dc + w[1:2] * _shift_up(dc, 1)) + w[0:1] * _shift_up(dc, 2)
        dw = jnp.concatenate([_colsum(dc * _shift_down(hx, 2)), _colsum(dc * _shift_down(hx, 1)), _colsum(dc * hx)], axis=0)
        return dh, dw, _colsum(dc)

    def body(hg_ref, hu_ref, da_ref, wg_ref, wu_ref, bg_ref, bu_ref, dhg_ref, dhu_ref, dwg_ref, dwu_ref, dbg_ref, dbu_ref):
        hg, hu, da = hg_ref[...], hu_ref[...], da_ref[...]
        wg, wu = wg_ref[...], wu_ref[...]
        cg = _causal_conv(hg, wg, bg_ref[...])
        cu = _causal_conv(hu, wu, bu_ref[...])
        sg = _sigmoid(cg)
        dcu = da * (cg * sg)
        dcg = da * cu * (sg * (1.0 + cg * (1.0 - sg)))
        dhg, dwg, dbg = conv_bwd(dcg, hg, wg)
        dhu, dwu, dbu = conv_bwd(dcu, hu, wu)
        dhg_ref[...] = dhg.astype(dhg_ref.dtype)
        dhu_ref[...] = dhu.astype(dhu_ref.dtype)
        s = pl.program_id(1)
        _accumulate(dwg_ref, dwg, s)
        _accumulate(dwu_ref, dwu, s)
        _accumulate(dbg_ref, dbg, s)
        _accumulate(dbu_ref, dbu, s)

    col = lambda j, s: (s, j)
    par = lambda j, s: (0, j)
    dhg, dhu, dwg, dwu, dbg, dbu, *side_res = _call_with_side(
        body, name, (nf, nseq),
        [pl.BlockSpec((S, tc), col), pl.BlockSpec((S, tc), lambda j, s: (s, nf + j)), pl.BlockSpec((S, tc), col),
         pl.BlockSpec((3, tc), par), pl.BlockSpec((3, tc), lambda j, s: (0, nf + j)),
         pl.BlockSpec((1, tc), par), pl.BlockSpec((1, tc), lambda j, s: (0, nf + j))],
        [pl.BlockSpec((S, tc), col), pl.BlockSpec((S, tc), col), pl.BlockSpec((3, tc), par), pl.BlockSpec((3, tc), par),
         pl.BlockSpec((1, tc), par), pl.BlockSpec((1, tc), par)],
        [jax.ShapeDtypeStruct((T, F), _MXU), jax.ShapeDtypeStruct((T, F), _MXU), jax.ShapeDtypeStruct((3, F), f32),
         jax.ShapeDtypeStruct((3, F), f32), jax.ShapeDtypeStruct((1, F), f32), jax.ShapeDtypeStruct((1, F), f32)],
        [], ("parallel", "arbitrary"), (h, h, da, cw, cw, cb, cb), side)
    return dhg, dhu, jnp.concatenate([dwg, dwu], axis=1), jnp.concatenate([dbg, dbu], axis=1), side_res


def _group_row(x, jj):
    C, d = x.shape
    n = C // GLA_SUB
    x3 = x.reshape(n, GLA_SUB, d)
    return jnp.broadcast_to(x3[:, jj:jj + 1, :], (n, GLA_SUB, d)).reshape(C, d)


def _group_sum(x):
    C, d = x.shape
    n = C // GLA_SUB
    s = jnp.sum(x.reshape(n, GLA_SUB, d), axis=1, keepdims=True)
    return jnp.broadcast_to(s, (n, GLA_SUB, d)).reshape(C, d)


def _chunk_cumsum(g):
    C = g.shape[0]
    row = lax.broadcasted_iota(jnp.int32, (C, C), 0)
    col = lax.broadcasted_iota(jnp.int32, (C, C), 1)
    return _exact_dot((row >= col).astype(f32), g)


def _chunk_suffix_sum(x):
    C = x.shape[0]
    row = lax.broadcasted_iota(jnp.int32, (C, C), 0)
    col = lax.broadcasted_iota(jnp.int32, (C, C), 1)
    return _exact_dot((col >= row).astype(f32), x)


def _gla_scores(q, k, b):
    C = q.shape[0]
    n = C // GLA_SUB
    row = lax.broadcasted_iota(jnp.int32, (C, C), 0)
    col = lax.broadcasted_iota(jnp.int32, (C, C), 1)
    blocks = [jnp.zeros((GLA_SUB, C), f32)]
    for s in range(1, n):
        lo = s * GLA_SUB
        bref = b[lo - 1:lo, :]
        qr = q[lo:lo + GLA_SUB] * jnp.exp(b[lo:lo + GLA_SUB] - bref)
        kr = k * jnp.exp(jnp.minimum(bref - b, 0.0))
        blocks.append(_nt(qr, kr))
    sub_start = (row // GLA_SUB) * GLA_SUB
    a = jnp.where(col < sub_start, jnp.concatenate(blocks, axis=0), 0.0)
    rin = lax.broadcasted_iota(jnp.int32, (C, 1), 0) % GLA_SUB
    for jj in range(GLA_SUB):
        e = jnp.exp(jnp.minimum(b - _group_row(b, jj), 0.0))
        colv = jnp.sum(q * _group_row(k, jj) * e, axis=1, keepdims=True)
        colv = jnp.where(rin >= jj, colv, 0.0)
        a = jnp.where(col == sub_start + jj, colv, a)
    return a


def _gla_scores_bwd(da, q, k, b):
    C = q.shape[0]
    n = C // GLA_SUB
    row = lax.broadcasted_iota(jnp.int32, (C, C), 0)
    col = lax.broadcasted_iota(jnp.int32, (C, C), 1)
    sub_start = (row // GLA_SUB) * GLA_SUB
    da_inter = jnp.where(col < sub_start, da, 0.0)
    dq_blocks = [jnp.zeros((GLA_SUB, q.shape[1]), f32)]
    dk = jnp.zeros_like(k)
    for s in range(1, n):
        lo = s * GLA_SUB
        bref = b[lo - 1:lo, :]
        eq = jnp.exp(b[lo:lo + GLA_SUB] - bref)
        ek = jnp.exp(jnp.minimum(bref - b, 0.0))
        das = da_inter[lo:lo + GLA_SUB]
        dq_blocks.append(_nn(das, k * ek) * eq)
        dk = dk + _tn(das, q[lo:lo + GLA_SUB] * eq) * ek
    dq = jnp.concatenate(dq_blocks, axis=0)
    rin = lax.broadcasted_iota(jnp.int32, (C, 1), 0) % GLA_SUB
    for jj in range(GLA_SUB):
        e = jnp.exp(jnp.minimum(b - _group_row(b, jj), 0.0))
        dac = jnp.sum(jnp.where(col == sub_start + jj, da, 0.0), axis=1, keepdims=True)
        dac = jnp.where(rin >= jj, dac, 0.0)
        w = dac * e
        dq = dq + w * _group_row(k, jj)
        dk = dk + jnp.where(rin == jj, _group_sum(w * q), 0.0)
    return dq, dk


def _gla_specs(nC, dk, dv):
    H = GLA_HEADS
    voff = (2 * H * dk) // dv
    assert voff * dv == 2 * H * dk
    return H, voff


def _gla_fwd(proj, gate, nseq, name, side=None):
    T = proj.shape[0]
    dk = gate.shape[1] // GLA_HEADS
    dv = 2 * dk
    C = GLA_CHUNK
    nC = T // nseq // C
    H, voff = _gla_specs(nC, dk, dv)
    scale = dk ** -0.5

    def body(q_ref, k_ref, v_ref, g_ref, o_ref, st_ref, state):
        c = pl.program_id(2)

        @pl.when(c == 0)
        def _():
            state[...] = jnp.zeros_like(state)

        q, k, v = q_ref[...] * scale, k_ref[...], v_ref[...]
        b = _chunk_cumsum(g_ref[...])
        st = state[...]
        st_ref[0] = st
        a = _gla_scores(q, k, b)
        o_ref[...] = _nt(q * jnp.exp(b), st) + _nn(a, v)
        bl = b[C - 1:C, :]
        state[...] = st * jnp.exp(bl) + _tn(v, k * jnp.exp(bl - b))

    row = lambda s, h, c: s * nC + c
    return _call_with_side(
        body, name, (nseq, H, nC),
        [pl.BlockSpec((C, dk), lambda s, h, c: (row(s, h, c), h)),
         pl.BlockSpec((C, dk), lambda s, h, c: (row(s, h, c), H + h)),
         pl.BlockSpec((C, dv), lambda s, h, c: (row(s, h, c), voff + h)),
         pl.BlockSpec((C, dk), lambda s, h, c: (row(s, h, c), h))],
        [pl.BlockSpec((C, dv), lambda s, h, c: (row(s, h, c), h)),
         pl.BlockSpec((1, dv, dk), lambda s, h, c: ((s * H + h) * nC + c, 0, 0))],
        [jax.ShapeDtypeStruct((T, H * dv), f32), jax.ShapeDtypeStruct((nseq * H * nC, dv, dk), f32)],
        [pltpu.VMEM((dv, dk), f32)], ("parallel", "parallel", "arbitrary"), (proj, proj, proj, gate), side)


def _gla_bwd(proj, gate, states, do, nseq, name):
    T = proj.shape[0]
    dk = gate.shape[1] // GLA_HEADS
    dv = 2 * dk
    C = GLA_CHUNK
    nC = T // nseq // C
    H, voff = _gla_specs(nC, dk, dv)
    scale = dk ** -0.5

    def body(q_ref, k_ref, v_ref, g_ref, do_ref, st_ref, dq_ref, dk_ref, dv_ref, dg_ref, dstate, term):
        c = pl.program_id(2)

        @pl.when(c == 0)
        def _():
            dstate[...] = jnp.zeros_like(dstate)
            term[...] = jnp.zeros_like(term)

        q, k, v, do = q_ref[...] * scale, k_ref[...], v_ref[...], do_ref[...]
        b = _chunk_cumsum(g_ref[...])
        st = st_ref[0]
        dst = dstate[...]
        eb = jnp.exp(b)
        bl = b[C - 1:C, :]
        kdec = jnp.exp(bl - b)
        a = _gla_scores(q, k, b)
        rowi = lax.broadcasted_iota(jnp.int32, (C, C), 0)
        coli = lax.broadcasted_iota(jnp.int32, (C, C), 1)
        da = jnp.where(coli <= rowi, _nt(do, v), 0.0)
        dq_s, dk_s = _gla_scores_bwd(da, q, k, b)
        dq = _nn(do, st) * eb + dq_s
        dkk = _nn(v, dst) * kdec + dk_s
        dv_ref[...] = (_tn(a, do) + _nt(k * kdec, dst)).astype(dv_ref.dtype)
        last = lax.broadcasted_iota(jnp.int32, (C, 1), 0) == C - 1
        db = q * dq - k * dkk + jnp.where(last, term[...], 0.0)
        dg_ref[...] = _chunk_suffix_sum(db)
        dq_ref[...] = (dq * scale).astype(dq_ref.dtype)
        dk_ref[...] = dkk.astype(dk_ref.dtype)
        dprev = dst * jnp.exp(bl) + _tn(do, q * eb)
        dstate[...] = dprev
        term[...] = _colsum(st * dprev)

    row = lambda s, h, c: s * nC + (nC - 1 - c)
    kspec = lambda off: pl.BlockSpec((C, dk), lambda s, h, c: (row(s, h, c), off + h))
    vspec = lambda off: pl.BlockSpec((C, dv), lambda s, h, c: (row(s, h, c), off + h))
    return pl.pallas_call(
        body, name=name, grid=(nseq, H, nC),
        in_specs=[kspec(0), kspec(H), vspec(voff), kspec(0), vspec(0),
                  pl.BlockSpec((1, dv, dk), lambda s, h, c: ((s * H + h) * nC + (nC - 1 - c), 0, 0))],
        out_specs=[kspec(0), kspec(0), vspec(0), kspec(0)],
        out_shape=[jax.ShapeDtypeStruct((T, H * dk), _MXU), jax.ShapeDtypeStruct((T, H * dk), _MXU),
                   jax.ShapeDtypeStruct((T, H * dv), _MXU), jax.ShapeDtypeStruct((T, H * dk), f32)],
        scratch_shapes=[pltpu.VMEM((dv, dk), f32), pltpu.VMEM((1, dk), f32)],
        compiler_params=_cparams(("parallel", "parallel", "arbitrary")),
    )(proj, proj, proj, gate, do, states)


def _head_slices(width, n):
    w = width // n
    return [slice(h * w, (h + 1) * w) for h in range(n)]


def _rms_gate_fwd(o, proj, ng, name):
    W = o.shape[1]

    def fn(o, r, ng):
        parts = []
        for sl in _head_slices(W, GLA_HEADS):
            oh = o[:, sl]
            rstd = lax.rsqrt(jnp.mean(oh * oh, axis=-1, keepdims=True) + RMS_EPS)
            rh = r[:, sl]
            parts.append((oh * rstd * ng) * (rh * _sigmoid(rh)))
        return jnp.concatenate(parts, axis=1)

    return _rowwise(fn, [_full(o), (proj, W, 2)], [ng], [(W, _MXU)], [], name)[0]


def _rms_gate_bwd(o, proj, ng, dy, name):
    W = o.shape[1]

    def fn(o, r, dy, ng):
        dos, drs = [], []
        dng = jnp.zeros((1, W // GLA_HEADS), f32)
        for sl in _head_slices(W, GLA_HEADS):
            oh, rh, dyh = o[:, sl], r[:, sl], dy[:, sl]
            rstd = lax.rsqrt(jnp.mean(oh * oh, axis=-1, keepdims=True) + RMS_EPS)
            ohat = oh * rstd
            sg = _sigmoid(rh)
            don = dyh * (rh * sg)
            drs.append(dyh * (ohat * ng) * (sg * (1.0 + rh * (1.0 - sg))))
            dng = dng + _colsum(don * ohat)
            dohat = don * ng
            dos.append(rstd * (dohat - ohat * jnp.mean(dohat * ohat, axis=-1, keepdims=True)))
        return jnp.concatenate(dos, axis=1), jnp.concatenate(drs, axis=1), dng

    return _rowwise(fn, [_full(o), (proj, W, 2), _full(dy)], [ng], [(W, _MXU), (W, _MXU)], [W // GLA_HEADS], name)


def _log_gate_fwd(z, bias, name):
    def fn(z, bias):
        t = z + bias
        return (jnp.minimum(t, 0.0) - jnp.log1p(jnp.exp(-jnp.abs(t)))) * (1.0 / GLA_GATE_NORMALIZER)

    return _rowwise(fn, [_full(z)], [bias], [(z.shape[1], f32)], [], name)[0]


def _log_gate_bwd(z, bias, dg, name):
    def fn(z, dg, bias):
        dz = dg * (1.0 / GLA_GATE_NORMALIZER) * _sigmoid(-(z + bias))
        return dz, _colsum(dz)

    return _rowwise(fn, [_full(z), _full(dg)], [bias], [(z.shape[1], f32)], [z.shape[1]], name)


def _band_masks(P, steps, has_prev):
    i = lax.broadcasted_iota(jnp.int32, (P, P), 0)
    j = lax.broadcasted_iota(jnp.int32, (P, P), 1)
    cur = (i - j >= 0) & (i - j <= steps)
    prev = (i + P - j <= steps) & has_prev
    return cur, prev


def _dil_dims(T, nseq, dilation):
    L = T // nseq // dilation
    P = min(DIL_BLOCK, L)
    return L, P, L // P


def _dil_tiling(T, nseq, dilation):
    L, P, nb = _dil_dims(T, nseq, dilation)
    hb = DIL_HEADS if dilation == 1 else 1
    row_sets = [pl.ds(r, P, stride=dilation) if dilation > 1 else pl.ds(0, P) for r in range(dilation)]
    head_cols = _head_slices(hb * DIL_HEAD_DIM, hb)
    return L, P, nb, P * dilation, hb, DIL_HEADS // hb, row_sets, head_cols


def _dil_fwd(proj, gi, window, dilation, nseq, name, side=None):
    T = proj.shape[0]
    H, dh = DIL_HEADS, DIL_HEAD_DIM
    L, P, nb, SB, hb, ng, row_sets, head_cols = _dil_tiling(T, nseq, dilation)
    steps = window // dilation
    scale = dh ** -0.5

    def body(q_ref, kc_ref, kp_ref, vc_ref, vp_ref, o_ref, lse_ref):
        sb, hg = pl.program_id(1), pl.program_id(2)
        mc, mp = _band_masks(P, steps, sb > 0)
        lane = lax.broadcasted_iota(jnp.int32, (P, LANE), 1)

        @pl.when(hg == 0)
        def _():
            lse_ref[...] = jnp.zeros_like(lse_ref)

        for hh, cols in enumerate(head_cols):
            for rows in row_sets:
                q = q_ref[rows, cols]
                sc = jnp.where(mc, _nt(q, kc_ref[rows, cols]) * scale, -jnp.inf)
                sp = jnp.where(mp, _nt(q, kp_ref[rows, cols]) * scale, -jnp.inf)
                m = jnp.maximum(jnp.max(sc, axis=-1, keepdims=True), jnp.max(sp, axis=-1, keepdims=True))
                pc, pp = jnp.exp(sc - m), jnp.exp(sp - m)
                l = jnp.sum(pc, axis=-1, keepdims=True) + jnp.sum(pp, axis=-1, keepdims=True)
                o_ref[rows, cols] = _nn(pc / l, vc_ref[rows, cols]) + _nn(pp / l, vp_ref[rows, cols])
                lse_ref[rows, :] = jnp.where(lane == hg * hb + hh, m + jnp.log(l), lse_ref[rows, :])

    cur = lambda part: pl.BlockSpec((SB, hb * dh), lambda s, sb, hg: (s * nb + sb, (gi * 3 + part) * ng + hg))
    prv = lambda part: pl.BlockSpec((SB, hb * dh), lambda s, sb, hg: (s * nb + jnp.maximum(sb - 1, 0), (gi * 3 + part) * ng + hg))
    return _call_with_side(
        body, name, (nseq, nb, ng), [cur(0), cur(1), prv(1), cur(2), prv(2)],
        [pl.BlockSpec((SB, hb * dh), lambda s, sb, hg: (s * nb + sb, hg)), pl.BlockSpec((SB, LANE), lambda s, sb, hg: (s * nb + sb, 0))],
        [jax.ShapeDtypeStruct((T, H * dh), f32), jax.ShapeDtypeStruct((T, LANE), f32)], [],
        ("parallel", "parallel", "arbitrary"), (proj, proj, proj, proj, proj), side)


def _dil_mix_fwd(os_, lses, name):
    W = os_[0].shape[1]
    G = len(os_)

    def fn(*a):
        o, l = a[:G], a[G:]
        lane = lax.broadcasted_iota(jnp.int32, l[0].shape, 1)
        tot = jnp.zeros(l[0].shape, f32)
        parts = []
        for h, sl in enumerate(_head_slices(W, DIL_HEADS)):
            lh = [x[:, h:h + 1] for x in l]
            m = functools.reduce(jnp.maximum, lh)
            e = [jnp.exp(x - m) for x in lh]
            z = functools.reduce(lambda u, v: u + v, e)
            acc = None
            for g in range(G):
                t = (e[g] / z) * o[g][:, sl]
                acc = t if acc is None else acc + t
            parts.append(acc)
            tot = jnp.where(lane == h, m + jnp.log(z), tot)
        y = jnp.concatenate(parts, axis=1)
        return y, y, tot

    return _rowwise(fn, [_full(x) for x in os_] + [_full(x) for x in lses], [], [(W, f32), (W, _MXU), (LANE, f32)], [], name)


def _dil_delta(do, o, name):
    W = o.shape[1]

    def fn(do, o):
        lane = lax.broadcasted_iota(jnp.int32, (do.shape[0], LANE), 1)
        d = jnp.zeros((do.shape[0], LANE), f32)
        for h, sl in enumerate(_head_slices(W, DIL_HEADS)):
            d = jnp.where(lane == h, jnp.sum(do[:, sl] * o[:, sl], axis=-1, keepdims=True), d)
        return d

    return _rowwise(fn, [_full(do), _full(o)], [], [(LANE, f32)], [], name)[0]


def _dil_bwd(proj, do, lse, delta, gi, window, dilation, nseq, name):
    T = proj.shape[0]
    H, dh = DIL_HEADS, DIL_HEAD_DIM
    L, P, nb, SB, hb, ng, row_sets, head_cols = _dil_tiling(T, nseq, dilation)
    steps = window // dilation
    scale = dh ** -0.5

    def probs(q, k, lse_h, mask):
        return jnp.where(mask, jnp.exp(_nt(q, k) * scale - lse_h), 0.0)

    def head_lane(ref, rows, h):
        lane = lax.broadcasted_iota(jnp.int32, (P, LANE), 1)
        return jnp.sum(jnp.where(lane == h, ref[rows, :], 0.0), axis=1, keepdims=True)

    def dq_body(q_ref, kc_ref, kp_ref, vc_ref, vp_ref, do_ref, lse_ref, del_ref, dq_ref):
        sb, hg = pl.program_id(1), pl.program_id(2)
        mc, mp = _band_masks(P, steps, sb > 0)
        for hh, cols in enumerate(head_cols):
            h = hg * hb + hh
            for rows in row_sets:
                q, doh = q_ref[rows, cols], do_ref[rows, cols]
                lse_h, del_h = head_lane(lse_ref, rows, h), head_lane(del_ref, rows, h)
                kc, kp = kc_ref[rows, cols], kp_ref[rows, cols]
                dsc = probs(q, kc, lse_h, mc) * (_nt(doh, vc_ref[rows, cols]) - del_h) * scale
                dsp = probs(q, kp, lse_h, mp) * (_nt(doh, vp_ref[rows, cols]) - del_h) * scale
                dq_ref[rows, cols] = _nn(dsc, kc) + _nn(dsp, kp)

    cur = lambda part: pl.BlockSpec((SB, hb * dh), lambda s, sb, hg: (s * nb + sb, (gi * 3 + part) * ng + hg))
    prv = lambda part: pl.BlockSpec((SB, hb * dh), lambda s, sb, hg: (s * nb + jnp.maximum(sb - 1, 0), (gi * 3 + part) * ng + hg))
    tok = pl.BlockSpec((SB, hb * dh), lambda s, sb, hg: (s * nb + sb, hg))
    aux = pl.BlockSpec((SB, LANE), lambda s, sb, hg: (s * nb + sb, 0))
    dq = pl.pallas_call(
        dq_body, name=name + "_dq", grid=(nseq, nb, ng),
        in_specs=[cur(0), cur(1), prv(1), cur(2), prv(2), tok, aux, aux],
        out_specs=tok, out_shape=jax.ShapeDtypeStruct((T, H * dh), f32),
        compiler_params=_cparams(("parallel", "parallel", "parallel")),
    )(proj, proj, proj, proj, proj, do, lse, delta)

    def dkv_body(k_ref, v_ref, qc_ref, qn_ref, doc_ref, don_ref, lsec_ref, lsen_ref, delc_ref, deln_ref, dk_ref, dv_ref):
        sb, hg = pl.program_id(1), pl.program_id(2)
        mc, mn = _band_masks(P, steps, sb < nb - 1)
        for hh, cols in enumerate(head_cols):
            h = hg * hb + hh
            for rows in row_sets:
                k, v = k_ref[rows, cols], v_ref[rows, cols]
                qc, doc = qc_ref[rows, cols], doc_ref[rows, cols]
                pc = probs(qc, k, head_lane(lsec_ref, rows, h), mc)
                dsc = pc * (_nt(doc, v) - head_lane(delc_ref, rows, h)) * scale
                qn, don = qn_ref[rows, cols], don_ref[rows, cols]
                pn = probs(qn, k, head_lane(lsen_ref, rows, h), mn)
                dsn = pn * (_nt(don, v) - head_lane(deln_ref, rows, h)) * scale
                dk_ref[rows, cols] = _tn(dsc, qc) + _tn(dsn, qn)
                dv_ref[rows, cols] = _tn(pc, doc) + _tn(pn, don)

    nxt = lambda s, sb: s * nb + jnp.minimum(sb + 1, nb - 1)
    qnx = pl.BlockSpec((SB, hb * dh), lambda s, sb, hg: (nxt(s, sb), gi * 3 * ng + hg))
    tokn = pl.BlockSpec((SB, hb * dh), lambda s, sb, hg: (nxt(s, sb), hg))
    auxn = pl.BlockSpec((SB, LANE), lambda s, sb, hg: (nxt(s, sb), 0))
    dkk, dvv = pl.pallas_call(
        dkv_body, name=name + "_dkv", grid=(nseq, nb, ng),
        in_specs=[cur(1), cur(2), cur(0), qnx, tok, tokn, aux, auxn, aux, auxn],
        out_specs=[tok, tok],
        out_shape=[jax.ShapeDtypeStruct((T, H * dh), f32), jax.ShapeDtypeStruct((T, H * dh), f32)],
        compiler_params=_cparams(("parallel", "parallel", "parallel")),
    )(proj, proj, proj, proj, do, do, lse, lse, delta, delta)
    return dq, dkk, dvv


def _adamw_math(w, g, m, v):
    m = ADAM_B1 * m + (1.0 - ADAM_B1) * g
    v = ADAM_B2 * v + (1.0 - ADAM_B2) * (g * g)
    m_hat = m / (1.0 - ADAM_B1 ** ADAM_STEP)
    v_hat = v / (1.0 - ADAM_B2 ** ADAM_STEP)
    return -ADAM_LR * (m_hat / (jnp.sqrt(v_hat) + ADAM_EPS) + ADAM_WD * w), m, v


def _adamw(w, g, m, v, name):
    W = w.shape[1]
    return _rowwise(_adamw_math, [_full(w), _full(g), _full(m), _full(v)], [], [(W, f32)] * 3, [], name, tm=512)


def _position():
    x, y, c = lax.axis_index("x"), lax.axis_index("y"), lax.axis_index("c")
    other_chips = [(1 - x, y), (x, 1 - y), (1 - x, 1 - y)]
    return x, y, c, other_chips


def _chip_index(x, y):
    return 2 * x + y


_ANY = pl.BlockSpec(memory_space=pl.ANY)


def _all_reduce_small(p, name):
    R, Wd = p.shape

    def body(p_ref, o_ref, buf, send_sems, recv_sems):
        x, y, c, _ = _position()
        me = 4 * x + 2 * y + c
        buf[me] = p_ref[...]
        copies = []
        for k in range(1, 8):
            fx, fy, fc = (k >> 2) & 1, (k >> 1) & 1, k & 1
            peer = (x + fx - 2 * x * fx, y + fy - 2 * y * fy, c + fc - 2 * c * fc)
            cp = pltpu.make_async_remote_copy(src_ref=p_ref, dst_ref=buf.at[me], send_sem=send_sems.at[k - 1],
                                              recv_sem=recv_sems.at[k - 1], device_id=peer, device_id_type=MESH)
            cp.start()
            copies.append(cp)
        for cp in copies:
            cp.wait()
        acc = buf[0]
        for s in range(1, 8):
            acc = acc + buf[s]
        o_ref[...] = acc

    return pl.pallas_call(
        body, name=name, out_shape=jax.ShapeDtypeStruct((R, Wd), f32),
        in_specs=[pl.BlockSpec(memory_space=pltpu.VMEM)], out_specs=pl.BlockSpec(memory_space=pltpu.VMEM),
        scratch_shapes=[pltpu.VMEM((8, R, Wd), f32), pltpu.SemaphoreType.DMA((7,)), pltpu.SemaphoreType.DMA((7,))],
        compiler_params=pltpu.CompilerParams(vmem_limit_bytes=VMEM_LIMIT),
    )(p)


def _layer_half(ref, h, axis):
    n = ref.shape[axis] // 2
    idx = (slice(None),) * axis + (pl.ds(h * n, n),)
    return ref.at[idx]


def _comm_call(body, name, ins, out_shapes, n_sems, n_local=0):
    scratch = [pltpu.SemaphoreType.DMA((n_sems,)), pltpu.SemaphoreType.DMA((n_sems,))]
    if n_local:
        scratch.append(pltpu.SemaphoreType.DMA((n_local,)))
    return pl.pallas_call(body, name=name, out_shape=out_shapes, in_specs=[_ANY] * len(ins), out_specs=[_ANY] * len(out_shapes),
                          scratch_shapes=scratch)(*ins)


class _SideCopies:
    def __init__(self, ins, outs, scratch, start, finish):
        self.ins, self.outs, self.scratch, self.start, self.finish = ins, outs, scratch, start, finish


def _row_half(ref, h):
    r = ref.shape[0] // 2
    start = h * r
    if r % 16 == 0:
        start = pl.multiple_of(start, 16)
    return ref.at[pl.ds(start, r)]


def _gather_plan(ws):
    n = len(ws)

    def copy(o_refs, sems, k, src, i, chip_idx, h, to):
        return pltpu.make_async_remote_copy(src_ref=src, dst_ref=_row_half(o_refs[i].at[chip_idx], h), send_sem=sems[0].at[k],
                                            recv_sem=sems[1].at[k], device_id=to, device_id_type=MESH)

    def own_copy(w_refs, o_refs, sems, i, p):
        return pltpu.make_async_copy(w_refs[i], o_refs[i].at[p], sems[2].at[i])

    def over_ici(w_refs, o_refs, sems, i, j, chip, dst_chip_idx, c):
        return copy(o_refs, sems, 3 * i + j, _row_half(w_refs[i], c), i, dst_chip_idx, c, (*chip, c))

    def start(w_refs, o_refs, *sems):
        x, y, c, chips = _position()
        p = _chip_index(x, y)
        for i in range(n):
            own_copy(w_refs, o_refs, sems, i, p).start()
            for j, chip in enumerate(chips):
                over_ici(w_refs, o_refs, sems, i, j, chip, p, c).start()

    def finish(w_refs, o_refs, *sems):
        x, y, c, chips = _position()
        p = _chip_index(x, y)
        sibling = (x, y, 1 - c)
        passed = []
        for i in range(n):
            for j, chip in enumerate(chips):
                q = _chip_index(*chip)
                over_ici(w_refs, o_refs, sems, i, j, chip, q, c).wait_recv()
                fwd = copy(o_refs, sems, 3 * n + 3 * i + j, _row_half(o_refs[i].at[q], c), i, q, c, sibling)
                fwd.start()
                passed.append(fwd)
        for i in range(n):
            for j, chip in enumerate(chips):
                copy(o_refs, sems, 3 * n + 3 * i + j, _row_half(w_refs[i], c), i, _chip_index(*chip), 1 - c, sibling).wait_recv()
        for i in range(n):
            for j, chip in enumerate(chips):
                over_ici(w_refs, o_refs, sems, i, j, chip, p, c).wait_send()
            own_copy(w_refs, o_refs, sems, i, p).wait()
        for fwd in passed:
            fwd.wait_send()

    scratch = [pltpu.SemaphoreType.DMA((6 * n,)), pltpu.SemaphoreType.DMA((6 * n,)), pltpu.SemaphoreType.DMA((n,))]
    return _SideCopies(list(ws), [jax.ShapeDtypeStruct((4,) + w.shape, w.dtype) for w in ws], scratch, start, finish)


def _run_copies(plan, name):
    n_i, n_o = len(plan.ins), len(plan.outs)

    def body(*refs):
        plan.start(refs[:n_i], refs[n_i:n_i + n_o], *refs[n_i + n_o:])
        plan.finish(refs[:n_i], refs[n_i:n_i + n_o], *refs[n_i + n_o:])

    return pl.pallas_call(body, name=name, out_shape=plan.outs, in_specs=[_ANY] * n_i, out_specs=[_ANY] * n_o,
                          scratch_shapes=plan.scratch)(*plan.ins)


def _sibling_halves(gs, name):
    n = len(gs)

    def body(*refs):
        g_refs, o_refs, send_sems, recv_sems = refs[:n], refs[n:2 * n], refs[2 * n], refs[2 * n + 1]
        x, y, c, _ = _position()
        copies = []
        for i in range(n):
            cp = pltpu.make_async_remote_copy(src_ref=_layer_half(g_refs[i], 1 - c, 1), dst_ref=o_refs[i], send_sem=send_sems.at[i],
                                              recv_sem=recv_sems.at[i], device_id=(x, y, 1 - c), device_id_type=MESH)
            cp.start()
            copies.append(cp)
        for cp in copies:
            cp.wait()

    outs = [jax.ShapeDtypeStruct((4, g.shape[1] // 2) + g.shape[2:], g.dtype) for g in gs]
    return _comm_call(body, name, gs, outs, n)


def _exchange_plan(hs):
    n = len(hs)

    def copies(h_refs, o_refs, send_sems, recv_sems):
        x, y, c, chips = _position()
        return [pltpu.make_async_remote_copy(src_ref=h_refs[i].at[_chip_index(*chip)], dst_ref=o_refs[i].at[j],
                                             send_sem=send_sems.at[3 * i + j], recv_sem=recv_sems.at[3 * i + j],
                                             device_id=(*chip, c), device_id_type=MESH)
                for i in range(n) for j, chip in enumerate(chips)]

    def start(h_refs, o_refs, *sems):
        for cp in copies(h_refs, o_refs, *sems):
            cp.start()

    def finish(h_refs, o_refs, *sems):
        for cp in copies(h_refs, o_refs, *sems):
            cp.wait()

    scratch = [pltpu.SemaphoreType.DMA((3 * n,)), pltpu.SemaphoreType.DMA((3 * n,))]
    return _SideCopies(list(hs), [jax.ShapeDtypeStruct((3,) + h.shape[1:], h.dtype) for h in hs], scratch, start, finish)


def _sibling_swap(ts, name):
    n = len(ts)

    def body(*refs):
        t_refs, o_refs, send_sems, recv_sems = refs[:n], refs[n:2 * n], refs[2 * n], refs[2 * n + 1]
        x, y, c, _ = _position()
        copies = []
        for i in range(n):
            cp = pltpu.make_async_remote_copy(src_ref=t_refs[i], dst_ref=o_refs[i], send_sem=send_sems.at[i], recv_sem=recv_sems.at[i],
                                              device_id=(x, y, 1 - c), device_id_type=MESH)
            cp.start()
            copies.append(cp)
        for cp in copies:
            cp.wait()

    return _comm_call(body, name, ts, [jax.ShapeDtypeStruct(t.shape, t.dtype) for t in ts], n)


BLOCK_ELEMS = 384 * 1024


def _block_2d(rows, cols, sub):
    tns = [t for t in range(LANE, cols + 1, LANE) if cols % t == 0] if cols % LANE == 0 else [cols]
    tms = [t for t in range(sub, rows + 1, sub) if rows % t == 0] or [rows]
    fits = [(tm * tn, tn, tm) for tm in tms for tn in tns if tm * tn <= BLOCK_ELEMS]
    assert fits, (rows, cols, sub)
    _, tn, tm = max(fits)
    return tm, tn


def _prefetch_call(body, name, scalars, grid, in_specs, out_specs, out_shape, args, sem):
    gs = pltpu.PrefetchScalarGridSpec(num_scalar_prefetch=1, grid=grid, in_specs=in_specs, out_specs=out_specs)
    return pl.pallas_call(body, name=name, grid_spec=gs, out_shape=out_shape, compiler_params=_cparams(sem))(scalars, *args)


def _add_own_half(g, got, c, name):
    _, nl, K, N = g.shape
    hl = nl // 2
    tm, tn = _block_2d(K, N, 16)

    def body(c_ref, g_ref, r_ref, o_ref):
        o_ref[...] = (g_ref[...].astype(f32) + r_ref[...].astype(f32)).astype(o_ref.dtype)

    blk = (1, 1, tm, tn)
    return _prefetch_call(
        body, name, jnp.reshape(c, (1,)).astype(jnp.int32), (4, hl, K // tm, N // tn),
        [pl.BlockSpec(blk, lambda s, l, i, j, c_ref: (s, c_ref[0] * hl + l, i, j)), pl.BlockSpec(blk, lambda s, l, i, j, c_ref: (s, l, i, j))],
        pl.BlockSpec(blk, lambda s, l, i, j, c_ref: (s, l, i, j)), jax.ShapeDtypeStruct((4, hl, K, N), g.dtype), (g, got),
        ("parallel",) * 4)


def _add_chips(h, got, p, name):
    _, nl, K, N = h.shape
    tm, tn = _block_2d(K, N, 16)

    def body(p_ref, h_ref, r0_ref, r1_ref, r2_ref, o_ref):
        o_ref[...] = ((h_ref[0].astype(f32) + r0_ref[0].astype(f32)) + r1_ref[0].astype(f32)) + r2_ref[0].astype(f32)

    blk = (1, 1, tm, tn)
    got_spec = lambda q: pl.BlockSpec(blk, lambda l, i, j, p_ref: (q, l, i, j))
    return _prefetch_call(
        body, name, jnp.reshape(p, (1,)).astype(jnp.int32), (nl, K // tm, N // tn),
        [pl.BlockSpec(blk, lambda l, i, j, p_ref: (p_ref[0], l, i, j)), got_spec(0), got_spec(1), got_spec(2)],
        pl.BlockSpec((1, tm, tn), lambda l, i, j, p_ref: (l, i, j)), jax.ShapeDtypeStruct((nl, K, N), f32), (h, got, got, got),
        ("parallel",) * 3)


def _adamw_halves(w, m, v, own, other, c, name):
    nl, K, N = w.shape
    tm, tn = _block_2d(K // 2, N, 8)
    nb = K // 2 // tm

    def body(c_ref, w_ref, m_ref, v_ref, own_ref, other_ref, g_out, d_out, m_out, v_out):
        g = jnp.where(pl.program_id(1) == c_ref[0], own_ref[...], other_ref[...])
        d, m_new, v_new = _adamw_math(w_ref[...], g, m_ref[...], v_ref[...])
        g_out[...] = g
        d_out[...] = d
        m_out[...] = m_new
        v_out[...] = v_new

    blk = (1, tm, tn)
    full = pl.BlockSpec(blk, lambda l, h, i, j, c_ref: (l, h * nb + i, j))
    half = pl.BlockSpec(blk, lambda l, h, i, j, c_ref: (l, i, j))
    return _prefetch_call(
        body, name, jnp.reshape(c, (1,)).astype(jnp.int32), (nl, 2, nb, N // tn), [full, full, full, half, half], [full] * 4,
        [jax.ShapeDtypeStruct((nl, K, N), f32)] * 4, (w, m, v, own, other), ("parallel",) * 4)


def _row_halves_view(g):
    return g.reshape(4, 2, g.shape[1] // 2, g.shape[2])


def _reduce_scatter_start(gs, name):
    c = lax.axis_index("c")
    views = [_row_halves_view(g) for g in gs]
    from_sibling = _sibling_halves(views, name + "_d2d")
    return [_add_own_half(g, r, c, f"{name}_add2_{i}") for i, (g, r) in enumerate(zip(views, from_sibling))]


def _reduce_scatter_finish(hs, got, name):
    p = _chip_index(lax.axis_index("x"), lax.axis_index("y"))
    return [_add_chips(h, r, p, f"{name}_add4_{i}")[0] for i, (h, r) in enumerate(zip(hs, got))]


_BIG = (("gla_w_in", 1), ("gla_w_out", 0), ("dil_w_in", 1), ("dil_w_out", 1), ("ffn_w_up", 1), ("ffn_w_down", 0))


def _pad_rows(a, mult):
    r = (-a.shape[0]) % mult
    return a if r == 0 else jnp.concatenate([a, jnp.zeros((r,) + a.shape[1:], a.dtype)], axis=0)


def _unshard(blocks, axis):
    _, K, N = blocks.shape
    if axis == 1:
        return blocks.transpose(1, 0, 2).reshape(K, 4 * N)
    return blocks.reshape(4 * K, N)


def _to_shards(mat, axis):
    K, N = mat.shape
    if axis == 1:
        return mat.reshape(K, 4, N // 4).transpose(1, 0, 2)
    return mat.reshape(4, K // 4, N)


_SMALL = (("gla_w_gate_up", 2), ("gla_gate_bias", None), ("gla_norm_g", None), ("ffn_conv_w", 2), ("ffn_conv_b", None),
          ("ln_g", 2), ("ln_b", 2))


def _pack_rows(arrs, width=LANE):
    flat = _pad_rows(jnp.concatenate([a.reshape(-1) for a in arrs]), 8 * width)
    return flat.reshape(-1, width)


def _unpack_rows(packed, shapes):
    flat, out, off = packed.reshape(-1), [], 0
    for s in shapes:
        n = 1
        for d in s:
            n *= d
        out.append(flat[off:off + n].reshape(s))
        off += n
    return out


def _gather_small_params(shards):
    x, y, c = lax.axis_index("x"), lax.axis_index("y"), lax.axis_index("c")
    names = [n for n, axis in _SMALL if axis is not None]
    mine = _pack_rows([shards[n] for n in names])
    mine = jnp.where(c == 0, mine, jnp.zeros_like(mine))
    rows = mine.shape[0]
    placed = lax.dynamic_update_slice(jnp.zeros((4 * rows, LANE), f32), mine, (_chip_index(x, y) * rows, 0))
    allp = _all_reduce_small(placed, "gather_small").reshape(4, rows, LANE)
    out = {n: shards[n] for n, axis in _SMALL if axis is None}
    per_chip = [_unpack_rows(allp[q], [shards[n].shape for n in names]) for q in range(4)]
    for i, n in enumerate(names):
        out[n] = jnp.concatenate([per_chip[q][i] for q in range(4)], axis=2)
    return out


def _reduce_small_grads(grads, shards):
    names = [n for n, _ in _SMALL]
    total = _all_reduce_small(_pack_rows([grads[n] for n in names]), "reduce_small")
    full = dict(zip(names, _unpack_rows(total, [grads[n].shape for n in names])))
    p = _chip_index(lax.axis_index("x"), lax.axis_index("y"))
    out = {}
    for n, axis in _SMALL:
        if axis is None:
            out[n] = full[n]
        else:
            w = shards[n].shape[axis]
            out[n] = lax.dynamic_slice_in_dim(full[n], p * w, w, axis=axis)
    return out


def _pad_cols(a, n):
    return a if a.shape[-1] == n else jnp.concatenate([a, jnp.zeros(a.shape[:-1] + (n - a.shape[-1],), a.dtype)], axis=-1)


def _ffn_width(F):
    return -(-F // 512) * 512


def kernel(x, gla_w_in, gla_w_gate_up, gla_gate_bias, gla_norm_g, gla_w_out, dil_w_in, dil_w_out, ffn_w_up, ffn_conv_w, ffn_conv_b, ffn_w_down, ln_g, ln_b, loss_target, m_gla_w_in, m_gla_w_gate_up, m_gla_gate_bias, m_gla_norm_g, m_gla_w_out, m_dil_w_in, m_dil_w_out, m_ffn_w_up, m_ffn_conv_w, m_ffn_conv_b, m_ffn_w_down, m_ln_g, m_ln_b, v_gla_w_in, v_gla_w_gate_up, v_gla_gate_bias, v_gla_norm_g, v_gla_w_out, v_dil_w_in, v_dil_w_out, v_ffn_w_up, v_ffn_conv_w, v_ffn_conv_b, v_ffn_w_down, v_ln_g, v_ln_b):
    names = ["gla_w_in", "gla_w_gate_up", "gla_gate_bias", "gla_norm_g", "gla_w_out", "dil_w_in", "dil_w_out", "ffn_w_up",
             "ffn_conv_w", "ffn_conv_b", "ffn_w_down", "ln_g", "ln_b"]
    w_sh = dict(zip(names, (gla_w_in, gla_w_gate_up, gla_gate_bias, gla_norm_g, gla_w_out, dil_w_in, dil_w_out, ffn_w_up,
                            ffn_conv_w, ffn_conv_b, ffn_w_down, ln_g, ln_b)))
    m_sh = dict(zip(names, (m_gla_w_in, m_gla_w_gate_up, m_gla_gate_bias, m_gla_norm_g, m_gla_w_out, m_dil_w_in, m_dil_w_out,
                            m_ffn_w_up, m_ffn_conv_w, m_ffn_conv_b, m_ffn_w_down, m_ln_g, m_ln_b)))
    v_sh = dict(zip(names, (v_gla_w_in, v_gla_w_gate_up, v_gla_gate_bias, v_gla_norm_g, v_gla_w_out, v_dil_w_in, v_dil_w_out,
                            v_ffn_w_up, v_ffn_conv_w, v_ffn_conv_b, v_ffn_w_down, v_ln_g, v_ln_b)))
    nseq, S, D = x.shape
    T = nseq * S
    small = _gather_small_params(w_sh)
    F = 4 * w_sh["ffn_w_down"].shape[1]
    Fp = _ffn_width(F)
    qkvr = 4 * w_sh["gla_w_in"].shape[2] - GLA_GATE_RANK
    c_idx = lax.axis_index("c")
    shard_axis = dict(_BIG)

    def pad_halves(a):
        return jnp.concatenate([_pad_cols(a[..., :F], Fp), _pad_cols(a[..., F:], Fp)], axis=-1)

    cw_all = pad_halves(small["ffn_conv_w"])
    cb_all = pad_halves(small["ffn_conv_b"][:, None, :])
    w_gate_up_all = jnp.pad(small["gla_w_gate_up"].astype(bf16), ((0, 0), (0, GATE_PAD - GLA_GATE_RANK), (0, 0)))

    def mixer_names(l):
        return ("gla_w_in", "gla_w_out") if l % 2 == 0 else ("dil_w_in", "dil_w_out")

    def shard_of(l, which):
        n_in, n_out = mixer_names(l)
        name, idx = {"w_in": (n_in, l // 2), "w_out": (n_out, l // 2), "w_up": ("ffn_w_up", l), "w_down": ("ffn_w_down", l)}[which]
        shard = w_sh[name][idx]
        return (shard.T if which == "w_up" else shard).astype(bf16)

    weights = [dict() for _ in range(DEPTH)]

    def install(l, which, gathered):
        n_in, n_out = mixer_names(l)
        if which == "w_in":
            weights[l]["w_in"] = w_in = _unshard(gathered, shard_axis[n_in])
            if l % 2 == 0:
                weights[l]["w_gate"] = _pad_cols(w_in[:, qkvr:], GATE_PAD)
        elif which == "w_out":
            weights[l]["w_out"] = _unshard(gathered, shard_axis[n_out])
        elif which == "w_up":
            wt = _unshard(gathered, 0)
            weights[l]["w_up_t"] = jnp.concatenate([_pad_rows(wt[:F], Fp), _pad_rows(wt[F:], Fp)], axis=0)
        else:
            weights[l]["w_down"] = _pad_rows(_unshard(gathered, shard_axis["ffn_w_down"]), Fp)

    kinds = ("w_in", "w_out", "w_up", "w_down")
    install(0, "w_in", _run_copies(_gather_plan([shard_of(0, "w_in")]), "gather_first")[0])

    riders = {"proj": [(0, "w_out"), (0, "w_down")], "core": [(0, "w_up")], "ffn_up": [(1, "w_in")]}

    def ridden(l, kernel):
        return [(l + dl, which) for dl, which in riders.get(kernel, []) if l + dl < DEPTH]

    def side_for(l, kernel):
        items = ridden(l, kernel)
        return _gather_plan([shard_of(m, which) for m, which in items]) if items else None

    def carried(l, kernel, res, n_own=1):
        items = ridden(l, kernel)
        if not items:
            return res
        for (m, which), gathered in zip(items, res[n_own:]):
            install(m, which, gathered)
        return res[0] if n_own == 1 else res[:n_own]

    h0 = x.reshape(T, D)
    saved = []
    cur, cur_b = h0, h0.astype(_MXU)
    fwd = dict(tm=1024, tn=1024, tk=2048)
    for i in range(DEPTH):
        j = i // 2
        tag = f"l{i}_"
        lg, lb = small["ln_g"][i], small["ln_b"][i]
        W = weights[i]
        if i % 2 == 0:
            gate_bias, norm_g = small["gla_gate_bias"][j][None, :], small["gla_norm_g"][j][None, :]
            proj = carried(i, "proj", _mm(cur_b, W["w_in"], "nn", tag + "gla_proj", n_out=qkvr, side=side_for(i, "proj"), **fwd))
            g_low = _mm(cur_b, W["w_gate"], "nn", tag + "gla_glow", tn=GATE_PAD)
            z = _mm(g_low, w_gate_up_all, "nn", tag + "gla_z", layer=j)
            gate = _log_gate_fwd(z, gate_bias, tag + "gla_gate")
            o, states = carried(i, "core", _gla_fwd(proj, gate, nseq, tag + "gla_core", side=side_for(i, "core")), n_own=2)
            y_b = _rms_gate_fwd(o, proj, norm_g, tag + "gla_norm")
            mix = _mm(y_b, W["w_out"], "nn", tag + "gla_out", **fwd)
            mixer_saved = (proj, g_low, z, gate, states, o, y_b)
        else:
            proj = carried(i, "proj", _mm(cur_b, W["w_in"], "nn", tag + "dil_proj", side=side_for(i, "proj"), **fwd))
            outs, lses = [], []
            for gi, (window, dilation) in enumerate(DIL_PATTERNS):
                kern = "core" if gi == 1 else None
                og, lg_ = carried(i, kern, _dil_fwd(proj, gi, window, dilation, nseq, tag + f"dil_attn{gi}", side=side_for(i, kern)), n_own=2)
                outs.append(og)
                lses.append(lg_)
            y, y_b, lse_tot = _dil_mix_fwd(outs, lses, tag + "dil_mix")
            mix = _mm(y_b, W["w_out"], "nn", tag + "dil_out", **fwd)
            mixer_saved = (proj, y, y_b, lse_tot)
        x1, x1_b = _ln_fwd(cur, mix, lg[0:1], lb[0:1], tag + "ln1")
        cw, cb = cw_all[i], cb_all[i]
        hh = carried(i, "ffn_up", _mm(x1_b, W["w_up_t"], "nt", tag + "ffn_up", side=side_for(i, "ffn_up"), **fwd))
        act_b = carried(i, "conv", _conv_gate_fwd(hh, cw, cb, nseq, tag + "ffn_conv", side=side_for(i, "conv")))
        ffn = carried(i, "ffn_down", _mm(act_b, W["w_down"], "nn", tag + "ffn_down", tm=1024, tn=512, tk=Fp, side=side_for(i, "ffn_down")))
        x2, x2_b = _ln_fwd(x1, ffn, lg[1:2], lb[1:2], tag + "ln2")
        saved.append((cur, cur_b, mix, x1, x1_b, hh, act_b, ffn, mixer_saved))
        cur, cur_b = x2, x2_b

    dy, sq = _loss_head(cur, loss_target.reshape(T, D), "loss_head")
    loss = lax.psum(0.5 * jnp.sum(sq) / D, ("x", "y", "c"))

    gb = {n: [None] * w_sh[n].shape[0] for n in names}
    own_half = {n: [None] * w_sh[n].shape[0] for n, _ in _BIG}
    bwd_riders = {"down_dx": "w_down", "conv": "w_up", "gate_dw": "w_in", "up_dx": "w_out"}
    riding = {"layer": None, "partial": None}

    def grad_slot(l, which):
        n_in, n_out = mixer_names(l)
        return {"w_in": (n_in, l // 2), "w_out": (n_out, l // 2), "w_up": ("ffn_w_up", l), "w_down": ("ffn_w_down", l)}[which]

    def ride(kernel):
        return None if riding["layer"] is None else _exchange_plan([riding["partial"][bwd_riders[kernel]]])

    def landed(kernel, got):
        if riding["layer"] is not None:
            l, which = riding["layer"], bwd_riders[kernel]
            n, idx = grad_slot(l, which)
            own_half[n][idx] = _reduce_scatter_finish([riding["partial"][which]], got, f"scatter_l{l}_{which}")[0]

    def unwrap(kernel, res):
        if riding["layer"] is None:
            return res
        landed(kernel, res[1:])
        return res[0]

    d_res = None
    d_in = dy
    for i in reversed(range(DEPTH)):
        j = i // 2
        tag = f"l{i}_b_"
        xin, xin_b, mix, x1, x1_b, hh, act_b, ffn, mixer_saved = saved[i]
        lg = small["ln_g"][i]
        cw, cb = cw_all[i], cb_all[i]
        dw_tiles = dict(tm=1024, tn=1024, tk=4096, out_dtype=bf16)
        dx_tiles = dict(tm=1024, tn=512, tk=6144)
        dys, scales = ([d_in], [1.0]) if d_res is None else ([d_res, d_in], [DEEPNORM_ALPHA, 1.0])
        du2, du2_b, dg2, db2 = _ln_bwd(x1, ffn, lg[1:2], dys, scales, tag + "ln2")
        gb["ffn_w_down"][i] = _mm(act_b, du2_b, "tn", tag + "ffn_down_dw", **dw_tiles)[:F]
        W = weights[i]
        dact = unwrap("down_dx", _mm(du2_b, W["w_down"], "nt", tag + "ffn_down_dx", side=ride("down_dx"), **dx_tiles))
        dhg, dhu, dcw, dcb, rode = _conv_gate_bwd(hh, dact, cw, cb, nseq, tag + "ffn_conv", side=ride("conv"))
        landed("conv", rode)
        gb["ffn_conv_w"][i] = jnp.concatenate([dcw[:, :F], dcw[:, Fp:Fp + F]], axis=1)
        gb["ffn_conv_b"][i] = jnp.concatenate([dcb[0, :F], dcb[0, Fp:Fp + F]], axis=0)
        dwg_t = unwrap("gate_dw", _mm(dhg, x1_b, "tn", tag + "ffn_gate_dw", side=ride("gate_dw"), **dw_tiles))
        dwu_t = _mm(dhu, x1_b, "tn", tag + "ffn_up_dw", **dw_tiles)
        gb["ffn_w_up"][i] = jnp.concatenate([dwg_t[:F], dwu_t[:F]], axis=0)
        dx1g = _mm(dhg, W["w_up_t"], "nn", tag + "ffn_gate_dx", **dx_tiles)
        dx1u = unwrap("up_dx", _mm(dhu, W["w_up_t"], "nn", tag + "ffn_up_dx", b_k0=Fp, side=ride("up_dx"), **dx_tiles))
        du1, du1_b, dg1, db1 = _ln_bwd(xin, mix, lg[0:1], [du2, dx1g, dx1u], [DEEPNORM_ALPHA, 1.0, 1.0], tag + "ln1")
        gb["ln_g"][i] = jnp.concatenate([dg1, dg2], axis=0)
        gb["ln_b"][i] = jnp.concatenate([db1, db2], axis=0)
        if i % 2 == 0:
            proj, g_low, z, gate, states, o, y_b = mixer_saved
            gate_bias, norm_g = small["gla_gate_bias"][j][None, :], small["gla_norm_g"][j][None, :]
            gb["gla_w_out"][j] = _mm(y_b, du1_b, "tn", tag + "gla_out_dw", **dw_tiles)
            dyy = _mm(du1_b, W["w_out"], "nt", tag + "gla_out_dx", **dx_tiles)
            do, dr, dng = _rms_gate_bwd(o, proj, norm_g, dyy, tag + "gla_norm")
            gb["gla_norm_g"][j] = dng[0]
            dq, dk_, dv_, dgate = _gla_bwd(proj, gate, states, do, nseq, tag + "gla_core")
            dz, dbias = _log_gate_bwd(z, gate_bias, dgate, tag + "gla_gate")
            gb["gla_gate_bias"][j] = dbias[0]
            gb["gla_w_gate_up"][j] = _mm(g_low, dz, "tn", tag + "gla_z_dw", tk=1024)[:GLA_GATE_RANK]
            dg_low = _mm(dz, w_gate_up_all, "nt", tag + "gla_z_dx", tn=GATE_PAD, layer=j)
            dproj = jnp.concatenate([dq, dk_, dv_, dr], axis=1)
            dw_main = _mm(xin_b, dproj, "tn", tag + "gla_proj_dw", **dw_tiles)
            dw_gate = _mm(xin_b, dg_low, "tn", tag + "gla_glow_dw", tm=1024, tn=GATE_PAD, tk=2048, out_dtype=bf16)[:, :GLA_GATE_RANK]
            gb["gla_w_in"][j] = jnp.concatenate([dw_main, dw_gate], axis=1)
            dxa = _mm(dproj, W["w_in"], "nt", tag + "gla_proj_dx", **dx_tiles)
            dxb = _mm(dg_low, W["w_gate"], "nt", tag + "gla_glow_dx")
            d_in = _axpy(dxa, dxb, 1.0, tag + "gla_dx_sum")
        else:
            proj, y, y_b, lse_tot = mixer_saved
            gb["dil_w_out"][j] = _mm(y_b, du1_b, "tn", tag + "dil_out_dw", **dw_tiles)
            dyy = _mm(du1_b, W["w_out"], "nt", tag + "dil_out_dx", **dx_tiles)
            delta = _dil_delta(dyy, y, tag + "dil_delta")
            pieces = []
            for gi, (window, dilation) in enumerate(DIL_PATTERNS):
                pieces += list(_dil_bwd(proj, dyy, lse_tot, delta, gi, window, dilation, nseq, tag + f"dil_attn{gi}"))
            dproj = jnp.concatenate(pieces, axis=1).astype(_MXU)
            gb["dil_w_in"][j] = _mm(xin_b, dproj, "tn", tag + "dil_proj_dw", **dw_tiles)
            d_in = _mm(dproj, W["w_in"], "nt", tag + "dil_proj_dx", **dx_tiles)
        d_res = du1
        local = [_to_shards(gb[n][idx], 0 if n == "ffn_w_up" else shard_axis[n]) for n, idx in (grad_slot(i, k) for k in kinds)]
        riding["layer"], riding["partial"] = i, dict(zip(kinds, _reduce_scatter_start(local, f"scatter_l{i}")))
    grad_x = _axpy(d_res, d_in, DEEPNORM_ALPHA, "grad_x").reshape(x.shape)

    last = [riding["partial"][k] for k in kinds]
    for k, own in zip(kinds, _reduce_scatter_finish(last, _run_copies(_exchange_plan(last), "scatter_l0_ici"), "scatter_l0")):
        n, idx = grad_slot(0, k)
        own_half[n][idx] = own
    own_all = [jnp.stack(own_half[n], axis=0) for n, _ in _BIG]
    other_all = _sibling_swap(own_all, "scatter_swap")
    grads, delta, new_m, new_v = {}, {}, {}, {}
    for (n, _), own, other in zip(_BIG, own_all, other_all):
        view = (lambda a: jnp.swapaxes(a, 1, 2)) if n == "ffn_w_up" else (lambda a: a)
        res = _adamw_halves(view(w_sh[n]), view(m_sh[n]), view(v_sh[n]), own, other, c_idx, "adamw_" + n)
        grads[n], delta[n], new_m[n], new_v[n] = [view(r) for r in res]
    grads.update(_reduce_small_grads({n: jnp.stack(gb[n], axis=0) for n, _ in _SMALL}, w_sh))
    small_names = [n for n, _ in _SMALL]
    packed = [_pack_rows([src[n] for n in small_names]) for src in (w_sh, grads, m_sh, v_sh)]
    res = _adamw(*packed, "adamw_small")
    shapes = [w_sh[n].shape for n in small_names]
    for dst, arr in zip((delta, new_m, new_v), res):
        dst.update(dict(zip(small_names, _unpack_rows(arr, shapes))))

    return (loss, grad_x, *[grads[n] for n in names], *[delta[n] for n in names], *[new_m[n] for n in names],
            *[new_v[n] for n in names])
```

```python
import functools

import jax
import jax.numpy as jnp
from jax import lax
from jax.experimental import pallas as pl
from jax.experimental.pallas import tpu as pltpu

f32 = jnp.float32
bf16 = jnp.bfloat16
_MXU = jnp.bfloat16

DEPTH = 4
GLA_HEADS = 4
GLA_GATE_RANK = 16
GLA_GATE_NORMALIZER = 16.0
GLA_CHUNK = 64
GLA_SUB = 16
DIL_PATTERNS = ((128, 1), (512, 4), (2048, 16))
DIL_HEADS = 8
DIL_HEAD_DIM = 128
DIL_BLOCK = 128
DEEPNORM_ALPHA = (2 * DEPTH) ** 0.25
LN_EPS = 1e-5
RMS_EPS = 1e-6
ADAM_LR = 0.001
ADAM_B1 = 0.9
ADAM_B2 = 0.999
ADAM_EPS = 1e-08
ADAM_WD = 0.01
ADAM_STEP = 10

LANE = 128
VMEM_LIMIT = 48 * 1024 * 1024
GATE_PAD = LANE
MESH = pl.DeviceIdType.MESH


def _cparams(sem=None):
    return pltpu.CompilerParams(dimension_semantics=sem, vmem_limit_bytes=VMEM_LIMIT)


def _div_tile(n, pref, unit):
    if n <= pref:
        return n
    best = None
    for t in range(unit, pref + 1, unit):
        if n % t == 0:
            best = t
    assert best is not None, (n, pref, unit)
    return best


def _dot(a, b, ca, cb):
    return lax.dot_general(a.astype(_MXU), b.astype(_MXU), (((ca,), (cb,)), ((), ())), preferred_element_type=f32)


def _nn(a, b):
    return _dot(a, b, 1, 0)


def _nt(a, b):
    return _dot(a, b, 1, 1)


def _tn(a, b):
    return _dot(a, b, 0, 0)


def _exact_dot(a, b):
    return jnp.dot(a, b, precision=lax.Precision.HIGHEST, preferred_element_type=f32)


def _sigmoid(x):
    return 1.0 / (1.0 + jnp.exp(-x))


def _call_with_side(body, name, grid, in_specs, out_specs, out_shape, scratch, sem, args, side):
    if side is None:
        return pl.pallas_call(body, name=name, grid=grid, in_specs=in_specs, out_specs=out_specs, out_shape=out_shape,
                              scratch_shapes=scratch, compiler_params=_cparams(sem))(*args)
    n_in, n_out, n_scr = len(in_specs), len(out_specs), len(scratch)
    n_si, n_so = len(side.ins), len(side.outs)

    def wrapped(*refs):
        ins, s_in = refs[:n_in], refs[n_in:n_in + n_si]
        outs, s_out = refs[n_in + n_si:n_in + n_si + n_out], refs[n_in + n_si + n_out:n_in + n_si + n_out + n_so]
        rest = refs[n_in + n_si + n_out + n_so:]
        sems = rest[n_scr:]
        ids = [pl.program_id(ax) for ax in range(len(grid))]
        first = functools.reduce(lambda u, v: u & v, [i == 0 for i in ids])
        last = functools.reduce(lambda u, v: u & v, [i == g - 1 for i, g in zip(ids, grid)])

        @pl.when(first)
        def _():
            side.start(s_in, s_out, *sems)

        body(*ins, *outs, *rest[:n_scr])

        @pl.when(last)
        def _():
            side.finish(s_in, s_out, *sems)

    return pl.pallas_call(
        wrapped, name=name, grid=grid, in_specs=list(in_specs) + [_ANY] * n_si, out_specs=list(out_specs) + [_ANY] * n_so,
        out_shape=list(out_shape) + list(side.outs), scratch_shapes=list(scratch) + list(side.scratch),
        compiler_params=_cparams(("arbitrary",) * len(grid)))(*args, *side.ins)


def _mm(a, b, mode, name, tm=1024, tn=512, tk=2048, out_dtype=f32, layer=None, n_out=None, b_k0=0, side=None):
    if mode == "nn":
        (M, K), N = a.shape, (n_out or b.shape[-1])
    elif mode == "nt":
        (M, K), N = a.shape, b.shape[-2]
    else:
        (K, M), N = a.shape, b.shape[-1]
    tm, tn, tk = _div_tile(M, tm, LANE), _div_tile(N, tn, LANE), _div_tile(K, tk, LANE)
    nk = K // tk
    if mode == "tn":
        a_spec = pl.BlockSpec((tk, tm), lambda i, j, k: (k, i))
    else:
        a_spec = pl.BlockSpec((tm, tk), lambda i, j, k: (i, k))
    lead = () if layer is None else (None,)
    pre = () if layer is None else (layer,)
    k0 = b_k0 // tk
    assert k0 * tk == b_k0 and (mode != "tn" or b_k0 == 0)
    if mode == "nt":
        b_spec = pl.BlockSpec(lead + (tn, tk), lambda i, j, k: pre + (j, k + k0))
    else:
        b_spec = pl.BlockSpec(lead + (tk, tn), lambda i, j, k: pre + (k + k0, j))
    ca, cb = {"nn": (1, 0), "nt": (1, 1), "tn": (0, 0)}[mode]

    def body(a_ref, b_ref, o_ref, *acc):
        p = _dot(a_ref[...], b_ref[...], ca, cb)
        if nk == 1:
            o_ref[...] = p.astype(o_ref.dtype)
        else:
            k = pl.program_id(2)
            acc_ref = acc[0]

            @pl.when(k == 0)
            def _():
                acc_ref[...] = p

            @pl.when(k > 0)
            def _():
                acc_ref[...] += p

            @pl.when(k == nk - 1)
            def _():
                o_ref[...] = acc_ref[...].astype(o_ref.dtype)

    res = _call_with_side(
        body, name, (M // tm, N // tn, nk), [a_spec, b_spec], [pl.BlockSpec((tm, tn), lambda i, j, k: (i, j))],
        [jax.ShapeDtypeStruct((M, N), out_dtype)], [pltpu.VMEM((tm, tn), f32)] if nk > 1 else [],
        ("parallel", "parallel", "arbitrary"), (a, b), side)
    return res if side else res[0]


def _rowwise(fn, rows, consts, outs, reds, name, tm=256):
    T = rows[0][0].shape[0]
    tm = _div_tile(T, tm, 8)
    n_r, n_c, n_o = len(rows), len(consts), len(outs)

    def body(*refs):
        ins = [r[...] for r in refs[: n_r + n_c]]
        res = fn(*ins)
        res = res if isinstance(res, (tuple, list)) else (res,)
        o_refs = refs[n_r + n_c: n_r + n_c + n_o]
        r_refs = refs[n_r + n_c + n_o:]
        for ref, val in zip(o_refs, res[:n_o]):
            ref[...] = val.astype(ref.dtype)
        i = pl.program_id(0)
        for ref, val in zip(r_refs, res[n_o:]):
            _accumulate(ref, val, i)

    in_specs = [pl.BlockSpec((tm, w), functools.partial(lambda i, cb: (i, cb), cb=cb)) for (_, w, cb) in rows]
    in_specs += [pl.BlockSpec(c.shape, lambda i: (0, 0)) for c in consts]
    out_specs = [pl.BlockSpec((tm, w), lambda i: (i, 0)) for (w, _) in outs]
    out_specs += [pl.BlockSpec((1, w), lambda i: (0, 0)) for w in reds]
    out_shape = [jax.ShapeDtypeStruct((T, w), dt) for (w, dt) in outs]
    out_shape += [jax.ShapeDtypeStruct((1, w), f32) for w in reds]
    return pl.pallas_call(
        body, name=name, grid=(T // tm,), in_specs=in_specs, out_specs=out_specs, out_shape=out_shape,
        compiler_params=_cparams(("arbitrary",)),
    )(*[r[0] for r in rows], *consts)


def _accumulate(ref, val, step):
    @pl.when(step == 0)
    def _():
        ref[...] = val

    @pl.when(step > 0)
    def _():
        ref[...] += val


def _full(a):
    return (a, a.shape[1], 0)


def _colsum(x):
    return jnp.sum(x, axis=0, keepdims=True)


def _ln_stats(u):
    mu = jnp.mean(u, axis=-1, keepdims=True)
    xc = u - mu
    var = jnp.mean(xc * xc, axis=-1, keepdims=True)
    rstd = lax.rsqrt(var + LN_EPS)
    return xc * rstd, rstd


def _ln_fwd(x, f, g, b, name):
    def fn(x, f, g, b):
        xhat, _ = _ln_stats(DEEPNORM_ALPHA * x + f)
        y = xhat * g + b
        return y, y

    return _rowwise(fn, [_full(x), _full(f)], [g, b], [(x.shape[1], f32), (x.shape[1], _MXU)], [], name)


def _ln_bwd(x, f, g, dys, scales, name):
    def fn(x, f, *rest):
        g = rest[-1]
        dy = None
        for d, s in zip(rest[:-1], scales):
            t = d if s == 1.0 else s * d
            dy = t if dy is None else dy + t
        xhat, rstd = _ln_stats(DEEPNORM_ALPHA * x + f)
        dxh = dy * g
        m1 = jnp.mean(dxh, axis=-1, keepdims=True)
        m2 = jnp.mean(dxh * xhat, axis=-1, keepdims=True)
        du = rstd * (dxh - m1 - xhat * m2)
        return du, du, _colsum(dy * xhat), _colsum(dy)

    D = x.shape[1]
    return _rowwise(fn, [_full(x), _full(f)] + [_full(d) for d in dys], [g], [(D, f32), (D, _MXU)], [D, D], name)


def _loss_head(y, t, name):
    D = y.shape[1]

    def fn(y, t):
        e = y - t
        return e * (1.0 / D), _colsum(e * e)

    return _rowwise(fn, [_full(y), _full(t)], [], [(D, f32)], [D], name)


def _axpy(a, b, alpha, name):
    def fn(a, b):
        return alpha * a + b

    return _rowwise(fn, [_full(a), _full(b)], [], [(a.shape[1], f32)], [], name)[0]


def _shift_down(x, k):
    row = lax.broadcasted_iota(jnp.int32, x.shape, 0)
    return jnp.where(row >= k, pltpu.roll(x, k, 0), 0.0)


def _shift_up(x, k):
    S = x.shape[0]
    row = lax.broadcasted_iota(jnp.int32, x.shape, 0)
    return jnp.where(row < S - k, pltpu.roll(x, S - k, 0), 0.0)


def _causal_conv(h, w, b):
    return ((b + w[0:1] * _shift_down(h, 2)) + w[1:2] * _shift_down(h, 1)) + w[2:3] * h


def _conv_gate_fwd(h, cw, cb, nseq, name, tc=256, side=None):
    T, F2 = h.shape
    F, S = F2 // 2, T // nseq
    tc = _div_tile(F, tc, LANE)
    nf = F // tc

    def body(hg_ref, hu_ref, wg_ref, wu_ref, bg_ref, bu_ref, a_ref):
        cg = _causal_conv(hg_ref[...], wg_ref[...], bg_ref[...])
        cu = _causal_conv(hu_ref[...], wu_ref[...], bu_ref[...])
        a_ref[...] = (cg * _sigmoid(cg) * cu).astype(a_ref.dtype)

    res = _call_with_side(
        body, name, (nseq, nf),
        [pl.BlockSpec((S, tc), lambda s, j: (s, j)), pl.BlockSpec((S, tc), lambda s, j: (s, nf + j)),
         pl.BlockSpec((3, tc), lambda s, j: (0, j)), pl.BlockSpec((3, tc), lambda s, j: (0, nf + j)),
         pl.BlockSpec((1, tc), lambda s, j: (0, j)), pl.BlockSpec((1, tc), lambda s, j: (0, nf + j))],
        [pl.BlockSpec((S, tc), lambda s, j: (s, j))], [jax.ShapeDtypeStruct((T, F), _MXU)], [],
        ("parallel", "parallel"), (h, h, cw, cw, cb, cb), side)
    return res if side else res[0]


def _conv_gate_bwd(h, da, cw, cb, nseq, name, tc=128, side=None):
    T, F2 = h.shape
    F, S = F2 // 2, T // nseq
    tc = _div_tile(F, tc, LANE)
    nf = F // tc

    def conv_bwd(dc, hx, w):
        dh = (w[2:3] * dc + w[1:2] * _shift_up(dc, 1)) + w[0:1] * _shift_up(dc, 2)
        dw = jnp.concatenate([_colsum(dc * _shift_down(hx, 2)), _colsum(dc * _shift_down(hx, 1)), _colsum(dc * hx)], axis=0)
        return dh, dw, _colsum(dc)

    def body(hg_ref, hu_ref, da_ref, wg_ref, wu_ref, bg_ref, bu_ref, dhg_ref, dhu_ref, dwg_ref, dwu_ref, dbg_ref, dbu_ref):
        hg, hu, da = hg_ref[...], hu_ref[...], da_ref[...]
        wg, wu = wg_ref[...], wu_ref[...]
        cg = _causal_conv(hg, wg, bg_ref[...])
        cu = _causal_conv(hu, wu, bu_ref[...])
        sg = _sigmoid(cg)
        dcu = da * (cg * sg)
        dcg = da * cu * (sg * (1.0 + cg * (1.0 - sg)))
        dhg, dwg, dbg = conv_bwd(dcg, hg, wg)
        dhu, dwu, dbu = conv_bwd(dcu, hu, wu)
        dhg_ref[...] = dhg.astype(dhg_ref.dtype)
        dhu_ref[...] = dhu.astype(dhu_ref.dtype)
        s = pl.program_id(1)
        _accumulate(dwg_ref, dwg, s)
        _accumulate(dwu_ref, dwu, s)
        _accumulate(dbg_ref, dbg, s)
        _accumulate(dbu_ref, dbu, s)

    col = lambda j, s: (s, j)
    par = lambda j, s: (0, j)
    dhg, dhu, dwg, dwu, dbg, dbu, *side_res = _call_with_side(
        body, name, (nf, nseq),
        [pl.BlockSpec((S, tc), col), pl.BlockSpec((S, tc), lambda j, s: (s, nf + j)), pl.BlockSpec((S, tc), col),
         pl.BlockSpec((3, tc), par), pl.BlockSpec((3, tc), lambda j, s: (0, nf + j)),
         pl.BlockSpec((1, tc), par), pl.BlockSpec((1, tc), lambda j, s: (0, nf + j))],
        [pl.BlockSpec((S, tc), col), pl.BlockSpec((S, tc), col), pl.BlockSpec((3, tc), par), pl.BlockSpec((3, tc), par),
         pl.BlockSpec((1, tc), par), pl.BlockSpec((1, tc), par)],
        [jax.ShapeDtypeStruct((T, F), _MXU), jax.ShapeDtypeStruct((T, F), _MXU), jax.ShapeDtypeStruct((3, F), f32),
         jax.ShapeDtypeStruct((3, F), f32), jax.ShapeDtypeStruct((1, F), f32), jax.ShapeDtypeStruct((1, F), f32)],
        [], ("parallel", "arbitrary"), (h, h, da, cw, cw, cb, cb), side)
    return dhg, dhu, jnp.concatenate([dwg, dwu], axis=1), jnp.concatenate([dbg, dbu], axis=1), side_res


def _group_row(x, jj):
    C, d = x.shape
    n = C // GLA_SUB
    x3 = x.reshape(n, GLA_SUB, d)
    return jnp.broadcast_to(x3[:, jj:jj + 1, :], (n, GLA_SUB, d)).reshape(C, d)


def _group_sum(x):
    C, d = x.shape
    n = C // GLA_SUB
    s = jnp.sum(x.reshape(n, GLA_SUB, d), axis=1, keepdims=True)
    return jnp.broadcast_to(s, (n, GLA_SUB, d)).reshape(C, d)


def _chunk_cumsum(g):
    C = g.shape[0]
    row = lax.broadcasted_iota(jnp.int32, (C, C), 0)
    col = lax.broadcasted_iota(jnp.int32, (C, C), 1)
    return _exact_dot((row >= col).astype(f32), g)


def _chunk_suffix_sum(x):
    C = x.shape[0]
    row = lax.broadcasted_iota(jnp.int32, (C, C), 0)
    col = lax.broadcasted_iota(jnp.int32, (C, C), 1)
    return _exact_dot((col >= row).astype(f32), x)


def _gla_scores(q, k, b):
    C = q.shape[0]
    n = C // GLA_SUB
    row = lax.broadcasted_iota(jnp.int32, (C, C), 0)
    col = lax.broadcasted_iota(jnp.int32, (C, C), 1)
    blocks = [jnp.zeros((GLA_SUB, C), f32)]
    for s in range(1, n):
        lo = s * GLA_SUB
        bref = b[lo - 1:lo, :]
        qr = q[lo:lo + GLA_SUB] * jnp.exp(b[lo:lo + GLA_SUB] - bref)
        kr = k * jnp.exp(jnp.minimum(bref - b, 0.0))
        blocks.append(_nt(qr, kr))
    sub_start = (row // GLA_SUB) * GLA_SUB
    a = jnp.where(col < sub_start, jnp.concatenate(blocks, axis=0), 0.0)
    rin = lax.broadcasted_iota(jnp.int32, (C, 1), 0) % GLA_SUB
    for jj in range(GLA_SUB):
        e = jnp.exp(jnp.minimum(b - _group_row(b, jj), 0.0))
        colv = jnp.sum(q * _group_row(k, jj) * e, axis=1, keepdims=True)
        colv = jnp.where(rin >= jj, colv, 0.0)
        a = jnp.where(col == sub_start + jj, colv, a)
    return a


def _gla_scores_bwd(da, q, k, b):
    C = q.shape[0]
    n = C // GLA_SUB
    row = lax.broadcasted_iota(jnp.int32, (C, C), 0)
    col = lax.broadcasted_iota(jnp.int32, (C, C), 1)
    sub_start = (row // GLA_SUB) * GLA_SUB
    da_inter = jnp.where(col < sub_start, da, 0.0)
    dq_blocks = [jnp.zeros((GLA_SUB, q.shape[1]), f32)]
    dk = jnp.zeros_like(k)
    for s in range(1, n):
        lo = s * GLA_SUB
        bref = b[lo - 1:lo, :]
        eq = jnp.exp(b[lo:lo + GLA_SUB] - bref)
        ek = jnp.exp(jnp.minimum(bref - b, 0.0))
        das = da_inter[lo:lo + GLA_SUB]
        dq_blocks.append(_nn(das, k * ek) * eq)
        dk = dk + _tn(das, q[lo:lo + GLA_SUB] * eq) * ek
    dq = jnp.concatenate(dq_blocks, axis=0)
    rin = lax.broadcasted_iota(jnp.int32, (C, 1), 0) % GLA_SUB
    for jj in range(GLA_SUB):
        e = jnp.exp(jnp.minimum(b - _group_row(b, jj), 0.0))
        dac = jnp.sum(jnp.where(col == sub_start + jj, da, 0.0), axis=1, keepdims=True)
        dac = jnp.where(rin >= jj, dac, 0.0)
        w = dac * e
        dq = dq + w * _group_row(k, jj)
        dk = dk + jnp.where(rin == jj, _group_sum(w * q), 0.0)
    return dq, dk


def _gla_specs(nC, dk, dv):
    H = GLA_HEADS
    voff = (2 * H * dk) // dv
    assert voff * dv == 2 * H * dk
    return H, voff


def _gla_fwd(proj, gate, nseq, name, side=None):
    T = proj.shape[0]
    dk = gate.shape[1] // GLA_HEADS
    dv = 2 * dk
    C = GLA_CHUNK
    nC = T // nseq // C
    H, voff = _gla_specs(nC, dk, dv)
    scale = dk ** -0.5

    def body(q_ref, k_ref, v_ref, g_ref, o_ref, st_ref, state):
        c = pl.program_id(2)

        @pl.when(c == 0)
        def _():
            state[...] = jnp.zeros_like(state)

        q, k, v = q_ref[...] * scale, k_ref[...], v_ref[...]
        b = _chunk_cumsum(g_ref[...])
        st = state[...]
        st_ref[0] = st
        a = _gla_scores(q, k, b)
        o_ref[...] = _nt(q * jnp.exp(b), st) + _nn(a, v)
        bl = b[C - 1:C, :]
        state[...] = st * jnp.exp(bl) + _tn(v, k * jnp.exp(bl - b))

    row = lambda s, h, c: s * nC + c
    return _call_with_side(
        body, name, (nseq, H, nC),
        [pl.BlockSpec((C, dk), lambda s, h, c: (row(s, h, c), h)),
         pl.BlockSpec((C, dk), lambda s, h, c: (row(s, h, c), H + h)),
         pl.BlockSpec((C, dv), lambda s, h, c: (row(s, h, c), voff + h)),
         pl.BlockSpec((C, dk), lambda s, h, c: (row(s, h, c), h))],
        [pl.BlockSpec((C, dv), lambda s, h, c: (row(s, h, c), h)),
         pl.BlockSpec((1, dv, dk), lambda s, h, c: ((s * H + h) * nC + c, 0, 0))],
        [jax.ShapeDtypeStruct((T, H * dv), f32), jax.ShapeDtypeStruct((nseq * H * nC, dv, dk), f32)],
        [pltpu.VMEM((dv, dk), f32)], ("parallel", "parallel", "arbitrary"), (proj, proj, proj, gate), side)


def _gla_bwd(proj, gate, states, do, nseq, name, side=None):
    T = proj.shape[0]
    dk = gate.shape[1] // GLA_HEADS
    dv = 2 * dk
    C = GLA_CHUNK
    nC = T // nseq // C
    H, voff = _gla_specs(nC, dk, dv)
    scale = dk ** -0.5

    def body(q_ref, k_ref, v_ref, g_ref, do_ref, st_ref, dq_ref, dk_ref, dv_ref, dg_ref, dstate, term):
        c = pl.program_id(2)

        @pl.when(c == 0)
        def _():
            dstate[...] = jnp.zeros_like(dstate)
            term[...] = jnp.zeros_like(term)

        q, k, v, do = q_ref[...] * scale, k_ref[...], v_ref[...], do_ref[...]
        b = _chunk_cumsum(g_ref[...])
        st = st_ref[0]
        dst = dstate[...]
        eb = jnp.exp(b)
        bl = b[C - 1:C, :]
        kdec = jnp.exp(bl - b)
        a = _gla_scores(q, k, b)
        rowi = lax.broadcasted_iota(jnp.int32, (C, C), 0)
        coli = lax.broadcasted_iota(jnp.int32, (C, C), 1)
        da = jnp.where(coli <= rowi, _nt(do, v), 0.0)
        dq_s, dk_s = _gla_scores_bwd(da, q, k, b)
        dq = _nn(do, st) * eb + dq_s
        dkk = _nn(v, dst) * kdec + dk_s
        dv_ref[...] = (_tn(a, do) + _nt(k * kdec, dst)).astype(dv_ref.dtype)
        last = lax.broadcasted_iota(jnp.int32, (C, 1), 0) == C - 1
        db = q * dq - k * dkk + jnp.where(last, term[...], 0.0)
        dg_ref[...] = _chunk_suffix_sum(db)
        dq_ref[...] = (dq * scale).astype(dq_ref.dtype)
        dk_ref[...] = dkk.astype(dk_ref.dtype)
        dprev = dst * jnp.exp(bl) + _tn(do, q * eb)
        dstate[...] = dprev
        term[...] = _colsum(st * dprev)

    row = lambda s, h, c: s * nC + (nC - 1 - c)
    kspec = lambda off: pl.BlockSpec((C, dk), lambda s, h, c: (row(s, h, c), off + h))
    vspec = lambda off: pl.BlockSpec((C, dv), lambda s, h, c: (row(s, h, c), off + h))
    return _call_with_side(
        body, name, (nseq, H, nC),
        [kspec(0), kspec(H), vspec(voff), kspec(0), vspec(0),
         pl.BlockSpec((1, dv, dk), lambda s, h, c: ((s * H + h) * nC + (nC - 1 - c), 0, 0))],
        [kspec(0), kspec(0), vspec(0), kspec(0)],
        [jax.ShapeDtypeStruct((T, H * dk), _MXU), jax.ShapeDtypeStruct((T, H * dk), _MXU),
         jax.ShapeDtypeStruct((T, H * dv), _MXU), jax.ShapeDtypeStruct((T, H * dk), f32)],
        [pltpu.VMEM((dv, dk), f32), pltpu.VMEM((1, dk), f32)], ("parallel", "parallel", "arbitrary"),
        (proj, proj, proj, gate, do, states), side)


def _head_slices(width, n):
    w = width // n
    return [slice(h * w, (h + 1) * w) for h in range(n)]


def _rms_gate_fwd(o, proj, ng, name):
    W = o.shape[1]

    def fn(o, r, ng):
        parts = []
        for sl in _head_slices(W, GLA_HEADS):
            oh = o[:, sl]
            rstd = lax.rsqrt(jnp.mean(oh * oh, axis=-1, keepdims=True) + RMS_EPS)
            rh = r[:, sl]
            parts.append((oh * rstd * ng) * (rh * _sigmoid(rh)))
        return jnp.concatenate(parts, axis=1)

    return _rowwise(fn, [_full(o), (proj, W, 2)], [ng], [(W, _MXU)], [], name)[0]


def _rms_gate_bwd(o, proj, ng, dy, name):
    W = o.shape[1]

    def fn(o, r, dy, ng):
        dos, drs = [], []
        dng = jnp.zeros((1, W // GLA_HEADS), f32)
        for sl in _head_slices(W, GLA_HEADS):
            oh, rh, dyh = o[:, sl], r[:, sl], dy[:, sl]
            rstd = lax.rsqrt(jnp.mean(oh * oh, axis=-1, keepdims=True) + RMS_EPS)
            ohat = oh * rstd
            sg = _sigmoid(rh)
            don = dyh * (rh * sg)
            drs.append(dyh * (ohat * ng) * (sg * (1.0 + rh * (1.0 - sg))))
            dng = dng + _colsum(don * ohat)
            dohat = don * ng
            dos.append(rstd * (dohat - ohat * jnp.mean(dohat * ohat, axis=-1, keepdims=True)))
        return jnp.concatenate(dos, axis=1), jnp.concatenate(drs, axis=1), dng

    return _rowwise(fn, [_full(o), (proj, W, 2), _full(dy)], [ng], [(W, _MXU), (W, _MXU)], [W // GLA_HEADS], name)


def _log_gate_fwd(z, bias, name):
    def fn(z, bias):
        t = z + bias
        return (jnp.minimum(t, 0.0) - jnp.log1p(jnp.exp(-jnp.abs(t)))) * (1.0 / GLA_GATE_NORMALIZER)

    return _rowwise(fn, [_full(z)], [bias], [(z.shape[1], f32)], [], name)[0]


def _log_gate_bwd(z, bias, dg, name):
    def fn(z, dg, bias):
        dz = dg * (1.0 / GLA_GATE_NORMALIZER) * _sigmoid(-(z + bias))
        return dz, _colsum(dz)

    return _rowwise(fn, [_full(z), _full(dg)], [bias], [(z.shape[1], f32)], [z.shape[1]], name)


def _band_masks(P, steps, has_prev):
    i = lax.broadcasted_iota(jnp.int32, (P, P), 0)
    j = lax.broadcasted_iota(jnp.int32, (P, P), 1)
    cur = (i - j >= 0) & (i - j <= steps)
    prev = (i + P - j <= steps) & has_prev
    return cur, prev


def _dil_dims(T, nseq, dilation):
    L = T // nseq // dilation
    P = min(DIL_BLOCK, L)
    return L, P, L // P


def _dil_tiling(T, nseq, dilation):
    L, P, nb = _dil_dims(T, nseq, dilation)
    hb = DIL_HEADS if dilation == 1 else 1
    row_sets = [pl.ds(r, P, stride=dilation) if dilation > 1 else pl.ds(0, P) for r in range(dilation)]
    head_cols = _head_slices(hb * DIL_HEAD_DIM, hb)
    return L, P, nb, P * dilation, hb, DIL_HEADS // hb, row_sets, head_cols


def _dil_fwd(proj, gi, window, dilation, nseq, name, side=None):
    T = proj.shape[0]
    H, dh = DIL_HEADS, DIL_HEAD_DIM
    L, P, nb, SB, hb, ng, row_sets, head_cols = _dil_tiling(T, nseq, dilation)
    steps = window // dilation
    scale = dh ** -0.5

    def body(q_ref, kc_ref, kp_ref, vc_ref, vp_ref, o_ref, lse_ref):
        sb, hg = pl.program_id(1), pl.program_id(2)
        mc, mp = _band_masks(P, steps, sb > 0)
        lane = lax.broadcasted_iota(jnp.int32, (P, LANE), 1)

        @pl.when(hg == 0)
        def _():
            lse_ref[...] = jnp.zeros_like(lse_ref)

        for hh, cols in enumerate(head_cols):
            for rows in row_sets:
                q = q_ref[rows, cols]
                sc = jnp.where(mc, _nt(q, kc_ref[rows, cols]) * scale, -jnp.inf)
                sp = jnp.where(mp, _nt(q, kp_ref[rows, cols]) * scale, -jnp.inf)
                m = jnp.maximum(jnp.max(sc, axis=-1, keepdims=True), jnp.max(sp, axis=-1, keepdims=True))
                pc, pp = jnp.exp(sc - m), jnp.exp(sp - m)
                l = jnp.sum(pc, axis=-1, keepdims=True) + jnp.sum(pp, axis=-1, keepdims=True)
                o_ref[rows, cols] = _nn(pc / l, vc_ref[rows, cols]) + _nn(pp / l, vp_ref[rows, cols])
                lse_ref[rows, :] = jnp.where(lane == hg * hb + hh, m + jnp.log(l), lse_ref[rows, :])

    cur = lambda part: pl.BlockSpec((SB, hb * dh), lambda s, sb, hg: (s * nb + sb, (gi * 3 + part) * ng + hg))
    prv = lambda part: pl.BlockSpec((SB, hb * dh), lambda s, sb, hg: (s * nb + jnp.maximum(sb - 1, 0), (gi * 3 + part) * ng + hg))
    return _call_with_side(
        body, name, (nseq, nb, ng), [cur(0), cur(1), prv(1), cur(2), prv(2)],
        [pl.BlockSpec((SB, hb * dh), lambda s, sb, hg: (s * nb + sb, hg)), pl.BlockSpec((SB, LANE), lambda s, sb, hg: (s * nb + sb, 0))],
        [jax.ShapeDtypeStruct((T, H * dh), f32), jax.ShapeDtypeStruct((T, LANE), f32)], [],
        ("parallel", "parallel", "arbitrary"), (proj, proj, proj, proj, proj), side)


def _dil_mix_fwd(os_, lses, name):
    W = os_[0].shape[1]
    G = len(os_)

    def fn(*a):
        o, l = a[:G], a[G:]
        lane = lax.broadcasted_iota(jnp.int32, l[0].shape, 1)
        tot = jnp.zeros(l[0].shape, f32)
        parts = []
        for h, sl in enumerate(_head_slices(W, DIL_HEADS)):
            lh = [x[:, h:h + 1] for x in l]
            m = functools.reduce(jnp.maximum, lh)
            e = [jnp.exp(x - m) for x in lh]
            z = functools.reduce(lambda u, v: u + v, e)
            acc = None
            for g in range(G):
                t = (e[g] / z) * o[g][:, sl]
                acc = t if acc is None else acc + t
            parts.append(acc)
            tot = jnp.where(lane == h, m + jnp.log(z), tot)
        y = jnp.concatenate(parts, axis=1)
        return y, y, tot

    return _rowwise(fn, [_full(x) for x in os_] + [_full(x) for x in lses], [], [(W, f32), (W, _MXU), (LANE, f32)], [], name)


def _dil_delta(do, o, name):
    W = o.shape[1]

    def fn(do, o):
        lane = lax.broadcasted_iota(jnp.int32, (do.shape[0], LANE), 1)
        d = jnp.zeros((do.shape[0], LANE), f32)
        for h, sl in enumerate(_head_slices(W, DIL_HEADS)):
            d = jnp.where(lane == h, jnp.sum(do[:, sl] * o[:, sl], axis=-1, keepdims=True), d)
        return d

    return _rowwise(fn, [_full(do), _full(o)], [], [(LANE, f32)], [], name)[0]


def _dil_bwd(proj, do, lse, delta, gi, window, dilation, nseq, name):
    T = proj.shape[0]
    H, dh = DIL_HEADS, DIL_HEAD_DIM
    L, P, nb, SB, hb, ng, row_sets, head_cols = _dil_tiling(T, nseq, dilation)
    steps = window // dilation
    scale = dh ** -0.5

    def probs(q, k, lse_h, mask):
        return jnp.where(mask, jnp.exp(_nt(q, k) * scale - lse_h), 0.0)

    def head_lane(ref, rows, h):
        lane = lax.broadcasted_iota(jnp.int32, (P, LANE), 1)
        return jnp.sum(jnp.where(lane == h, ref[rows, :], 0.0), axis=1, keepdims=True)

    def dq_body(q_ref, kc_ref, kp_ref, vc_ref, vp_ref, do_ref, lse_ref, del_ref, dq_ref):
        sb, hg = pl.program_id(1), pl.program_id(2)
        mc, mp = _band_masks(P, steps, sb > 0)
        for hh, cols in enumerate(head_cols):
            h = hg * hb + hh
            for rows in row_sets:
                q, doh = q_ref[rows, cols], do_ref[rows, cols]
                lse_h, del_h = head_lane(lse_ref, rows, h), head_lane(del_ref, rows, h)
                kc, kp = kc_ref[rows, cols], kp_ref[rows, cols]
                dsc = probs(q, kc, lse_h, mc) * (_nt(doh, vc_ref[rows, cols]) - del_h) * scale
                dsp = probs(q, kp, lse_h, mp) * (_nt(doh, vp_ref[rows, cols]) - del_h) * scale
                dq_ref[rows, cols] = _nn(dsc, kc) + _nn(dsp, kp)

    cur = lambda part: pl.BlockSpec((SB, hb * dh), lambda s, sb, hg: (s * nb + sb, (gi * 3 + part) * ng + hg))
    prv = lambda part: pl.BlockSpec((SB, hb * dh), lambda s, sb, hg: (s * nb + jnp.maximum(sb - 1, 0), (gi * 3 + part) * ng + hg))
    tok = pl.BlockSpec((SB, hb * dh), lambda s, sb, hg: (s * nb + sb, hg))
    aux = pl.BlockSpec((SB, LANE), lambda s, sb, hg: (s * nb + sb, 0))
    dq = pl.pallas_call(
        dq_body, name=name + "_dq", grid=(nseq, nb, ng),
        in_specs=[cur(0), cur(1), prv(1), cur(2), prv(2), tok, aux, aux],
        out_specs=tok, out_shape=jax.ShapeDtypeStruct((T, H * dh), f32),
        compiler_params=_cparams(("parallel", "parallel", "parallel")),
    )(proj, proj, proj, proj, proj, do, lse, delta)

    def dkv_body(k_ref, v_ref, qc_ref, qn_ref, doc_ref, don_ref, lsec_ref, lsen_ref, delc_ref, deln_ref, dk_ref, dv_ref):
        sb, hg = pl.program_id(1), pl.program_id(2)
        mc, mn = _band_masks(P, steps, sb < nb - 1)
        for hh, cols in enumerate(head_cols):
            h = hg * hb + hh
            for rows in row_sets:
                k, v = k_ref[rows, cols], v_ref[rows, cols]
                qc, doc = qc_ref[rows, cols], doc_ref[rows, cols]
                pc = probs(qc, k, head_lane(lsec_ref, rows, h), mc)
                dsc = pc * (_nt(doc, v) - head_lane(delc_ref, rows, h)) * scale
                qn, don = qn_ref[rows, cols], don_ref[rows, cols]
                pn = probs(qn, k, head_lane(lsen_ref, rows, h), mn)
                dsn = pn * (_nt(don, v) - head_lane(deln_ref, rows, h)) * scale
                dk_ref[rows, cols] = _tn(dsc, qc) + _tn(dsn, qn)
                dv_ref[rows, cols] = _tn(pc, doc) + _tn(pn, don)

    nxt = lambda s, sb: s * nb + jnp.minimum(sb + 1, nb - 1)
    qnx = pl.BlockSpec((SB, hb * dh), lambda s, sb, hg: (nxt(s, sb), gi * 3 * ng + hg))
    tokn = pl.BlockSpec((SB, hb * dh), lambda s, sb, hg: (nxt(s, sb), hg))
    auxn = pl.BlockSpec((SB, LANE), lambda s, sb, hg: (nxt(s, sb), 0))
    dkk, dvv = pl.pallas_call(
        dkv_body, name=name + "_dkv", grid=(nseq, nb, ng),
        in_specs=[cur(1), cur(2), cur(0), qnx, tok, tokn, aux, auxn, aux, auxn],
        out_specs=[tok, tok],
        out_shape=[jax.ShapeDtypeStruct((T, H * dh), f32), jax.ShapeDtypeStruct((T, H * dh), f32)],
        compiler_params=_cparams(("parallel", "parallel", "parallel")),
    )(proj, proj, proj, proj, do, do, lse, lse, delta, delta)
    return dq, dkk, dvv


def _adamw_math(w, g, m, v):
    m = ADAM_B1 * m + (1.0 - ADAM_B1) * g
    v = ADAM_B2 * v + (1.0 - ADAM_B2) * (g * g)
    m_hat = m / (1.0 - ADAM_B1 ** ADAM_STEP)
    v_hat = v / (1.0 - ADAM_B2 ** ADAM_STEP)
    return -ADAM_LR * (m_hat / (jnp.sqrt(v_hat) + ADAM_EPS) + ADAM_WD * w), m, v


def _adamw(w, g, m, v, name):
    W = w.shape[1]
    return _rowwise(_adamw_math, [_full(w), _full(g), _full(m), _full(v)], [], [(W, f32)] * 3, [], name, tm=512)


def _position():
    x, y, c = lax.axis_index("x"), lax.axis_index("y"), lax.axis_index("c")
    other_chips = [(1 - x, y), (x, 1 - y), (1 - x, 1 - y)]
    return x, y, c, other_chips


def _chip_index(x, y):
    return 2 * x + y


_ANY = pl.BlockSpec(memory_space=pl.ANY)


def _all_reduce_small(p, name):
    R, Wd = p.shape

    def body(p_ref, o_ref, buf, send_sems, recv_sems):
        x, y, c, _ = _position()
        me = 4 * x + 2 * y + c
        buf[me] = p_ref[...]
        copies = []
        for k in range(1, 8):
            fx, fy, fc = (k >> 2) & 1, (k >> 1) & 1, k & 1
            peer = (x + fx - 2 * x * fx, y + fy - 2 * y * fy, c + fc - 2 * c * fc)
            cp = pltpu.make_async_remote_copy(src_ref=p_ref, dst_ref=buf.at[me], send_sem=send_sems.at[k - 1],
                                              recv_sem=recv_sems.at[k - 1], device_id=peer, device_id_type=MESH)
            cp.start()
            copies.append(cp)
        for cp in copies:
            cp.wait()
        acc = buf[0]
        for s in range(1, 8):
            acc = acc + buf[s]
        o_ref[...] = acc

    return pl.pallas_call(
        body, name=name, out_shape=jax.ShapeDtypeStruct((R, Wd), f32),
        in_specs=[pl.BlockSpec(memory_space=pltpu.VMEM)], out_specs=pl.BlockSpec(memory_space=pltpu.VMEM),
        scratch_shapes=[pltpu.VMEM((8, R, Wd), f32), pltpu.SemaphoreType.DMA((7,)), pltpu.SemaphoreType.DMA((7,))],
        compiler_params=pltpu.CompilerParams(vmem_limit_bytes=VMEM_LIMIT),
    )(p)


def _layer_half(ref, h, axis):
    n = ref.shape[axis] // 2
    idx = (slice(None),) * axis + (pl.ds(h * n, n),)
    return ref.at[idx]


def _comm_call(body, name, ins, out_shapes, n_sems, n_local=0):
    scratch = [pltpu.SemaphoreType.DMA((n_sems,)), pltpu.SemaphoreType.DMA((n_sems,))]
    if n_local:
        scratch.append(pltpu.SemaphoreType.DMA((n_local,)))
    return pl.pallas_call(body, name=name, out_shape=out_shapes, in_specs=[_ANY] * len(ins), out_specs=[_ANY] * len(out_shapes),
                          scratch_shapes=scratch)(*ins)


class _SideCopies:
    def __init__(self, ins, outs, scratch, start, finish):
        self.ins, self.outs, self.scratch, self.start, self.finish = ins, outs, scratch, start, finish


def _row_half(ref, h):
    r = ref.shape[0] // 2
    start = h * r
    if r % 16 == 0:
        start = pl.multiple_of(start, 16)
    return ref.at[pl.ds(start, r)]


def _gather_plan(ws):
    n = len(ws)

    def copy(o_refs, sems, k, src, i, chip_idx, h, to):
        return pltpu.make_async_remote_copy(src_ref=src, dst_ref=_row_half(o_refs[i].at[chip_idx], h), send_sem=sems[0].at[k],
                                            recv_sem=sems[1].at[k], device_id=to, device_id_type=MESH)

    def own_copy(w_refs, o_refs, sems, i, p):
        return pltpu.make_async_copy(w_refs[i], o_refs[i].at[p], sems[2].at[i])

    def over_ici(w_refs, o_refs, sems, i, j, chip, dst_chip_idx, c):
        return copy(o_refs, sems, 3 * i + j, _row_half(w_refs[i], c), i, dst_chip_idx, c, (*chip, c))

    def start(w_refs, o_refs, *sems):
        x, y, c, chips = _position()
        p = _chip_index(x, y)
        for i in range(n):
            own_copy(w_refs, o_refs, sems, i, p).start()
            for j, chip in enumerate(chips):
                over_ici(w_refs, o_refs, sems, i, j, chip, p, c).start()

    def finish(w_refs, o_refs, *sems):
        x, y, c, chips = _position()
        p = _chip_index(x, y)
        sibling = (x, y, 1 - c)
        passed = []
        for i in range(n):
            for j, chip in enumerate(chips):
                q = _chip_index(*chip)
                over_ici(w_refs, o_refs, sems, i, j, chip, q, c).wait_recv()
                fwd = copy(o_refs, sems, 3 * n + 3 * i + j, _row_half(o_refs[i].at[q], c), i, q, c, sibling)
                fwd.start()
                passed.append(fwd)
        for i in range(n):
            for j, chip in enumerate(chips):
                copy(o_refs, sems, 3 * n + 3 * i + j, _row_half(w_refs[i], c), i, _chip_index(*chip), 1 - c, sibling).wait_recv()
        for i in range(n):
            for j, chip in enumerate(chips):
                over_ici(w_refs, o_refs, sems, i, j, chip, p, c).wait_send()
            own_copy(w_refs, o_refs, sems, i, p).wait()
        for fwd in passed:
            fwd.wait_send()

    scratch = [pltpu.SemaphoreType.DMA((6 * n,)), pltpu.SemaphoreType.DMA((6 * n,)), pltpu.SemaphoreType.DMA((n,))]
    return _SideCopies(list(ws), [jax.ShapeDtypeStruct((4,) + w.shape, w.dtype) for w in ws], scratch, start, finish)


def _run_copies(plan, name):
    n_i, n_o = len(plan.ins), len(plan.outs)

    def body(*refs):
        plan.start(refs[:n_i], refs[n_i:n_i + n_o], *refs[n_i + n_o:])
        plan.finish(refs[:n_i], refs[n_i:n_i + n_o], *refs[n_i + n_o:])

    return pl.pallas_call(body, name=name, out_shape=plan.outs, in_specs=[_ANY] * n_i, out_specs=[_ANY] * n_o,
                          scratch_shapes=plan.scratch)(*plan.ins)


def _sibling_halves(gs, name):
    n = len(gs)

    def body(*refs):
        g_refs, o_refs, send_sems, recv_sems = refs[:n], refs[n:2 * n], refs[2 * n], refs[2 * n + 1]
        x, y, c, _ = _position()
        copies = []
        for i in range(n):
            cp = pltpu.make_async_remote_copy(src_ref=_layer_half(g_refs[i], 1 - c, 1), dst_ref=o_refs[i], send_sem=send_sems.at[i],
                                              recv_sem=recv_sems.at[i], device_id=(x, y, 1 - c), device_id_type=MESH)
            cp.start()
            copies.append(cp)
        for cp in copies:
            cp.wait()

    outs = [jax.ShapeDtypeStruct((4, g.shape[1] // 2) + g.shape[2:], g.dtype) for g in gs]
    return _comm_call(body, name, gs, outs, n)


def _exchange_plan(hs):
    n = len(hs)

    def copies(h_refs, o_refs, send_sems, recv_sems):
        x, y, c, chips = _position()
        return [pltpu.make_async_remote_copy(src_ref=h_refs[i].at[_chip_index(*chip)], dst_ref=o_refs[i].at[j],
                                             send_sem=send_sems.at[3 * i + j], recv_sem=recv_sems.at[3 * i + j],
                                             device_id=(*chip, c), device_id_type=MESH)
                for i in range(n) for j, chip in enumerate(chips)]

    def start(h_refs, o_refs, *sems):
        for cp in copies(h_refs, o_refs, *sems):
            cp.start()

    def finish(h_refs, o_refs, *sems):
        for cp in copies(h_refs, o_refs, *sems):
            cp.wait()

    scratch = [pltpu.SemaphoreType.DMA((3 * n,)), pltpu.SemaphoreType.DMA((3 * n,))]
    return _SideCopies(list(hs), [jax.ShapeDtypeStruct((3,) + h.shape[1:], h.dtype) for h in hs], scratch, start, finish)


def _sibling_swap(ts, name):
    n = len(ts)

    def body(*refs):
        t_refs, o_refs, send_sems, recv_sems = refs[:n], refs[n:2 * n], refs[2 * n], refs[2 * n + 1]
        x, y, c, _ = _position()
        copies = []
        for i in range(n):
            cp = pltpu.make_async_remote_copy(src_ref=t_refs[i], dst_ref=o_refs[i], send_sem=send_sems.at[i], recv_sem=recv_sems.at[i],
                                              device_id=(x, y, 1 - c), device_id_type=MESH)
            cp.start()
            copies.append(cp)
        for cp in copies:
            cp.wait()

    return _comm_call(body, name, ts, [jax.ShapeDtypeStruct(t.shape, t.dtype) for t in ts], n)


BLOCK_ELEMS = 384 * 1024


def _block_2d(rows, cols, sub):
    tns = [t for t in range(LANE, cols + 1, LANE) if cols % t == 0] if cols % LANE == 0 else [cols]
    tms = [t for t in range(sub, rows + 1, sub) if rows % t == 0] or [rows]
    fits = [(tm * tn, tn, tm) for tm in tms for tn in tns if tm * tn <= BLOCK_ELEMS]
    assert fits, (rows, cols, sub)
    _, tn, tm = max(fits)
    return tm, tn


def _prefetch_call(body, name, scalars, grid, in_specs, out_specs, out_shape, args, sem):
    gs = pltpu.PrefetchScalarGridSpec(num_scalar_prefetch=1, grid=grid, in_specs=in_specs, out_specs=out_specs)
    return pl.pallas_call(body, name=name, grid_spec=gs, out_shape=out_shape, compiler_params=_cparams(sem))(scalars, *args)


def _add_own_half(g, got, c, name):
    _, nl, K, N = g.shape
    hl = nl // 2
    tm, tn = _block_2d(K, N, 16)

    def body(c_ref, g_ref, r_ref, o_ref):
        o_ref[...] = (g_ref[...].astype(f32) + r_ref[...].astype(f32)).astype(o_ref.dtype)

    blk = (1, 1, tm, tn)
    return _prefetch_call(
        body, name, jnp.reshape(c, (1,)).astype(jnp.int32), (4, hl, K // tm, N // tn),
        [pl.BlockSpec(blk, lambda s, l, i, j, c_ref: (s, c_ref[0] * hl + l, i, j)), pl.BlockSpec(blk, lambda s, l, i, j, c_ref: (s, l, i, j))],
        pl.BlockSpec(blk, lambda s, l, i, j, c_ref: (s, l, i, j)), jax.ShapeDtypeStruct((4, hl, K, N), g.dtype), (g, got),
        ("parallel",) * 4)


def _add_chips(h, got, p, name):
    _, nl, K, N = h.shape
    tm, tn = _block_2d(K, N, 16)

    def body(p_ref, h_ref, r0_ref, r1_ref, r2_ref, o_ref):
        o_ref[...] = ((h_ref[0].astype(f32) + r0_ref[0].astype(f32)) + r1_ref[0].astype(f32)) + r2_ref[0].astype(f32)

    blk = (1, 1, tm, tn)
    got_spec = lambda q: pl.BlockSpec(blk, lambda l, i, j, p_ref: (q, l, i, j))
    return _prefetch_call(
        body, name, jnp.reshape(p, (1,)).astype(jnp.int32), (nl, K // tm, N // tn),
        [pl.BlockSpec(blk, lambda l, i, j, p_ref: (p_ref[0], l, i, j)), got_spec(0), got_spec(1), got_spec(2)],
        pl.BlockSpec((1, tm, tn), lambda l, i, j, p_ref: (l, i, j)), jax.ShapeDtypeStruct((nl, K, N), f32), (h, got, got, got),
        ("parallel",) * 3)


def _adamw_halves(w, m, v, own, other, c, name):
    nl, K, N = w.shape
    tm, tn = _block_2d(K // 2, N, 8)
    nb = K // 2 // tm

    def body(c_ref, w_ref, m_ref, v_ref, own_ref, other_ref, g_out, d_out, m_out, v_out):
        g = jnp.where(pl.program_id(1) == c_ref[0], own_ref[...], other_ref[...])
        d, m_new, v_new = _adamw_math(w_ref[...], g, m_ref[...], v_ref[...])
        g_out[...] = g
        d_out[...] = d
        m_out[...] = m_new
        v_out[...] = v_new

    blk = (1, tm, tn)
    full = pl.BlockSpec(blk, lambda l, h, i, j, c_ref: (l, h * nb + i, j))
    half = pl.BlockSpec(blk, lambda l, h, i, j, c_ref: (l, i, j))
    return _prefetch_call(
        body, name, jnp.reshape(c, (1,)).astype(jnp.int32), (nl, 2, nb, N // tn), [full, full, full, half, half], [full] * 4,
        [jax.ShapeDtypeStruct((nl, K, N), f32)] * 4, (w, m, v, own, other), ("parallel",) * 4)


def _row_halves_view(g):
    return g.reshape(4, 2, g.shape[1] // 2, g.shape[2])


def _reduce_scatter_start(gs, name):
    c = lax.axis_index("c")
    views = [_row_halves_view(g) for g in gs]
    from_sibling = _sibling_halves(views, name + "_d2d")
    return [_add_own_half(g, r, c, f"{name}_add2_{i}") for i, (g, r) in enumerate(zip(views, from_sibling))]


def _reduce_scatter_finish(hs, got, name):
    p = _chip_index(lax.axis_index("x"), lax.axis_index("y"))
    return [_add_chips(h, r, p, f"{name}_add4_{i}")[0] for i, (h, r) in enumerate(zip(hs, got))]


_BIG = (("gla_w_in", 1), ("gla_w_out", 0), ("dil_w_in", 1), ("dil_w_out", 1), ("ffn_w_up", 1), ("ffn_w_down", 0))


def _pad_rows(a, mult):
    r = (-a.shape[0]) % mult
    return a if r == 0 else jnp.concatenate([a, jnp.zeros((r,) + a.shape[1:], a.dtype)], axis=0)


def _unshard(blocks, axis):
    _, K, N = blocks.shape
    if axis == 1:
        return blocks.transpose(1, 0, 2).reshape(K, 4 * N)
    return blocks.reshape(4 * K, N)


def _to_shards(mat, axis):
    K, N = mat.shape
    if axis == 1:
        return mat.reshape(K, 4, N // 4).transpose(1, 0, 2)
    return mat.reshape(4, K // 4, N)


_SMALL = (("gla_w_gate_up", 2), ("gla_gate_bias", None), ("gla_norm_g", None), ("ffn_conv_w", 2), ("ffn_conv_b", None),
          ("ln_g", 2), ("ln_b", 2))


def _pack_rows(arrs, width=LANE):
    flat = _pad_rows(jnp.concatenate([a.reshape(-1) for a in arrs]), 8 * width)
    return flat.reshape(-1, width)


def _unpack_rows(packed, shapes):
    flat, out, off = packed.reshape(-1), [], 0
    for s in shapes:
        n = 1
        for d in s:
            n *= d
        out.append(flat[off:off + n].reshape(s))
        off += n
    return out


def _gather_small_params(shards):
    x, y, c = lax.axis_index("x"), lax.axis_index("y"), lax.axis_index("c")
    names = [n for n, axis in _SMALL if axis is not None]
    mine = _pack_rows([shards[n] for n in names])
    mine = jnp.where(c == 0, mine, jnp.zeros_like(mine))
    rows = mine.shape[0]
    placed = lax.dynamic_update_slice(jnp.zeros((4 * rows, LANE), f32), mine, (_chip_index(x, y) * rows, 0))
    allp = _all_reduce_small(placed, "gather_small").reshape(4, rows, LANE)
    out = {n: shards[n] for n, axis in _SMALL if axis is None}
    per_chip = [_unpack_rows(allp[q], [shards[n].shape for n in names]) for q in range(4)]
    for i, n in enumerate(names):
        out[n] = jnp.concatenate([per_chip[q][i] for q in range(4)], axis=2)
    return out


def _reduce_small_grads(grads, shards):
    names = [n for n, _ in _SMALL]
    total = _all_reduce_small(_pack_rows([grads[n] for n in names]), "reduce_small")
    full = dict(zip(names, _unpack_rows(total, [grads[n].shape for n in names])))
    p = _chip_index(lax.axis_index("x"), lax.axis_index("y"))
    out = {}
    for n, axis in _SMALL:
        if axis is None:
            out[n] = full[n]
        else:
            w = shards[n].shape[axis]
            out[n] = lax.dynamic_slice_in_dim(full[n], p * w, w, axis=axis)
    return out


def _pad_cols(a, n):
    return a if a.shape[-1] == n else jnp.concatenate([a, jnp.zeros(a.shape[:-1] + (n - a.shape[-1],), a.dtype)], axis=-1)


def _ffn_width(F):
    return -(-F // 512) * 512


def kernel(x, gla_w_in, gla_w_gate_up, gla_gate_bias, gla_norm_g, gla_w_out, dil_w_in, dil_w_out, ffn_w_up, ffn_conv_w, ffn_conv_b, ffn_w_down, ln_g, ln_b, loss_target, m_gla_w_in, m_gla_w_gate_up, m_gla_gate_bias, m_gla_norm_g, m_gla_w_out, m_dil_w_in, m_dil_w_out, m_ffn_w_up, m_ffn_conv_w, m_ffn_conv_b, m_ffn_w_down, m_ln_g, m_ln_b, v_gla_w_in, v_gla_w_gate_up, v_gla_gate_bias, v_gla_norm_g, v_gla_w_out, v_dil_w_in, v_dil_w_out, v_ffn_w_up, v_ffn_conv_w, v_ffn_conv_b, v_ffn_w_down, v_ln_g, v_ln_b):
    names = ["gla_w_in", "gla_w_gate_up", "gla_gate_bias", "gla_norm_g", "gla_w_out", "dil_w_in", "dil_w_out", "ffn_w_up",
             "ffn_conv_w", "ffn_conv_b", "ffn_w_down", "ln_g", "ln_b"]
    w_sh = dict(zip(names, (gla_w_in, gla_w_gate_up, gla_gate_bias, gla_norm_g, gla_w_out, dil_w_in, dil_w_out, ffn_w_up,
                            ffn_conv_w, ffn_conv_b, ffn_w_down, ln_g, ln_b)))
    m_sh = dict(zip(names, (m_gla_w_in, m_gla_w_gate_up, m_gla_gate_bias, m_gla_norm_g, m_gla_w_out, m_dil_w_in, m_dil_w_out,
                            m_ffn_w_up, m_ffn_conv_w, m_ffn_conv_b, m_ffn_w_down, m_ln_g, m_ln_b)))
    v_sh = dict(zip(names, (v_gla_w_in, v_gla_w_gate_up, v_gla_gate_bias, v_gla_norm_g, v_gla_w_out, v_dil_w_in, v_dil_w_out,
                            v_ffn_w_up, v_ffn_conv_w, v_ffn_conv_b, v_ffn_w_down, v_ln_g, v_ln_b)))
    nseq, S, D = x.shape
    T = nseq * S
    small = _gather_small_params(w_sh)
    F = 4 * w_sh["ffn_w_down"].shape[1]
    Fp = _ffn_width(F)
    qkvr = 4 * w_sh["gla_w_in"].shape[2] - GLA_GATE_RANK
    c_idx = lax.axis_index("c")
    shard_axis = dict(_BIG)

    def pad_halves(a):
        return jnp.concatenate([_pad_cols(a[..., :F], Fp), _pad_cols(a[..., F:], Fp)], axis=-1)

    cw_all = pad_halves(small["ffn_conv_w"])
    cb_all = pad_halves(small["ffn_conv_b"][:, None, :])
    w_gate_up_all = jnp.pad(small["gla_w_gate_up"].astype(bf16), ((0, 0), (0, GATE_PAD - GLA_GATE_RANK), (0, 0)))

    def mixer_names(l):
        return ("gla_w_in", "gla_w_out") if l % 2 == 0 else ("dil_w_in", "dil_w_out")

    def shard_of(l, which):
        n_in, n_out = mixer_names(l)
        if which in ("w_up_a", "w_up_b"):
            half = w_sh["ffn_w_up"].shape[2] // 2
            shard = w_sh["ffn_w_up"][l].T
            return (shard[:half] if which == "w_up_a" else shard[half:]).astype(bf16)
        name, idx = {"w_in": (n_in, l // 2), "w_out": (n_out, l // 2), "w_down": ("ffn_w_down", l)}[which]
        return w_sh[name][idx].astype(bf16)

    weights = [dict() for _ in range(DEPTH)]

    def install(l, which, gathered):
        n_in, n_out = mixer_names(l)
        if which == "w_in":
            weights[l]["w_in"] = w_in = _unshard(gathered, shard_axis[n_in])
            if l % 2 == 0:
                weights[l]["w_gate"] = _pad_cols(w_in[:, qkvr:], GATE_PAD)
        elif which == "w_out":
            weights[l]["w_out"] = _unshard(gathered, shard_axis[n_out])
        elif which in ("w_up_a", "w_up_b"):
            weights[l][which] = gathered
            if "w_up_a" in weights[l] and "w_up_b" in weights[l]:
                both = jnp.concatenate([weights[l].pop("w_up_a"), weights[l].pop("w_up_b")], axis=1)
                wt = _unshard(both, 0)
                weights[l]["w_up_t"] = jnp.concatenate([_pad_rows(wt[:F], Fp), _pad_rows(wt[F:], Fp)], axis=0)
        else:
            weights[l]["w_down"] = _pad_rows(_unshard(gathered, shard_axis["ffn_w_down"]), Fp)

    kinds = ("w_in", "w_out", "w_up", "w_down")
    install(0, "w_in", _run_copies(_gather_plan([shard_of(0, "w_in")]), "gather_first")[0])

    riders = [{"proj": [(0, "w_out"), (0, "w_down")], "core": [(0, "w_up_a"), (0, "w_up_b")], "ffn_up": [(1, "w_in")]},
              {"proj": [(0, "w_out"), (0, "w_down")], "attn0": [(0, "w_up_a")], "core": [(0, "w_up_b")], "ffn_up": [(1, "w_in")]}]

    def ridden(l, kernel):
        return [(l + dl, which) for dl, which in riders[l % 2].get(kernel, []) if l + dl < DEPTH]

    def side_for(l, kernel):
        items = ridden(l, kernel)
        return _gather_plan([shard_of(m, which) for m, which in items]) if items else None

    def carried(l, kernel, res, n_own=1):
        items = ridden(l, kernel)
        if not items:
            return res
        for (m, which), gathered in zip(items, res[n_own:]):
            install(m, which, gathered)
        return res[0] if n_own == 1 else res[:n_own]

    h0 = x.reshape(T, D)
    saved = []
    cur, cur_b = h0, h0.astype(_MXU)
    fwd = dict(tm=1024, tn=1024, tk=2048)
    for i in range(DEPTH):
        j = i // 2
        tag = f"l{i}_"
        lg, lb = small["ln_g"][i], small["ln_b"][i]
        W = weights[i]
        if i % 2 == 0:
            gate_bias, norm_g = small["gla_gate_bias"][j][None, :], small["gla_norm_g"][j][None, :]
            proj = carried(i, "proj", _mm(cur_b, W["w_in"], "nn", tag + "gla_proj", n_out=qkvr, side=side_for(i, "proj"), **fwd))
            g_low = _mm(cur_b, W["w_gate"], "nn", tag + "gla_glow", tn=GATE_PAD)
            z = _mm(g_low, w_gate_up_all, "nn", tag + "gla_z", layer=j)
            gate = _log_gate_fwd(z, gate_bias, tag + "gla_gate")
            o, states = carried(i, "core", _gla_fwd(proj, gate, nseq, tag + "gla_core", side=side_for(i, "core")), n_own=2)
            y_b = _rms_gate_fwd(o, proj, norm_g, tag + "gla_norm")
            mix = _mm(y_b, W["w_out"], "nn", tag + "gla_out", **fwd)
            mixer_saved = (proj, g_low, z, gate, states, o, y_b)
        else:
            proj = carried(i, "proj", _mm(cur_b, W["w_in"], "nn", tag + "dil_proj", side=side_for(i, "proj"), **fwd))
            outs, lses = [], []
            for gi, (window, dilation) in enumerate(DIL_PATTERNS):
                kern = {0: "attn0", 1: "core"}.get(gi)
                og, lg_ = carried(i, kern, _dil_fwd(proj, gi, window, dilation, nseq, tag + f"dil_attn{gi}", side=side_for(i, kern)), n_own=2)
                outs.append(og)
                lses.append(lg_)
            y, y_b, lse_tot = _dil_mix_fwd(outs, lses, tag + "dil_mix")
            mix = _mm(y_b, W["w_out"], "nn", tag + "dil_out", **fwd)
            mixer_saved = (proj, y, y_b, lse_tot)
        x1, x1_b = _ln_fwd(cur, mix, lg[0:1], lb[0:1], tag + "ln1")
        cw, cb = cw_all[i], cb_all[i]
        hh = carried(i, "ffn_up", _mm(x1_b, W["w_up_t"], "nt", tag + "ffn_up", side=side_for(i, "ffn_up"), **fwd))
        act_b = carried(i, "conv", _conv_gate_fwd(hh, cw, cb, nseq, tag + "ffn_conv", side=side_for(i, "conv")))
        ffn = carried(i, "ffn_down", _mm(act_b, W["w_down"], "nn", tag + "ffn_down", tm=1024, tn=512, tk=Fp, side=side_for(i, "ffn_down")))
        x2, x2_b = _ln_fwd(x1, ffn, lg[1:2], lb[1:2], tag + "ln2")
        saved.append((cur, cur_b, mix, x1, x1_b, hh, act_b, ffn, mixer_saved))
        cur, cur_b = x2, x2_b

    dy, sq = _loss_head(cur, loss_target.reshape(T, D), "loss_head")
    loss = lax.psum(0.5 * jnp.sum(sq) / D, ("x", "y", "c"))

    gb = {n: [None] * w_sh[n].shape[0] for n in names}
    own_half = {n: [None] * w_sh[n].shape[0] for n, _ in _BIG}
    bwd_riders = [{"down_dx": "w_down", "conv": "w_up", "core_bwd": "w_in", "up_dx": "w_out"},
                  {"down_dx": "w_down", "conv": "w_up", "gate_dw": "w_in", "up_dx": "w_out"}]
    riding = {"layer": None, "partial": None}

    def grad_slot(l, which):
        n_in, n_out = mixer_names(l)
        return {"w_in": (n_in, l // 2), "w_out": (n_out, l // 2), "w_up": ("ffn_w_up", l), "w_down": ("ffn_w_down", l)}[which]

    def rider_of(kernel):
        return None if riding["layer"] is None else bwd_riders[(riding["layer"] - 1) % 2].get(kernel)

    def ride(kernel):
        which = rider_of(kernel)
        return _exchange_plan([riding["partial"][which]]) if which else None

    def landed(kernel, got):
        which = rider_of(kernel)
        if which:
            l = riding["layer"]
            n, idx = grad_slot(l, which)
            own_half[n][idx] = _reduce_scatter_finish([riding["partial"][which]], got, f"scatter_l{l}_{which}")[0]

    def unwrap(kernel, res, n_own=1):
        if rider_of(kernel) is None:
            return res
        landed(kernel, res[n_own:])
        return res[0] if n_own == 1 else res[:n_own]

    d_res = None
    d_in = dy
    for i in reversed(range(DEPTH)):
        j = i // 2
        tag = f"l{i}_b_"
        xin, xin_b, mix, x1, x1_b, hh, act_b, ffn, mixer_saved = saved[i]
        lg = small["ln_g"][i]
        cw, cb = cw_all[i], cb_all[i]
        dw_tiles = dict(tm=1024, tn=1024, tk=4096, out_dtype=bf16)
        dx_tiles = dict(tm=1024, tn=512, tk=6144)
        dys, scales = ([d_in], [1.0]) if d_res is None else ([d_res, d_in], [DEEPNORM_ALPHA, 1.0])
        du2, du2_b, dg2, db2 = _ln_bwd(x1, ffn, lg[1:2], dys, scales, tag + "ln2")
        gb["ffn_w_down"][i] = _mm(act_b, du2_b, "tn", tag + "ffn_down_dw", **dw_tiles)[:F]
        W = weights[i]
        dact = unwrap("down_dx", _mm(du2_b, W["w_down"], "nt", tag + "ffn_down_dx", side=ride("down_dx"), **dx_tiles))
        dhg, dhu, dcw, dcb, rode = _conv_gate_bwd(hh, dact, cw, cb, nseq, tag + "ffn_conv", side=ride("conv"))
        landed("conv", rode)
        gb["ffn_conv_w"][i] = jnp.concatenate([dcw[:, :F], dcw[:, Fp:Fp + F]], axis=1)
        gb["ffn_conv_b"][i] = jnp.concatenate([dcb[0, :F], dcb[0, Fp:Fp + F]], axis=0)
        dwg_t = unwrap("gate_dw", _mm(dhg, x1_b, "tn", tag + "ffn_gate_dw", side=ride("gate_dw"), **dw_tiles))
        dwu_t = _mm(dhu, x1_b, "tn", tag + "ffn_up_dw", **dw_tiles)
        gb["ffn_w_up"][i] = jnp.concatenate([dwg_t[:F], dwu_t[:F]], axis=0)
        dx1g = _mm(dhg, W["w_up_t"], "nn", tag + "ffn_gate_dx", **dx_tiles)
        dx1u = unwrap("up_dx", _mm(dhu, W["w_up_t"], "nn", tag + "ffn_up_dx", b_k0=Fp, side=ride("up_dx"), **dx_tiles))
        du1, du1_b, dg1, db1 = _ln_bwd(xin, mix, lg[0:1], [du2, dx1g, dx1u], [DEEPNORM_ALPHA, 1.0, 1.0], tag + "ln1")
        gb["ln_g"][i] = jnp.concatenate([dg1, dg2], axis=0)
        gb["ln_b"][i] = jnp.concatenate([db1, db2], axis=0)
        if i % 2 == 0:
            proj, g_low, z, gate, states, o, y_b = mixer_saved
            gate_bias, norm_g = small["gla_gate_bias"][j][None, :], small["gla_norm_g"][j][None, :]
            gb["gla_w_out"][j] = _mm(y_b, du1_b, "tn", tag + "gla_out_dw", **dw_tiles)
            dyy = _mm(du1_b, W["w_out"], "nt", tag + "gla_out_dx", **dx_tiles)
            do, dr, dng = _rms_gate_bwd(o, proj, norm_g, dyy, tag + "gla_norm")
            gb["gla_norm_g"][j] = dng[0]
            dq, dk_, dv_, dgate = unwrap("core_bwd", _gla_bwd(proj, gate, states, do, nseq, tag + "gla_core", side=ride("core_bwd")), n_own=4)
            dz, dbias = _log_gate_bwd(z, gate_bias, dgate, tag + "gla_gate")
            gb["gla_gate_bias"][j] = dbias[0]
            gb["gla_w_gate_up"][j] = _mm(g_low, dz, "tn", tag + "gla_z_dw", tk=1024)[:GLA_GATE_RANK]
            dg_low = _mm(dz, w_gate_up_all, "nt", tag + "gla_z_dx", tn=GATE_PAD, layer=j)
            dproj = jnp.concatenate([dq, dk_, dv_, dr], axis=1)
            dw_main = _mm(xin_b, dproj, "tn", tag + "gla_proj_dw", **dw_tiles)
            dw_gate = _mm(xin_b, dg_low, "tn", tag + "gla_glow_dw", tm=1024, tn=GATE_PAD, tk=2048, out_dtype=bf16)[:, :GLA_GATE_RANK]
            gb["gla_w_in"][j] = jnp.concatenate([dw_main, dw_gate], axis=1)
            dxa = _mm(dproj, W["w_in"], "nt", tag + "gla_proj_dx", **dx_tiles)
            dxb = _mm(dg_low, W["w_gate"], "nt", tag + "gla_glow_dx")
            d_in = _axpy(dxa, dxb, 1.0, tag + "gla_dx_sum")
        else:
            proj, y, y_b, lse_tot = mixer_saved
            gb["dil_w_out"][j] = _mm(y_b, du1_b, "tn", tag + "dil_out_dw", **dw_tiles)
            dyy = _mm(du1_b, W["w_out"], "nt", tag + "dil_out_dx", **dx_tiles)
            delta = _dil_delta(dyy, y, tag + "dil_delta")
            pieces = []
            for gi, (window, dilation) in enumerate(DIL_PATTERNS):
                pieces += list(_dil_bwd(proj, dyy, lse_tot, delta, gi, window, dilation, nseq, tag + f"dil_attn{gi}"))
            dproj = jnp.concatenate(pieces, axis=1).astype(_MXU)
            gb["dil_w_in"][j] = _mm(xin_b, dproj, "tn", tag + "dil_proj_dw", **dw_tiles)
            d_in = _mm(dproj, W["w_in"], "nt", tag + "dil_proj_dx", **dx_tiles)
        d_res = du1
        local = [_to_shards(gb[n][idx], 0 if n == "ffn_w_up" else shard_axis[n]) for n, idx in (grad_slot(i, k) for k in kinds)]
        riding["layer"], riding["partial"] = i, dict(zip(kinds, _reduce_scatter_start(local, f"scatter_l{i}")))
    grad_x = _axpy(d_res, d_in, DEEPNORM_ALPHA, "grad_x").reshape(x.shape)

    last = [riding["partial"][k] for k in kinds]
    for k, own in zip(kinds, _reduce_scatter_finish(last, _run_copies(_exchange_plan(last), "scatter_l0_ici"), "scatter_l0")):
        n, idx = grad_slot(0, k)
        own_half[n][idx] = own
    own_all = [jnp.stack(own_half[n], axis=0) for n, _ in _BIG]
    other_all = _sibling_swap(own_all, "scatter_swap")
    grads, delta, new_m, new_v = {}, {}, {}, {}
    for (n, _), own, other in zip(_BIG, own_all, other_all):
        view = (lambda a: jnp.swapaxes(a, 1, 2)) if n == "ffn_w_up" else (lambda a: a)
        res = _adamw_halves(view(w_sh[n]), view(m_sh[n]), view(v_sh[n]), own, other, c_idx, "adamw_" + n)
        grads[n], delta[n], new_m[n], new_v[n] = [view(r) for r in res]
    grads.update(_reduce_small_grads({n: jnp.stack(gb[n], axis=0) for n, _ in _SMALL}, w_sh))
    small_names = [n for n, _ in _SMALL]
    packed = [_pack_rows([src[n] for n in small_names]) for src in (w_sh, grads, m_sh, v_sh)]
    res = _adamw(*packed, "adamw_small")
    shapes = [w_sh[n].shape for n in small_names]
    for dst, arr in zip((delta, new_m, new_v), res):
        dst.update(dict(zip(small_names, _unpack_rows(arr, shapes))))

    return (loss, grad_x, *[grads[n] for n in names], *[delta[n] for n in names], *[new_m[n] for n in names],
            *[new_v[n] for n in names])
```

```python
import functools

import jax
import jax.numpy as jnp
from jax import lax
from jax.experimental import pallas as pl
from jax.experimental.pallas import tpu as pltpu

f32 = jnp.float32
bf16 = jnp.bfloat16
_MXU = jnp.bfloat16

DEPTH = 4
GLA_HEADS = 4
GLA_GATE_RANK = 16
GLA_GATE_NORMALIZER = 16.0
GLA_CHUNK = 64
GLA_SUB = 16
DIL_PATTERNS = ((128, 1), (512, 4), (2048, 16))
DIL_HEADS = 8
DIL_HEAD_DIM = 128
DIL_BLOCK = 128
DEEPNORM_ALPHA = (2 * DEPTH) ** 0.25
LN_EPS = 1e-5
RMS_EPS = 1e-6
ADAM_LR = 0.001
ADAM_B1 = 0.9
ADAM_B2 = 0.999
ADAM_EPS = 1e-08
ADAM_WD = 0.01
ADAM_STEP = 10

LANE = 128
VMEM_LIMIT = 48 * 1024 * 1024
GATE_PAD = LANE
MESH = pl.DeviceIdType.MESH


def _cparams(sem=None):
    return pltpu.CompilerParams(dimension_semantics=sem, vmem_limit_bytes=VMEM_LIMIT)


def _div_tile(n, pref, unit):
    if n <= pref:
        return n
    best = None
    for t in range(unit, pref + 1, unit):
        if n % t == 0:
            best = t
    assert best is not None, (n, pref, unit)
    return best


def _dot(a, b, ca, cb):
    return lax.dot_general(a.astype(_MXU), b.astype(_MXU), (((ca,), (cb,)), ((), ())), preferred_element_type=f32)


def _nn(a, b):
    return _dot(a, b, 1, 0)


def _nt(a, b):
    return _dot(a, b, 1, 1)


def _tn(a, b):
    return _dot(a, b, 0, 0)


def _exact_dot(a, b):
    return jnp.dot(a, b, precision=lax.Precision.HIGHEST, preferred_element_type=f32)


def _sigmoid(x):
    return 1.0 / (1.0 + jnp.exp(-x))


def _call_with_side(body, name, grid, in_specs, out_specs, out_shape, scratch, sem, args, side):
    if side is None:
        return pl.pallas_call(body, name=name, grid=grid, in_specs=in_specs, out_specs=out_specs, out_shape=out_shape,
                              scratch_shapes=scratch, compiler_params=_cparams(sem))(*args)
    n_in, n_out, n_scr = len(in_specs), len(out_specs), len(scratch)
    n_si, n_so = len(side.ins), len(side.outs)

    def wrapped(*refs):
        ins, s_in = refs[:n_in], refs[n_in:n_in + n_si]
        outs, s_out = refs[n_in + n_si:n_in + n_si + n_out], refs[n_in + n_si + n_out:n_in + n_si + n_out + n_so]
        rest = refs[n_in + n_si + n_out + n_so:]
        sems = rest[n_scr:]
        ids = [pl.program_id(ax) for ax in range(len(grid))]
        first = functools.reduce(lambda u, v: u & v, [i == 0 for i in ids])
        last = functools.reduce(lambda u, v: u & v, [i == g - 1 for i, g in zip(ids, grid)])

        @pl.when(first)
        def _():
            side.start(s_in, s_out, *sems)

        body(*ins, *outs, *rest[:n_scr])

        @pl.when(last)
        def _():
            side.finish(s_in, s_out, *sems)

    return pl.pallas_call(
        wrapped, name=name, grid=grid, in_specs=list(in_specs) + [_ANY] * n_si, out_specs=list(out_specs) + [_ANY] * n_so,
        out_shape=list(out_shape) + list(side.outs), scratch_shapes=list(scratch) + list(side.scratch),
        compiler_params=_cparams(("arbitrary",) * len(grid)))(*args, *side.ins)


def _mm(a, b, mode, name, tm=1024, tn=512, tk=2048, out_dtype=f32, layer=None, n_out=None, b_k0=0, side=None):
    if mode == "nn":
        (M, K), N = a.shape, (n_out or b.shape[-1])
    elif mode == "nt":
        (M, K), N = a.shape, b.shape[-2]
    else:
        (K, M), N = a.shape, b.shape[-1]
    tm, tn, tk = _div_tile(M, tm, LANE), _div_tile(N, tn, LANE), _div_tile(K, tk, LANE)
    nk = K // tk
    if mode == "tn":
        a_spec = pl.BlockSpec((tk, tm), lambda i, j, k: (k, i))
    else:
        a_spec = pl.BlockSpec((tm, tk), lambda i, j, k: (i, k))
    lead = () if layer is None else (None,)
    pre = () if layer is None else (layer,)
    k0 = b_k0 // tk
    assert k0 * tk == b_k0 and (mode != "tn" or b_k0 == 0)
    if mode == "nt":
        b_spec = pl.BlockSpec(lead + (tn, tk), lambda i, j, k: pre + (j, k + k0))
    else:
        b_spec = pl.BlockSpec(lead + (tk, tn), lambda i, j, k: pre + (k + k0, j))
    ca, cb = {"nn": (1, 0), "nt": (1, 1), "tn": (0, 0)}[mode]

    def body(a_ref, b_ref, o_ref, *acc):
        p = _dot(a_ref[...], b_ref[...], ca, cb)
        if nk == 1:
            o_ref[...] = p.astype(o_ref.dtype)
        else:
            k = pl.program_id(2)
            acc_ref = acc[0]

            @pl.when(k == 0)
            def _():
                acc_ref[...] = p

            @pl.when(k > 0)
            def _():
                acc_ref[...] += p

            @pl.when(k == nk - 1)
            def _():
                o_ref[...] = acc_ref[...].astype(o_ref.dtype)

    res = _call_with_side(
        body, name, (M // tm, N // tn, nk), [a_spec, b_spec], [pl.BlockSpec((tm, tn), lambda i, j, k: (i, j))],
        [jax.ShapeDtypeStruct((M, N), out_dtype)], [pltpu.VMEM((tm, tn), f32)] if nk > 1 else [],
        ("parallel", "parallel", "arbitrary"), (a, b), side)
    return res if side else res[0]


def _rowwise(fn, rows, consts, outs, reds, name, tm=256):
    T = rows[0][0].shape[0]
    tm = _div_tile(T, tm, 8)
    n_r, n_c, n_o = len(rows), len(consts), len(outs)

    def body(*refs):
        ins = [r[...] for r in refs[: n_r + n_c]]
        res = fn(*ins)
        res = res if isinstance(res, (tuple, list)) else (res,)
        o_refs = refs[n_r + n_c: n_r + n_c + n_o]
        r_refs = refs[n_r + n_c + n_o:]
        for ref, val in zip(o_refs, res[:n_o]):
            ref[...] = val.astype(ref.dtype)
        i = pl.program_id(0)
        for ref, val in zip(r_refs, res[n_o:]):
            _accumulate(ref, val, i)

    in_specs = [pl.BlockSpec((tm, w), functools.partial(lambda i, cb: (i, cb), cb=cb)) for (_, w, cb) in rows]
    in_specs += [pl.BlockSpec(c.shape, lambda i: (0, 0)) for c in consts]
    out_specs = [pl.BlockSpec((tm, w), lambda i: (i, 0)) for (w, _) in outs]
    out_specs += [pl.BlockSpec((1, w), lambda i: (0, 0)) for w in reds]
    out_shape = [jax.ShapeDtypeStruct((T, w), dt) for (w, dt) in outs]
    out_shape += [jax.ShapeDtypeStruct((1, w), f32) for w in reds]
    return pl.pallas_call(
        body, name=name, grid=(T // tm,), in_specs=in_specs, out_specs=out_specs, out_shape=out_shape,
        compiler_params=_cparams(("arbitrary",)),
    )(*[r[0] for r in rows], *consts)


def _accumulate(ref, val, step):
    @pl.when(step == 0)
    def _():
        ref[...] = val

    @pl.when(step > 0)
    def _():
        ref[...] += val


def _full(a):
    return (a, a.shape[1], 0)


def _colsum(x):
    return jnp.sum(x, axis=0, keepdims=True)


def _ln_stats(u):
    mu = jnp.mean(u, axis=-1, keepdims=True)
    xc = u - mu
    var = jnp.mean(xc * xc, axis=-1, keepdims=True)
    rstd = lax.rsqrt(var + LN_EPS)
    return xc * rstd, rstd


def _ln_fwd(x, f, g, b, name):
    def fn(x, f, g, b):
        xhat, _ = _ln_stats(DEEPNORM_ALPHA * x + f)
        y = xhat * g + b
        return y, y

    return _rowwise(fn, [_full(x), _full(f)], [g, b], [(x.shape[1], f32), (x.shape[1], _MXU)], [], name)


def _ln_bwd(x, f, g, dys, scales, name):
    def fn(x, f, *rest):
        g = rest[-1]
        dy = None
        for d, s in zip(rest[:-1], scales):
            t = d if s == 1.0 else s * d
            dy = t if dy is None else dy + t
        xhat, rstd = _ln_stats(DEEPNORM_ALPHA * x + f)
        dxh = dy * g
        m1 = jnp.mean(dxh, axis=-1, keepdims=True)
        m2 = jnp.mean(dxh * xhat, axis=-1, keepdims=True)
        du = rstd * (dxh - m1 - xhat * m2)
        return du, du, _colsum(dy * xhat), _colsum(dy)

    D = x.shape[1]
    return _rowwise(fn, [_full(x), _full(f)] + [_full(d) for d in dys], [g], [(D, f32), (D, _MXU)], [D, D], name)


def _loss_head(y, t, name):
    D = y.shape[1]

    def fn(y, t):
        e = y - t
        return e * (1.0 / D), _colsum(e * e)

    return _rowwise(fn, [_full(y), _full(t)], [], [(D, f32)], [D], name)


def _axpy(a, b, alpha, name):
    def fn(a, b):
        return alpha * a + b

    return _rowwise(fn, [_full(a), _full(b)], [], [(a.shape[1], f32)], [], name)[0]


def _shift_down(x, k):
    row = lax.broadcasted_iota(jnp.int32, x.shape, 0)
    return jnp.where(row >= k, pltpu.roll(x, k, 0), 0.0)


def _shift_up(x, k):
    S = x.shape[0]
    row = lax.broadcasted_iota(jnp.int32, x.shape, 0)
    return jnp.where(row < S - k, pltpu.roll(x, S - k, 0), 0.0)


def _causal_conv(h, w, b):
    return ((b + w[0:1] * _shift_down(h, 2)) + w[1:2] * _shift_down(h, 1)) + w[2:3] * h


def _conv_gate_fwd(h, cw, cb, nseq, name, tc=256, side=None):
    T, F2 = h.shape
    F, S = F2 // 2, T // nseq
    tc = _div_tile(F, tc, LANE)
    nf = F // tc

    def body(hg_ref, hu_ref, wg_ref, wu_ref, bg_ref, bu_ref, a_ref):
        cg = _causal_conv(hg_ref[...], wg_ref[...], bg_ref[...])
        cu = _causal_conv(hu_ref[...], wu_ref[...], bu_ref[...])
        a_ref[...] = (cg * _sigmoid(cg) * cu).astype(a_ref.dtype)

    res = _call_with_side(
        body, name, (nseq, nf),
        [pl.BlockSpec((S, tc), lambda s, j: (s, j)), pl.BlockSpec((S, tc), lambda s, j: (s, nf + j)),
         pl.BlockSpec((3, tc), lambda s, j: (0, j)), pl.BlockSpec((3, tc), lambda s, j: (0, nf + j)),
         pl.BlockSpec((1, tc), lambda s, j: (0, j)), pl.BlockSpec((1, tc), lambda s, j: (0, nf + j))],
        [pl.BlockSpec((S, tc), lambda s, j: (s, j))], [jax.ShapeDtypeStruct((T, F), _MXU)], [],
        ("parallel", "parallel"), (h, h, cw, cw, cb, cb), side)
    return res if side else res[0]


def _conv_gate_bwd(h, da, cw, cb, nseq, name, tc=128, side=None):
    T, F2 = h.shape
    F, S = F2 // 2, T // nseq
    tc = _div_tile(F, tc, LANE)
    nf = F // tc

    def conv_bwd(dc, hx, w):
        dh = (w[2:3] * dc + w[1:2] * _shift_up(dc, 1)) + w[0:1] * _shift_up(dc, 2)
        dw = jnp.concatenate([_colsum(dc * _shift_down(hx, 2)), _colsum(dc * _shift_down(hx, 1)), _colsum(dc * hx)], axis=0)
        return dh, dw, _colsum(dc)

    def body(hg_ref, hu_ref, da_ref, wg_ref, wu_ref, bg_ref, bu_ref, dhg_ref, dhu_ref, dwg_ref, dwu_ref, dbg_ref, dbu_ref):
        hg, hu, da = hg_ref[...], hu_ref[...], da_ref[...]
        wg, wu = wg_ref[...], wu_ref[...]
        cg = _causal_conv(hg, wg, bg_ref[...])
        cu = _causal_conv(hu, wu, bu_ref[...])
        sg = _sigmoid(cg)
        dcu = da * (cg * sg)
        dcg = da * cu * (sg * (1.0 + cg * (1.0 - sg)))
        dhg, dwg, dbg = conv_bwd(dcg, hg, wg)
        dhu, dwu, dbu = conv_bwd(dcu, hu, wu)
        dhg_ref[...] = dhg.astype(dhg_ref.dtype)
        dhu_ref[...] = dhu.astype(dhu_ref.dtype)
        s = pl.program_id(1)
        _accumulate(dwg_ref, dwg, s)
        _accumulate(dwu_ref, dwu, s)
        _accumulate(dbg_ref, dbg, s)
        _accumulate(dbu_ref, dbu, s)

    col = lambda j, s: (s, j)
    par = lambda j, s: (0, j)
    dhg, dhu, dwg, dwu, dbg, dbu, *side_res = _call_with_side(
        body, name, (nf, nseq),
        [pl.BlockSpec((S, tc), col), pl.BlockSpec((S, tc), lambda j, s: (s, nf + j)), pl.BlockSpec((S, tc), col),
         pl.BlockSpec((3, tc), par), pl.BlockSpec((3, tc), lambda j, s: (0, nf + j)),
         pl.BlockSpec((1, tc), par), pl.BlockSpec((1, tc), lambda j, s: (0, nf + j))],
        [pl.BlockSpec((S, tc), col), pl.BlockSpec((S, tc), col), pl.BlockSpec((3, tc), par), pl.BlockSpec((3, tc), par),
         pl.BlockSpec((1, tc), par), pl.BlockSpec((1, tc), par)],
        [jax.ShapeDtypeStruct((T, F), _MXU), jax.ShapeDtypeStruct((T, F), _MXU), jax.ShapeDtypeStruct((3, F), f32),
         jax.ShapeDtypeStruct((3, F), f32), jax.ShapeDtypeStruct((1, F), f32), jax.ShapeDtypeStruct((1, F), f32)],
        [], ("parallel", "arbitrary"), (h, h, da, cw, cw, cb, cb), side)
    return dhg, dhu, jnp.concatenate([dwg, dwu], axis=1), jnp.concatenate([dbg, dbu], axis=1), side_res


def _group_row(x, jj):
    C, d = x.shape
    n = C // GLA_SUB
    x3 = x.reshape(n, GLA_SUB, d)
    return jnp.broadcast_to(x3[:, jj:jj + 1, :], (n, GLA_SUB, d)).reshape(C, d)


def _group_sum(x):
    C, d = x.shape
    n = C // GLA_SUB
    s = jnp.sum(x.reshape(n, GLA_SUB, d), axis=1, keepdims=True)
    return jnp.broadcast_to(s, (n, GLA_SUB, d)).reshape(C, d)


def _chunk_cumsum(g):
    C = g.shape[0]
    row = lax.broadcasted_iota(jnp.int32, (C, C), 0)
    col = lax.broadcasted_iota(jnp.int32, (C, C), 1)
    return _exact_dot((row >= col).astype(f32), g)


def _chunk_suffix_sum(x):
    C = x.shape[0]
    row = lax.broadcasted_iota(jnp.int32, (C, C), 0)
    col = lax.broadcasted_iota(jnp.int32, (C, C), 1)
    return _exact_dot((col >= row).astype(f32), x)


def _gla_scores(q, k, b):
    C = q.shape[0]
    n = C // GLA_SUB
    row = lax.broadcasted_iota(jnp.int32, (C, C), 0)
    col = lax.broadcasted_iota(jnp.int32, (C, C), 1)
    blocks = [jnp.zeros((GLA_SUB, C), f32)]
    for s in range(1, n):
        lo = s * GLA_SUB
        bref = b[lo - 1:lo, :]
        qr = q[lo:lo + GLA_SUB] * jnp.exp(b[lo:lo + GLA_SUB] - bref)
        kr = k * jnp.exp(jnp.minimum(bref - b, 0.0))
        blocks.append(_nt(qr, kr))
    sub_start = (row // GLA_SUB) * GLA_SUB
    a = jnp.where(col < sub_start, jnp.concatenate(blocks, axis=0), 0.0)
    rin = lax.broadcasted_iota(jnp.int32, (C, 1), 0) % GLA_SUB
    for jj in range(GLA_SUB):
        e = jnp.exp(jnp.minimum(b - _group_row(b, jj), 0.0))
        colv = jnp.sum(q * _group_row(k, jj) * e, axis=1, keepdims=True)
        colv = jnp.where(rin >= jj, colv, 0.0)
        a = jnp.where(col == sub_start + jj, colv, a)
    return a


def _gla_scores_bwd(da, q, k, b):
    C = q.shape[0]
    n = C // GLA_SUB
    row = lax.broadcasted_iota(jnp.int32, (C, C), 0)
    col = lax.broadcasted_iota(jnp.int32, (C, C), 1)
    sub_start = (row // GLA_SUB) * GLA_SUB
    da_inter = jnp.where(col < sub_start, da, 0.0)
    dq_blocks = [jnp.zeros((GLA_SUB, q.shape[1]), f32)]
    dk = jnp.zeros_like(k)
    for s in range(1, n):
        lo = s * GLA_SUB
        bref = b[lo - 1:lo, :]
        eq = jnp.exp(b[lo:lo + GLA_SUB] - bref)
        ek = jnp.exp(jnp.minimum(bref - b, 0.0))
        das = da_inter[lo:lo + GLA_SUB]
        dq_blocks.append(_nn(das, k * ek) * eq)
        dk = dk + _tn(das, q[lo:lo + GLA_SUB] * eq) * ek
    dq = jnp.concatenate(dq_blocks, axis=0)
    rin = lax.broadcasted_iota(jnp.int32, (C, 1), 0) % GLA_SUB
    for jj in range(GLA_SUB):
        e = jnp.exp(jnp.minimum(b - _group_row(b, jj), 0.0))
        dac = jnp.sum(jnp.where(col == sub_start + jj, da, 0.0), axis=1, keepdims=True)
        dac = jnp.where(rin >= jj, dac, 0.0)
        w = dac * e
        dq = dq + w * _group_row(k, jj)
        dk = dk + jnp.where(rin == jj, _group_sum(w * q), 0.0)
    return dq, dk


def _gla_specs(nC, dk, dv):
    H = GLA_HEADS
    voff = (2 * H * dk) // dv
    assert voff * dv == 2 * H * dk
    return H, voff


def _gla_fwd(proj, gate, nseq, name, side=None):
    T = proj.shape[0]
    dk = gate.shape[1] // GLA_HEADS
    dv = 2 * dk
    C = GLA_CHUNK
    nC = T // nseq // C
    H, voff = _gla_specs(nC, dk, dv)
    scale = dk ** -0.5

    def body(q_ref, k_ref, v_ref, g_ref, o_ref, st_ref, state):
        c = pl.program_id(2)

        @pl.when(c == 0)
        def _():
            state[...] = jnp.zeros_like(state)

        q, k, v = q_ref[...] * scale, k_ref[...], v_ref[...]
        b = _chunk_cumsum(g_ref[...])
        st = state[...]
        st_ref[0] = st
        a = _gla_scores(q, k, b)
        o_ref[...] = _nt(q * jnp.exp(b), st) + _nn(a, v)
        bl = b[C - 1:C, :]
        state[...] = st * jnp.exp(bl) + _tn(v, k * jnp.exp(bl - b))

    row = lambda s, h, c: s * nC + c
    return _call_with_side(
        body, name, (nseq, H, nC),
        [pl.BlockSpec((C, dk), lambda s, h, c: (row(s, h, c), h)),
         pl.BlockSpec((C, dk), lambda s, h, c: (row(s, h, c), H + h)),
         pl.BlockSpec((C, dv), lambda s, h, c: (row(s, h, c), voff + h)),
         pl.BlockSpec((C, dk), lambda s, h, c: (row(s, h, c), h))],
        [pl.BlockSpec((C, dv), lambda s, h, c: (row(s, h, c), h)),
         pl.BlockSpec((1, dv, dk), lambda s, h, c: ((s * H + h) * nC + c, 0, 0))],
        [jax.ShapeDtypeStruct((T, H * dv), f32), jax.ShapeDtypeStruct((nseq * H * nC, dv, dk), f32)],
        [pltpu.VMEM((dv, dk), f32)], ("parallel", "parallel", "arbitrary"), (proj, proj, proj, gate), side)


def _gla_bwd(proj, gate, states, do, nseq, name, side=None):
    T = proj.shape[0]
    dk = gate.shape[1] // GLA_HEADS
    dv = 2 * dk
    C = GLA_CHUNK
    nC = T // nseq // C
    H, voff = _gla_specs(nC, dk, dv)
    scale = dk ** -0.5

    def body(q_ref, k_ref, v_ref, g_ref, do_ref, st_ref, dq_ref, dk_ref, dv_ref, dg_ref, dstate, term):
        c = pl.program_id(2)

        @pl.when(c == 0)
        def _():
            dstate[...] = jnp.zeros_like(dstate)
            term[...] = jnp.zeros_like(term)

        q, k, v, do = q_ref[...] * scale, k_ref[...], v_ref[...], do_ref[...]
        b = _chunk_cumsum(g_ref[...])
        st = st_ref[0]
        dst = dstate[...]
        eb = jnp.exp(b)
        bl = b[C - 1:C, :]
        kdec = jnp.exp(bl - b)
        a = _gla_scores(q, k, b)
        rowi = lax.broadcasted_iota(jnp.int32, (C, C), 0)
        coli = lax.broadcasted_iota(jnp.int32, (C, C), 1)
        da = jnp.where(coli <= rowi, _nt(do, v), 0.0)
        dq_s, dk_s = _gla_scores_bwd(da, q, k, b)
        dq = _nn(do, st) * eb + dq_s
        dkk = _nn(v, dst) * kdec + dk_s
        dv_ref[...] = (_tn(a, do) + _nt(k * kdec, dst)).astype(dv_ref.dtype)
        last = lax.broadcasted_iota(jnp.int32, (C, 1), 0) == C - 1
        db = q * dq - k * dkk + jnp.where(last, term[...], 0.0)
        dg_ref[...] = _chunk_suffix_sum(db)
        dq_ref[...] = (dq * scale).astype(dq_ref.dtype)
        dk_ref[...] = dkk.astype(dk_ref.dtype)
        dprev = dst * jnp.exp(bl) + _tn(do, q * eb)
        dstate[...] = dprev
        term[...] = _colsum(st * dprev)

    row = lambda s, h, c: s * nC + (nC - 1 - c)
    kspec = lambda off: pl.BlockSpec((C, dk), lambda s, h, c: (row(s, h, c), off + h))
    vspec = lambda off: pl.BlockSpec((C, dv), lambda s, h, c: (row(s, h, c), off + h))
    return _call_with_side(
        body, name, (nseq, H, nC),
        [kspec(0), kspec(H), vspec(voff), kspec(0), vspec(0),
         pl.BlockSpec((1, dv, dk), lambda s, h, c: ((s * H + h) * nC + (nC - 1 - c), 0, 0))],
        [kspec(0), kspec(0), vspec(0), kspec(0)],
        [jax.ShapeDtypeStruct((T, H * dk), _MXU), jax.ShapeDtypeStruct((T, H * dk), _MXU),
         jax.ShapeDtypeStruct((T, H * dv), _MXU), jax.ShapeDtypeStruct((T, H * dk), f32)],
        [pltpu.VMEM((dv, dk), f32), pltpu.VMEM((1, dk), f32)], ("parallel", "parallel", "arbitrary"),
        (proj, proj, proj, gate, do, states), side)


def _head_slices(width, n):
    w = width // n
    return [slice(h * w, (h + 1) * w) for h in range(n)]


def _rms_gate_fwd(o, proj, ng, name):
    W = o.shape[1]

    def fn(o, r, ng):
        parts = []
        for sl in _head_slices(W, GLA_HEADS):
            oh = o[:, sl]
            rstd = lax.rsqrt(jnp.mean(oh * oh, axis=-1, keepdims=True) + RMS_EPS)
            rh = r[:, sl]
            parts.append((oh * rstd * ng) * (rh * _sigmoid(rh)))
        return jnp.concatenate(parts, axis=1)

    return _rowwise(fn, [_full(o), (proj, W, 2)], [ng], [(W, _MXU)], [], name)[0]


def _rms_gate_bwd(o, proj, ng, dy, name):
    W = o.shape[1]

    def fn(o, r, dy, ng):
        dos, drs = [], []
        dng = jnp.zeros((1, W // GLA_HEADS), f32)
        for sl in _head_slices(W, GLA_HEADS):
            oh, rh, dyh = o[:, sl], r[:, sl], dy[:, sl]
            rstd = lax.rsqrt(jnp.mean(oh * oh, axis=-1, keepdims=True) + RMS_EPS)
            ohat = oh * rstd
            sg = _sigmoid(rh)
            don = dyh * (rh * sg)
            drs.append(dyh * (ohat * ng) * (sg * (1.0 + rh * (1.0 - sg))))
            dng = dng + _colsum(don * ohat)
            dohat = don * ng
            dos.append(rstd * (dohat - ohat * jnp.mean(dohat * ohat, axis=-1, keepdims=True)))
        return jnp.concatenate(dos, axis=1), jnp.concatenate(drs, axis=1), dng

    return _rowwise(fn, [_full(o), (proj, W, 2), _full(dy)], [ng], [(W, _MXU), (W, _MXU)], [W // GLA_HEADS], name)


def _log_gate_fwd(z, bias, name):
    def fn(z, bias):
        t = z + bias
        return (jnp.minimum(t, 0.0) - jnp.log1p(jnp.exp(-jnp.abs(t)))) * (1.0 / GLA_GATE_NORMALIZER)

    return _rowwise(fn, [_full(z)], [bias], [(z.shape[1], f32)], [], name)[0]


def _log_gate_bwd(z, bias, dg, name):
    def fn(z, dg, bias):
        dz = dg * (1.0 / GLA_GATE_NORMALIZER) * _sigmoid(-(z + bias))
        return dz, _colsum(dz)

    return _rowwise(fn, [_full(z), _full(dg)], [bias], [(z.shape[1], f32)], [z.shape[1]], name)


def _band_masks(P, steps, has_prev):
    i = lax.broadcasted_iota(jnp.int32, (P, P), 0)
    j = lax.broadcasted_iota(jnp.int32, (P, P), 1)
    cur = (i - j >= 0) & (i - j <= steps)
    prev = (i + P - j <= steps) & has_prev
    return cur, prev


def _dil_dims(T, nseq, dilation):
    L = T // nseq // dilation
    P = min(DIL_BLOCK, L)
    return L, P, L // P


def _dil_tiling(T, nseq, dilation):
    L, P, nb = _dil_dims(T, nseq, dilation)
    hb = DIL_HEADS if dilation == 1 else 1
    row_sets = [pl.ds(r, P, stride=dilation) if dilation > 1 else pl.ds(0, P) for r in range(dilation)]
    head_cols = _head_slices(hb * DIL_HEAD_DIM, hb)
    return L, P, nb, P * dilation, hb, DIL_HEADS // hb, row_sets, head_cols


def _dil_fwd(proj, gi, window, dilation, nseq, name, side=None):
    T = proj.shape[0]
    H, dh = DIL_HEADS, DIL_HEAD_DIM
    L, P, nb, SB, hb, ng, row_sets, head_cols = _dil_tiling(T, nseq, dilation)
    steps = window // dilation
    scale = dh ** -0.5

    def body(q_ref, kc_ref, kp_ref, vc_ref, vp_ref, o_ref, lse_ref):
        sb, hg = pl.program_id(1), pl.program_id(2)
        mc, mp = _band_masks(P, steps, sb > 0)
        lane = lax.broadcasted_iota(jnp.int32, (P, LANE), 1)

        @pl.when(hg == 0)
        def _():
            lse_ref[...] = jnp.zeros_like(lse_ref)

        for hh, cols in enumerate(head_cols):
            for rows in row_sets:
                q = q_ref[rows, cols]
                sc = jnp.where(mc, _nt(q, kc_ref[rows, cols]) * scale, -jnp.inf)
                sp = jnp.where(mp, _nt(q, kp_ref[rows, cols]) * scale, -jnp.inf)
                m = jnp.maximum(jnp.max(sc, axis=-1, keepdims=True), jnp.max(sp, axis=-1, keepdims=True))
                pc, pp = jnp.exp(sc - m), jnp.exp(sp - m)
                l = jnp.sum(pc, axis=-1, keepdims=True) + jnp.sum(pp, axis=-1, keepdims=True)
                o_ref[rows, cols] = _nn(pc / l, vc_ref[rows, cols]) + _nn(pp / l, vp_ref[rows, cols])
                lse_ref[rows, :] = jnp.where(lane == hg * hb + hh, m + jnp.log(l), lse_ref[rows, :])

    cur = lambda part: pl.BlockSpec((SB, hb * dh), lambda s, sb, hg: (s * nb + sb, (gi * 3 + part) * ng + hg))
    prv = lambda part: pl.BlockSpec((SB, hb * dh), lambda s, sb, hg: (s * nb + jnp.maximum(sb - 1, 0), (gi * 3 + part) * ng + hg))
    return _call_with_side(
        body, name, (nseq, nb, ng), [cur(0), cur(1), prv(1), cur(2), prv(2)],
        [pl.BlockSpec((SB, hb * dh), lambda s, sb, hg: (s * nb + sb, hg)), pl.BlockSpec((SB, LANE), lambda s, sb, hg: (s * nb + sb, 0))],
        [jax.ShapeDtypeStruct((T, H * dh), f32), jax.ShapeDtypeStruct((T, LANE), f32)], [],
        ("parallel", "parallel", "arbitrary"), (proj, proj, proj, proj, proj), side)


def _dil_mix_fwd(os_, lses, name):
    W = os_[0].shape[1]
    G = len(os_)

    def fn(*a):
        o, l = a[:G], a[G:]
        lane = lax.broadcasted_iota(jnp.int32, l[0].shape, 1)
        tot = jnp.zeros(l[0].shape, f32)
        parts = []
        for h, sl in enumerate(_head_slices(W, DIL_HEADS)):
            lh = [x[:, h:h + 1] for x in l]
            m = functools.reduce(jnp.maximum, lh)
            e = [jnp.exp(x - m) for x in lh]
            z = functools.reduce(lambda u, v: u + v, e)
            acc = None
            for g in range(G):
                t = (e[g] / z) * o[g][:, sl]
                acc = t if acc is None else acc + t
            parts.append(acc)
            tot = jnp.where(lane == h, m + jnp.log(z), tot)
        y = jnp.concatenate(parts, axis=1)
        return y, y, tot

    return _rowwise(fn, [_full(x) for x in os_] + [_full(x) for x in lses], [], [(W, f32), (W, _MXU), (LANE, f32)], [], name)


def _dil_delta(do, o, name):
    W = o.shape[1]

    def fn(do, o):
        lane = lax.broadcasted_iota(jnp.int32, (do.shape[0], LANE), 1)
        d = jnp.zeros((do.shape[0], LANE), f32)
        for h, sl in enumerate(_head_slices(W, DIL_HEADS)):
            d = jnp.where(lane == h, jnp.sum(do[:, sl] * o[:, sl], axis=-1, keepdims=True), d)
        return d

    return _rowwise(fn, [_full(do), _full(o)], [], [(LANE, f32)], [], name)[0]


def _dil_bwd(proj, do, lse, delta, gi, window, dilation, nseq, name):
    T = proj.shape[0]
    H, dh = DIL_HEADS, DIL_HEAD_DIM
    L, P, nb, SB, hb, ng, row_sets, head_cols = _dil_tiling(T, nseq, dilation)
    steps = window // dilation
    scale = dh ** -0.5

    def probs(q, k, lse_h, mask):
        return jnp.where(mask, jnp.exp(_nt(q, k) * scale - lse_h), 0.0)

    def head_lane(ref, rows, h):
        lane = lax.broadcasted_iota(jnp.int32, (P, LANE), 1)
        return jnp.sum(jnp.where(lane == h, ref[rows, :], 0.0), axis=1, keepdims=True)

    def dq_body(q_ref, kc_ref, kp_ref, vc_ref, vp_ref, do_ref, lse_ref, del_ref, dq_ref):
        sb, hg = pl.program_id(1), pl.program_id(2)
        mc, mp = _band_masks(P, steps, sb > 0)
        for hh, cols in enumerate(head_cols):
            h = hg * hb + hh
            for rows in row_sets:
                q, doh = q_ref[rows, cols], do_ref[rows, cols]
                lse_h, del_h = head_lane(lse_ref, rows, h), head_lane(del_ref, rows, h)
                kc, kp = kc_ref[rows, cols], kp_ref[rows, cols]
                dsc = probs(q, kc, lse_h, mc) * (_nt(doh, vc_ref[rows, cols]) - del_h) * scale
                dsp = probs(q, kp, lse_h, mp) * (_nt(doh, vp_ref[rows, cols]) - del_h) * scale
                dq_ref[rows, cols] = _nn(dsc, kc) + _nn(dsp, kp)

    cur = lambda part: pl.BlockSpec((SB, hb * dh), lambda s, sb, hg: (s * nb + sb, (gi * 3 + part) * ng + hg))
    prv = lambda part: pl.BlockSpec((SB, hb * dh), lambda s, sb, hg: (s * nb + jnp.maximum(sb - 1, 0), (gi * 3 + part) * ng + hg))
    tok = pl.BlockSpec((SB, hb * dh), lambda s, sb, hg: (s * nb + sb, hg))
    aux = pl.BlockSpec((SB, LANE), lambda s, sb, hg: (s * nb + sb, 0))
    dq = pl.pallas_call(
        dq_body, name=name + "_dq", grid=(nseq, nb, ng),
        in_specs=[cur(0), cur(1), prv(1), cur(2), prv(2), tok, aux, aux],
        out_specs=tok, out_shape=jax.ShapeDtypeStruct((T, H * dh), f32),
        compiler_params=_cparams(("parallel", "parallel", "parallel")),
    )(proj, proj, proj, proj, proj, do, lse, delta)

    def dkv_body(k_ref, v_ref, qc_ref, qn_ref, doc_ref, don_ref, lsec_ref, lsen_ref, delc_ref, deln_ref, dk_ref, dv_ref):
        sb, hg = pl.program_id(1), pl.program_id(2)
        mc, mn = _band_masks(P, steps, sb < nb - 1)
        for hh, cols in enumerate(head_cols):
            h = hg * hb + hh
            for rows in row_sets:
                k, v = k_ref[rows, cols], v_ref[rows, cols]
                qc, doc = qc_ref[rows, cols], doc_ref[rows, cols]
                pc = probs(qc, k, head_lane(lsec_ref, rows, h), mc)
                dsc = pc * (_nt(doc, v) - head_lane(delc_ref, rows, h)) * scale
                qn, don = qn_ref[rows, cols], don_ref[rows, cols]
                pn = probs(qn, k, head_lane(lsen_ref, rows, h), mn)
                dsn = pn * (_nt(don, v) - head_lane(deln_ref, rows, h)) * scale
                dk_ref[rows, cols] = _tn(dsc, qc) + _tn(dsn, qn)
                dv_ref[rows, cols] = _tn(pc, doc) + _tn(pn, don)

    nxt = lambda s, sb: s * nb + jnp.minimum(sb + 1, nb - 1)
    qnx = pl.BlockSpec((SB, hb * dh), lambda s, sb, hg: (nxt(s, sb), gi * 3 * ng + hg))
    tokn = pl.BlockSpec((SB, hb * dh), lambda s, sb, hg: (nxt(s, sb), hg))
    auxn = pl.BlockSpec((SB, LANE), lambda s, sb, hg: (nxt(s, sb), 0))
    dkk, dvv = pl.pallas_call(
        dkv_body, name=name + "_dkv", grid=(nseq, nb, ng),
        in_specs=[cur(1), cur(2), cur(0), qnx, tok, tokn, aux, auxn, aux, auxn],
        out_specs=[tok, tok],
        out_shape=[jax.ShapeDtypeStruct((T, H * dh), f32), jax.ShapeDtypeStruct((T, H * dh), f32)],
        compiler_params=_cparams(("parallel", "parallel", "parallel")),
    )(proj, proj, proj, proj, do, do, lse, lse, delta, delta)
    return dq, dkk, dvv


def _adamw_math(w, g, m, v):
    m = ADAM_B1 * m + (1.0 - ADAM_B1) * g
    v = ADAM_B2 * v + (1.0 - ADAM_B2) * (g * g)
    m_hat = m / (1.0 - ADAM_B1 ** ADAM_STEP)
    v_hat = v / (1.0 - ADAM_B2 ** ADAM_STEP)
    return -ADAM_LR * (m_hat / (jnp.sqrt(v_hat) + ADAM_EPS) + ADAM_WD * w), m, v


def _adamw(w, g, m, v, name):
    W = w.shape[1]
    return _rowwise(_adamw_math, [_full(w), _full(g), _full(m), _full(v)], [], [(W, f32)] * 3, [], name, tm=512)


def _position():
    x, y, c = lax.axis_index("x"), lax.axis_index("y"), lax.axis_index("c")
    other_chips = [(1 - x, y), (x, 1 - y), (1 - x, 1 - y)]
    return x, y, c, other_chips


def _chip_index(x, y):
    return 2 * x + y


_ANY = pl.BlockSpec(memory_space=pl.ANY)


def _all_reduce_small(p, name):
    R, Wd = p.shape

    def body(p_ref, o_ref, buf, send_sems, recv_sems):
        x, y, c, _ = _position()
        me = 4 * x + 2 * y + c
        buf[me] = p_ref[...]
        copies = []
        for k in range(1, 8):
            fx, fy, fc = (k >> 2) & 1, (k >> 1) & 1, k & 1
            peer = (x + fx - 2 * x * fx, y + fy - 2 * y * fy, c + fc - 2 * c * fc)
            cp = pltpu.make_async_remote_copy(src_ref=p_ref, dst_ref=buf.at[me], send_sem=send_sems.at[k - 1],
                                              recv_sem=recv_sems.at[k - 1], device_id=peer, device_id_type=MESH)
            cp.start()
            copies.append(cp)
        for cp in copies:
            cp.wait()
        acc = buf[0]
        for s in range(1, 8):
            acc = acc + buf[s]
        o_ref[...] = acc

    return pl.pallas_call(
        body, name=name, out_shape=jax.ShapeDtypeStruct((R, Wd), f32),
        in_specs=[pl.BlockSpec(memory_space=pltpu.VMEM)], out_specs=pl.BlockSpec(memory_space=pltpu.VMEM),
        scratch_shapes=[pltpu.VMEM((8, R, Wd), f32), pltpu.SemaphoreType.DMA((7,)), pltpu.SemaphoreType.DMA((7,))],
        compiler_params=pltpu.CompilerParams(vmem_limit_bytes=VMEM_LIMIT),
    )(p)


def _layer_half(ref, h, axis):
    n = ref.shape[axis] // 2
    idx = (slice(None),) * axis + (pl.ds(h * n, n),)
    return ref.at[idx]


def _comm_call(body, name, ins, out_shapes, n_sems, n_local=0):
    scratch = [pltpu.SemaphoreType.DMA((n_sems,)), pltpu.SemaphoreType.DMA((n_sems,))]
    if n_local:
        scratch.append(pltpu.SemaphoreType.DMA((n_local,)))
    return pl.pallas_call(body, name=name, out_shape=out_shapes, in_specs=[_ANY] * len(ins), out_specs=[_ANY] * len(out_shapes),
                          scratch_shapes=scratch)(*ins)


class _SideCopies:
    def __init__(self, ins, outs, scratch, start, finish):
        self.ins, self.outs, self.scratch, self.start, self.finish = ins, outs, scratch, start, finish


def _row_half(ref, h):
    r = ref.shape[0] // 2
    start = h * r
    if r % 16 == 0:
        start = pl.multiple_of(start, 16)
    return ref.at[pl.ds(start, r)]


def _gather_plan(ws):
    n = len(ws)

    def copy(o_refs, sems, k, src, i, chip_idx, h, to):
        return pltpu.make_async_remote_copy(src_ref=src, dst_ref=_row_half(o_refs[i].at[chip_idx], h), send_sem=sems[0].at[k],
                                            recv_sem=sems[1].at[k], device_id=to, device_id_type=MESH)

    def own_copy(w_refs, o_refs, sems, i, p):
        return pltpu.make_async_copy(w_refs[i], o_refs[i].at[p], sems[2].at[i])

    def over_ici(w_refs, o_refs, sems, i, j, chip, dst_chip_idx, c):
        return copy(o_refs, sems, 3 * i + j, _row_half(w_refs[i], c), i, dst_chip_idx, c, (*chip, c))

    def start(w_refs, o_refs, *sems):
        x, y, c, chips = _position()
        p = _chip_index(x, y)
        for i in range(n):
            own_copy(w_refs, o_refs, sems, i, p).start()
            for j, chip in enumerate(chips):
                over_ici(w_refs, o_refs, sems, i, j, chip, p, c).start()

    def finish(w_refs, o_refs, *sems):
        x, y, c, chips = _position()
        p = _chip_index(x, y)
        sibling = (x, y, 1 - c)
        passed = []
        for i in range(n):
            for j, chip in enumerate(chips):
                q = _chip_index(*chip)
                over_ici(w_refs, o_refs, sems, i, j, chip, q, c).wait_recv()
                fwd = copy(o_refs, sems, 3 * n + 3 * i + j, _row_half(o_refs[i].at[q], c), i, q, c, sibling)
                fwd.start()
                passed.append(fwd)
        for i in range(n):
            for j, chip in enumerate(chips):
                copy(o_refs, sems, 3 * n + 3 * i + j, _row_half(w_refs[i], c), i, _chip_index(*chip), 1 - c, sibling).wait_recv()
        for i in range(n):
            for j, chip in enumerate(chips):
                over_ici(w_refs, o_refs, sems, i, j, chip, p, c).wait_send()
            own_copy(w_refs, o_refs, sems, i, p).wait()
        for fwd in passed:
            fwd.wait_send()

    scratch = [pltpu.SemaphoreType.DMA((6 * n,)), pltpu.SemaphoreType.DMA((6 * n,)), pltpu.SemaphoreType.DMA((n,))]
    return _SideCopies(list(ws), [jax.ShapeDtypeStruct((4,) + w.shape, w.dtype) for w in ws], scratch, start, finish)


def _run_copies(plan, name):
    n_i, n_o = len(plan.ins), len(plan.outs)

    def body(*refs):
        plan.start(refs[:n_i], refs[n_i:n_i + n_o], *refs[n_i + n_o:])
        plan.finish(refs[:n_i], refs[n_i:n_i + n_o], *refs[n_i + n_o:])

    return pl.pallas_call(body, name=name, out_shape=plan.outs, in_specs=[_ANY] * n_i, out_specs=[_ANY] * n_o,
                          scratch_shapes=plan.scratch)(*plan.ins)


def _sibling_halves(gs, name):
    n = len(gs)

    def body(*refs):
        g_refs, o_refs, send_sems, recv_sems = refs[:n], refs[n:2 * n], refs[2 * n], refs[2 * n + 1]
        x, y, c, _ = _position()
        copies = []
        for i in range(n):
            cp = pltpu.make_async_remote_copy(src_ref=_layer_half(g_refs[i], 1 - c, 1), dst_ref=o_refs[i], send_sem=send_sems.at[i],
                                              recv_sem=recv_sems.at[i], device_id=(x, y, 1 - c), device_id_type=MESH)
            cp.start()
            copies.append(cp)
        for cp in copies:
            cp.wait()

    outs = [jax.ShapeDtypeStruct((4, g.shape[1] // 2) + g.shape[2:], g.dtype) for g in gs]
    return _comm_call(body, name, gs, outs, n)


def _exchange_plan(hs):
    n = len(hs)

    def copies(h_refs, o_refs, send_sems, recv_sems):
        x, y, c, chips = _position()
        return [pltpu.make_async_remote_copy(src_ref=h_refs[i].at[_chip_index(*chip)], dst_ref=o_refs[i].at[j],
                                             send_sem=send_sems.at[3 * i + j], recv_sem=recv_sems.at[3 * i + j],
                                             device_id=(*chip, c), device_id_type=MESH)
                for i in range(n) for j, chip in enumerate(chips)]

    def start(h_refs, o_refs, *sems):
        for cp in copies(h_refs, o_refs, *sems):
            cp.start()

    def finish(h_refs, o_refs, *sems):
        for cp in copies(h_refs, o_refs, *sems):
            cp.wait()

    scratch = [pltpu.SemaphoreType.DMA((3 * n,)), pltpu.SemaphoreType.DMA((3 * n,))]
    return _SideCopies(list(hs), [jax.ShapeDtypeStruct((3,) + h.shape[1:], h.dtype) for h in hs], scratch, start, finish)


def _sibling_swap(ts, name):
    n = len(ts)

    def body(*refs):
        t_refs, o_refs, send_sems, recv_sems = refs[:n], refs[n:2 * n], refs[2 * n], refs[2 * n + 1]
        x, y, c, _ = _position()
        copies = []
        for i in range(n):
            cp = pltpu.make_async_remote_copy(src_ref=t_refs[i], dst_ref=o_refs[i], send_sem=send_sems.at[i], recv_sem=recv_sems.at[i],
                                              device_id=(x, y, 1 - c), device_id_type=MESH)
            cp.start()
            copies.append(cp)
        for cp in copies:
            cp.wait()

    return _comm_call(body, name, ts, [jax.ShapeDtypeStruct(t.shape, t.dtype) for t in ts], n)


BLOCK_ELEMS = 384 * 1024


def _block_2d(rows, cols, sub):
    tns = [t for t in range(LANE, cols + 1, LANE) if cols % t == 0] if cols % LANE == 0 else [cols]
    tms = [t for t in range(sub, rows + 1, sub) if rows % t == 0] or [rows]
    fits = [(tm * tn, tn, tm) for tm in tms for tn in tns if tm * tn <= BLOCK_ELEMS]
    assert fits, (rows, cols, sub)
    _, tn, tm = max(fits)
    return tm, tn


def _prefetch_call(body, name, scalars, grid, in_specs, out_specs, out_shape, args, sem):
    gs = pltpu.PrefetchScalarGridSpec(num_scalar_prefetch=1, grid=grid, in_specs=in_specs, out_specs=out_specs)
    return pl.pallas_call(body, name=name, grid_spec=gs, out_shape=out_shape, compiler_params=_cparams(sem))(scalars, *args)


def _add_own_half(g, got, c, name):
    _, nl, K, N = g.shape
    hl = nl // 2
    tm, tn = _block_2d(K, N, 16)

    def body(c_ref, g_ref, r_ref, o_ref):
        o_ref[...] = (g_ref[...].astype(f32) + r_ref[...].astype(f32)).astype(o_ref.dtype)

    blk = (1, 1, tm, tn)
    return _prefetch_call(
        body, name, jnp.reshape(c, (1,)).astype(jnp.int32), (4, hl, K // tm, N // tn),
        [pl.BlockSpec(blk, lambda s, l, i, j, c_ref: (s, c_ref[0] * hl + l, i, j)), pl.BlockSpec(blk, lambda s, l, i, j, c_ref: (s, l, i, j))],
        pl.BlockSpec(blk, lambda s, l, i, j, c_ref: (s, l, i, j)), jax.ShapeDtypeStruct((4, hl, K, N), g.dtype), (g, got),
        ("parallel",) * 4)


def _add_chips(h, got, p, name):
    _, nl, K, N = h.shape
    tm, tn = _block_2d(K, N, 16)

    def body(p_ref, h_ref, r0_ref, r1_ref, r2_ref, o_ref):
        o_ref[...] = ((h_ref[0].astype(f32) + r0_ref[0].astype(f32)) + r1_ref[0].astype(f32)) + r2_ref[0].astype(f32)

    blk = (1, 1, tm, tn)
    got_spec = lambda q: pl.BlockSpec(blk, lambda l, i, j, p_ref: (q, l, i, j))
    return _prefetch_call(
        body, name, jnp.reshape(p, (1,)).astype(jnp.int32), (nl, K // tm, N // tn),
        [pl.BlockSpec(blk, lambda l, i, j, p_ref: (p_ref[0], l, i, j)), got_spec(0), got_spec(1), got_spec(2)],
        pl.BlockSpec((1, tm, tn), lambda l, i, j, p_ref: (l, i, j)), jax.ShapeDtypeStruct((nl, K, N), f32), (h, got, got, got),
        ("parallel",) * 3)


def _adamw_halves(w, m, v, own, other, c, name):
    nl, K, N = w.shape
    tm, tn = _block_2d(K // 2, N, 8)
    nb = K // 2 // tm

    def body(c_ref, w_ref, m_ref, v_ref, own_ref, other_ref, g_out, d_out, m_out, v_out):
        g = jnp.where(pl.program_id(1) == c_ref[0], own_ref[...], other_ref[...])
        d, m_new, v_new = _adamw_math(w_ref[...], g, m_ref[...], v_ref[...])
        g_out[...] = g
        d_out[...] = d
        m_out[...] = m_new
        v_out[...] = v_new

    blk = (1, tm, tn)
    full = pl.BlockSpec(blk, lambda l, h, i, j, c_ref: (l, h * nb + i, j))
    half = pl.BlockSpec(blk, lambda l, h, i, j, c_ref: (l, i, j))
    return _prefetch_call(
        body, name, jnp.reshape(c, (1,)).astype(jnp.int32), (nl, 2, nb, N // tn), [full, full, full, half, half], [full] * 4,
        [jax.ShapeDtypeStruct((nl, K, N), f32)] * 4, (w, m, v, own, other), ("parallel",) * 4)


def _row_halves_view(g):
    return g.reshape(4, 2, g.shape[1] // 2, g.shape[2])


def _reduce_scatter_start(gs, name):
    c = lax.axis_index("c")
    views = [_row_halves_view(g) for g in gs]
    from_sibling = _sibling_halves(views, name + "_d2d")
    return [_add_own_half(g, r, c, f"{name}_add2_{i}") for i, (g, r) in enumerate(zip(views, from_sibling))]


def _reduce_scatter_finish(hs, got, name):
    p = _chip_index(lax.axis_index("x"), lax.axis_index("y"))
    return [_add_chips(h, r, p, f"{name}_add4_{i}")[0] for i, (h, r) in enumerate(zip(hs, got))]


_BIG = (("gla_w_in", 1), ("gla_w_out", 0), ("dil_w_in", 1), ("dil_w_out", 1), ("ffn_w_up", 1), ("ffn_w_down", 0))


def _pad_rows(a, mult):
    r = (-a.shape[0]) % mult
    return a if r == 0 else jnp.concatenate([a, jnp.zeros((r,) + a.shape[1:], a.dtype)], axis=0)


def _unshard(blocks, axis):
    _, K, N = blocks.shape
    if axis == 1:
        return blocks.transpose(1, 0, 2).reshape(K, 4 * N)
    return blocks.reshape(4 * K, N)


def _to_shards(mat, axis):
    K, N = mat.shape
    if axis == 1:
        return mat.reshape(K, 4, N // 4).transpose(1, 0, 2)
    return mat.reshape(4, K // 4, N)


_SMALL = (("gla_w_gate_up", 2), ("gla_gate_bias", None), ("gla_norm_g", None), ("ffn_conv_w", 2), ("ffn_conv_b", None),
          ("ln_g", 2), ("ln_b", 2))


def _pack_rows(arrs, width=LANE):
    flat = _pad_rows(jnp.concatenate([a.reshape(-1) for a in arrs]), 8 * width)
    return flat.reshape(-1, width)


def _unpack_rows(packed, shapes):
    flat, out, off = packed.reshape(-1), [], 0
    for s in shapes:
        n = 1
        for d in s:
            n *= d
        out.append(flat[off:off + n].reshape(s))
        off += n
    return out


def _gather_small_params(shards):
    x, y, c = lax.axis_index("x"), lax.axis_index("y"), lax.axis_index("c")
    names = [n for n, axis in _SMALL if axis is not None]
    mine = _pack_rows([shards[n] for n in names])
    mine = jnp.where(c == 0, mine, jnp.zeros_like(mine))
    rows = mine.shape[0]
    placed = lax.dynamic_update_slice(jnp.zeros((4 * rows, LANE), f32), mine, (_chip_index(x, y) * rows, 0))
    allp = _all_reduce_small(placed, "gather_small").reshape(4, rows, LANE)
    out = {n: shards[n] for n, axis in _SMALL if axis is None}
    per_chip = [_unpack_rows(allp[q], [shards[n].shape for n in names]) for q in range(4)]
    for i, n in enumerate(names):
        out[n] = jnp.concatenate([per_chip[q][i] for q in range(4)], axis=2)
    return out


def _reduce_small_grads(grads, shards):
    names = [n for n, _ in _SMALL]
    total = _all_reduce_small(_pack_rows([grads[n] for n in names]), "reduce_small")
    full = dict(zip(names, _unpack_rows(total, [grads[n].shape for n in names])))
    p = _chip_index(lax.axis_index("x"), lax.axis_index("y"))
    out = {}
    for n, axis in _SMALL:
        if axis is None:
            out[n] = full[n]
        else:
            w = shards[n].shape[axis]
            out[n] = lax.dynamic_slice_in_dim(full[n], p * w, w, axis=axis)
    return out


def _pad_cols(a, n):
    return a if a.shape[-1] == n else jnp.concatenate([a, jnp.zeros(a.shape[:-1] + (n - a.shape[-1],), a.dtype)], axis=-1)


def _ffn_width(F):
    return -(-F // 512) * 512


def kernel(x, gla_w_in, gla_w_gate_up, gla_gate_bias, gla_norm_g, gla_w_out, dil_w_in, dil_w_out, ffn_w_up, ffn_conv_w, ffn_conv_b, ffn_w_down, ln_g, ln_b, loss_target, m_gla_w_in, m_gla_w_gate_up, m_gla_gate_bias, m_gla_norm_g, m_gla_w_out, m_dil_w_in, m_dil_w_out, m_ffn_w_up, m_ffn_conv_w, m_ffn_conv_b, m_ffn_w_down, m_ln_g, m_ln_b, v_gla_w_in, v_gla_w_gate_up, v_gla_gate_bias, v_gla_norm_g, v_gla_w_out, v_dil_w_in, v_dil_w_out, v_ffn_w_up, v_ffn_conv_w, v_ffn_conv_b, v_ffn_w_down, v_ln_g, v_ln_b):
    names = ["gla_w_in", "gla_w_gate_up", "gla_gate_bias", "gla_norm_g", "gla_w_out", "dil_w_in", "dil_w_out", "ffn_w_up",
             "ffn_conv_w", "ffn_conv_b", "ffn_w_down", "ln_g", "ln_b"]
    w_sh = dict(zip(names, (gla_w_in, gla_w_gate_up, gla_gate_bias, gla_norm_g, gla_w_out, dil_w_in, dil_w_out, ffn_w_up,
                            ffn_conv_w, ffn_conv_b, ffn_w_down, ln_g, ln_b)))
    m_sh = dict(zip(names, (m_gla_w_in, m_gla_w_gate_up, m_gla_gate_bias, m_gla_norm_g, m_gla_w_out, m_dil_w_in, m_dil_w_out,
                            m_ffn_w_up, m_ffn_conv_w, m_ffn_conv_b, m_ffn_w_down, m_ln_g, m_ln_b)))
    v_sh = dict(zip(names, (v_gla_w_in, v_gla_w_gate_up, v_gla_gate_bias, v_gla_norm_g, v_gla_w_out, v_dil_w_in, v_dil_w_out,
                            v_ffn_w_up, v_ffn_conv_w, v_ffn_conv_b, v_ffn_w_down, v_ln_g, v_ln_b)))
    nseq, S, D = x.shape
    T = nseq * S
    small = _gather_small_params(w_sh)
    F = 4 * w_sh["ffn_w_down"].shape[1]
    Fp = _ffn_width(F)
    qkvr = 4 * w_sh["gla_w_in"].shape[2] - GLA_GATE_RANK
    c_idx = lax.axis_index("c")
    shard_axis = dict(_BIG)

    def pad_halves(a):
        return jnp.concatenate([_pad_cols(a[..., :F], Fp), _pad_cols(a[..., F:], Fp)], axis=-1)

    cw_all = pad_halves(small["ffn_conv_w"])
    cb_all = pad_halves(small["ffn_conv_b"][:, None, :])
    w_gate_up_all = jnp.pad(small["gla_w_gate_up"].astype(bf16), ((0, 0), (0, GATE_PAD - GLA_GATE_RANK), (0, 0)))

    def mixer_names(l):
        return ("gla_w_in", "gla_w_out") if l % 2 == 0 else ("dil_w_in", "dil_w_out")

    def shard_of(l, which):
        n_in, n_out = mixer_names(l)
        if which in ("w_up_a", "w_up_b"):
            half = w_sh["ffn_w_up"].shape[2] // 2
            shard = w_sh["ffn_w_up"][l].T
            return (shard[:half] if which == "w_up_a" else shard[half:]).astype(bf16)
        name, idx = {"w_in": (n_in, l // 2), "w_out": (n_out, l // 2), "w_down": ("ffn_w_down", l)}[which]
        return w_sh[name][idx].astype(bf16)

    weights = [dict() for _ in range(DEPTH)]

    def install(l, which, gathered):
        n_in, n_out = mixer_names(l)
        if which == "w_in":
            weights[l]["w_in"] = w_in = _unshard(gathered, shard_axis[n_in])
            if l % 2 == 0:
                weights[l]["w_gate"] = _pad_cols(w_in[:, qkvr:], GATE_PAD)
        elif which == "w_out":
            weights[l]["w_out"] = _unshard(gathered, shard_axis[n_out])
        elif which in ("w_up_a", "w_up_b"):
            weights[l][which] = gathered
            if "w_up_a" in weights[l] and "w_up_b" in weights[l]:
                both = jnp.concatenate([weights[l].pop("w_up_a"), weights[l].pop("w_up_b")], axis=1)
                wt = _unshard(both, 0)
                weights[l]["w_up_t"] = jnp.concatenate([_pad_rows(wt[:F], Fp), _pad_rows(wt[F:], Fp)], axis=0)
        else:
            weights[l]["w_down"] = _pad_rows(_unshard(gathered, shard_axis["ffn_w_down"]), Fp)

    kinds = ("w_in", "w_out", "w_up", "w_down")
    install(0, "w_in", _run_copies(_gather_plan([shard_of(0, "w_in")]), "gather_first")[0])

    riders = [{"proj": [(0, "w_out"), (0, "w_down")], "core": [(0, "w_up_a"), (0, "w_up_b")], "ffn_up": [(1, "w_in")]},
              {"proj": [(0, "w_out"), (0, "w_down")], "attn0": [(0, "w_up_a")], "core": [(0, "w_up_b")], "ffn_up": [(1, "w_in")]}]

    def ridden(l, kernel):
        return [(l + dl, which) for dl, which in riders[l % 2].get(kernel, []) if l + dl < DEPTH]

    def side_for(l, kernel):
        items = ridden(l, kernel)
        return _gather_plan([shard_of(m, which) for m, which in items]) if items else None

    def carried(l, kernel, res, n_own=1):
        items = ridden(l, kernel)
        if not items:
            return res
        for (m, which), gathered in zip(items, res[n_own:]):
            install(m, which, gathered)
        return res[0] if n_own == 1 else res[:n_own]

    h0 = x.reshape(T, D)
    saved = []
    cur, cur_b = h0, h0.astype(_MXU)
    fwd = dict(tm=1024, tn=1024, tk=2048)
    for i in range(DEPTH):
        j = i // 2
        tag = f"l{i}_"
        lg, lb = small["ln_g"][i], small["ln_b"][i]
        W = weights[i]
        if i % 2 == 0:
            gate_bias, norm_g = small["gla_gate_bias"][j][None, :], small["gla_norm_g"][j][None, :]
            proj = carried(i, "proj", _mm(cur_b, W["w_in"], "nn", tag + "gla_proj", n_out=qkvr, side=side_for(i, "proj"), **fwd))
            g_low = _mm(cur_b, W["w_gate"], "nn", tag + "gla_glow", tn=GATE_PAD)
            z = _mm(g_low, w_gate_up_all, "nn", tag + "gla_z", layer=j)
            gate = _log_gate_fwd(z, gate_bias, tag + "gla_gate")
            o, states = carried(i, "core", _gla_fwd(proj, gate, nseq, tag + "gla_core", side=side_for(i, "core")), n_own=2)
            y_b = _rms_gate_fwd(o, proj, norm_g, tag + "gla_norm")
            mix = _mm(y_b, W["w_out"], "nn", tag + "gla_out", **fwd)
            mixer_saved = (proj, g_low, z, gate, states, o, y_b)
        else:
            proj = carried(i, "proj", _mm(cur_b, W["w_in"], "nn", tag + "dil_proj", side=side_for(i, "proj"), **fwd))
            outs, lses = [], []
            for gi, (window, dilation) in enumerate(DIL_PATTERNS):
                kern = {0: "attn0", 1: "core"}.get(gi)
                og, lg_ = carried(i, kern, _dil_fwd(proj, gi, window, dilation, nseq, tag + f"dil_attn{gi}", side=side_for(i, kern)), n_own=2)
                outs.append(og)
                lses.append(lg_)
            y, y_b, lse_tot = _dil_mix_fwd(outs, lses, tag + "dil_mix")
            mix = _mm(y_b, W["w_out"], "nn", tag + "dil_out", **fwd)
            mixer_saved = (proj, y, y_b, lse_tot)
        x1, x1_b = _ln_fwd(cur, mix, lg[0:1], lb[0:1], tag + "ln1")
        cw, cb = cw_all[i], cb_all[i]
        hh = carried(i, "ffn_up", _mm(x1_b, W["w_up_t"], "nt", tag + "ffn_up", side=side_for(i, "ffn_up"), **fwd))
        act_b = carried(i, "conv", _conv_gate_fwd(hh, cw, cb, nseq, tag + "ffn_conv", side=side_for(i, "conv")))
        ffn = carried(i, "ffn_down", _mm(act_b, W["w_down"], "nn", tag + "ffn_down", tm=1024, tn=512, tk=Fp, side=side_for(i, "ffn_down")))
        x2, x2_b = _ln_fwd(x1, ffn, lg[1:2], lb[1:2], tag + "ln2")
        saved.append((cur, cur_b, mix, x1, x1_b, hh, act_b, ffn, mixer_saved))
        cur, cur_b = x2, x2_b

    dy, sq = _loss_head(cur, loss_target.reshape(T, D), "loss_head")
    loss = lax.psum(0.5 * jnp.sum(sq) / D, ("x", "y", "c"))

    gb = {n: [None] * w_sh[n].shape[0] for n in names}
    own_half = {n: [None] * w_sh[n].shape[0] for n, _ in _BIG}
    bwd_riders = [{"down_dx": [(1, "w_down")], "conv": [(1, "w_up")], "gate_dw": [(0, "w_down")],
                   "core_bwd": [(1, "w_in"), (0, "w_up")], "up_dx": [(1, "w_out")]},
                  {"down_dx": [(1, "w_down")], "conv": [(1, "w_up")], "gate_dw": [(1, "w_in")], "up_dx": [(1, "w_out")]}]
    pending = {}
    at = {"layer": None}

    def grad_slot(l, which):
        n_in, n_out = mixer_names(l)
        return {"w_in": (n_in, l // 2), "w_out": (n_out, l // 2), "w_up": ("ffn_w_up", l), "w_down": ("ffn_w_down", l)}[which]

    def start_scatter(l, whiches, tag):
        local = []
        for which in whiches:
            n, idx = grad_slot(l, which)
            local.append(_to_shards(gb[n][idx], 0 if n == "ffn_w_up" else shard_axis[n]))
        for which, part in zip(whiches, _reduce_scatter_start(local, f"scatter_l{l}_{tag}")):
            pending[(l, which)] = part

    def riders_of(kernel):
        l = at["layer"]
        return [(l + dl, which) for dl, which in bwd_riders[l % 2].get(kernel, []) if (l + dl, which) in pending]

    def ride(kernel):
        keys = riders_of(kernel)
        return _exchange_plan([pending[k] for k in keys]) if keys else None

    def landed(kernel, got):
        for (l, which), g in zip(riders_of(kernel), got):
            n, idx = grad_slot(l, which)
            own_half[n][idx] = _reduce_scatter_finish([pending.pop((l, which))], [g], f"scatter_l{l}_{which}")[0]

    def unwrap(kernel, res, n_own=1):
        if not riders_of(kernel):
            return res
        landed(kernel, res[n_own:])
        return res[0] if n_own == 1 else res[:n_own]

    d_res = None
    d_in = dy
    for i in reversed(range(DEPTH)):
        j = i // 2
        tag = f"l{i}_b_"
        xin, xin_b, mix, x1, x1_b, hh, act_b, ffn, mixer_saved = saved[i]
        lg = small["ln_g"][i]
        cw, cb = cw_all[i], cb_all[i]
        dw_tiles = dict(tm=1024, tn=1024, tk=4096, out_dtype=bf16)
        dx_tiles = dict(tm=1024, tn=512, tk=6144)
        dys, scales = ([d_in], [1.0]) if d_res is None else ([d_res, d_in], [DEEPNORM_ALPHA, 1.0])
        du2, du2_b, dg2, db2 = _ln_bwd(x1, ffn, lg[1:2], dys, scales, tag + "ln2")
        at["layer"] = i
        early = ("w_down", "w_up") if i % 2 == 0 else ()
        gb["ffn_w_down"][i] = _mm(act_b, du2_b, "tn", tag + "ffn_down_dw", **dw_tiles)[:F]
        if "w_down" in early:
            start_scatter(i, ["w_down"], "down")
        W = weights[i]
        dact = unwrap("down_dx", _mm(du2_b, W["w_down"], "nt", tag + "ffn_down_dx", side=ride("down_dx"), **dx_tiles))
        dhg, dhu, dcw, dcb, rode = _conv_gate_bwd(hh, dact, cw, cb, nseq, tag + "ffn_conv", side=ride("conv"))
        landed("conv", rode)
        gb["ffn_conv_w"][i] = jnp.concatenate([dcw[:, :F], dcw[:, Fp:Fp + F]], axis=1)
        gb["ffn_conv_b"][i] = jnp.concatenate([dcb[0, :F], dcb[0, Fp:Fp + F]], axis=0)
        dwg_t = unwrap("gate_dw", _mm(dhg, x1_b, "tn", tag + "ffn_gate_dw", side=ride("gate_dw"), **dw_tiles))
        dwu_t = _mm(dhu, x1_b, "tn", tag + "ffn_up_dw", **dw_tiles)
        gb["ffn_w_up"][i] = jnp.concatenate([dwg_t[:F], dwu_t[:F]], axis=0)
        if "w_up" in early:
            start_scatter(i, ["w_up"], "up")
        dx1g = _mm(dhg, W["w_up_t"], "nn", tag + "ffn_gate_dx", **dx_tiles)
        dx1u = unwrap("up_dx", _mm(dhu, W["w_up_t"], "nn", tag + "ffn_up_dx", b_k0=Fp, side=ride("up_dx"), **dx_tiles))
        du1, du1_b, dg1, db1 = _ln_bwd(xin, mix, lg[0:1], [du2, dx1g, dx1u], [DEEPNORM_ALPHA, 1.0, 1.0], tag + "ln1")
        gb["ln_g"][i] = jnp.concatenate([dg1, dg2], axis=0)
        gb["ln_b"][i] = jnp.concatenate([db1, db2], axis=0)
        if i % 2 == 0:
            proj, g_low, z, gate, states, o, y_b = mixer_saved
            gate_bias, norm_g = small["gla_gate_bias"][j][None, :], small["gla_norm_g"][j][None, :]
            gb["gla_w_out"][j] = _mm(y_b, du1_b, "tn", tag + "gla_out_dw", **dw_tiles)
            dyy = _mm(du1_b, W["w_out"], "nt", tag + "gla_out_dx", **dx_tiles)
            do, dr, dng = _rms_gate_bwd(o, proj, norm_g, dyy, tag + "gla_norm")
            gb["gla_norm_g"][j] = dng[0]
            dq, dk_, dv_, dgate = unwrap("core_bwd", _gla_bwd(proj, gate, states, do, nseq, tag + "gla_core", side=ride("core_bwd")), n_own=4)
            dz, dbias = _log_gate_bwd(z, gate_bias, dgate, tag + "gla_gate")
            gb["gla_gate_bias"][j] = dbias[0]
            gb["gla_w_gate_up"][j] = _mm(g_low, dz, "tn", tag + "gla_z_dw", tk=1024)[:GLA_GATE_RANK]
            dg_low = _mm(dz, w_gate_up_all, "nt", tag + "gla_z_dx", tn=GATE_PAD, layer=j)
            dproj = jnp.concatenate([dq, dk_, dv_, dr], axis=1)
            dw_main = _mm(xin_b, dproj, "tn", tag + "gla_proj_dw", **dw_tiles)
            dw_gate = _mm(xin_b, dg_low, "tn", tag + "gla_glow_dw", tm=1024, tn=GATE_PAD, tk=2048, out_dtype=bf16)[:, :GLA_GATE_RANK]
            gb["gla_w_in"][j] = jnp.concatenate([dw_main, dw_gate], axis=1)
            dxa = _mm(dproj, W["w_in"], "nt", tag + "gla_proj_dx", **dx_tiles)
            dxb = _mm(dg_low, W["w_gate"], "nt", tag + "gla_glow_dx")
            d_in = _axpy(dxa, dxb, 1.0, tag + "gla_dx_sum")
        else:
            proj, y, y_b, lse_tot = mixer_saved
            gb["dil_w_out"][j] = _mm(y_b, du1_b, "tn", tag + "dil_out_dw", **dw_tiles)
            dyy = _mm(du1_b, W["w_out"], "nt", tag + "dil_out_dx", **dx_tiles)
            delta = _dil_delta(dyy, y, tag + "dil_delta")
            pieces = []
            for gi, (window, dilation) in enumerate(DIL_PATTERNS):
                pieces += list(_dil_bwd(proj, dyy, lse_tot, delta, gi, window, dilation, nseq, tag + f"dil_attn{gi}"))
            dproj = jnp.concatenate(pieces, axis=1).astype(_MXU)
            gb["dil_w_in"][j] = _mm(xin_b, dproj, "tn", tag + "dil_proj_dw", **dw_tiles)
            d_in = _mm(dproj, W["w_in"], "nt", tag + "dil_proj_dx", **dx_tiles)
        d_res = du1
        start_scatter(i, [k for k in kinds if k not in early], "rest")
    grad_x = _axpy(d_res, d_in, DEEPNORM_ALPHA, "grad_x").reshape(x.shape)

    left = sorted(pending)
    last = [pending.pop(k) for k in left]
    for (l, which), own in zip(left, _reduce_scatter_finish(last, _run_copies(_exchange_plan(last), "scatter_last_ici"), "scatter_last")):
        n, idx = grad_slot(l, which)
        own_half[n][idx] = own
    own_all = [jnp.stack(own_half[n], axis=0) for n, _ in _BIG]
    other_all = _sibling_swap(own_all, "scatter_swap")
    grads, delta, new_m, new_v = {}, {}, {}, {}
    for (n, _), own, other in zip(_BIG, own_all, other_all):
        view = (lambda a: jnp.swapaxes(a, 1, 2)) if n == "ffn_w_up" else (lambda a: a)
        res = _adamw_halves(view(w_sh[n]), view(m_sh[n]), view(v_sh[n]), own, other, c_idx, "adamw_" + n)
        grads[n], delta[n], new_m[n], new_v[n] = [view(r) for r in res]
    grads.update(_reduce_small_grads({n: jnp.stack(gb[n], axis=0) for n, _ in _SMALL}, w_sh))
    small_names = [n for n, _ in _SMALL]
    packed = [_pack_rows([src[n] for n in small_names]) for src in (w_sh, grads, m_sh, v_sh)]
    res = _adamw(*packed, "adamw_small")
    shapes = [w_sh[n].shape for n in small_names]
    for dst, arr in zip((delta, new_m, new_v), res):
        dst.update(dict(zip(small_names, _unpack_rows(arr, shapes))))

    return (loss, grad_x, *[grads[n] for n in names], *[delta[n] for n in names], *[new_m[n] for n in names],
            *[new_v[n] for n in names])
```

```python
import functools

import jax
import jax.numpy as jnp
from jax import lax
from jax.experimental import pallas as pl
from jax.experimental.pallas import tpu as pltpu

f32 = jnp.float32
bf16 = jnp.bfloat16
_MXU = jnp.bfloat16

DEPTH = 4
GLA_HEADS = 4
GLA_GATE_RANK = 16
GLA_GATE_NORMALIZER = 16.0
GLA_CHUNK = 64
GLA_SUB = 16
DIL_PATTERNS = ((128, 1), (512, 4), (2048, 16))
DIL_HEADS = 8
DIL_HEAD_DIM = 128
DIL_BLOCK = 128
DEEPNORM_ALPHA = (2 * DEPTH) ** 0.25
LN_EPS = 1e-5
RMS_EPS = 1e-6
ADAM_LR = 0.001
ADAM_B1 = 0.9
ADAM_B2 = 0.999
ADAM_EPS = 1e-08
ADAM_WD = 0.01
ADAM_STEP = 10

LANE = 128
VMEM_LIMIT = 48 * 1024 * 1024
GATE_PAD = LANE
MESH = pl.DeviceIdType.MESH


def _cparams(sem=None):
    return pltpu.CompilerParams(dimension_semantics=sem, vmem_limit_bytes=VMEM_LIMIT)


def _div_tile(n, pref, unit):
    if n <= pref:
        return n
    best = None
    for t in range(unit, pref + 1, unit):
        if n % t == 0:
            best = t
    assert best is not None, (n, pref, unit)
    return best


def _dot(a, b, ca, cb):
    return lax.dot_general(a.astype(_MXU), b.astype(_MXU), (((ca,), (cb,)), ((), ())), preferred_element_type=f32)


def _nn(a, b):
    return _dot(a, b, 1, 0)


def _nt(a, b):
    return _dot(a, b, 1, 1)


def _tn(a, b):
    return _dot(a, b, 0, 0)


def _exact_dot(a, b):
    return jnp.dot(a, b, precision=lax.Precision.HIGHEST, preferred_element_type=f32)


def _sigmoid(x):
    return 1.0 / (1.0 + jnp.exp(-x))


def _call_with_side(body, name, grid, in_specs, out_specs, out_shape, scratch, sem, args, side):
    if side is None:
        return pl.pallas_call(body, name=name, grid=grid, in_specs=in_specs, out_specs=out_specs, out_shape=out_shape,
                              scratch_shapes=scratch, compiler_params=_cparams(sem))(*args)
    n_in, n_out, n_scr = len(in_specs), len(out_specs), len(scratch)
    n_si, n_so = len(side.ins), len(side.outs)

    def wrapped(*refs):
        ins, s_in = refs[:n_in], refs[n_in:n_in + n_si]
        outs, s_out = refs[n_in + n_si:n_in + n_si + n_out], refs[n_in + n_si + n_out:n_in + n_si + n_out + n_so]
        rest = refs[n_in + n_si + n_out + n_so:]
        sems = rest[n_scr:]
        ids = [pl.program_id(ax) for ax in range(len(grid))]
        first = functools.reduce(lambda u, v: u & v, [i == 0 for i in ids])
        last = functools.reduce(lambda u, v: u & v, [i == g - 1 for i, g in zip(ids, grid)])

        @pl.when(first)
        def _():
            side.start(s_in, s_out, *sems)

        body(*ins, *outs, *rest[:n_scr])

        @pl.when(last)
        def _():
            side.finish(s_in, s_out, *sems)

    return pl.pallas_call(
        wrapped, name=name, grid=grid, in_specs=list(in_specs) + [_ANY] * n_si, out_specs=list(out_specs) + [_ANY] * n_so,
        out_shape=list(out_shape) + list(side.outs), scratch_shapes=list(scratch) + list(side.scratch),
        compiler_params=_cparams(("arbitrary",) * len(grid)))(*args, *side.ins)


def _mm(a, b, mode, name, tm=1024, tn=512, tk=2048, out_dtype=f32, layer=None, n_out=None, b_k0=0, side=None):
    if mode == "nn":
        (M, K), N = a.shape, (n_out or b.shape[-1])
    elif mode == "nt":
        (M, K), N = a.shape, b.shape[-2]
    else:
        (K, M), N = a.shape, b.shape[-1]
    tm, tn, tk = _div_tile(M, tm, LANE), _div_tile(N, tn, LANE), _div_tile(K, tk, LANE)
    nk = K // tk
    if mode == "tn":
        a_spec = pl.BlockSpec((tk, tm), lambda i, j, k: (k, i))
    else:
        a_spec = pl.BlockSpec((tm, tk), lambda i, j, k: (i, k))
    lead = () if layer is None else (None,)
    pre = () if layer is None else (layer,)
    k0 = b_k0 // tk
    assert k0 * tk == b_k0 and (mode != "tn" or b_k0 == 0)
    if mode == "nt":
        b_spec = pl.BlockSpec(lead + (tn, tk), lambda i, j, k: pre + (j, k + k0))
    else:
        b_spec = pl.BlockSpec(lead + (tk, tn), lambda i, j, k: pre + (k + k0, j))
    ca, cb = {"nn": (1, 0), "nt": (1, 1), "tn": (0, 0)}[mode]

    def body(a_ref, b_ref, o_ref, *acc):
        p = _dot(a_ref[...], b_ref[...], ca, cb)
        if nk == 1:
            o_ref[...] = p.astype(o_ref.dtype)
        else:
            k = pl.program_id(2)
            acc_ref = acc[0]

            @pl.when(k == 0)
            def _():
                acc_ref[...] = p

            @pl.when(k > 0)
            def _():
                acc_ref[...] += p

            @pl.when(k == nk - 1)
            def _():
                o_ref[...] = acc_ref[...].astype(o_ref.dtype)

    res = _call_with_side(
        body, name, (M // tm, N // tn, nk), [a_spec, b_spec], [pl.BlockSpec((tm, tn), lambda i, j, k: (i, j))],
        [jax.ShapeDtypeStruct((M, N), out_dtype)], [pltpu.VMEM((tm, tn), f32)] if nk > 1 else [],
        ("parallel", "parallel", "arbitrary"), (a, b), side)
    return res if side else res[0]


def _rowwise(fn, rows, consts, outs, reds, name, tm=256):
    T = rows[0][0].shape[0]
    tm = _div_tile(T, tm, 8)
    n_r, n_c, n_o = len(rows), len(consts), len(outs)

    def body(*refs):
        ins = [r[...] for r in refs[: n_r + n_c]]
        res = fn(*ins)
        res = res if isinstance(res, (tuple, list)) else (res,)
        o_refs = refs[n_r + n_c: n_r + n_c + n_o]
        r_refs = refs[n_r + n_c + n_o:]
        for ref, val in zip(o_refs, res[:n_o]):
            ref[...] = val.astype(ref.dtype)
        i = pl.program_id(0)
        for ref, val in zip(r_refs, res[n_o:]):
            _accumulate(ref, val, i)

    in_specs = [pl.BlockSpec((tm, w), functools.partial(lambda i, cb: (i, cb), cb=cb)) for (_, w, cb) in rows]
    in_specs += [pl.BlockSpec(c.shape, lambda i: (0, 0)) for c in consts]
    out_specs = [pl.BlockSpec((tm, w), lambda i: (i, 0)) for (w, _) in outs]
    out_specs += [pl.BlockSpec((1, w), lambda i: (0, 0)) for w in reds]
    out_shape = [jax.ShapeDtypeStruct((T, w), dt) for (w, dt) in outs]
    out_shape += [jax.ShapeDtypeStruct((1, w), f32) for w in reds]
    return pl.pallas_call(
        body, name=name, grid=(T // tm,), in_specs=in_specs, out_specs=out_specs, out_shape=out_shape,
        compiler_params=_cparams(("arbitrary",)),
    )(*[r[0] for r in rows], *consts)


def _accumulate(ref, val, step):
    @pl.when(step == 0)
    def _():
        ref[...] = val

    @pl.when(step > 0)
    def _():
        ref[...] += val


def _full(a):
    return (a, a.shape[1], 0)


def _colsum(x):
    return jnp.sum(x, axis=0, keepdims=True)


def _ln_stats(u):
    mu = jnp.mean(u, axis=-1, keepdims=True)
    xc = u - mu
    var = jnp.mean(xc * xc, axis=-1, keepdims=True)
    rstd = lax.rsqrt(var + LN_EPS)
    return xc * rstd, rstd


def _ln_fwd(x, f, g, b, name):
    def fn(x, f, g, b):
        xhat, _ = _ln_stats(DEEPNORM_ALPHA * x + f)
        y = xhat * g + b
        return y, y

    return _rowwise(fn, [_full(x), _full(f)], [g, b], [(x.shape[1], f32), (x.shape[1], _MXU)], [], name)


def _ln_bwd(x, f, g, dys, scales, name):
    def fn(x, f, *rest):
        g = rest[-1]
        dy = None
        for d, s in zip(rest[:-1], scales):
            t = d if s == 1.0 else s * d
            dy = t if dy is None else dy + t
        xhat, rstd = _ln_stats(DEEPNORM_ALPHA * x + f)
        dxh = dy * g
        m1 = jnp.mean(dxh, axis=-1, keepdims=True)
        m2 = jnp.mean(dxh * xhat, axis=-1, keepdims=True)
        du = rstd * (dxh - m1 - xhat * m2)
        return du, du, _colsum(dy * xhat), _colsum(dy)

    D = x.shape[1]
    return _rowwise(fn, [_full(x), _full(f)] + [_full(d) for d in dys], [g], [(D, f32), (D, _MXU)], [D, D], name)


def _loss_head(y, t, name):
    D = y.shape[1]

    def fn(y, t):
        e = y - t
        return e * (1.0 / D), _colsum(e * e)

    return _rowwise(fn, [_full(y), _full(t)], [], [(D, f32)], [D], name)


def _axpy(a, b, alpha, name):
    def fn(a, b):
        return alpha * a + b

    return _rowwise(fn, [_full(a), _full(b)], [], [(a.shape[1], f32)], [], name)[0]


def _shift_down(x, k):
    row = lax.broadcasted_iota(jnp.int32, x.shape, 0)
    return jnp.where(row >= k, pltpu.roll(x, k, 0), 0.0)


def _shift_up(x, k):
    S = x.shape[0]
    row = lax.broadcasted_iota(jnp.int32, x.shape, 0)
    return jnp.where(row < S - k, pltpu.roll(x, S - k, 0), 0.0)


def _causal_conv(h, w, b):
    return ((b + w[0:1] * _shift_down(h, 2)) + w[1:2] * _shift_down(h, 1)) + w[2:3] * h


def _conv_gate_fwd(h, cw, cb, nseq, name, tc=256, side=None):
    T, F2 = h.shape
    F, S = F2 // 2, T // nseq
    tc = _div_tile(F, tc, LANE)
    nf = F // tc

    def body(hg_ref, hu_ref, wg_ref, wu_ref, bg_ref, bu_ref, a_ref):
        cg = _causal_conv(hg_ref[...], wg_ref[...], bg_ref[...])
        cu = _causal_conv(hu_ref[...], wu_ref[...], bu_ref[...])
        a_ref[...] = (cg * _sigmoid(cg) * cu).astype(a_ref.dtype)

    res = _call_with_side(
        body, name, (nseq, nf),
        [pl.BlockSpec((S, tc), lambda s, j: (s, j)), pl.BlockSpec((S, tc), lambda s, j: (s, nf + j)),
         pl.BlockSpec((3, tc), lambda s, j: (0, j)), pl.BlockSpec((3, tc), lambda s, j: (0, nf + j)),
         pl.BlockSpec((1, tc), lambda s, j: (0, j)), pl.BlockSpec((1, tc), lambda s, j: (0, nf + j))],
        [pl.BlockSpec((S, tc), lambda s, j: (s, j))], [jax.ShapeDtypeStruct((T, F), _MXU)], [],
        ("parallel", "parallel"), (h, h, cw, cw, cb, cb), side)
    return res if side else res[0]


def _conv_gate_bwd(h, da, cw, cb, nseq, name, tc=128, side=None):
    T, F2 = h.shape
    F, S = F2 // 2, T // nseq
    tc = _div_tile(F, tc, LANE)
    nf = F // tc

    def conv_bwd(dc, hx, w):
        dh = (w[2:3] * dc + w[1:2] * _shift_up(dc, 1)) + w[0:1] * _shift_up(dc, 2)
        dw = jnp.concatenate([_colsum(dc * _shift_down(hx, 2)), _colsum(dc * _shift_down(hx, 1)), _colsum(dc * hx)], axis=0)
        return dh, dw, _colsum(dc)

    def body(hg_ref, hu_ref, da_ref, wg_ref, wu_ref, bg_ref, bu_ref, dhg_ref, dhu_ref, dwg_ref, dwu_ref, dbg_ref, dbu_ref):
        hg, hu, da = hg_ref[...], hu_ref[...], da_ref[...]
        wg, wu = wg_ref[...], wu_ref[...]
        cg = _causal_conv(hg, wg, bg_ref[...])
        cu = _causal_conv(hu, wu, bu_ref[...])
        sg = _sigmoid(cg)
        dcu = da * (cg * sg)
        dcg = da * cu * (sg * (1.0 + cg * (1.0 - sg)))
        dhg, dwg, dbg = conv_bwd(dcg, hg, wg)
        dhu, dwu, dbu = conv_bwd(dcu, hu, wu)
        dhg_ref[...] = dhg.astype(dhg_ref.dtype)
        dhu_ref[...] = dhu.astype(dhu_ref.dtype)
        s = pl.program_id(1)
        _accumulate(dwg_ref, dwg, s)
        _accumulate(dwu_ref, dwu, s)
        _accumulate(dbg_ref, dbg, s)
        _accumulate(dbu_ref, dbu, s)

    col = lambda j, s: (s, j)
    par = lambda j, s: (0, j)
    dhg, dhu, dwg, dwu, dbg, dbu, *side_res = _call_with_side(
        body, name, (nf, nseq),
        [pl.BlockSpec((S, tc), col), pl.BlockSpec((S, tc), lambda j, s: (s, nf + j)), pl.BlockSpec((S, tc), col),
         pl.BlockSpec((3, tc), par), pl.BlockSpec((3, tc), lambda j, s: (0, nf + j)),
         pl.BlockSpec((1, tc), par), pl.BlockSpec((1, tc), lambda j, s: (0, nf + j))],
        [pl.BlockSpec((S, tc), col), pl.BlockSpec((S, tc), col), pl.BlockSpec((3, tc), par), pl.BlockSpec((3, tc), par),
         pl.BlockSpec((1, tc), par), pl.BlockSpec((1, tc), par)],
        [jax.ShapeDtypeStruct((T, F), _MXU), jax.ShapeDtypeStruct((T, F), _MXU), jax.ShapeDtypeStruct((3, F), f32),
         jax.ShapeDtypeStruct((3, F), f32), jax.ShapeDtypeStruct((1, F), f32), jax.ShapeDtypeStruct((1, F), f32)],
        [], ("parallel", "arbitrary"), (h, h, da, cw, cw, cb, cb), side)
    return dhg, dhu, jnp.concatenate([dwg, dwu], axis=1), jnp.concatenate([dbg, dbu], axis=1), side_res


def _group_row(x, jj):
    C, d = x.shape
    n = C // GLA_SUB
    x3 = x.reshape(n, GLA_SUB, d)
    return jnp.broadcast_to(x3[:, jj:jj + 1, :], (n, GLA_SUB, d)).reshape(C, d)


def _group_sum(x):
    C, d = x.shape
    n = C // GLA_SUB
    s = jnp.sum(x.reshape(n, GLA_SUB, d), axis=1, keepdims=True)
    return jnp.broadcast_to(s, (n, GLA_SUB, d)).reshape(C, d)


def _chunk_cumsum(g):
    C = g.shape[0]
    row = lax.broadcasted_iota(jnp.int32, (C, C), 0)
    col = lax.broadcasted_iota(jnp.int32, (C, C), 1)
    return _exact_dot((row >= col).astype(f32), g)


def _chunk_suffix_sum(x):
    C = x.shape[0]
    row = lax.broadcasted_iota(jnp.int32, (C, C), 0)
    col = lax.broadcasted_iota(jnp.int32, (C, C), 1)
    return _exact_dot((col >= row).astype(f32), x)


def _gla_scores(q, k, b):
    C = q.shape[0]
    n = C // GLA_SUB
    row = lax.broadcasted_iota(jnp.int32, (C, C), 0)
    col = lax.broadcasted_iota(jnp.int32, (C, C), 1)
    blocks = [jnp.zeros((GLA_SUB, C), f32)]
    for s in range(1, n):
        lo = s * GLA_SUB
        bref = b[lo - 1:lo, :]
        qr = q[lo:lo + GLA_SUB] * jnp.exp(b[lo:lo + GLA_SUB] - bref)
        kr = k * jnp.exp(jnp.minimum(bref - b, 0.0))
        blocks.append(_nt(qr, kr))
    sub_start = (row // GLA_SUB) * GLA_SUB
    a = jnp.where(col < sub_start, jnp.concatenate(blocks, axis=0), 0.0)
    rin = lax.broadcasted_iota(jnp.int32, (C, 1), 0) % GLA_SUB
    for jj in range(GLA_SUB):
        e = jnp.exp(jnp.minimum(b - _group_row(b, jj), 0.0))
        colv = jnp.sum(q * _group_row(k, jj) * e, axis=1, keepdims=True)
        colv = jnp.where(rin >= jj, colv, 0.0)
        a = jnp.where(col == sub_start + jj, colv, a)
    return a


def _gla_scores_bwd(da, q, k, b):
    C = q.shape[0]
    n = C // GLA_SUB
    row = lax.broadcasted_iota(jnp.int32, (C, C), 0)
    col = lax.broadcasted_iota(jnp.int32, (C, C), 1)
    sub_start = (row // GLA_SUB) * GLA_SUB
    da_inter = jnp.where(col < sub_start, da, 0.0)
    dq_blocks = [jnp.zeros((GLA_SUB, q.shape[1]), f32)]
    dk = jnp.zeros_like(k)
    for s in range(1, n):
        lo = s * GLA_SUB
        bref = b[lo - 1:lo, :]
        eq = jnp.exp(b[lo:lo + GLA_SUB] - bref)
        ek = jnp.exp(jnp.minimum(bref - b, 0.0))
        das = da_inter[lo:lo + GLA_SUB]
        dq_blocks.append(_nn(das, k * ek) * eq)
        dk = dk + _tn(das, q[lo:lo + GLA_SUB] * eq) * ek
    dq = jnp.concatenate(dq_blocks, axis=0)
    rin = lax.broadcasted_iota(jnp.int32, (C, 1), 0) % GLA_SUB
    for jj in range(GLA_SUB):
        e = jnp.exp(jnp.minimum(b - _group_row(b, jj), 0.0))
        dac = jnp.sum(jnp.where(col == sub_start + jj, da, 0.0), axis=1, keepdims=True)
        dac = jnp.where(rin >= jj, dac, 0.0)
        w = dac * e
        dq = dq + w * _group_row(k, jj)
        dk = dk + jnp.where(rin == jj, _group_sum(w * q), 0.0)
    return dq, dk


def _gla_specs(nC, dk, dv):
    H = GLA_HEADS
    voff = (2 * H * dk) // dv
    assert voff * dv == 2 * H * dk
    return H, voff


def _gla_fwd(proj, gate, nseq, name, side=None):
    T = proj.shape[0]
    dk = gate.shape[1] // GLA_HEADS
    dv = 2 * dk
    C = GLA_CHUNK
    nC = T // nseq // C
    H, voff = _gla_specs(nC, dk, dv)
    scale = dk ** -0.5

    S = T // nseq
    pv, gv = proj.reshape(nseq, S, proj.shape[1]), gate.reshape(nseq, S, gate.shape[1])

    def body(q_ref, k_ref, v_ref, g_ref, o_ref, st_ref, state):
        c = pl.program_id(1)

        @pl.when(c == 0)
        def _():
            state[...] = jnp.zeros_like(state)

        for s in range(nseq):
            q, k, v = q_ref[s] * scale, k_ref[s], v_ref[s]
            b = _chunk_cumsum(g_ref[s])
            st = state[s]
            st_ref[s, 0] = st
            a = _gla_scores(q, k, b)
            o_ref[s] = _nt(q * jnp.exp(b), st) + _nn(a, v)
            bl = b[C - 1:C, :]
            state[s] = st * jnp.exp(bl) + _tn(v, k * jnp.exp(bl - b))

    res = _call_with_side(
        body, name, (H, nC),
        [pl.BlockSpec((nseq, C, dk), lambda h, c: (0, c, h)),
         pl.BlockSpec((nseq, C, dk), lambda h, c: (0, c, H + h)),
         pl.BlockSpec((nseq, C, dv), lambda h, c: (0, c, voff + h)),
         pl.BlockSpec((nseq, C, dk), lambda h, c: (0, c, h))],
        [pl.BlockSpec((nseq, C, dv), lambda h, c: (0, c, h)),
         pl.BlockSpec((nseq, 1, dv, dk), lambda h, c: (0, h * nC + c, 0, 0))],
        [jax.ShapeDtypeStruct((nseq, S, H * dv), f32), jax.ShapeDtypeStruct((nseq, H * nC, dv, dk), f32)],
        [pltpu.VMEM((nseq, dv, dk), f32)], ("parallel", "arbitrary"), (pv, pv, pv, gv), side)
    return [res[0].reshape(T, H * dv), res[1].reshape(nseq * H * nC, dv, dk)] + list(res[2:])


def _gla_bwd(proj, gate, states, do, nseq, name, side=None):
    T = proj.shape[0]
    dk = gate.shape[1] // GLA_HEADS
    dv = 2 * dk
    C = GLA_CHUNK
    nC = T // nseq // C
    H, voff = _gla_specs(nC, dk, dv)
    scale = dk ** -0.5

    def body(q_ref, k_ref, v_ref, g_ref, do_ref, st_ref, dq_ref, dk_ref, dv_ref, dg_ref, dstate, term):
        c = pl.program_id(2)

        @pl.when(c == 0)
        def _():
            dstate[...] = jnp.zeros_like(dstate)
            term[...] = jnp.zeros_like(term)

        q, k, v, do = q_ref[...] * scale, k_ref[...], v_ref[...], do_ref[...]
        b = _chunk_cumsum(g_ref[...])
        st = st_ref[0]
        dst = dstate[...]
        eb = jnp.exp(b)
        bl = b[C - 1:C, :]
        kdec = jnp.exp(bl - b)
        a = _gla_scores(q, k, b)
        rowi = lax.broadcasted_iota(jnp.int32, (C, C), 0)
        coli = lax.broadcasted_iota(jnp.int32, (C, C), 1)
        da = jnp.where(coli <= rowi, _nt(do, v), 0.0)
        dq_s, dk_s = _gla_scores_bwd(da, q, k, b)
        dq = _nn(do, st) * eb + dq_s
        dkk = _nn(v, dst) * kdec + dk_s
        dv_ref[...] = (_tn(a, do) + _nt(k * kdec, dst)).astype(dv_ref.dtype)
        last = lax.broadcasted_iota(jnp.int32, (C, 1), 0) == C - 1
        db = q * dq - k * dkk + jnp.where(last, term[...], 0.0)
        dg_ref[...] = _chunk_suffix_sum(db)
        dq_ref[...] = (dq * scale).astype(dq_ref.dtype)
        dk_ref[...] = dkk.astype(dk_ref.dtype)
        dprev = dst * jnp.exp(bl) + _tn(do, q * eb)
        dstate[...] = dprev
        term[...] = _colsum(st * dprev)

    row = lambda s, h, c: s * nC + (nC - 1 - c)
    kspec = lambda off: pl.BlockSpec((C, dk), lambda s, h, c: (row(s, h, c), off + h))
    vspec = lambda off: pl.BlockSpec((C, dv), lambda s, h, c: (row(s, h, c), off + h))
    return _call_with_side(
        body, name, (nseq, H, nC),
        [kspec(0), kspec(H), vspec(voff), kspec(0), vspec(0),
         pl.BlockSpec((1, dv, dk), lambda s, h, c: ((s * H + h) * nC + (nC - 1 - c), 0, 0))],
        [kspec(0), kspec(0), vspec(0), kspec(0)],
        [jax.ShapeDtypeStruct((T, H * dk), _MXU), jax.ShapeDtypeStruct((T, H * dk), _MXU),
         jax.ShapeDtypeStruct((T, H * dv), _MXU), jax.ShapeDtypeStruct((T, H * dk), f32)],
        [pltpu.VMEM((dv, dk), f32), pltpu.VMEM((1, dk), f32)], ("parallel", "parallel", "arbitrary"),
        (proj, proj, proj, gate, do, states), side)


def _head_slices(width, n):
    w = width // n
    return [slice(h * w, (h + 1) * w) for h in range(n)]


def _rms_gate_fwd(o, proj, ng, name):
    W = o.shape[1]

    def fn(o, r, ng):
        parts = []
        for sl in _head_slices(W, GLA_HEADS):
            oh = o[:, sl]
            rstd = lax.rsqrt(jnp.mean(oh * oh, axis=-1, keepdims=True) + RMS_EPS)
            rh = r[:, sl]
            parts.append((oh * rstd * ng) * (rh * _sigmoid(rh)))
        return jnp.concatenate(parts, axis=1)

    return _rowwise(fn, [_full(o), (proj, W, 2)], [ng], [(W, _MXU)], [], name)[0]


def _rms_gate_bwd(o, proj, ng, dy, name):
    W = o.shape[1]

    def fn(o, r, dy, ng):
        dos, drs = [], []
        dng = jnp.zeros((1, W // GLA_HEADS), f32)
        for sl in _head_slices(W, GLA_HEADS):
            oh, rh, dyh = o[:, sl], r[:, sl], dy[:, sl]
            rstd = lax.rsqrt(jnp.mean(oh * oh, axis=-1, keepdims=True) + RMS_EPS)
            ohat = oh * rstd
            sg = _sigmoid(rh)
            don = dyh * (rh * sg)
            drs.append(dyh * (ohat * ng) * (sg * (1.0 + rh * (1.0 - sg))))
            dng = dng + _colsum(don * ohat)
            dohat = don * ng
            dos.append(rstd * (dohat - ohat * jnp.mean(dohat * ohat, axis=-1, keepdims=True)))
        return jnp.concatenate(dos, axis=1), jnp.concatenate(drs, axis=1), dng

    return _rowwise(fn, [_full(o), (proj, W, 2), _full(dy)], [ng], [(W, _MXU), (W, _MXU)], [W // GLA_HEADS], name)


def _log_gate_fwd(z, bias, name):
    def fn(z, bias):
        t = z + bias
        return (jnp.minimum(t, 0.0) - jnp.log1p(jnp.exp(-jnp.abs(t)))) * (1.0 / GLA_GATE_NORMALIZER)

    return _rowwise(fn, [_full(z)], [bias], [(z.shape[1], f32)], [], name)[0]


def _log_gate_bwd(z, bias, dg, name):
    def fn(z, dg, bias):
        dz = dg * (1.0 / GLA_GATE_NORMALIZER) * _sigmoid(-(z + bias))
        return dz, _colsum(dz)

    return _rowwise(fn, [_full(z), _full(dg)], [bias], [(z.shape[1], f32)], [z.shape[1]], name)


def _band_masks(P, steps, has_prev):
    i = lax.broadcasted_iota(jnp.int32, (P, P), 0)
    j = lax.broadcasted_iota(jnp.int32, (P, P), 1)
    cur = (i - j >= 0) & (i - j <= steps)
    prev = (i + P - j <= steps) & has_prev
    return cur, prev


def _dil_dims(T, nseq, dilation):
    L = T // nseq // dilation
    P = min(DIL_BLOCK, L)
    return L, P, L // P


def _dil_tiling(T, nseq, dilation):
    L, P, nb = _dil_dims(T, nseq, dilation)
    hb = DIL_HEADS if dilation == 1 else 1
    row_sets = [pl.ds(r, P, stride=dilation) if dilation > 1 else pl.ds(0, P) for r in range(dilation)]
    head_cols = _head_slices(hb * DIL_HEAD_DIM, hb)
    return L, P, nb, P * dilation, hb, DIL_HEADS // hb, row_sets, head_cols


def _dil_fwd(proj, gi, window, dilation, nseq, name, side=None):
    T = proj.shape[0]
    H, dh = DIL_HEADS, DIL_HEAD_DIM
    L, P, nb, SB, hb, ng, row_sets, head_cols = _dil_tiling(T, nseq, dilation)
    steps = window // dilation
    scale = dh ** -0.5

    def body(q_ref, kc_ref, kp_ref, vc_ref, vp_ref, o_ref, lse_ref):
        sb, hg = pl.program_id(1), pl.program_id(2)
        mc, mp = _band_masks(P, steps, sb > 0)
        lane = lax.broadcasted_iota(jnp.int32, (P, LANE), 1)

        @pl.when(hg == 0)
        def _():
            lse_ref[...] = jnp.zeros_like(lse_ref)

        for hh, cols in enumerate(head_cols):
            for rows in row_sets:
                q = q_ref[rows, cols]
                sc = jnp.where(mc, _nt(q, kc_ref[rows, cols]) * scale, -jnp.inf)
                sp = jnp.where(mp, _nt(q, kp_ref[rows, cols]) * scale, -jnp.inf)
                m = jnp.maximum(jnp.max(sc, axis=-1, keepdims=True), jnp.max(sp, axis=-1, keepdims=True))
                pc, pp = jnp.exp(sc - m), jnp.exp(sp - m)
                l = jnp.sum(pc, axis=-1, keepdims=True) + jnp.sum(pp, axis=-1, keepdims=True)
                o_ref[rows, cols] = _nn(pc / l, vc_ref[rows, cols]) + _nn(pp / l, vp_ref[rows, cols])
                lse_ref[rows, :] = jnp.where(lane == hg * hb + hh, m + jnp.log(l), lse_ref[rows, :])

    cur = lambda part: pl.BlockSpec((SB, hb * dh), lambda s, sb, hg: (s * nb + sb, (gi * 3 + part) * ng + hg))
    prv = lambda part: pl.BlockSpec((SB, hb * dh), lambda s, sb, hg: (s * nb + jnp.maximum(sb - 1, 0), (gi * 3 + part) * ng + hg))
    return _call_with_side(
        body, name, (nseq, nb, ng), [cur(0), cur(1), prv(1), cur(2), prv(2)],
        [pl.BlockSpec((SB, hb * dh), lambda s, sb, hg: (s * nb + sb, hg)), pl.BlockSpec((SB, LANE), lambda s, sb, hg: (s * nb + sb, 0))],
        [jax.ShapeDtypeStruct((T, H * dh), f32), jax.ShapeDtypeStruct((T, LANE), f32)], [],
        ("parallel", "parallel", "arbitrary"), (proj, proj, proj, proj, proj), side)


def _dil_mix_fwd(os_, lses, name):
    W = os_[0].shape[1]
    G = len(os_)

    def fn(*a):
        o, l = a[:G], a[G:]
        lane = lax.broadcasted_iota(jnp.int32, l[0].shape, 1)
        tot = jnp.zeros(l[0].shape, f32)
        parts = []
        for h, sl in enumerate(_head_slices(W, DIL_HEADS)):
            lh = [x[:, h:h + 1] for x in l]
            m = functools.reduce(jnp.maximum, lh)
            e = [jnp.exp(x - m) for x in lh]
            z = functools.reduce(lambda u, v: u + v, e)
            acc = None
            for g in range(G):
                t = (e[g] / z) * o[g][:, sl]
                acc = t if acc is None else acc + t
            parts.append(acc)
            tot = jnp.where(lane == h, m + jnp.log(z), tot)
        y = jnp.concatenate(parts, axis=1)
        return y, y, tot

    return _rowwise(fn, [_full(x) for x in os_] + [_full(x) for x in lses], [], [(W, f32), (W, _MXU), (LANE, f32)], [], name)


def _dil_delta(do, o, name):
    W = o.shape[1]

    def fn(do, o):
        lane = lax.broadcasted_iota(jnp.int32, (do.shape[0], LANE), 1)
        d = jnp.zeros((do.shape[0], LANE), f32)
        for h, sl in enumerate(_head_slices(W, DIL_HEADS)):
            d = jnp.where(lane == h, jnp.sum(do[:, sl] * o[:, sl], axis=-1, keepdims=True), d)
        return d

    return _rowwise(fn, [_full(do), _full(o)], [], [(LANE, f32)], [], name)[0]


def _dil_bwd(proj, do, lse, delta, gi, window, dilation, nseq, name):
    T = proj.shape[0]
    H, dh = DIL_HEADS, DIL_HEAD_DIM
    L, P, nb, SB, hb, ng, row_sets, head_cols = _dil_tiling(T, nseq, dilation)
    steps = window // dilation
    scale = dh ** -0.5

    def probs(q, k, lse_h, mask):
        return jnp.where(mask, jnp.exp(_nt(q, k) * scale - lse_h), 0.0)

    def head_lane(ref, rows, h):
        lane = lax.broadcasted_iota(jnp.int32, (P, LANE), 1)
        return jnp.sum(jnp.where(lane == h, ref[rows, :], 0.0), axis=1, keepdims=True)

    def dq_body(q_ref, kc_ref, kp_ref, vc_ref, vp_ref, do_ref, lse_ref, del_ref, dq_ref):
        sb, hg = pl.program_id(1), pl.program_id(2)
        mc, mp = _band_masks(P, steps, sb > 0)
        for hh, cols in enumerate(head_cols):
            h = hg * hb + hh
            for rows in row_sets:
                q, doh = q_ref[rows, cols], do_ref[rows, cols]
                lse_h, del_h = head_lane(lse_ref, rows, h), head_lane(del_ref, rows, h)
                kc, kp = kc_ref[rows, cols], kp_ref[rows, cols]
                dsc = probs(q, kc, lse_h, mc) * (_nt(doh, vc_ref[rows, cols]) - del_h) * scale
                dsp = probs(q, kp, lse_h, mp) * (_nt(doh, vp_ref[rows, cols]) - del_h) * scale
                dq_ref[rows, cols] = _nn(dsc, kc) + _nn(dsp, kp)

    cur = lambda part: pl.BlockSpec((SB, hb * dh), lambda s, sb, hg: (s * nb + sb, (gi * 3 + part) * ng + hg))
    prv = lambda part: pl.BlockSpec((SB, hb * dh), lambda s, sb, hg: (s * nb + jnp.maximum(sb - 1, 0), (gi * 3 + part) * ng + hg))
    tok = pl.BlockSpec((SB, hb * dh), lambda s, sb, hg: (s * nb + sb, hg))
    aux = pl.BlockSpec((SB, LANE), lambda s, sb, hg: (s * nb + sb, 0))
    dq = pl.pallas_call(
        dq_body, name=name + "_dq", grid=(nseq, nb, ng),
        in_specs=[cur(0), cur(1), prv(1), cur(2), prv(2), tok, aux, aux],
        out_specs=tok, out_shape=jax.ShapeDtypeStruct((T, H * dh), f32),
        compiler_params=_cparams(("parallel", "parallel", "parallel")),
    )(proj, proj, proj, proj, proj, do, lse, delta)

    def dkv_body(k_ref, v_ref, qc_ref, qn_ref, doc_ref, don_ref, lsec_ref, lsen_ref, delc_ref, deln_ref, dk_ref, dv_ref):
        sb, hg = pl.program_id(1), pl.program_id(2)
        mc, mn = _band_masks(P, steps, sb < nb - 1)
        for hh, cols in enumerate(head_cols):
            h = hg * hb + hh
            for rows in row_sets:
                k, v = k_ref[rows, cols], v_ref[rows, cols]
                qc, doc = qc_ref[rows, cols], doc_ref[rows, cols]
                pc = probs(qc, k, head_lane(lsec_ref, rows, h), mc)
                dsc = pc * (_nt(doc, v) - head_lane(delc_ref, rows, h)) * scale
                qn, don = qn_ref[rows, cols], don_ref[rows, cols]
                pn = probs(qn, k, head_lane(lsen_ref, rows, h), mn)
                dsn = pn * (_nt(don, v) - head_lane(deln_ref, rows, h)) * scale
                dk_ref[rows, cols] = _tn(dsc, qc) + _tn(dsn, qn)
                dv_ref[rows, cols] = _tn(pc, doc) + _tn(pn, don)

    nxt = lambda s, sb: s * nb + jnp.minimum(sb + 1, nb - 1)
    qnx = pl.BlockSpec((SB, hb * dh), lambda s, sb, hg: (nxt(s, sb), gi * 3 * ng + hg))
    tokn = pl.BlockSpec((SB, hb * dh), lambda s, sb, hg: (nxt(s, sb), hg))
    auxn = pl.BlockSpec((SB, LANE), lambda s, sb, hg: (nxt(s, sb), 0))
    dkk, dvv = pl.pallas_call(
        dkv_body, name=name + "_dkv", grid=(nseq, nb, ng),
        in_specs=[cur(1), cur(2), cur(0), qnx, tok, tokn, aux, auxn, aux, auxn],
        out_specs=[tok, tok],
        out_shape=[jax.ShapeDtypeStruct((T, H * dh), f32), jax.ShapeDtypeStruct((T, H * dh), f32)],
        compiler_params=_cparams(("parallel", "parallel", "parallel")),
    )(proj, proj, proj, proj, do, do, lse, lse, delta, delta)
    return dq, dkk, dvv


def _adamw_math(w, g, m, v):
    m = ADAM_B1 * m + (1.0 - ADAM_B1) * g
    v = ADAM_B2 * v + (1.0 - ADAM_B2) * (g * g)
    m_hat = m / (1.0 - ADAM_B1 ** ADAM_STEP)
    v_hat = v / (1.0 - ADAM_B2 ** ADAM_STEP)
    return -ADAM_LR * (m_hat / (jnp.sqrt(v_hat) + ADAM_EPS) + ADAM_WD * w), m, v


def _adamw(w, g, m, v, name):
    W = w.shape[1]
    return _rowwise(_adamw_math, [_full(w), _full(g), _full(m), _full(v)], [], [(W, f32)] * 3, [], name, tm=512)


def _position():
    x, y, c = lax.axis_index("x"), lax.axis_index("y"), lax.axis_index("c")
    other_chips = [(1 - x, y), (x, 1 - y), (1 - x, 1 - y)]
    return x, y, c, other_chips


def _chip_index(x, y):
    return 2 * x + y


_ANY = pl.BlockSpec(memory_space=pl.ANY)


def _all_reduce_small(p, name):
    R, Wd = p.shape

    def body(p_ref, o_ref, buf, send_sems, recv_sems):
        x, y, c, _ = _position()
        me = 4 * x + 2 * y + c
        buf[me] = p_ref[...]
        copies = []
        for k in range(1, 8):
            fx, fy, fc = (k >> 2) & 1, (k >> 1) & 1, k & 1
            peer = (x + fx - 2 * x * fx, y + fy - 2 * y * fy, c + fc - 2 * c * fc)
            cp = pltpu.make_async_remote_copy(src_ref=p_ref, dst_ref=buf.at[me], send_sem=send_sems.at[k - 1],
                                              recv_sem=recv_sems.at[k - 1], device_id=peer, device_id_type=MESH)
            cp.start()
            copies.append(cp)
        for cp in copies:
            cp.wait()
        acc = buf[0]
        for s in range(1, 8):
            acc = acc + buf[s]
        o_ref[...] = acc

    return pl.pallas_call(
        body, name=name, out_shape=jax.ShapeDtypeStruct((R, Wd), f32),
        in_specs=[pl.BlockSpec(memory_space=pltpu.VMEM)], out_specs=pl.BlockSpec(memory_space=pltpu.VMEM),
        scratch_shapes=[pltpu.VMEM((8, R, Wd), f32), pltpu.SemaphoreType.DMA((7,)), pltpu.SemaphoreType.DMA((7,))],
        compiler_params=pltpu.CompilerParams(vmem_limit_bytes=VMEM_LIMIT),
    )(p)


def _layer_half(ref, h, axis):
    n = ref.shape[axis] // 2
    idx = (slice(None),) * axis + (pl.ds(h * n, n),)
    return ref.at[idx]


def _comm_call(body, name, ins, out_shapes, n_sems, n_local=0):
    scratch = [pltpu.SemaphoreType.DMA((n_sems,)), pltpu.SemaphoreType.DMA((n_sems,))]
    if n_local:
        scratch.append(pltpu.SemaphoreType.DMA((n_local,)))
    return pl.pallas_call(body, name=name, out_shape=out_shapes, in_specs=[_ANY] * len(ins), out_specs=[_ANY] * len(out_shapes),
                          scratch_shapes=scratch)(*ins)


class _SideCopies:
    def __init__(self, ins, outs, scratch, start, finish):
        self.ins, self.outs, self.scratch, self.start, self.finish = ins, outs, scratch, start, finish


def _row_half(ref, h):
    r = ref.shape[0] // 2
    start = h * r
    if r % 16 == 0:
        start = pl.multiple_of(start, 16)
    return ref.at[pl.ds(start, r)]


def _gather_plan(ws):
    n = len(ws)

    def copy(o_refs, sems, k, src, i, chip_idx, h, to):
        return pltpu.make_async_remote_copy(src_ref=src, dst_ref=_row_half(o_refs[i].at[chip_idx], h), send_sem=sems[0].at[k],
                                            recv_sem=sems[1].at[k], device_id=to, device_id_type=MESH)

    def own_copy(w_refs, o_refs, sems, i, p):
        return pltpu.make_async_copy(w_refs[i], o_refs[i].at[p], sems[2].at[i])

    def over_ici(w_refs, o_refs, sems, i, j, chip, dst_chip_idx, c):
        return copy(o_refs, sems, 3 * i + j, _row_half(w_refs[i], c), i, dst_chip_idx, c, (*chip, c))

    def start(w_refs, o_refs, *sems):
        x, y, c, chips = _position()
        p = _chip_index(x, y)
        for i in range(n):
            own_copy(w_refs, o_refs, sems, i, p).start()
            for j, chip in enumerate(chips):
                over_ici(w_refs, o_refs, sems, i, j, chip, p, c).start()

    def finish(w_refs, o_refs, *sems):
        x, y, c, chips = _position()
        p = _chip_index(x, y)
        sibling = (x, y, 1 - c)
        passed = []
        for i in range(n):
            for j, chip in enumerate(chips):
                q = _chip_index(*chip)
                over_ici(w_refs, o_refs, sems, i, j, chip, q, c).wait_recv()
                fwd = copy(o_refs, sems, 3 * n + 3 * i + j, _row_half(o_refs[i].at[q], c), i, q, c, sibling)
                fwd.start()
                passed.append(fwd)
        for i in range(n):
            for j, chip in enumerate(chips):
                copy(o_refs, sems, 3 * n + 3 * i + j, _row_half(w_refs[i], c), i, _chip_index(*chip), 1 - c, sibling).wait_recv()
        for i in range(n):
            for j, chip in enumerate(chips):
                over_ici(w_refs, o_refs, sems, i, j, chip, p, c).wait_send()
            own_copy(w_refs, o_refs, sems, i, p).wait()
        for fwd in passed:
            fwd.wait_send()

    scratch = [pltpu.SemaphoreType.DMA((6 * n,)), pltpu.SemaphoreType.DMA((6 * n,)), pltpu.SemaphoreType.DMA((n,))]
    return _SideCopies(list(ws), [jax.ShapeDtypeStruct((4,) + w.shape, w.dtype) for w in ws], scratch, start, finish)


def _run_copies(plan, name):
    n_i, n_o = len(plan.ins), len(plan.outs)

    def body(*refs):
        plan.start(refs[:n_i], refs[n_i:n_i + n_o], *refs[n_i + n_o:])
        plan.finish(refs[:n_i], refs[n_i:n_i + n_o], *refs[n_i + n_o:])

    return pl.pallas_call(body, name=name, out_shape=plan.outs, in_specs=[_ANY] * n_i, out_specs=[_ANY] * n_o,
                          scratch_shapes=plan.scratch)(*plan.ins)


def _sibling_halves(gs, name):
    n = len(gs)

    def body(*refs):
        g_refs, o_refs, send_sems, recv_sems = refs[:n], refs[n:2 * n], refs[2 * n], refs[2 * n + 1]
        x, y, c, _ = _position()
        copies = []
        for i in range(n):
            cp = pltpu.make_async_remote_copy(src_ref=_layer_half(g_refs[i], 1 - c, 1), dst_ref=o_refs[i], send_sem=send_sems.at[i],
                                              recv_sem=recv_sems.at[i], device_id=(x, y, 1 - c), device_id_type=MESH)
            cp.start()
            copies.append(cp)
        for cp in copies:
            cp.wait()

    outs = [jax.ShapeDtypeStruct((4, g.shape[1] // 2) + g.shape[2:], g.dtype) for g in gs]
    return _comm_call(body, name, gs, outs, n)


def _exchange_plan(hs):
    n = len(hs)

    def copies(h_refs, o_refs, send_sems, recv_sems):
        x, y, c, chips = _position()
        return [pltpu.make_async_remote_copy(src_ref=h_refs[i].at[_chip_index(*chip)], dst_ref=o_refs[i].at[j],
                                             send_sem=send_sems.at[3 * i + j], recv_sem=recv_sems.at[3 * i + j],
                                             device_id=(*chip, c), device_id_type=MESH)
                for i in range(n) for j, chip in enumerate(chips)]

    def start(h_refs, o_refs, *sems):
        for cp in copies(h_refs, o_refs, *sems):
            cp.start()

    def finish(h_refs, o_refs, *sems):
        for cp in copies(h_refs, o_refs, *sems):
            cp.wait()

    scratch = [pltpu.SemaphoreType.DMA((3 * n,)), pltpu.SemaphoreType.DMA((3 * n,))]
    return _SideCopies(list(hs), [jax.ShapeDtypeStruct((3,) + h.shape[1:], h.dtype) for h in hs], scratch, start, finish)


def _sibling_swap(ts, name):
    n = len(ts)

    def body(*refs):
        t_refs, o_refs, send_sems, recv_sems = refs[:n], refs[n:2 * n], refs[2 * n], refs[2 * n + 1]
        x, y, c, _ = _position()
        copies = []
        for i in range(n):
            cp = pltpu.make_async_remote_copy(src_ref=t_refs[i], dst_ref=o_refs[i], send_sem=send_sems.at[i], recv_sem=recv_sems.at[i],
                                              device_id=(x, y, 1 - c), device_id_type=MESH)
            cp.start()
            copies.append(cp)
        for cp in copies:
            cp.wait()

    return _comm_call(body, name, ts, [jax.ShapeDtypeStruct(t.shape, t.dtype) for t in ts], n)


BLOCK_ELEMS = 384 * 1024


def _block_2d(rows, cols, sub):
    tns = [t for t in range(LANE, cols + 1, LANE) if cols % t == 0] if cols % LANE == 0 else [cols]
    tms = [t for t in range(sub, rows + 1, sub) if rows % t == 0] or [rows]
    fits = [(tm * tn, tn, tm) for tm in tms for tn in tns if tm * tn <= BLOCK_ELEMS]
    assert fits, (rows, cols, sub)
    _, tn, tm = max(fits)
    return tm, tn


def _prefetch_call(body, name, scalars, grid, in_specs, out_specs, out_shape, args, sem):
    gs = pltpu.PrefetchScalarGridSpec(num_scalar_prefetch=1, grid=grid, in_specs=in_specs, out_specs=out_specs)
    return pl.pallas_call(body, name=name, grid_spec=gs, out_shape=out_shape, compiler_params=_cparams(sem))(scalars, *args)


def _add_own_half(g, got, c, name):
    _, nl, K, N = g.shape
    hl = nl // 2
    tm, tn = _block_2d(K, N, 16)

    def body(c_ref, g_ref, r_ref, o_ref):
        o_ref[...] = (g_ref[...].astype(f32) + r_ref[...].astype(f32)).astype(o_ref.dtype)

    blk = (1, 1, tm, tn)
    return _prefetch_call(
        body, name, jnp.reshape(c, (1,)).astype(jnp.int32), (4, hl, K // tm, N // tn),
        [pl.BlockSpec(blk, lambda s, l, i, j, c_ref: (s, c_ref[0] * hl + l, i, j)), pl.BlockSpec(blk, lambda s, l, i, j, c_ref: (s, l, i, j))],
        pl.BlockSpec(blk, lambda s, l, i, j, c_ref: (s, l, i, j)), jax.ShapeDtypeStruct((4, hl, K, N), g.dtype), (g, got),
        ("parallel",) * 4)


def _add_chips(h, got, p, name):
    _, nl, K, N = h.shape
    tm, tn = _block_2d(K, N, 16)

    def body(p_ref, h_ref, r0_ref, r1_ref, r2_ref, o_ref):
        o_ref[...] = ((h_ref[0].astype(f32) + r0_ref[0].astype(f32)) + r1_ref[0].astype(f32)) + r2_ref[0].astype(f32)

    blk = (1, 1, tm, tn)
    got_spec = lambda q: pl.BlockSpec(blk, lambda l, i, j, p_ref: (q, l, i, j))
    return _prefetch_call(
        body, name, jnp.reshape(p, (1,)).astype(jnp.int32), (nl, K // tm, N // tn),
        [pl.BlockSpec(blk, lambda l, i, j, p_ref: (p_ref[0], l, i, j)), got_spec(0), got_spec(1), got_spec(2)],
        pl.BlockSpec((1, tm, tn), lambda l, i, j, p_ref: (l, i, j)), jax.ShapeDtypeStruct((nl, K, N), f32), (h, got, got, got),
        ("parallel",) * 3)


def _adamw_halves(w, m, v, own, other, c, name):
    nl, K, N = w.shape
    tm, tn = _block_2d(K // 2, N, 8)
    nb = K // 2 // tm

    def body(c_ref, w_ref, m_ref, v_ref, own_ref, other_ref, g_out, d_out, m_out, v_out):
        g = jnp.where(pl.program_id(1) == c_ref[0], own_ref[...], other_ref[...])
        d, m_new, v_new = _adamw_math(w_ref[...], g, m_ref[...], v_ref[...])
        g_out[...] = g
        d_out[...] = d
        m_out[...] = m_new
        v_out[...] = v_new

    blk = (1, tm, tn)
    full = pl.BlockSpec(blk, lambda l, h, i, j, c_ref: (l, h * nb + i, j))
    half = pl.BlockSpec(blk, lambda l, h, i, j, c_ref: (l, i, j))
    return _prefetch_call(
        body, name, jnp.reshape(c, (1,)).astype(jnp.int32), (nl, 2, nb, N // tn), [full, full, full, half, half], [full] * 4,
        [jax.ShapeDtypeStruct((nl, K, N), f32)] * 4, (w, m, v, own, other), ("parallel",) * 4)


def _row_halves_view(g):
    return g.reshape(4, 2, g.shape[1] // 2, g.shape[2])


def _reduce_scatter_start(gs, name):
    c = lax.axis_index("c")
    views = [_row_halves_view(g) for g in gs]
    from_sibling = _sibling_halves(views, name + "_d2d")
    return [_add_own_half(g, r, c, f"{name}_add2_{i}") for i, (g, r) in enumerate(zip(views, from_sibling))]


def _reduce_scatter_finish(hs, got, name):
    p = _chip_index(lax.axis_index("x"), lax.axis_index("y"))
    return [_add_chips(h, r, p, f"{name}_add4_{i}")[0] for i, (h, r) in enumerate(zip(hs, got))]


_BIG = (("gla_w_in", 1), ("gla_w_out", 0), ("dil_w_in", 1), ("dil_w_out", 1), ("ffn_w_up", 1), ("ffn_w_down", 0))


def _pad_rows(a, mult):
    r = (-a.shape[0]) % mult
    return a if r == 0 else jnp.concatenate([a, jnp.zeros((r,) + a.shape[1:], a.dtype)], axis=0)


def _unshard(blocks, axis):
    _, K, N = blocks.shape
    if axis == 1:
        return blocks.transpose(1, 0, 2).reshape(K, 4 * N)
    return blocks.reshape(4 * K, N)


def _to_shards(mat, axis):
    K, N = mat.shape
    if axis == 1:
        return mat.reshape(K, 4, N // 4).transpose(1, 0, 2)
    return mat.reshape(4, K // 4, N)


_SMALL = (("gla_w_gate_up", 2), ("gla_gate_bias", None), ("gla_norm_g", None), ("ffn_conv_w", 2), ("ffn_conv_b", None),
          ("ln_g", 2), ("ln_b", 2))


def _pack_rows(arrs, width=LANE):
    flat = _pad_rows(jnp.concatenate([a.reshape(-1) for a in arrs]), 8 * width)
    return flat.reshape(-1, width)


def _unpack_rows(packed, shapes):
    flat, out, off = packed.reshape(-1), [], 0
    for s in shapes:
        n = 1
        for d in s:
            n *= d
        out.append(flat[off:off + n].reshape(s))
        off += n
    return out


def _gather_small_params(shards):
    x, y, c = lax.axis_index("x"), lax.axis_index("y"), lax.axis_index("c")
    names = [n for n, axis in _SMALL if axis is not None]
    mine = _pack_rows([shards[n] for n in names])
    mine = jnp.where(c == 0, mine, jnp.zeros_like(mine))
    rows = mine.shape[0]
    placed = lax.dynamic_update_slice(jnp.zeros((4 * rows, LANE), f32), mine, (_chip_index(x, y) * rows, 0))
    allp = _all_reduce_small(placed, "gather_small").reshape(4, rows, LANE)
    out = {n: shards[n] for n, axis in _SMALL if axis is None}
    per_chip = [_unpack_rows(allp[q], [shards[n].shape for n in names]) for q in range(4)]
    for i, n in enumerate(names):
        out[n] = jnp.concatenate([per_chip[q][i] for q in range(4)], axis=2)
    return out


def _reduce_small_grads(grads, shards):
    names = [n for n, _ in _SMALL]
    total = _all_reduce_small(_pack_rows([grads[n] for n in names]), "reduce_small")
    full = dict(zip(names, _unpack_rows(total, [grads[n].shape for n in names])))
    p = _chip_index(lax.axis_index("x"), lax.axis_index("y"))
    out = {}
    for n, axis in _SMALL:
        if axis is None:
            out[n] = full[n]
        else:
            w = shards[n].shape[axis]
            out[n] = lax.dynamic_slice_in_dim(full[n], p * w, w, axis=axis)
    return out


def _pad_cols(a, n):
    return a if a.shape[-1] == n else jnp.concatenate([a, jnp.zeros(a.shape[:-1] + (n - a.shape[-1],), a.dtype)], axis=-1)


def _ffn_width(F):
    return -(-F // 512) * 512


def kernel(x, gla_w_in, gla_w_gate_up, gla_gate_bias, gla_norm_g, gla_w_out, dil_w_in, dil_w_out, ffn_w_up, ffn_conv_w, ffn_conv_b, ffn_w_down, ln_g, ln_b, loss_target, m_gla_w_in, m_gla_w_gate_up, m_gla_gate_bias, m_gla_norm_g, m_gla_w_out, m_dil_w_in, m_dil_w_out, m_ffn_w_up, m_ffn_conv_w, m_ffn_conv_b, m_ffn_w_down, m_ln_g, m_ln_b, v_gla_w_in, v_gla_w_gate_up, v_gla_gate_bias, v_gla_norm_g, v_gla_w_out, v_dil_w_in, v_dil_w_out, v_ffn_w_up, v_ffn_conv_w, v_ffn_conv_b, v_ffn_w_down, v_ln_g, v_ln_b):
    names = ["gla_w_in", "gla_w_gate_up", "gla_gate_bias", "gla_norm_g", "gla_w_out", "dil_w_in", "dil_w_out", "ffn_w_up",
             "ffn_conv_w", "ffn_conv_b", "ffn_w_down", "ln_g", "ln_b"]
    w_sh = dict(zip(names, (gla_w_in, gla_w_gate_up, gla_gate_bias, gla_norm_g, gla_w_out, dil_w_in, dil_w_out, ffn_w_up,
                            ffn_conv_w, ffn_conv_b, ffn_w_down, ln_g, ln_b)))
    m_sh = dict(zip(names, (m_gla_w_in, m_gla_w_gate_up, m_gla_gate_bias, m_gla_norm_g, m_gla_w_out, m_dil_w_in, m_dil_w_out,
                            m_ffn_w_up, m_ffn_conv_w, m_ffn_conv_b, m_ffn_w_down, m_ln_g, m_ln_b)))
    v_sh = dict(zip(names, (v_gla_w_in, v_gla_w_gate_up, v_gla_gate_bias, v_gla_norm_g, v_gla_w_out, v_dil_w_in, v_dil_w_out,
                            v_ffn_w_up, v_ffn_conv_w, v_ffn_conv_b, v_ffn_w_down, v_ln_g, v_ln_b)))
    nseq, S, D = x.shape
    T = nseq * S
    small = _gather_small_params(w_sh)
    F = 4 * w_sh["ffn_w_down"].shape[1]
    Fp = _ffn_width(F)
    qkvr = 4 * w_sh["gla_w_in"].shape[2] - GLA_GATE_RANK
    c_idx = lax.axis_index("c")
    shard_axis = dict(_BIG)

    def pad_halves(a):
        return jnp.concatenate([_pad_cols(a[..., :F], Fp), _pad_cols(a[..., F:], Fp)], axis=-1)

    cw_all = pad_halves(small["ffn_conv_w"])
    cb_all = pad_halves(small["ffn_conv_b"][:, None, :])
    w_gate_up_all = jnp.pad(small["gla_w_gate_up"].astype(bf16), ((0, 0), (0, GATE_PAD - GLA_GATE_RANK), (0, 0)))

    def mixer_names(l):
        return ("gla_w_in", "gla_w_out") if l % 2 == 0 else ("dil_w_in", "dil_w_out")

    def shard_of(l, which):
        n_in, n_out = mixer_names(l)
        if which in ("w_up_a", "w_up_b"):
            half = w_sh["ffn_w_up"].shape[2] // 2
            shard = w_sh["ffn_w_up"][l].T
            return (shard[:half] if which == "w_up_a" else shard[half:]).astype(bf16)
        name, idx = {"w_in": (n_in, l // 2), "w_out": (n_out, l // 2), "w_down": ("ffn_w_down", l)}[which]
        return w_sh[name][idx].astype(bf16)

    weights = [dict() for _ in range(DEPTH)]

    def install(l, which, gathered):
        n_in, n_out = mixer_names(l)
        if which == "w_in":
            weights[l]["w_in"] = w_in = _unshard(gathered, shard_axis[n_in])
            if l % 2 == 0:
                weights[l]["w_gate"] = _pad_cols(w_in[:, qkvr:], GATE_PAD)
        elif which == "w_out":
            weights[l]["w_out"] = _unshard(gathered, shard_axis[n_out])
        elif which in ("w_up_a", "w_up_b"):
            weights[l][which] = gathered
            if "w_up_a" in weights[l] and "w_up_b" in weights[l]:
                both = jnp.concatenate([weights[l].pop("w_up_a"), weights[l].pop("w_up_b")], axis=1)
                wt = _unshard(both, 0)
                weights[l]["w_up_t"] = jnp.concatenate([_pad_rows(wt[:F], Fp), _pad_rows(wt[F:], Fp)], axis=0)
        else:
            weights[l]["w_down"] = _pad_rows(_unshard(gathered, shard_axis["ffn_w_down"]), Fp)

    kinds = ("w_in", "w_out", "w_up", "w_down")
    install(0, "w_in", _run_copies(_gather_plan([shard_of(0, "w_in")]), "gather_first")[0])

    riders = [{"proj": [(0, "w_out"), (0, "w_down")], "core": [(0, "w_up_a"), (0, "w_up_b")], "ffn_up": [(1, "w_in")]},
              {"proj": [(0, "w_out"), (0, "w_down")], "attn0": [(0, "w_up_a")], "core": [(0, "w_up_b")], "ffn_up": [(1, "w_in")]}]

    def ridden(l, kernel):
        return [(l + dl, which) for dl, which in riders[l % 2].get(kernel, []) if l + dl < DEPTH]

    def side_for(l, kernel):
        items = ridden(l, kernel)
        return _gather_plan([shard_of(m, which) for m, which in items]) if items else None

    def carried(l, kernel, res, n_own=1):
        items = ridden(l, kernel)
        if not items:
            return res
        for (m, which), gathered in zip(items, res[n_own:]):
            install(m, which, gathered)
        return res[0] if n_own == 1 else res[:n_own]

    h0 = x.reshape(T, D)
    saved = []
    cur, cur_b = h0, h0.astype(_MXU)
    fwd = dict(tm=1024, tn=1024, tk=2048)
    for i in range(DEPTH):
        j = i // 2
        tag = f"l{i}_"
        lg, lb = small["ln_g"][i], small["ln_b"][i]
        W = weights[i]
        if i % 2 == 0:
            gate_bias, norm_g = small["gla_gate_bias"][j][None, :], small["gla_norm_g"][j][None, :]
            proj = carried(i, "proj", _mm(cur_b, W["w_in"], "nn", tag + "gla_proj", n_out=qkvr, side=side_for(i, "proj"), **fwd))
            g_low = _mm(cur_b, W["w_gate"], "nn", tag + "gla_glow", tn=GATE_PAD)
            z = _mm(g_low, w_gate_up_all, "nn", tag + "gla_z", layer=j)
            gate = _log_gate_fwd(z, gate_bias, tag + "gla_gate")
            o, states = carried(i, "core", _gla_fwd(proj, gate, nseq, tag + "gla_core", side=side_for(i, "core")), n_own=2)
            y_b = _rms_gate_fwd(o, proj, norm_g, tag + "gla_norm")
            mix = _mm(y_b, W["w_out"], "nn", tag + "gla_out", **fwd)
            mixer_saved = (proj, g_low, z, gate, states, o, y_b)
        else:
            proj = carried(i, "proj", _mm(cur_b, W["w_in"], "nn", tag + "dil_proj", side=side_for(i, "proj"), **fwd))
            outs, lses = [], []
            for gi, (window, dilation) in enumerate(DIL_PATTERNS):
                kern = {0: "attn0", 1: "core"}.get(gi)
                og, lg_ = carried(i, kern, _dil_fwd(proj, gi, window, dilation, nseq, tag + f"dil_attn{gi}", side=side_for(i, kern)), n_own=2)
                outs.append(og)
                lses.append(lg_)
            y, y_b, lse_tot = _dil_mix_fwd(outs, lses, tag + "dil_mix")
            mix = _mm(y_b, W["w_out"], "nn", tag + "dil_out", **fwd)
            mixer_saved = (proj, y, y_b, lse_tot)
        x1, x1_b = _ln_fwd(cur, mix, lg[0:1], lb[0:1], tag + "ln1")
        cw, cb = cw_all[i], cb_all[i]
        hh = carried(i, "ffn_up", _mm(x1_b, W["w_up_t"], "nt", tag + "ffn_up", side=side_for(i, "ffn_up"), **fwd))
        act_b = carried(i, "conv", _conv_gate_fwd(hh, cw, cb, nseq, tag + "ffn_conv", side=side_for(i, "conv")))
        ffn = carried(i, "ffn_down", _mm(act_b, W["w_down"], "nn", tag + "ffn_down", tm=1024, tn=512, tk=Fp, side=side_for(i, "ffn_down")))
        x2, x2_b = _ln_fwd(x1, ffn, lg[1:2], lb[1:2], tag + "ln2")
        saved.append((cur, cur_b, mix, x1, x1_b, hh, act_b, ffn, mixer_saved))
        cur, cur_b = x2, x2_b

    dy, sq = _loss_head(cur, loss_target.reshape(T, D), "loss_head")
    loss = lax.psum(0.5 * jnp.sum(sq) / D, ("x", "y", "c"))

    gb = {n: [None] * w_sh[n].shape[0] for n in names}
    own_half = {n: [None] * w_sh[n].shape[0] for n, _ in _BIG}
    bwd_riders = [{"down_dx": [(1, "w_down")], "conv": [(1, "w_up")], "gate_dw": [(0, "w_down")],
                   "core_bwd": [(1, "w_in"), (0, "w_up")], "up_dx": [(1, "w_out")]},
                  {"down_dx": [(1, "w_down")], "conv": [(1, "w_up")], "gate_dw": [(1, "w_in")], "up_dx": [(1, "w_out")]}]
    pending = {}
    at = {"layer": None}

    def grad_slot(l, which):
        n_in, n_out = mixer_names(l)
        return {"w_in": (n_in, l // 2), "w_out": (n_out, l // 2), "w_up": ("ffn_w_up", l), "w_down": ("ffn_w_down", l)}[which]

    def start_scatter(l, whiches, tag):
        local = []
        for which in whiches:
            n, idx = grad_slot(l, which)
            local.append(_to_shards(gb[n][idx], 0 if n == "ffn_w_up" else shard_axis[n]))
        for which, part in zip(whiches, _reduce_scatter_start(local, f"scatter_l{l}_{tag}")):
            pending[(l, which)] = part

    def riders_of(kernel):
        l = at["layer"]
        return [(l + dl, which) for dl, which in bwd_riders[l % 2].get(kernel, []) if (l + dl, which) in pending]

    def ride(kernel):
        keys = riders_of(kernel)
        return _exchange_plan([pending[k] for k in keys]) if keys else None

    def landed(kernel, got):
        for (l, which), g in zip(riders_of(kernel), got):
            n, idx = grad_slot(l, which)
            own_half[n][idx] = _reduce_scatter_finish([pending.pop((l, which))], [g], f"scatter_l{l}_{which}")[0]

    def unwrap(kernel, res, n_own=1):
        if not riders_of(kernel):
            return res
        landed(kernel, res[n_own:])
        return res[0] if n_own == 1 else res[:n_own]

    d_res = None
    d_in = dy
    for i in reversed(range(DEPTH)):
        j = i // 2
        tag = f"l{i}_b_"
        xin, xin_b, mix, x1, x1_b, hh, act_b, ffn, mixer_saved = saved[i]
        lg = small["ln_g"][i]
        cw, cb = cw_all[i], cb_all[i]
        dw_tiles = dict(tm=1024, tn=1024, tk=4096, out_dtype=bf16)
        dx_tiles = dict(tm=1024, tn=512, tk=6144)
        dys, scales = ([d_in], [1.0]) if d_res is None else ([d_res, d_in], [DEEPNORM_ALPHA, 1.0])
        du2, du2_b, dg2, db2 = _ln_bwd(x1, ffn, lg[1:2], dys, scales, tag + "ln2")
        at["layer"] = i
        early = ("w_down", "w_up") if i % 2 == 0 else ()
        gb["ffn_w_down"][i] = _mm(act_b, du2_b, "tn", tag + "ffn_down_dw", **dw_tiles)[:F]
        if "w_down" in early:
            start_scatter(i, ["w_down"], "down")
        W = weights[i]
        dact = unwrap("down_dx", _mm(du2_b, W["w_down"], "nt", tag + "ffn_down_dx", side=ride("down_dx"), **dx_tiles))
        dhg, dhu, dcw, dcb, rode = _conv_gate_bwd(hh, dact, cw, cb, nseq, tag + "ffn_conv", side=ride("conv"))
        landed("conv", rode)
        gb["ffn_conv_w"][i] = jnp.concatenate([dcw[:, :F], dcw[:, Fp:Fp + F]], axis=1)
        gb["ffn_conv_b"][i] = jnp.concatenate([dcb[0, :F], dcb[0, Fp:Fp + F]], axis=0)
        dwg_t = unwrap("gate_dw", _mm(dhg, x1_b, "tn", tag + "ffn_gate_dw", side=ride("gate_dw"), **dw_tiles))
        dwu_t = _mm(dhu, x1_b, "tn", tag + "ffn_up_dw", **dw_tiles)
        gb["ffn_w_up"][i] = jnp.concatenate([dwg_t[:F], dwu_t[:F]], axis=0)
        if "w_up" in early:
            start_scatter(i, ["w_up"], "up")
        dx1g = _mm(dhg, W["w_up_t"], "nn", tag + "ffn_gate_dx", **dx_tiles)
        dx1u = unwrap("up_dx", _mm(dhu, W["w_up_t"], "nn", tag + "ffn_up_dx", b_k0=Fp, side=ride("up_dx"), **dx_tiles))
        du1, du1_b, dg1, db1 = _ln_bwd(xin, mix, lg[0:1], [du2, dx1g, dx1u], [DEEPNORM_ALPHA, 1.0, 1.0], tag + "ln1")
        gb["ln_g"][i] = jnp.concatenate([dg1, dg2], axis=0)
        gb["ln_b"][i] = jnp.concatenate([db1, db2], axis=0)
        if i % 2 == 0:
            proj, g_low, z, gate, states, o, y_b = mixer_saved
            gate_bias, norm_g = small["gla_gate_bias"][j][None, :], small["gla_norm_g"][j][None, :]
            gb["gla_w_out"][j] = _mm(y_b, du1_b, "tn", tag + "gla_out_dw", **dw_tiles)
            dyy = _mm(du1_b, W["w_out"], "nt", tag + "gla_out_dx", **dx_tiles)
            do, dr, dng = _rms_gate_bwd(o, proj, norm_g, dyy, tag + "gla_norm")
            gb["gla_norm_g"][j] = dng[0]
            dq, dk_, dv_, dgate = unwrap("core_bwd", _gla_bwd(proj, gate, states, do, nseq, tag + "gla_core", side=ride("core_bwd")), n_own=4)
            dz, dbias = _log_gate_bwd(z, gate_bias, dgate, tag + "gla_gate")
            gb["gla_gate_bias"][j] = dbias[0]
            gb["gla_w_gate_up"][j] = _mm(g_low, dz, "tn", tag + "gla_z_dw", tk=1024)[:GLA_GATE_RANK]
            dg_low = _mm(dz, w_gate_up_all, "nt", tag + "gla_z_dx", tn=GATE_PAD, layer=j)
            dproj = jnp.concatenate([dq, dk_, dv_, dr], axis=1)
            dw_main = _mm(xin_b, dproj, "tn", tag + "gla_proj_dw", **dw_tiles)
            dw_gate = _mm(xin_b, dg_low, "tn", tag + "gla_glow_dw", tm=1024, tn=GATE_PAD, tk=2048, out_dtype=bf16)[:, :GLA_GATE_RANK]
            gb["gla_w_in"][j] = jnp.concatenate([dw_main, dw_gate], axis=1)
            dxa = _mm(dproj, W["w_in"], "nt", tag + "gla_proj_dx", **dx_tiles)
            dxb = _mm(dg_low, W["w_gate"], "nt", tag + "gla_glow_dx")
            d_in = _axpy(dxa, dxb, 1.0, tag + "gla_dx_sum")
        else:
            proj, y, y_b, lse_tot = mixer_saved
            gb["dil_w_out"][j] = _mm(y_b, du1_b, "tn", tag + "dil_out_dw", **dw_tiles)
            dyy = _mm(du1_b, W["w_out"], "nt", tag + "dil_out_dx", **dx_tiles)
            delta = _dil_delta(dyy, y, tag + "dil_delta")
            pieces = []
            for gi, (window, dilation) in enumerate(DIL_PATTERNS):
                pieces += list(_dil_bwd(proj, dyy, lse_tot, delta, gi, window, dilation, nseq, tag + f"dil_attn{gi}"))
            dproj = jnp.concatenate(pieces, axis=1).astype(_MXU)
            gb["dil_w_in"][j] = _mm(xin_b, dproj, "tn", tag + "dil_proj_dw", **dw_tiles)
            d_in = _mm(dproj, W["w_in"], "nt", tag + "dil_proj_dx", **dx_tiles)
        d_res = du1
        start_scatter(i, [k for k in kinds if k not in early], "rest")
    grad_x = _axpy(d_res, d_in, DEEPNORM_ALPHA, "grad_x").reshape(x.shape)

    left = sorted(pending)
    last = [pending.pop(k) for k in left]
    for (l, which), own in zip(left, _reduce_scatter_finish(last, _run_copies(_exchange_plan(last), "scatter_last_ici"), "scatter_last")):
        n, idx = grad_slot(l, which)
        own_half[n][idx] = own
    own_all = [jnp.stack(own_half[n], axis=0) for n, _ in _BIG]
    other_all = _sibling_swap(own_all, "scatter_swap")
    grads, delta, new_m, new_v = {}, {}, {}, {}
    for (n, _), own, other in zip(_BIG, own_all, other_all):
        view = (lambda a: jnp.swapaxes(a, 1, 2)) if n == "ffn_w_up" else (lambda a: a)
        res = _adamw_halves(view(w_sh[n]), view(m_sh[n]), view(v_sh[n]), own, other, c_idx, "adamw_" + n)
        grads[n], delta[n], new_m[n], new_v[n] = [view(r) for r in res]
    grads.update(_reduce_small_grads({n: jnp.stack(gb[n], axis=0) for n, _ in _SMALL}, w_sh))
    small_names = [n for n, _ in _SMALL]
    packed = [_pack_rows([src[n] for n in small_names]) for src in (w_sh, grads, m_sh, v_sh)]
    res = _adamw(*packed, "adamw_small")
    shapes = [w_sh[n].shape for n in small_names]
    for dst, arr in zip((delta, new_m, new_v), res):
        dst.update(dict(zip(small_names, _unpack_rows(arr, shapes))))

    return (loss, grad_x, *[grads[n] for n in names], *[delta[n] for n in names], *[new_m[n] for n in names],
            *[new_v[n] for n in names])
```
